```python
import jax
import jax.numpy as jnp
from jax import lax
import numpy as np

D_MODEL = 1024
BATCH = 4
SEQ = 8192
DEPTH = 1
DEC_BATCH = 32
DEC_SEQ = 4
PAST_LEN = 16384
PAGE_SIZE = 128

ML_WIDTH = D_MODEL // 2
ML_HEADS = 4
ML_HEAD_DIM = ML_WIDTH // ML_HEADS
ML_CHUNK = 128
NSA_WIDTH = D_MODEL - ML_WIDTH
NSA_HEADS = 8
NSA_HEAD_DIM = NSA_WIDTH // NSA_HEADS
NSA_KV_HEADS = 2
NSA_GROUP = NSA_HEADS // NSA_KV_HEADS
CMP_BLOCK = 32
CMP_STRIDE = 16
SEL_BLOCK = 64
N_SEL = 16
WINDOW = 512
Q_BLOCK = 128
FORCED_SCORE = 1.0e4
INVALID_SCORE = -1.0
MEM_LEN = 256
MEM_HEADS = 4
MEM_HEAD_DIM = D_MODEL // MEM_HEADS
PEER_HEADS = 8
PEER_N_KEYS = 128
PEER_N_EXPERTS = PEER_N_KEYS * PEER_N_KEYS
PEER_TOPK = 16
PEER_QUERY_DIM = 256
PEER_TOKEN_BLOCK = 128
FORGET_BIAS = 3.0
EPS = 1e-6
TINY = 1e-30
IN_SPLITS = (ML_WIDTH, ML_WIDTH, ML_WIDTH, ML_WIDTH, ML_HEADS, ML_HEADS,
             NSA_WIDTH, 6 * NSA_KV_HEADS * NSA_HEAD_DIM, 3 * NSA_HEADS)
IN_COLS = sum(IN_SPLITS)
IN_OFFSETS = tuple(int(o) for o in np.cumsum(IN_SPLITS)[:-1])

kernel_name = 'hymba_mlstm_nsa_peer_step'

f32 = jnp.float32


def rmsnorm(x, g):
    xf = x.astype(f32)
    return (xf * lax.rsqrt(jnp.mean(xf * xf, -1, keepdims=True) + EPS) * g).astype(x.dtype)


def masked_softmax(s, mask):
    s = jnp.where(mask, s.astype(f32), -jnp.inf)
    mx = jnp.max(s, -1, keepdims=True)
    mx = jnp.where(jnp.isfinite(mx), mx, 0.0)
    e = jnp.exp(s - mx)
    return e / jnp.maximum(e.sum(-1, keepdims=True), TINY)


def mixer_projections(xn, w_in, b_in, b_forget):
    B, T = xn.shape[:2]
    p = xn @ w_in + b_in
    ml_q, ml_k, ml_v, ml_o, ml_i, ml_f, nq, nkv, ng = jnp.split(p, IN_OFFSETS, axis=-1)
    hd = (B, T, ML_HEADS, ML_HEAD_DIM)
    return (ml_q.reshape(hd), ml_k.reshape(hd), ml_v.reshape(hd), ml_o, ml_i, ml_f + b_forget,
            nq.reshape(B, T, NSA_HEADS, NSA_HEAD_DIM),
            nkv.reshape(B, T, 6, NSA_KV_HEADS, NSA_HEAD_DIM),
            jax.nn.sigmoid(ng).reshape(B, T, NSA_HEADS, 3))


def mlstm_chunk(carry, inp):
    c, n, m = carry
    q, k, v, ig, lf = inp
    L = q.shape[2]
    b = jnp.cumsum(lf, axis=-1)
    causal = jnp.tril(jnp.ones((L, L), bool))
    log_d = jnp.where(causal, b[..., :, None] - b[..., None, :] + ig[..., None, :], -jnp.inf)
    inter = b + m[..., None]
    m_t = jnp.maximum(inter, log_d.max(-1))
    d_mat = jnp.exp(log_d - m_t[..., None])
    a = jnp.exp(inter - m_t)
    qk = jnp.einsum('bhtd,bhsd->bhts', q, k) * d_mat
    num = a[..., None] * jnp.einsum('bhtd,bhde->bhte', q, c) + jnp.einsum('bhts,bhse->bhte', qk, v)
    den = a * jnp.einsum('bhtd,bhd->bht', q, n) + qk.sum(-1)
    h = num / jnp.maximum(jnp.abs(den), jnp.exp(-m_t))[..., None]
    m_new = m_t[..., -1]
    w = jnp.exp(b[..., -1:] - b + ig - m_new[..., None])
    decay = jnp.exp(b[..., -1] + m - m_new)
    c_new = decay[..., None, None] * c + jnp.einsum('bhs,bhsd,bhse->bhde', w, k, v)
    n_new = decay[..., None] * n + jnp.einsum('bhs,bhsd->bhd', w, k)
    return (c_new, n_new, m_new), h


def mlstm_mixer(q, k, v, i_pre, f_pre, o_pre, head_gain, c0, n0, m0):
    B, T, H, d = q.shape
    L = ML_CHUNK if T % ML_CHUNK == 0 else T
    nc = T // L

    def chunks(a):
        a = a.astype(f32).reshape((B, nc, L) + a.shape[2:])
        return jnp.transpose(a, (1, 0, 3, 2) + tuple(range(4, a.ndim)))

    xs = (chunks(q), chunks(k * d ** -0.5), chunks(v), chunks(i_pre),
          chunks(jax.nn.log_sigmoid(f_pre.astype(f32))))
    (c, n, m), h = lax.scan(mlstm_chunk, (c0.astype(f32), n0.astype(f32), m0.astype(f32)), xs)
    h = jnp.transpose(h, (1, 0, 3, 2, 4)).reshape(B, T, H, d)
    h = jax.nn.sigmoid(o_pre.astype(f32)).reshape(B, T, H, d) * h
    h = h * lax.rsqrt(jnp.mean(h * h, -1, keepdims=True) + EPS) * head_gain
    return h.reshape(B, T, H * d).astype(q.dtype), c, n, m


def nsa_compress(rows, pe, w1, b1, w2):
    B, G, T, d = rows.shape
    n_seg = T // CMP_STRIDE
    seg = rows[:, :, :n_seg * CMP_STRIDE].reshape(B, G, n_seg, CMP_STRIDE * d)
    half = CMP_STRIDE * d
    pre = seg[:, :, :-1] @ w1[:half] + seg[:, :, 1:] @ w1[half:] + (pe.reshape(-1) @ w1 + b1)
    ends = jnp.arange(n_seg - 1) * CMP_STRIDE + (CMP_BLOCK - 1)
    return jax.nn.gelu(pre) @ w2, ends


def to_sel_blocks(rows):
    B, G, T, d = rows.shape
    nsb = -(-T // SEL_BLOCK)
    rows = jnp.pad(rows, ((0, 0), (0, 0), (0, nsb * SEL_BLOCK - T), (0, 0)))
    return rows.reshape(B, G, nsb, SEL_BLOCK, d)


def cmp_to_sel(imp, nsb):
    r = SEL_BLOCK // CMP_STRIDE
    nc = imp.shape[-1]
    lead = imp.shape[:-1]
    tot = r * nsb
    padw = [(0, 0)] * len(lead)
    first = jnp.pad(imp, padw + [(0, tot - nc)]).reshape(lead + (nsb, r)).sum(-1)
    second = jnp.pad(imp, padw + [(1, tot - nc - 1)]).reshape(lead + (nsb, r)).sum(-1)
    return 0.5 * (first + second)


def nsa_attend(q, gates, q_pos, kc, vc, c_end, ks, vs, kw, vw, w_pos):
    B, G, R, Q, d = q.shape
    scale = d ** -0.5
    p_c = masked_softmax(jnp.einsum('bgrqd,bgnd->bgrqn', q, kc) * scale, c_end[None, :] <= q_pos[:, None])
    o_c = jnp.einsum('bgrqn,bgnd->bgrqd', p_c, vc)
    nsb = ks.shape[2]
    imp = cmp_to_sel(p_c.sum(2), nsb)
    blk = jnp.arange(nsb)
    valid = blk[None, :] * SEL_BLOCK <= q_pos[:, None]
    forced = (blk[None, :] == 0) | (blk[None, :] == q_pos[:, None] // SEL_BLOCK)
    score = jnp.where(forced, FORCED_SCORE, jnp.where(valid, imp, INVALID_SCORE))
    _, idx = lax.top_k(score, min(N_SEL, nsb))
    bi = jnp.arange(B)[:, None, None, None]
    gi = jnp.arange(G)[None, :, None, None]
    kg = ks[bi, gi, idx].reshape(B, G, Q, -1, d)
    vg = vs[bi, gi, idx].reshape(B, G, Q, -1, d)
    k_pos = (idx[..., None] * SEL_BLOCK + jnp.arange(SEL_BLOCK)).reshape(B, G, Q, -1)
    p_s = masked_softmax(jnp.einsum('bgrqd,bgqsd->bgrqs', q, kg) * scale,
                         (k_pos <= q_pos[None, None, :, None])[:, :, None])
    o_s = jnp.einsum('bgrqs,bgqsd->bgrqd', p_s, vg)
    wmask = ((w_pos[None, :] <= q_pos[:, None]) & (w_pos[None, :] >= q_pos[:, None] - WINDOW)
             & (w_pos[None, :] >= 0))
    p_w = masked_softmax(jnp.einsum('bgrqd,bgkd->bgrqk', q, kw) * scale, wmask)
    o_w = jnp.einsum('bgrqk,bgkd->bgrqd', p_w, vw)
    return gates[..., 0:1] * o_c + gates[..., 1:2] * o_s + gates[..., 2:3] * o_w


def nsa_mixer(q, gates, q_pos0, cmp_rows, slc_rows, win_rows, win_pos0, cmp_pe, cmp_w1, cmp_b1, cmp_w2):
    B, Tq, _, d = q.shape
    G, R = NSA_KV_HEADS, NSA_GROUP
    bg = lambda a: jnp.swapaxes(a, 1, 2)
    kc, c_end = nsa_compress(bg(cmp_rows[:, :, 0]), cmp_pe[0], cmp_w1[0], cmp_b1[0], cmp_w2[0])
    vc, _ = nsa_compress(bg(cmp_rows[:, :, 1]), cmp_pe[1], cmp_w1[1], cmp_b1[1], cmp_w2[1])
    ks = to_sel_blocks(bg(slc_rows[:, :, 0]))
    vs = to_sel_blocks(bg(slc_rows[:, :, 1]))
    kw = bg(win_rows[:, :, 0])
    vw = bg(win_rows[:, :, 1])
    qg = q.reshape(B, Tq, G, R, d).transpose(0, 2, 3, 1, 4)
    gg = gates.reshape(B, Tq, G, R, 3).transpose(0, 2, 3, 1, 4)
    qb = Q_BLOCK if Tq % Q_BLOCK == 0 else Tq
    nb = Tq // qb
    if nb == 1:
        w_pos = win_pos0 + jnp.arange(kw.shape[2])
        o = nsa_attend(qg, gg, q_pos0 + jnp.arange(Tq), kc, vc, c_end, ks, vs, kw, vw, w_pos)
    else:
        off = q_pos0 - WINDOW - win_pos0
        span = WINDOW + qb

        def block(args):
            q_blk, g_blk, j = args
            start = j * qb
            kw_b = lax.dynamic_slice_in_dim(kw, start + off, span, axis=2)
            vw_b = lax.dynamic_slice_in_dim(vw, start + off, span, axis=2)
            w_pos = q_pos0 + start - WINDOW + jnp.arange(span)
            return nsa_attend(q_blk, g_blk, q_pos0 + start + jnp.arange(qb), kc, vc, c_end,
                              ks, vs, kw_b, vw_b, w_pos)

        qs = qg.reshape(B, G, R, nb, qb, d).transpose(3, 0, 1, 2, 4, 5)
        gs = gg.reshape(B, G, R, nb, qb, 3).transpose(3, 0, 1, 2, 4, 5)
        o = lax.map(block, (qs, gs, jnp.arange(nb)))
        o = o.transpose(1, 2, 3, 0, 4, 5).reshape(B, G, R, Tq, d)
    return o.transpose(0, 3, 1, 2, 4).reshape(B, Tq, NSA_HEADS * d)


def gather_pages(pool, page_table):
    g = pool[page_table]
    return g.reshape((g.shape[0], g.shape[1] * g.shape[2]) + g.shape[3:])


def memory_kv(mem, g_mem, w_k, w_v):
    mn = rmsnorm(mem, g_mem)
    B, M = mem.shape[:2]
    k = (mn @ w_k).reshape(B, M, MEM_HEADS, MEM_HEAD_DIM)
    v = (mn @ w_v).reshape(B, M, MEM_HEADS, MEM_HEAD_DIM)
    return jnp.stack([k, v], axis=2)


def cross_attend(xn, mem_kv, w_q, w_o):
    B, T = xn.shape[:2]
    q = (xn @ w_q).reshape(B, T, MEM_HEADS, MEM_HEAD_DIM)
    s = jnp.einsum('bthd,bmhd->bhtm', q, mem_kv[:, :, 0]) * MEM_HEAD_DIM ** -0.5
    p = jax.nn.softmax(s.astype(f32), axis=-1)
    o = jnp.einsum('bhtm,bmhd->bthd', p, mem_kv[:, :, 1].astype(f32)).reshape(B, T, -1)
    return (o.astype(xn.dtype) @ w_o).astype(xn.dtype)


def peer_ffn(xn, wq, sub_keys, exp_u, exp_v):
    B, T, D = xn.shape
    N = B * T
    blk = PEER_TOKEN_BLOCK if N % PEER_TOKEN_BLOCK == 0 else N

    def one(xb):
        n = xb.shape[0]
        q = (xb @ wq).reshape(n, PEER_HEADS, 2, PEER_QUERY_DIM // 2)
        s = jnp.einsum('nhpd,hpkd->nhpk', q, sub_keys)
        s_top, i_top = lax.top_k(s, PEER_TOPK)
        cand_s = (s_top[:, :, 0, :, None] + s_top[:, :, 1, None, :]).reshape(n, PEER_HEADS, -1)
        cand_i = (i_top[:, :, 0, :, None] * PEER_N_KEYS + i_top[:, :, 1, None, :]).reshape(n, PEER_HEADS, -1)
        top_s, pos = lax.top_k(cand_s, PEER_TOPK)
        e = jnp.take_along_axis(cand_i, pos, axis=-1)
        g = jax.nn.softmax(top_s.astype(f32), axis=-1)
        act = jax.nn.gelu(jnp.einsum('nd,nhkd->nhk', xb, exp_u[e]).astype(f32))
        return jnp.einsum('nhk,nhkd->nd', (g * act).astype(xb.dtype), exp_v[e])

    out = lax.map(one, xn.reshape(N // blk, blk, D))
    return out.reshape(B, T, D).astype(xn.dtype)


def setup_inputs(seed: int = 0) -> dict:
    key = jax.random.key(seed)
    ks = jax.random.split(key, 40)
    n_pages = PAST_LEN // PAGE_SIZE
    n_pool = (DEC_BATCH * n_pages * 5) // 4
    win_len = min(WINDOW, PAST_LEN)
    kv_row = (2, NSA_KV_HEADS, NSA_HEAD_DIM)
    nrm = lambda k, shape, s: jax.random.normal(k, shape, f32) * s
    gain = lambda k, shape: 1.0 + 0.01 * jax.random.normal(k, shape, f32)
    page_table = jax.random.permutation(ks[4], n_pool)[:DEC_BATCH * n_pages]
    page_table = page_table.reshape(DEC_BATCH, n_pages).astype(jnp.int32)
    return {
        'x_prompt': nrm(ks[0], (BATCH, SEQ, D_MODEL), 1.0),
        'x_sample': nrm(ks[1], (DEC_BATCH, DEC_SEQ, D_MODEL), 1.0),
        'cache_cmp_kv': nrm(ks[2], (DEPTH, n_pool, PAGE_SIZE) + kv_row, 1.0),
        'cache_slc_kv': nrm(ks[3], (DEPTH, n_pool, PAGE_SIZE) + kv_row, 1.0),
        'cache_win_kv': nrm(ks[5], (DEPTH, DEC_BATCH, win_len) + kv_row, 1.0),
        'state_mlstm_c': nrm(ks[6], (DEPTH, DEC_BATCH, ML_HEADS, ML_HEAD_DIM, ML_HEAD_DIM), 0.1),
        'state_mlstm_n': nrm(ks[7], (DEPTH, DEC_BATCH, ML_HEADS, ML_HEAD_DIM), 0.3),
        'state_mlstm_m': nrm(ks[8], (DEPTH, DEC_BATCH, ML_HEADS), 1.0),
        'cache_mem_kv': nrm(ks[9], (DEPTH, DEC_BATCH, MEM_LEN, 2, MEM_HEADS, MEM_HEAD_DIM), 1.0),
        'page_table': page_table,
        'mem_prompt': nrm(ks[10], (BATCH, MEM_LEN, D_MODEL), 1.0),
        'g_mix': gain(ks[11], (DEPTH, D_MODEL)),
        'w_in': nrm(ks[12], (DEPTH, D_MODEL, IN_COLS), D_MODEL ** -0.5),
        'b_in': nrm(ks[13], (DEPTH, IN_COLS), 0.01),
        'b_forget': FORGET_BIAS + 0.5 * jax.random.normal(ks[14], (DEPTH, ML_HEADS), f32),
        'ml_head_gain': gain(ks[15], (DEPTH, ML_HEADS, ML_HEAD_DIM)),
        'cmp_pe': nrm(ks[16], (DEPTH, 2, CMP_BLOCK, NSA_HEAD_DIM), 0.02),
        'cmp_w1': nrm(ks[17], (DEPTH, 2, CMP_BLOCK * NSA_HEAD_DIM, NSA_HEAD_DIM), (CMP_BLOCK * NSA_HEAD_DIM) ** -0.5),
        'cmp_b1': nrm(ks[18], (DEPTH, 2, NSA_HEAD_DIM), 0.01),
        'cmp_w2': nrm(ks[19], (DEPTH, 2, NSA_HEAD_DIM, NSA_HEAD_DIM), NSA_HEAD_DIM ** -0.5),
        'w_out': nrm(ks[20], (DEPTH, ML_WIDTH + NSA_WIDTH, D_MODEL), (ML_WIDTH + NSA_WIDTH) ** -0.5),
        'g_xattn': gain(ks[21], (DEPTH, D_MODEL)),
        'g_mem': gain(ks[22], (DEPTH, D_MODEL)),
        'w_xq': nrm(ks[23], (DEPTH, D_MODEL, MEM_HEADS * MEM_HEAD_DIM), D_MODEL ** -0.5),
        'w_xk': nrm(ks[24], (DEPTH, D_MODEL, MEM_HEADS * MEM_HEAD_DIM), D_MODEL ** -0.5),
        'w_xv': nrm(ks[25], (DEPTH, D_MODEL, MEM_HEADS * MEM_HEAD_DIM), D_MODEL ** -0.5),
        'w_xo': nrm(ks[26], (DEPTH, MEM_HEADS * MEM_HEAD_DIM, D_MODEL), (MEM_HEADS * MEM_HEAD_DIM) ** -0.5),
        'g_ffn': gain(ks[27], (DEPTH, D_MODEL)),
        'peer_wq': nrm(ks[28], (DEPTH, D_MODEL, PEER_HEADS * PEER_QUERY_DIM), D_MODEL ** -0.5),
        'peer_sub_keys': nrm(ks[29], (DEPTH, PEER_HEADS, 2, PEER_N_KEYS, PEER_QUERY_DIM // 2), (PEER_QUERY_DIM // 2) ** -0.5),
        'peer_u': nrm(ks[30], (DEPTH, PEER_N_EXPERTS, D_MODEL), D_MODEL ** -0.5),
        'peer_v': nrm(ks[31], (DEPTH, PEER_N_EXPERTS, D_MODEL), 0.3),
        'g_final': gain(ks[32], (D_MODEL,)),
    }


def reference(x_prompt, x_sample, cache_cmp_kv, cache_slc_kv, cache_win_kv, state_mlstm_c, state_mlstm_n,
              state_mlstm_m, cache_mem_kv, page_table, mem_prompt, g_mix, w_in, b_in, b_forget, ml_head_gain,
              cmp_pe, cmp_w1, cmp_b1, cmp_w2, w_out, g_xattn, g_mem, w_xq, w_xk, w_xv, w_xo, g_ffn, peer_wq,
              peer_sub_keys, peer_u, peer_v, g_final):
    B, T = x_prompt.shape[:2]
    S = x_sample.shape[1]
    past_len = page_table.shape[1] * PAGE_SIZE
    xp, xs = x_prompt, x_sample
    p_cmp, p_slc, p_win, p_c, p_n, p_m, p_mem = [], [], [], [], [], [], []
    s_cmp, s_slc, s_win, s_c, s_n, s_m = [], [], [], [], [], []
    for l in range(DEPTH):
        xn = rmsnorm(xp, g_mix[l])
        mq, mk, mv, mo, mi, mf, nq, nkv, ng = mixer_projections(xn, w_in[l], b_in[l], b_forget[l])
        h_ml, c_new, n_new, m_new = mlstm_mixer(
            mq, mk, mv, mi, mf, mo, ml_head_gain[l],
            jnp.zeros((B, ML_HEADS, ML_HEAD_DIM, ML_HEAD_DIM), f32),
            jnp.zeros((B, ML_HEADS, ML_HEAD_DIM), f32),
            jnp.full((B, ML_HEADS), -jnp.inf, f32))
        cmp_rows, slc_rows, win_rows = nkv[:, :, 0:2], nkv[:, :, 2:4], nkv[:, :, 4:6]
        win_pad = jnp.pad(win_rows, ((0, 0), (WINDOW, 0), (0, 0), (0, 0), (0, 0)))
        h_nsa = nsa_mixer(nq, ng, 0, cmp_rows, slc_rows, win_pad, -WINDOW,
                          cmp_pe[l], cmp_w1[l], cmp_b1[l], cmp_w2[l])
        xp = xp + (jnp.concatenate([h_ml, h_nsa.astype(h_ml.dtype)], -1) @ w_out[l]).astype(xp.dtype)
        mem_kv = memory_kv(mem_prompt, g_mem[l], w_xk[l], w_xv[l])
        xp = xp + cross_attend(rmsnorm(xp, g_xattn[l]), mem_kv, w_xq[l], w_xo[l])
        xp = xp + peer_ffn(rmsnorm(xp, g_ffn[l]), peer_wq[l], peer_sub_keys[l], peer_u[l], peer_v[l])
        p_cmp.append(cmp_rows)
        p_slc.append(slc_rows)
        p_win.append(win_rows[:, T - min(WINDOW, T):])
        p_c.append(c_new)
        p_n.append(n_new)
        p_m.append(m_new)
        p_mem.append(mem_kv)

        xn = rmsnorm(xs, g_mix[l])
        mq, mk, mv, mo, mi, mf, nq, nkv, ng = mixer_projections(xn, w_in[l], b_in[l], b_forget[l])
        h_ml, c_new, n_new, m_new = mlstm_mixer(mq, mk, mv, mi, mf, mo, ml_head_gain[l],
                                                state_mlstm_c[l], state_mlstm_n[l], state_mlstm_m[l])
        cmp_rows, slc_rows, win_rows = nkv[:, :, 0:2], nkv[:, :, 2:4], nkv[:, :, 4:6]
        cmp_full = jnp.concatenate([gather_pages(cache_cmp_kv[l], page_table), cmp_rows.astype(cache_cmp_kv.dtype)], axis=1)
        slc_full = jnp.concatenate([gather_pages(cache_slc_kv[l], page_table), slc_rows.astype(cache_slc_kv.dtype)], axis=1)
        win_buf = cache_win_kv[l]
        win_ext = jnp.concatenate([win_buf, win_rows.astype(win_buf.dtype)], axis=1)
        h_nsa = nsa_mixer(nq, ng, past_len, cmp_full, slc_full, win_ext, past_len - win_buf.shape[1],
                          cmp_pe[l], cmp_w1[l], cmp_b1[l], cmp_w2[l])
        xs = xs + (jnp.concatenate([h_ml, h_nsa.astype(h_ml.dtype)], -1) @ w_out[l]).astype(xs.dtype)
        xs = xs + cross_attend(rmsnorm(xs, g_xattn[l]), cache_mem_kv[l], w_xq[l], w_xo[l])
        xs = xs + peer_ffn(rmsnorm(xs, g_ffn[l]), peer_wq[l], peer_sub_keys[l], peer_u[l], peer_v[l])
        w_keep = min(WINDOW, past_len + S)
        s_cmp.append(cmp_rows)
        s_slc.append(slc_rows)
        s_win.append(win_ext[:, win_ext.shape[1] - w_keep:])
        s_c.append(c_new)
        s_n.append(n_new)
        s_m.append(m_new)

    y_prompt = rmsnorm(xp, g_final)
    y_sample = rmsnorm(xs, g_final)
    return (y_prompt, y_sample,
            jnp.stack(p_cmp), jnp.stack(p_slc), jnp.stack(p_win), jnp.stack(p_c), jnp.stack(p_n),
            jnp.stack(p_m), jnp.stack(p_mem),
            jnp.stack(s_cmp), jnp.stack(s_slc), jnp.stack(s_win), jnp.stack(s_c), jnp.stack(s_n),
            jnp.stack(s_m))
```

```python
import functools

import jax
import jax.numpy as jnp
from jax import lax
import numpy as np
from jax.experimental import pallas as pl
from jax.experimental.pallas import tpu as pltpu

D_MODEL = 1024
DEPTH = 1
PAGE_SIZE = 128

ML_WIDTH = D_MODEL // 2
ML_HEADS = 4
ML_HEAD_DIM = ML_WIDTH // ML_HEADS
ML_CHUNK = 128
NSA_WIDTH = D_MODEL - ML_WIDTH
NSA_HEADS = 8
NSA_HEAD_DIM = NSA_WIDTH // NSA_HEADS
NSA_KV_HEADS = 2
NSA_GROUP = NSA_HEADS // NSA_KV_HEADS
CMP_BLOCK = 32
CMP_STRIDE = 16
SEL_BLOCK = 64
N_SEL = 16
WINDOW = 512
Q_BLOCK = 128
FORCED_SCORE = 1.0e4
INVALID_SCORE = -1.0
MEM_LEN = 256
MEM_HEADS = 4
MEM_HEAD_DIM = D_MODEL // MEM_HEADS
PEER_HEADS = 8
PEER_N_KEYS = 128
PEER_N_EXPERTS = PEER_N_KEYS * PEER_N_KEYS
PEER_TOPK = 16
PEER_QUERY_DIM = 256
PEER_TOKEN_BLOCK = 128
EPS = 1e-6
TINY = 1e-30
IN_SPLITS = (ML_WIDTH, ML_WIDTH, ML_WIDTH, ML_WIDTH, ML_HEADS, ML_HEADS,
             NSA_WIDTH, 6 * NSA_KV_HEADS * NSA_HEAD_DIM, 3 * NSA_HEADS)
IN_COLS = sum(IN_SPLITS)
IN_OFFSETS = tuple(int(o) for o in np.cumsum(IN_SPLITS)[:-1])

f32 = jnp.float32


def rmsnorm(x, g):
    xf = x.astype(f32)
    return (xf * lax.rsqrt(jnp.mean(xf * xf, -1, keepdims=True) + EPS) * g).astype(x.dtype)


def _rmsnorm_body(x_ref, g_ref, o_ref):
    x = x_ref[...]
    o_ref[...] = x * lax.rsqrt(jnp.mean(x * x, -1, keepdims=True) + EPS) * g_ref[...]


def rmsnorm_pallas(x, g):
    shape = x.shape
    x2 = x.reshape(-1, shape[-1])
    n, d = x2.shape
    tm = min(n, 512)
    out = pl.pallas_call(
        _rmsnorm_body,
        grid=(n // tm,),
        in_specs=[pl.BlockSpec((tm, d), lambda i: (i, 0)), pl.BlockSpec((1, d), lambda i: (0, 0))],
        out_specs=pl.BlockSpec((tm, d), lambda i: (i, 0)),
        out_shape=jax.ShapeDtypeStruct((n, d), f32),
    )(x2, g.reshape(1, d))
    return out.reshape(shape)


def masked_softmax(s, mask):
    s = jnp.where(mask, s.astype(f32), -jnp.inf)
    mx = jnp.max(s, -1, keepdims=True)
    mx = jnp.where(jnp.isfinite(mx), mx, 0.0)
    e = jnp.exp(s - mx)
    return e / jnp.maximum(e.sum(-1, keepdims=True), TINY)


def mixer_projections(xn, w_in, b_in, b_forget):
    B, T = xn.shape[:2]
    p = xn @ w_in + b_in
    ml_q, ml_k, ml_v, ml_o, ml_i, ml_f, nq, nkv, ng = jnp.split(p, IN_OFFSETS, axis=-1)
    hd = (B, T, ML_HEADS, ML_HEAD_DIM)
    return (ml_q.reshape(hd), ml_k.reshape(hd), ml_v.reshape(hd), ml_o, ml_i, ml_f + b_forget,
            nq.reshape(B, T, NSA_HEADS, NSA_HEAD_DIM),
            nkv.reshape(B, T, 6, NSA_KV_HEADS, NSA_HEAD_DIM),
            jax.nn.sigmoid(ng).reshape(B, T, NSA_HEADS, 3))


def mlstm_chunk(carry, inp):
    c, n, m = carry
    q, k, v, ig, lf = inp
    L = q.shape[2]
    b = jnp.cumsum(lf, axis=-1)
    causal = jnp.tril(jnp.ones((L, L), bool))
    log_d = jnp.where(causal, b[..., :, None] - b[..., None, :] + ig[..., None, :], -jnp.inf)
    inter = b + m[..., None]
    m_t = jnp.maximum(inter, log_d.max(-1))
    d_mat = jnp.exp(log_d - m_t[..., None])
    a = jnp.exp(inter - m_t)
    qk = jnp.einsum('bhtd,bhsd->bhts', q, k) * d_mat
    num = a[..., None] * jnp.einsum('bhtd,bhde->bhte', q, c) + jnp.einsum('bhts,bhse->bhte', qk, v)
    den = a * jnp.einsum('bhtd,bhd->bht', q, n) + qk.sum(-1)
    h = num / jnp.maximum(jnp.abs(den), jnp.exp(-m_t))[..., None]
    m_new = m_t[..., -1]
    w = jnp.exp(b[..., -1:] - b + ig - m_new[..., None])
    decay = jnp.exp(b[..., -1] + m - m_new)
    c_new = decay[..., None, None] * c + jnp.einsum('bhs,bhsd,bhse->bhde', w, k, v)
    n_new = decay[..., None] * n + jnp.einsum('bhs,bhsd->bhd', w, k)
    return (c_new, n_new, m_new), h


def mlstm_mixer(q, k, v, i_pre, f_pre, o_pre, head_gain, c0, n0, m0):
    B, T, H, d = q.shape
    L = ML_CHUNK if T % ML_CHUNK == 0 else T
    nc = T // L

    def chunks(a):
        a = a.astype(f32).reshape((B, nc, L) + a.shape[2:])
        return jnp.transpose(a, (1, 0, 3, 2) + tuple(range(4, a.ndim)))

    xs = (chunks(q), chunks(k * d ** -0.5), chunks(v), chunks(i_pre),
          chunks(jax.nn.log_sigmoid(f_pre.astype(f32))))
    (c, n, m), h = lax.scan(mlstm_chunk, (c0.astype(f32), n0.astype(f32), m0.astype(f32)), xs)
    h = jnp.transpose(h, (1, 0, 3, 2, 4)).reshape(B, T, H, d)
    h = jax.nn.sigmoid(o_pre.astype(f32)).reshape(B, T, H, d) * h
    h = h * lax.rsqrt(jnp.mean(h * h, -1, keepdims=True) + EPS) * head_gain
    return h.reshape(B, T, H * d).astype(q.dtype), c, n, m


def nsa_compress(rows, pe, w1, b1, w2):
    B, G, T, d = rows.shape
    n_seg = T // CMP_STRIDE
    seg = rows[:, :, :n_seg * CMP_STRIDE].reshape(B, G, n_seg, CMP_STRIDE * d)
    half = CMP_STRIDE * d
    pre = seg[:, :, :-1] @ w1[:half] + seg[:, :, 1:] @ w1[half:] + (pe.reshape(-1) @ w1 + b1)
    ends = jnp.arange(n_seg - 1) * CMP_STRIDE + (CMP_BLOCK - 1)
    return jax.nn.gelu(pre) @ w2, ends


def to_sel_blocks(rows):
    B, G, T, d = rows.shape
    nsb = -(-T // SEL_BLOCK)
    rows = jnp.pad(rows, ((0, 0), (0, 0), (0, nsb * SEL_BLOCK - T), (0, 0)))
    return rows.reshape(B, G, nsb, SEL_BLOCK, d)


def cmp_to_sel(imp, nsb):
    r = SEL_BLOCK // CMP_STRIDE
    nc = imp.shape[-1]
    lead = imp.shape[:-1]
    tot = r * nsb
    padw = [(0, 0)] * len(lead)
    first = jnp.pad(imp, padw + [(0, tot - nc)]).reshape(lead + (nsb, r)).sum(-1)
    second = jnp.pad(imp, padw + [(1, tot - nc - 1)]).reshape(lead + (nsb, r)).sum(-1)
    return 0.5 * (first + second)


def nsa_attend(q, gates, q_pos, kc, vc, c_end, ks, vs, kw, vw, w_pos):
    B, G, R, Q, d = q.shape
    scale = d ** -0.5
    p_c = masked_softmax(jnp.einsum('bgrqd,bgnd->bgrqn', q, kc) * scale, c_end[None, :] <= q_pos[:, None])
    o_c = jnp.einsum('bgrqn,bgnd->bgrqd', p_c, vc)
    nsb = ks.shape[2]
    imp = cmp_to_sel(p_c.sum(2), nsb)
    blk = jnp.arange(nsb)
    valid = blk[None, :] * SEL_BLOCK <= q_pos[:, None]
    forced = (blk[None, :] == 0) | (blk[None, :] == q_pos[:, None] // SEL_BLOCK)
    score = jnp.where(forced, FORCED_SCORE, jnp.where(valid, imp, INVALID_SCORE))
    _, idx = lax.top_k(score, min(N_SEL, nsb))
    bi = jnp.arange(B)[:, None, None, None]
    gi = jnp.arange(G)[None, :, None, None]
    kg = ks[bi, gi, idx].reshape(B, G, Q, -1, d)
    vg = vs[bi, gi, idx].reshape(B, G, Q, -1, d)
    k_pos = (idx[..., None] * SEL_BLOCK + jnp.arange(SEL_BLOCK)).reshape(B, G, Q, -1)
    p_s = masked_softmax(jnp.einsum('bgrqd,bgqsd->bgrqs', q, kg) * scale,
                         (k_pos <= q_pos[None, None, :, None])[:, :, None])
    o_s = jnp.einsum('bgrqs,bgqsd->bgrqd', p_s, vg)
    wmask = ((w_pos[None, :] <= q_pos[:, None]) & (w_pos[None, :] >= q_pos[:, None] - WINDOW)
             & (w_pos[None, :] >= 0))
    p_w = masked_softmax(jnp.einsum('bgrqd,bgkd->bgrqk', q, kw) * scale, wmask)
    o_w = jnp.einsum('bgrqk,bgkd->bgrqd', p_w, vw)
    return gates[..., 0:1] * o_c + gates[..., 1:2] * o_s + gates[..., 2:3] * o_w


def nsa_mixer(q, gates, q_pos0, cmp_rows, slc_rows, win_rows, win_pos0, cmp_pe, cmp_w1, cmp_b1, cmp_w2):
    B, Tq, _, d = q.shape
    G, R = NSA_KV_HEADS, NSA_GROUP
    bg = lambda a: jnp.swapaxes(a, 1, 2)
    kc, c_end = nsa_compress(bg(cmp_rows[:, :, 0]), cmp_pe[0], cmp_w1[0], cmp_b1[0], cmp_w2[0])
    vc, _ = nsa_compress(bg(cmp_rows[:, :, 1]), cmp_pe[1], cmp_w1[1], cmp_b1[1], cmp_w2[1])
    ks = to_sel_blocks(bg(slc_rows[:, :, 0]))
    vs = to_sel_blocks(bg(slc_rows[:, :, 1]))
    kw = bg(win_rows[:, :, 0])
    vw = bg(win_rows[:, :, 1])
    qg = q.reshape(B, Tq, G, R, d).transpose(0, 2, 3, 1, 4)
    gg = gates.reshape(B, Tq, G, R, 3).transpose(0, 2, 3, 1, 4)
    qb = Q_BLOCK if Tq % Q_BLOCK == 0 else Tq
    nb = Tq // qb
    if nb == 1:
        w_pos = win_pos0 + jnp.arange(kw.shape[2])
        o = nsa_attend(qg, gg, q_pos0 + jnp.arange(Tq), kc, vc, c_end, ks, vs, kw, vw, w_pos)
    else:
        off = q_pos0 - WINDOW - win_pos0
        span = WINDOW + qb

        def block(args):
            q_blk, g_blk, j = args
            start = j * qb
            kw_b = lax.dynamic_slice_in_dim(kw, start + off, span, axis=2)
            vw_b = lax.dynamic_slice_in_dim(vw, start + off, span, axis=2)
            w_pos = q_pos0 + start - WINDOW + jnp.arange(span)
            return nsa_attend(q_blk, g_blk, q_pos0 + start + jnp.arange(qb), kc, vc, c_end,
                              ks, vs, kw_b, vw_b, w_pos)

        qs = qg.reshape(B, G, R, nb, qb, d).transpose(3, 0, 1, 2, 4, 5)
        gs = gg.reshape(B, G, R, nb, qb, 3).transpose(3, 0, 1, 2, 4, 5)
        o = lax.map(block, (qs, gs, jnp.arange(nb)))
        o = o.transpose(1, 2, 3, 0, 4, 5).reshape(B, G, R, Tq, d)
    return o.transpose(0, 3, 1, 2, 4).reshape(B, Tq, NSA_HEADS * d)


def gather_pages(pool, page_table):
    g = pool[page_table]
    return g.reshape((g.shape[0], g.shape[1] * g.shape[2]) + g.shape[3:])


def memory_kv(mem, g_mem, w_k, w_v):
    mn = rmsnorm(mem, g_mem)
    B, M = mem.shape[:2]
    k = (mn @ w_k).reshape(B, M, MEM_HEADS, MEM_HEAD_DIM)
    v = (mn @ w_v).reshape(B, M, MEM_HEADS, MEM_HEAD_DIM)
    return jnp.stack([k, v], axis=2)


def cross_attend(xn, mem_kv, w_q, w_o):
    B, T = xn.shape[:2]
    q = (xn @ w_q).reshape(B, T, MEM_HEADS, MEM_HEAD_DIM)
    s = jnp.einsum('bthd,bmhd->bhtm', q, mem_kv[:, :, 0]) * MEM_HEAD_DIM ** -0.5
    p = jax.nn.softmax(s.astype(f32), axis=-1)
    o = jnp.einsum('bhtm,bmhd->bthd', p, mem_kv[:, :, 1].astype(f32)).reshape(B, T, -1)
    return (o.astype(xn.dtype) @ w_o).astype(xn.dtype)


def peer_ffn(xn, wq, sub_keys, exp_u, exp_v):
    B, T, D = xn.shape
    N = B * T
    blk = PEER_TOKEN_BLOCK if N % PEER_TOKEN_BLOCK == 0 else N

    def one(xb):
        n = xb.shape[0]
        q = (xb @ wq).reshape(n, PEER_HEADS, 2, PEER_QUERY_DIM // 2)
        s = jnp.einsum('nhpd,hpkd->nhpk', q, sub_keys)
        s_top, i_top = lax.top_k(s, PEER_TOPK)
        cand_s = (s_top[:, :, 0, :, None] + s_top[:, :, 1, None, :]).reshape(n, PEER_HEADS, -1)
        cand_i = (i_top[:, :, 0, :, None] * PEER_N_KEYS + i_top[:, :, 1, None, :]).reshape(n, PEER_HEADS, -1)
        top_s, pos = lax.top_k(cand_s, PEER_TOPK)
        e = jnp.take_along_axis(cand_i, pos, axis=-1)
        g = jax.nn.softmax(top_s.astype(f32), axis=-1)
        act = jax.nn.gelu(jnp.einsum('nd,nhkd->nhk', xb, exp_u[e]).astype(f32))
        return jnp.einsum('nhk,nhkd->nd', (g * act).astype(xb.dtype), exp_v[e])

    out = lax.map(one, xn.reshape(N // blk, blk, D))
    return out.reshape(B, T, D).astype(xn.dtype)


def kernel(x_prompt, x_sample, cache_cmp_kv, cache_slc_kv, cache_win_kv, state_mlstm_c, state_mlstm_n,
           state_mlstm_m, cache_mem_kv, page_table, mem_prompt, g_mix, w_in, b_in, b_forget, ml_head_gain,
           cmp_pe, cmp_w1, cmp_b1, cmp_w2, w_out, g_xattn, g_mem, w_xq, w_xk, w_xv, w_xo, g_ffn, peer_wq,
           peer_sub_keys, peer_u, peer_v, g_final):
    B, T = x_prompt.shape[:2]
    S = x_sample.shape[1]
    past_len = page_table.shape[1] * PAGE_SIZE
    xp, xs = x_prompt, x_sample
    l = 0
    xn = rmsnorm(xp, g_mix[l])
    mq, mk, mv, mo, mi, mf, nq, nkv, ng = mixer_projections(xn, w_in[l], b_in[l], b_forget[l])
    h_ml, p_c, p_n, p_m = mlstm_mixer(
        mq, mk, mv, mi, mf, mo, ml_head_gain[l],
        jnp.zeros((B, ML_HEADS, ML_HEAD_DIM, ML_HEAD_DIM), f32),
        jnp.zeros((B, ML_HEADS, ML_HEAD_DIM), f32),
        jnp.full((B, ML_HEADS), -jnp.inf, f32))
    p_cmp, p_slc, win_rows = nkv[:, :, 0:2], nkv[:, :, 2:4], nkv[:, :, 4:6]
    win_pad = jnp.pad(win_rows, ((0, 0), (WINDOW, 0), (0, 0), (0, 0), (0, 0)))
    h_nsa = nsa_mixer(nq, ng, 0, p_cmp, p_slc, win_pad, -WINDOW,
                      cmp_pe[l], cmp_w1[l], cmp_b1[l], cmp_w2[l])
    xp = xp + (jnp.concatenate([h_ml, h_nsa.astype(h_ml.dtype)], -1) @ w_out[l]).astype(xp.dtype)
    p_mem = memory_kv(mem_prompt, g_mem[l], w_xk[l], w_xv[l])
    xp = xp + cross_attend(rmsnorm(xp, g_xattn[l]), p_mem, w_xq[l], w_xo[l])
    xp = xp + peer_ffn(rmsnorm(xp, g_ffn[l]), peer_wq[l], peer_sub_keys[l], peer_u[l], peer_v[l])
    p_win = win_rows[:, T - min(WINDOW, T):]

    xn = rmsnorm(xs, g_mix[l])
    mq, mk, mv, mo, mi, mf, nq, nkv, ng = mixer_projections(xn, w_in[l], b_in[l], b_forget[l])
    h_ml, s_c, s_n, s_m = mlstm_mixer(mq, mk, mv, mi, mf, mo, ml_head_gain[l],
                                      state_mlstm_c[l], state_mlstm_n[l], state_mlstm_m[l])
    s_cmp, s_slc, win_rows = nkv[:, :, 0:2], nkv[:, :, 2:4], nkv[:, :, 4:6]
    cmp_full = jnp.concatenate([gather_pages(cache_cmp_kv[l], page_table), s_cmp], axis=1)
    slc_full = jnp.concatenate([gather_pages(cache_slc_kv[l], page_table), s_slc], axis=1)
    win_buf = cache_win_kv[l]
    win_ext = jnp.concatenate([win_buf, win_rows], axis=1)
    h_nsa = nsa_mixer(nq, ng, past_len, cmp_full, slc_full, win_ext, past_len - win_buf.shape[1],
                      cmp_pe[l], cmp_w1[l], cmp_b1[l], cmp_w2[l])
    xs = xs + (jnp.concatenate([h_ml, h_nsa.astype(h_ml.dtype)], -1) @ w_out[l]).astype(xs.dtype)
    xs = xs + cross_attend(rmsnorm(xs, g_xattn[l]), cache_mem_kv[l], w_xq[l], w_xo[l])
    xs = xs + peer_ffn(rmsnorm(xs, g_ffn[l]), peer_wq[l], peer_sub_keys[l], peer_u[l], peer_v[l])
    w_keep = min(WINDOW, past_len + S)
    s_win = win_ext[:, win_ext.shape[1] - w_keep:]

    y_prompt = rmsnorm_pallas(xp, g_final)
    y_sample = rmsnorm_pallas(xs, g_final)
    st = lambda a: a[None]
    return (y_prompt, y_sample,
            st(p_cmp), st(p_slc), st(p_win), st(p_c), st(p_n), st(p_m), st(p_mem),
            st(s_cmp), st(s_slc), st(s_win), st(s_c), st(s_n), st(s_m))
```

```python
import functools

import jax
import jax.numpy as jnp
from jax import lax
import numpy as np
from jax.experimental import pallas as pl
from jax.experimental.pallas import tpu as pltpu

D_MODEL = 1024
DEPTH = 1
PAGE_SIZE = 128

ML_WIDTH = D_MODEL // 2
ML_HEADS = 4
ML_HEAD_DIM = ML_WIDTH // ML_HEADS
ML_CHUNK = 128
NSA_WIDTH = D_MODEL - ML_WIDTH
NSA_HEADS = 8
NSA_HEAD_DIM = NSA_WIDTH // NSA_HEADS
NSA_KV_HEADS = 2
NSA_GROUP = NSA_HEADS // NSA_KV_HEADS
CMP_BLOCK = 32
CMP_STRIDE = 16
SEL_BLOCK = 64
N_SEL = 16
WINDOW = 512
Q_BLOCK = 128
FORCED_SCORE = 1.0e4
INVALID_SCORE = -1.0
MEM_LEN = 256
MEM_HEADS = 4
MEM_HEAD_DIM = D_MODEL // MEM_HEADS
PEER_HEADS = 8
PEER_N_KEYS = 128
PEER_N_EXPERTS = PEER_N_KEYS * PEER_N_KEYS
PEER_TOPK = 16
PEER_QUERY_DIM = 256
PEER_TOKEN_BLOCK = 128
EPS = 1e-6
TINY = 1e-30
IN_SPLITS = (ML_WIDTH, ML_WIDTH, ML_WIDTH, ML_WIDTH, ML_HEADS, ML_HEADS,
             NSA_WIDTH, 6 * NSA_KV_HEADS * NSA_HEAD_DIM, 3 * NSA_HEADS)
IN_COLS = sum(IN_SPLITS)
IN_OFFSETS = tuple(int(o) for o in np.cumsum(IN_SPLITS)[:-1])

f32 = jnp.float32


def rmsnorm(x, g):
    xf = x.astype(f32)
    return (xf * lax.rsqrt(jnp.mean(xf * xf, -1, keepdims=True) + EPS) * g).astype(x.dtype)


def _rmsnorm_body(x_ref, g_ref, o_ref):
    x = x_ref[...]
    o_ref[...] = x * lax.rsqrt(jnp.mean(x * x, -1, keepdims=True) + EPS) * g_ref[...]


def rmsnorm_pallas(x, g):
    shape = x.shape
    x2 = x.reshape(-1, shape[-1])
    n, d = x2.shape
    tm = min(n, 512)
    out = pl.pallas_call(
        _rmsnorm_body,
        grid=(n // tm,),
        in_specs=[pl.BlockSpec((tm, d), lambda i: (i, 0)), pl.BlockSpec((1, d), lambda i: (0, 0))],
        out_specs=pl.BlockSpec((tm, d), lambda i: (i, 0)),
        out_shape=jax.ShapeDtypeStruct((n, d), f32),
    )(x2, g.reshape(1, d))
    return out.reshape(shape)


def masked_softmax(s, mask):
    s = jnp.where(mask, s.astype(f32), -jnp.inf)
    mx = jnp.max(s, -1, keepdims=True)
    mx = jnp.where(jnp.isfinite(mx), mx, 0.0)
    e = jnp.exp(s - mx)
    return e / jnp.maximum(e.sum(-1, keepdims=True), TINY)


def mixer_projections(xn, w_in, b_in, b_forget):
    B, T = xn.shape[:2]
    p = xn @ w_in + b_in
    ml_q, ml_k, ml_v, ml_o, ml_i, ml_f, nq, nkv, ng = jnp.split(p, IN_OFFSETS, axis=-1)
    hd = (B, T, ML_HEADS, ML_HEAD_DIM)
    return (ml_q.reshape(hd), ml_k.reshape(hd), ml_v.reshape(hd), ml_o, ml_i, ml_f + b_forget,
            nq.reshape(B, T, NSA_HEADS, NSA_HEAD_DIM),
            nkv.reshape(B, T, 6, NSA_KV_HEADS, NSA_HEAD_DIM),
            jax.nn.sigmoid(ng).reshape(B, T, NSA_HEADS, 3))


def mlstm_chunk(carry, inp):
    c, n, m = carry
    q, k, v, ig, lf = inp
    L = q.shape[2]
    b = jnp.cumsum(lf, axis=-1)
    causal = jnp.tril(jnp.ones((L, L), bool))
    log_d = jnp.where(causal, b[..., :, None] - b[..., None, :] + ig[..., None, :], -jnp.inf)
    inter = b + m[..., None]
    m_t = jnp.maximum(inter, log_d.max(-1))
    d_mat = jnp.exp(log_d - m_t[..., None])
    a = jnp.exp(inter - m_t)
    qk = jnp.einsum('bhtd,bhsd->bhts', q, k) * d_mat
    num = a[..., None] * jnp.einsum('bhtd,bhde->bhte', q, c) + jnp.einsum('bhts,bhse->bhte', qk, v)
    den = a * jnp.einsum('bhtd,bhd->bht', q, n) + qk.sum(-1)
    h = num / jnp.maximum(jnp.abs(den), jnp.exp(-m_t))[..., None]
    m_new = m_t[..., -1]
    w = jnp.exp(b[..., -1:] - b + ig - m_new[..., None])
    decay = jnp.exp(b[..., -1] + m - m_new)
    c_new = decay[..., None, None] * c + jnp.einsum('bhs,bhsd,bhse->bhde', w, k, v)
    n_new = decay[..., None] * n + jnp.einsum('bhs,bhsd->bhd', w, k)
    return (c_new, n_new, m_new), h


def mlstm_mixer(q, k, v, i_pre, f_pre, o_pre, head_gain, c0, n0, m0):
    B, T, H, d = q.shape
    L = ML_CHUNK if T % ML_CHUNK == 0 else T
    nc = T // L

    def chunks(a):
        a = a.astype(f32).reshape((B, nc, L) + a.shape[2:])
        return jnp.transpose(a, (1, 0, 3, 2) + tuple(range(4, a.ndim)))

    xs = (chunks(q), chunks(k * d ** -0.5), chunks(v), chunks(i_pre),
          chunks(jax.nn.log_sigmoid(f_pre.astype(f32))))
    (c, n, m), h = lax.scan(mlstm_chunk, (c0.astype(f32), n0.astype(f32), m0.astype(f32)), xs)
    h = jnp.transpose(h, (1, 0, 3, 2, 4)).reshape(B, T, H, d)
    h = jax.nn.sigmoid(o_pre.astype(f32)).reshape(B, T, H, d) * h
    h = h * lax.rsqrt(jnp.mean(h * h, -1, keepdims=True) + EPS) * head_gain
    return h.reshape(B, T, H * d).astype(q.dtype), c, n, m


def nsa_compress(rows, pe, w1, b1, w2):
    B, G, T, d = rows.shape
    n_seg = T // CMP_STRIDE
    seg = rows[:, :, :n_seg * CMP_STRIDE].reshape(B, G, n_seg, CMP_STRIDE * d)
    half = CMP_STRIDE * d
    pre = seg[:, :, :-1] @ w1[:half] + seg[:, :, 1:] @ w1[half:] + (pe.reshape(-1) @ w1 + b1)
    ends = jnp.arange(n_seg - 1) * CMP_STRIDE + (CMP_BLOCK - 1)
    return jax.nn.gelu(pre) @ w2, ends


def to_sel_blocks(rows):
    B, G, T, d = rows.shape
    nsb = -(-T // SEL_BLOCK)
    rows = jnp.pad(rows, ((0, 0), (0, 0), (0, nsb * SEL_BLOCK - T), (0, 0)))
    return rows.reshape(B, G, nsb, SEL_BLOCK, d)


def cmp_to_sel(imp, nsb):
    r = SEL_BLOCK // CMP_STRIDE
    nc = imp.shape[-1]
    lead = imp.shape[:-1]
    tot = r * nsb
    padw = [(0, 0)] * len(lead)
    first = jnp.pad(imp, padw + [(0, tot - nc)]).reshape(lead + (nsb, r)).sum(-1)
    second = jnp.pad(imp, padw + [(1, tot - nc - 1)]).reshape(lead + (nsb, r)).sum(-1)
    return 0.5 * (first + second)


def nsa_attend(q, gates, q_pos, kc, vc, c_end, ks, vs, kw, vw, w_pos):
    B, G, R, Q, d = q.shape
    scale = d ** -0.5
    p_c = masked_softmax(jnp.einsum('bgrqd,bgnd->bgrqn', q, kc) * scale, c_end[None, :] <= q_pos[:, None])
    o_c = jnp.einsum('bgrqn,bgnd->bgrqd', p_c, vc)
    nsb = ks.shape[2]
    imp = cmp_to_sel(p_c.sum(2), nsb)
    blk = jnp.arange(nsb)
    valid = blk[None, :] * SEL_BLOCK <= q_pos[:, None]
    forced = (blk[None, :] == 0) | (blk[None, :] == q_pos[:, None] // SEL_BLOCK)
    score = jnp.where(forced, FORCED_SCORE, jnp.where(valid, imp, INVALID_SCORE))
    _, idx = lax.top_k(score, min(N_SEL, nsb))
    bi = jnp.arange(B)[:, None, None, None]
    gi = jnp.arange(G)[None, :, None, None]
    kg = ks[bi, gi, idx].reshape(B, G, Q, -1, d)
    vg = vs[bi, gi, idx].reshape(B, G, Q, -1, d)
    k_pos = (idx[..., None] * SEL_BLOCK + jnp.arange(SEL_BLOCK)).reshape(B, G, Q, -1)
    p_s = masked_softmax(jnp.einsum('bgrqd,bgqsd->bgrqs', q, kg) * scale,
                         (k_pos <= q_pos[None, None, :, None])[:, :, None])
    o_s = jnp.einsum('bgrqs,bgqsd->bgrqd', p_s, vg)
    wmask = ((w_pos[None, :] <= q_pos[:, None]) & (w_pos[None, :] >= q_pos[:, None] - WINDOW)
             & (w_pos[None, :] >= 0))
    p_w = masked_softmax(jnp.einsum('bgrqd,bgkd->bgrqk', q, kw) * scale, wmask)
    o_w = jnp.einsum('bgrqk,bgkd->bgrqd', p_w, vw)
    return gates[..., 0:1] * o_c + gates[..., 1:2] * o_s + gates[..., 2:3] * o_w


def nsa_mixer(q, gates, q_pos0, cmp_rows, slc_rows, win_rows, win_pos0, cmp_pe, cmp_w1, cmp_b1, cmp_w2):
    B, Tq, _, d = q.shape
    G, R = NSA_KV_HEADS, NSA_GROUP
    bg = lambda a: jnp.swapaxes(a, 1, 2)
    kc, c_end = nsa_compress(bg(cmp_rows[:, :, 0]), cmp_pe[0], cmp_w1[0], cmp_b1[0], cmp_w2[0])
    vc, _ = nsa_compress(bg(cmp_rows[:, :, 1]), cmp_pe[1], cmp_w1[1], cmp_b1[1], cmp_w2[1])
    ks = to_sel_blocks(bg(slc_rows[:, :, 0]))
    vs = to_sel_blocks(bg(slc_rows[:, :, 1]))
    kw = bg(win_rows[:, :, 0])
    vw = bg(win_rows[:, :, 1])
    qg = q.reshape(B, Tq, G, R, d).transpose(0, 2, 3, 1, 4)
    gg = gates.reshape(B, Tq, G, R, 3).transpose(0, 2, 3, 1, 4)
    qb = Q_BLOCK if Tq % Q_BLOCK == 0 else Tq
    nb = Tq // qb
    if nb == 1:
        w_pos = win_pos0 + jnp.arange(kw.shape[2])
        o = nsa_attend(qg, gg, q_pos0 + jnp.arange(Tq), kc, vc, c_end, ks, vs, kw, vw, w_pos)
    else:
        off = q_pos0 - WINDOW - win_pos0
        span = WINDOW + qb

        def block(args):
            q_blk, g_blk, j = args
            start = j * qb
            kw_b = lax.dynamic_slice_in_dim(kw, start + off, span, axis=2)
            vw_b = lax.dynamic_slice_in_dim(vw, start + off, span, axis=2)
            w_pos = q_pos0 + start - WINDOW + jnp.arange(span)
            return nsa_attend(q_blk, g_blk, q_pos0 + start + jnp.arange(qb), kc, vc, c_end,
                              ks, vs, kw_b, vw_b, w_pos)

        qs = qg.reshape(B, G, R, nb, qb, d).transpose(3, 0, 1, 2, 4, 5)
        gs = gg.reshape(B, G, R, nb, qb, 3).transpose(3, 0, 1, 2, 4, 5)
        o = lax.map(block, (qs, gs, jnp.arange(nb)))
        o = o.transpose(1, 2, 3, 0, 4, 5).reshape(B, G, R, Tq, d)
    return o.transpose(0, 3, 1, 2, 4).reshape(B, Tq, NSA_HEADS * d)


def gather_pages(pool, page_table):
    g = pool[page_table]
    return g.reshape((g.shape[0], g.shape[1] * g.shape[2]) + g.shape[3:])


def memory_kv(mem, g_mem, w_k, w_v):
    mn = rmsnorm(mem, g_mem)
    B, M = mem.shape[:2]
    k = (mn @ w_k).reshape(B, M, MEM_HEADS, MEM_HEAD_DIM)
    v = (mn @ w_v).reshape(B, M, MEM_HEADS, MEM_HEAD_DIM)
    return jnp.stack([k, v], axis=2)


def cross_attend(xn, mem_kv, w_q, w_o):
    B, T = xn.shape[:2]
    q = (xn @ w_q).reshape(B, T, MEM_HEADS, MEM_HEAD_DIM)
    s = jnp.einsum('bthd,bmhd->bhtm', q, mem_kv[:, :, 0]) * MEM_HEAD_DIM ** -0.5
    p = jax.nn.softmax(s.astype(f32), axis=-1)
    o = jnp.einsum('bhtm,bmhd->bthd', p, mem_kv[:, :, 1].astype(f32)).reshape(B, T, -1)
    return (o.astype(xn.dtype) @ w_o).astype(xn.dtype)


PEER_PICKS = PEER_HEADS * PEER_TOPK
PEER_TOKENS_PER_STEP = 64
PEER_ROW_BUFFERS = 4


def _gelu_tanh(x):
    return 0.5 * x * (1.0 + jnp.tanh(0.7978845608028654 * (x + 0.044715 * x * x * x)))


def _peer_expert_body(ids_ref, x_ref, g_ref, uv_ref, o_ref, rows, sems):
    tokens = x_ref.shape[0]
    depth = PEER_ROW_BUFFERS

    def issue(t, slot):
        for k in range(PEER_PICKS):
            e = ids_ref[t, k]
            pltpu.make_async_copy(uv_ref.at[pl.ds(e, 1)], rows.at[slot, pl.ds(k, 1)], sems.at[slot]).start()

    def wait(slot):
        pltpu.make_async_copy(uv_ref.at[pl.ds(0, PEER_PICKS)], rows.at[slot], sems.at[slot]).wait()

    for t in range(depth - 1):
        issue(t, t)

    def step(t, carry):
        slot = lax.rem(t, depth)
        ahead = t + depth - 1

        @pl.when(ahead < tokens)
        def _():
            issue(ahead, lax.rem(ahead, depth))

        wait(slot)
        x = x_ref[pl.ds(t, 1), :]
        u = rows[slot, :, :D_MODEL]
        v = rows[slot, :, D_MODEL:]
        act = lax.dot_general(x, u, (((1,), (1,)), ((), ())), preferred_element_type=f32)
        w = g_ref[pl.ds(t, 1), :] * _gelu_tanh(act)
        o_ref[pl.ds(t, 1), :] = jnp.dot(w, v, preferred_element_type=f32)
        return carry

    lax.fori_loop(0, tokens, step, 0)


def peer_experts(xn, ids, gates, uv):
    n, d = xn.shape
    tb = PEER_TOKENS_PER_STEP
    assert n % tb == 0 and tb >= PEER_ROW_BUFFERS
    return pl.pallas_call(
        _peer_expert_body,
        grid=(n // tb,),
        in_specs=[
            pl.BlockSpec((tb, PEER_PICKS), lambda i: (i, 0), memory_space=pltpu.SMEM),
            pl.BlockSpec((tb, d), lambda i: (i, 0)),
            pl.BlockSpec((tb, PEER_PICKS), lambda i: (i, 0)),
            pl.BlockSpec(memory_space=pl.ANY),
        ],
        out_specs=pl.BlockSpec((tb, d), lambda i: (i, 0)),
        out_shape=jax.ShapeDtypeStruct((n, d), f32),
        scratch_shapes=[
            pltpu.VMEM((PEER_ROW_BUFFERS, PEER_PICKS, 2 * d), f32),
            pltpu.SemaphoreType.DMA((PEER_ROW_BUFFERS,)),
        ],
        compiler_params=pltpu.CompilerParams(dimension_semantics=("arbitrary",)),
        name="peer_experts",
    )(ids, xn, gates, uv)


PEER_ROUTE_TOKENS = 128
PEER_HALF_DIM = PEER_QUERY_DIM // 2


def _top_rows(s, row_id, k, payload=None):
    vals, picks = [], []
    sentinel = s.shape[0]
    for _ in range(k):
        m = jnp.max(s, axis=0, keepdims=True)
        first = jnp.min(jnp.where(s == m, row_id, sentinel), axis=0, keepdims=True)
        hit = row_id == first
        vals.append(m)
        if payload is None:
            picks.append(first)
        else:
            picks.append(jnp.max(jnp.where(hit, payload, -1), axis=0, keepdims=True))
        s = jnp.where(hit, -jnp.inf, s)
    return jnp.concatenate(vals, 0), jnp.concatenate(picks, 0)


def _peer_route_body(x_ref, wq_ref, sk_ref, ids_ref, gate_ref, q_scr):
    tn = x_ref.shape[0]
    q = jnp.dot(x_ref[...].astype(jnp.bfloat16), wq_ref[...], preferred_element_type=f32)
    for j in range(2 * PEER_HEADS):
        q_scr[j] = q[:, j * PEER_HALF_DIM:(j + 1) * PEER_HALF_DIM].astype(jnp.bfloat16)
    key_id = lax.broadcasted_iota(jnp.int32, (PEER_N_KEYS, tn), 0)
    cand_id = lax.broadcasted_iota(jnp.int32, (PEER_TOPK * PEER_TOPK, tn), 0)

    def head(h, carry):
        tops = []
        for p in range(2):
            s = lax.dot_general(sk_ref[2 * h + p], q_scr[2 * h + p], (((1,), (1,)), ((), ())),
                                preferred_element_type=f32)
            tops.append(_top_rows(s, key_id, PEER_TOPK))
        (v0, i0), (v1, i1) = tops
        cand_s = (v0[:, None, :] + v1[None, :, :]).reshape(PEER_TOPK * PEER_TOPK, tn)
        cand_e = (i0[:, None, :] * PEER_N_KEYS + i1[None, :, :]).reshape(PEER_TOPK * PEER_TOPK, tn)
        top_s, top_e = _top_rows(cand_s, cand_id, PEER_TOPK, payload=cand_e)
        e = jnp.exp(top_s - top_s[0:1])
        ids_ref[h] = top_e
        gate_ref[h] = e / jnp.sum(e, axis=0, keepdims=True)
        return carry

    lax.fori_loop(0, PEER_HEADS, head, 0)


def peer_route(xn, wq, sub_keys):
    n, d = xn.shape
    tn = PEER_ROUTE_TOKENS
    assert n % tn == 0
    n_q = 2 * PEER_HEADS * PEER_HALF_DIM
    sk = sub_keys.reshape(2 * PEER_HEADS, PEER_N_KEYS, PEER_HALF_DIM).astype(jnp.bfloat16)
    ids_t, gates_t = pl.pallas_call(
        _peer_route_body,
        grid=(n // tn,),
        in_specs=[
            pl.BlockSpec((tn, d), lambda i: (i, 0)),
            pl.BlockSpec((d, n_q), lambda i: (0, 0)),
            pl.BlockSpec((2 * PEER_HEADS, PEER_N_KEYS, PEER_HALF_DIM), lambda i: (0, 0, 0)),
        ],
        out_specs=[
            pl.BlockSpec((PEER_HEADS, PEER_TOPK, tn), lambda i: (0, 0, i)),
            pl.BlockSpec((PEER_HEADS, PEER_TOPK, tn), lambda i: (0, 0, i)),
        ],
        out_shape=[
            jax.ShapeDtypeStruct((PEER_HEADS, PEER_TOPK, n), jnp.int32),
            jax.ShapeDtypeStruct((PEER_HEADS, PEER_TOPK, n), f32),
        ],
        scratch_shapes=[pltpu.VMEM((2 * PEER_HEADS, tn, PEER_HALF_DIM), jnp.bfloat16)],
        compiler_params=pltpu.CompilerParams(dimension_semantics=("arbitrary",)),
        name="peer_route",
    )(xn, wq.astype(jnp.bfloat16), sk)
    to_rows = lambda a: a.reshape(PEER_PICKS, n).T
    return to_rows(ids_t), to_rows(gates_t)


def peer_ffn_tokens(xn, wq, sub_keys, uv):
    ids, gates = peer_route(xn, wq, sub_keys)
    return peer_experts(xn, ids, gates, uv)


def kernel(x_prompt, x_sample, cache_cmp_kv, cache_slc_kv, cache_win_kv, state_mlstm_c, state_mlstm_n,
           state_mlstm_m, cache_mem_kv, page_table, mem_prompt, g_mix, w_in, b_in, b_forget, ml_head_gain,
           cmp_pe, cmp_w1, cmp_b1, cmp_w2, w_out, g_xattn, g_mem, w_xq, w_xk, w_xv, w_xo, g_ffn, peer_wq,
           peer_sub_keys, peer_u, peer_v, g_final):
    B, T = x_prompt.shape[:2]
    S = x_sample.shape[1]
    past_len = page_table.shape[1] * PAGE_SIZE
    xp, xs = x_prompt, x_sample
    l = 0
    xn = rmsnorm(xp, g_mix[l])
    mq, mk, mv, mo, mi, mf, nq, nkv, ng = mixer_projections(xn, w_in[l], b_in[l], b_forget[l])
    h_ml, p_c, p_n, p_m = mlstm_mixer(
        mq, mk, mv, mi, mf, mo, ml_head_gain[l],
        jnp.zeros((B, ML_HEADS, ML_HEAD_DIM, ML_HEAD_DIM), f32),
        jnp.zeros((B, ML_HEADS, ML_HEAD_DIM), f32),
        jnp.full((B, ML_HEADS), -jnp.inf, f32))
    p_cmp, p_slc, win_rows = nkv[:, :, 0:2], nkv[:, :, 2:4], nkv[:, :, 4:6]
    win_pad = jnp.pad(win_rows, ((0, 0), (WINDOW, 0), (0, 0), (0, 0), (0, 0)))
    h_nsa = nsa_mixer(nq, ng, 0, p_cmp, p_slc, win_pad, -WINDOW,
                      cmp_pe[l], cmp_w1[l], cmp_b1[l], cmp_w2[l])
    xp = xp + (jnp.concatenate([h_ml, h_nsa.astype(h_ml.dtype)], -1) @ w_out[l]).astype(xp.dtype)
    p_mem = memory_kv(mem_prompt, g_mem[l], w_xk[l], w_xv[l])
    xp = xp + cross_attend(rmsnorm(xp, g_xattn[l]), p_mem, w_xq[l], w_xo[l])
    p_win = win_rows[:, T - min(WINDOW, T):]

    xn = rmsnorm(xs, g_mix[l])
    mq, mk, mv, mo, mi, mf, nq, nkv, ng = mixer_projections(xn, w_in[l], b_in[l], b_forget[l])
    h_ml, s_c, s_n, s_m = mlstm_mixer(mq, mk, mv, mi, mf, mo, ml_head_gain[l],
                                      state_mlstm_c[l], state_mlstm_n[l], state_mlstm_m[l])
    s_cmp, s_slc, win_rows = nkv[:, :, 0:2], nkv[:, :, 2:4], nkv[:, :, 4:6]
    cmp_full = jnp.concatenate([gather_pages(cache_cmp_kv[l], page_table), s_cmp], axis=1)
    slc_full = jnp.concatenate([gather_pages(cache_slc_kv[l], page_table), s_slc], axis=1)
    win_buf = cache_win_kv[l]
    win_ext = jnp.concatenate([win_buf, win_rows], axis=1)
    h_nsa = nsa_mixer(nq, ng, past_len, cmp_full, slc_full, win_ext, past_len - win_buf.shape[1],
                      cmp_pe[l], cmp_w1[l], cmp_b1[l], cmp_w2[l])
    xs = xs + (jnp.concatenate([h_ml, h_nsa.astype(h_ml.dtype)], -1) @ w_out[l]).astype(xs.dtype)
    xs = xs + cross_attend(rmsnorm(xs, g_xattn[l]), cache_mem_kv[l], w_xq[l], w_xo[l])
    w_keep = min(WINDOW, past_len + S)
    s_win = win_ext[:, win_ext.shape[1] - w_keep:]

    n_p, n_s = B * T, xs.shape[0] * S
    x_all = jnp.concatenate([xp.reshape(n_p, D_MODEL), xs.reshape(n_s, D_MODEL)], 0)
    uv = jnp.concatenate([peer_u[l], peer_v[l]], axis=1)
    ffn = peer_ffn_tokens(rmsnorm(x_all, g_ffn[l]), peer_wq[l], peer_sub_keys[l], uv)
    xp = xp + ffn[:n_p].reshape(xp.shape)
    xs = xs + ffn[n_p:].reshape(xs.shape)

    y_prompt = rmsnorm_pallas(xp, g_final)
    y_sample = rmsnorm_pallas(xs, g_final)
    st = lambda a: a[None]
    return (y_prompt, y_sample,
            st(p_cmp), st(p_slc), st(p_win), st(p_c), st(p_n), st(p_m), st(p_mem),
            st(s_cmp), st(s_slc), st(s_win), st(s_c), st(s_n), st(s_m))
```

```python
import functools

import jax
import jax.numpy as jnp
from jax import lax
import numpy as np
from jax.experimental import pallas as pl
from jax.experimental.pallas import tpu as pltpu

D_MODEL = 1024
DEPTH = 1
PAGE_SIZE = 128

ML_WIDTH = D_MODEL // 2
ML_HEADS = 4
ML_HEAD_DIM = ML_WIDTH // ML_HEADS
ML_CHUNK = 128
NSA_WIDTH = D_MODEL - ML_WIDTH
NSA_HEADS = 8
NSA_HEAD_DIM = NSA_WIDTH // NSA_HEADS
NSA_KV_HEADS = 2
NSA_GROUP = NSA_HEADS // NSA_KV_HEADS
CMP_BLOCK = 32
CMP_STRIDE = 16
SEL_BLOCK = 64
N_SEL = 16
WINDOW = 512
Q_BLOCK = 128
FORCED_SCORE = 1.0e4
INVALID_SCORE = -1.0
MEM_LEN = 256
MEM_HEADS = 4
MEM_HEAD_DIM = D_MODEL // MEM_HEADS
PEER_HEADS = 8
PEER_N_KEYS = 128
PEER_N_EXPERTS = PEER_N_KEYS * PEER_N_KEYS
PEER_TOPK = 16
PEER_QUERY_DIM = 256
PEER_TOKEN_BLOCK = 128
EPS = 1e-6
TINY = 1e-30
IN_SPLITS = (ML_WIDTH, ML_WIDTH, ML_WIDTH, ML_WIDTH, ML_HEADS, ML_HEADS,
             NSA_WIDTH, 6 * NSA_KV_HEADS * NSA_HEAD_DIM, 3 * NSA_HEADS)
IN_COLS = sum(IN_SPLITS)
IN_OFFSETS = tuple(int(o) for o in np.cumsum(IN_SPLITS)[:-1])

f32 = jnp.float32


def rmsnorm(x, g):
    xf = x.astype(f32)
    return (xf * lax.rsqrt(jnp.mean(xf * xf, -1, keepdims=True) + EPS) * g).astype(x.dtype)


def _rmsnorm_body(x_ref, g_ref, o_ref):
    x = x_ref[...]
    o_ref[...] = x * lax.rsqrt(jnp.mean(x * x, -1, keepdims=True) + EPS) * g_ref[...]


def rmsnorm_pallas(x, g):
    shape = x.shape
    x2 = x.reshape(-1, shape[-1])
    n, d = x2.shape
    tm = min(n, 512)
    out = pl.pallas_call(
        _rmsnorm_body,
        grid=(n // tm,),
        in_specs=[pl.BlockSpec((tm, d), lambda i: (i, 0)), pl.BlockSpec((1, d), lambda i: (0, 0))],
        out_specs=pl.BlockSpec((tm, d), lambda i: (i, 0)),
        out_shape=jax.ShapeDtypeStruct((n, d), f32),
    )(x2, g.reshape(1, d))
    return out.reshape(shape)


def masked_softmax(s, mask):
    s = jnp.where(mask, s.astype(f32), -jnp.inf)
    mx = jnp.max(s, -1, keepdims=True)
    mx = jnp.where(jnp.isfinite(mx), mx, 0.0)
    e = jnp.exp(s - mx)
    return e / jnp.maximum(e.sum(-1, keepdims=True), TINY)


def mixer_projections(xn, w_in, b_in, b_forget):
    B, T = xn.shape[:2]
    p = xn @ w_in + b_in
    ml_q, ml_k, ml_v, ml_o, ml_i, ml_f, nq, nkv, ng = jnp.split(p, IN_OFFSETS, axis=-1)
    hd = (B, T, ML_HEADS, ML_HEAD_DIM)
    return (ml_q.reshape(hd), ml_k.reshape(hd), ml_v.reshape(hd), ml_o, ml_i, ml_f + b_forget,
            nq.reshape(B, T, NSA_HEADS, NSA_HEAD_DIM),
            nkv.reshape(B, T, 6, NSA_KV_HEADS, NSA_HEAD_DIM),
            jax.nn.sigmoid(ng).reshape(B, T, NSA_HEADS, 3))


def mlstm_chunk(carry, inp):
    c, n, m = carry
    q, k, v, ig, lf = inp
    L = q.shape[2]
    b = jnp.cumsum(lf, axis=-1)
    causal = jnp.tril(jnp.ones((L, L), bool))
    log_d = jnp.where(causal, b[..., :, None] - b[..., None, :] + ig[..., None, :], -jnp.inf)
    inter = b + m[..., None]
    m_t = jnp.maximum(inter, log_d.max(-1))
    d_mat = jnp.exp(log_d - m_t[..., None])
    a = jnp.exp(inter - m_t)
    qk = jnp.einsum('bhtd,bhsd->bhts', q, k) * d_mat
    num = a[..., None] * jnp.einsum('bhtd,bhde->bhte', q, c) + jnp.einsum('bhts,bhse->bhte', qk, v)
    den = a * jnp.einsum('bhtd,bhd->bht', q, n) + qk.sum(-1)
    h = num / jnp.maximum(jnp.abs(den), jnp.exp(-m_t))[..., None]
    m_new = m_t[..., -1]
    w = jnp.exp(b[..., -1:] - b + ig - m_new[..., None])
    decay = jnp.exp(b[..., -1] + m - m_new)
    c_new = decay[..., None, None] * c + jnp.einsum('bhs,bhsd,bhse->bhde', w, k, v)
    n_new = decay[..., None] * n + jnp.einsum('bhs,bhsd->bhd', w, k)
    return (c_new, n_new, m_new), h


def mlstm_mixer(q, k, v, i_pre, f_pre, o_pre, head_gain, c0, n0, m0):
    B, T, H, d = q.shape
    L = ML_CHUNK if T % ML_CHUNK == 0 else T
    nc = T // L

    def chunks(a):
        a = a.astype(f32).reshape((B, nc, L) + a.shape[2:])
        return jnp.transpose(a, (1, 0, 3, 2) + tuple(range(4, a.ndim)))

    xs = (chunks(q), chunks(k * d ** -0.5), chunks(v), chunks(i_pre),
          chunks(jax.nn.log_sigmoid(f_pre.astype(f32))))
    (c, n, m), h = lax.scan(mlstm_chunk, (c0.astype(f32), n0.astype(f32), m0.astype(f32)), xs)
    h = jnp.transpose(h, (1, 0, 3, 2, 4)).reshape(B, T, H, d)
    h = jax.nn.sigmoid(o_pre.astype(f32)).reshape(B, T, H, d) * h
    h = h * lax.rsqrt(jnp.mean(h * h, -1, keepdims=True) + EPS) * head_gain
    return h.reshape(B, T, H * d).astype(q.dtype), c, n, m


def nsa_compress(rows, pe, w1, b1, w2):
    B, G, T, d = rows.shape
    n_seg = T // CMP_STRIDE
    seg = rows[:, :, :n_seg * CMP_STRIDE].reshape(B, G, n_seg, CMP_STRIDE * d)
    half = CMP_STRIDE * d
    pre = seg[:, :, :-1] @ w1[:half] + seg[:, :, 1:] @ w1[half:] + (pe.reshape(-1) @ w1 + b1)
    ends = jnp.arange(n_seg - 1) * CMP_STRIDE + (CMP_BLOCK - 1)
    return jax.nn.gelu(pre) @ w2, ends


def to_sel_blocks(rows):
    B, G, T, d = rows.shape
    nsb = -(-T // SEL_BLOCK)
    rows = jnp.pad(rows, ((0, 0), (0, 0), (0, nsb * SEL_BLOCK - T), (0, 0)))
    return rows.reshape(B, G, nsb, SEL_BLOCK, d)


def cmp_to_sel(imp, nsb):
    r = SEL_BLOCK // CMP_STRIDE
    nc = imp.shape[-1]
    lead = imp.shape[:-1]
    tot = r * nsb
    padw = [(0, 0)] * len(lead)
    first = jnp.pad(imp, padw + [(0, tot - nc)]).reshape(lead + (nsb, r)).sum(-1)
    second = jnp.pad(imp, padw + [(1, tot - nc - 1)]).reshape(lead + (nsb, r)).sum(-1)
    return 0.5 * (first + second)


def nsa_attend(q, gates, q_pos, kc, vc, c_end, ks, vs, kw, vw, w_pos):
    B, G, R, Q, d = q.shape
    scale = d ** -0.5
    p_c = masked_softmax(jnp.einsum('bgrqd,bgnd->bgrqn', q, kc) * scale, c_end[None, :] <= q_pos[:, None])
    o_c = jnp.einsum('bgrqn,bgnd->bgrqd', p_c, vc)
    nsb = ks.shape[2]
    imp = cmp_to_sel(p_c.sum(2), nsb)
    blk = jnp.arange(nsb)
    valid = blk[None, :] * SEL_BLOCK <= q_pos[:, None]
    forced = (blk[None, :] == 0) | (blk[None, :] == q_pos[:, None] // SEL_BLOCK)
    score = jnp.where(forced, FORCED_SCORE, jnp.where(valid, imp, INVALID_SCORE))
    _, idx = lax.top_k(score, min(N_SEL, nsb))
    bi = jnp.arange(B)[:, None, None, None]
    gi = jnp.arange(G)[None, :, None, None]
    kg = ks[bi, gi, idx].reshape(B, G, Q, -1, d)
    vg = vs[bi, gi, idx].reshape(B, G, Q, -1, d)
    k_pos = (idx[..., None] * SEL_BLOCK + jnp.arange(SEL_BLOCK)).reshape(B, G, Q, -1)
    p_s = masked_softmax(jnp.einsum('bgrqd,bgqsd->bgrqs', q, kg) * scale,
                         (k_pos <= q_pos[None, None, :, None])[:, :, None])
    o_s = jnp.einsum('bgrqs,bgqsd->bgrqd', p_s, vg)
    wmask = ((w_pos[None, :] <= q_pos[:, None]) & (w_pos[None, :] >= q_pos[:, None] - WINDOW)
             & (w_pos[None, :] >= 0))
    p_w = masked_softmax(jnp.einsum('bgrqd,bgkd->bgrqk', q, kw) * scale, wmask)
    o_w = jnp.einsum('bgrqk,bgkd->bgrqd', p_w, vw)
    return gates[..., 0:1] * o_c + gates[..., 1:2] * o_s + gates[..., 2:3] * o_w


def nsa_mixer(q, gates, q_pos0, cmp_rows, slc_rows, win_rows, win_pos0, cmp_pe, cmp_w1, cmp_b1, cmp_w2):
    B, Tq, _, d = q.shape
    G, R = NSA_KV_HEADS, NSA_GROUP
    bg = lambda a: jnp.swapaxes(a, 1, 2)
    kc, c_end = nsa_compress(bg(cmp_rows[:, :, 0]), cmp_pe[0], cmp_w1[0], cmp_b1[0], cmp_w2[0])
    vc, _ = nsa_compress(bg(cmp_rows[:, :, 1]), cmp_pe[1], cmp_w1[1], cmp_b1[1], cmp_w2[1])
    ks = to_sel_blocks(bg(slc_rows[:, :, 0]))
    vs = to_sel_blocks(bg(slc_rows[:, :, 1]))
    kw = bg(win_rows[:, :, 0])
    vw = bg(win_rows[:, :, 1])
    qg = q.reshape(B, Tq, G, R, d).transpose(0, 2, 3, 1, 4)
    gg = gates.reshape(B, Tq, G, R, 3).transpose(0, 2, 3, 1, 4)
    qb = Q_BLOCK if Tq % Q_BLOCK == 0 else Tq
    nb = Tq // qb
    if nb == 1:
        w_pos = win_pos0 + jnp.arange(kw.shape[2])
        o = nsa_attend(qg, gg, q_pos0 + jnp.arange(Tq), kc, vc, c_end, ks, vs, kw, vw, w_pos)
    else:
        off = q_pos0 - WINDOW - win_pos0
        span = WINDOW + qb

        def block(args):
            q_blk, g_blk, j = args
            start = j * qb
            kw_b = lax.dynamic_slice_in_dim(kw, start + off, span, axis=2)
            vw_b = lax.dynamic_slice_in_dim(vw, start + off, span, axis=2)
            w_pos = q_pos0 + start - WINDOW + jnp.arange(span)
            return nsa_attend(q_blk, g_blk, q_pos0 + start + jnp.arange(qb), kc, vc, c_end,
                              ks, vs, kw_b, vw_b, w_pos)

        qs = qg.reshape(B, G, R, nb, qb, d).transpose(3, 0, 1, 2, 4, 5)
        gs = gg.reshape(B, G, R, nb, qb, 3).transpose(3, 0, 1, 2, 4, 5)
        o = lax.map(block, (qs, gs, jnp.arange(nb)))
        o = o.transpose(1, 2, 3, 0, 4, 5).reshape(B, G, R, Tq, d)
    return o.transpose(0, 3, 1, 2, 4).reshape(B, Tq, NSA_HEADS * d)


NSA_SEL_CHUNK = 512
NSA_WIN_CHUNKS = WINDOW // Q_BLOCK + 1
NSA_KV_LANES = NSA_KV_HEADS * NSA_HEAD_DIM


def _softmax_masked(s, mask):
    sm = jnp.where(mask, s, -jnp.inf)
    mx = jnp.max(sm, axis=-1, keepdims=True)
    mx = jnp.where(mx == -jnp.inf, 0.0, mx)
    e = jnp.exp(sm - mx)
    return e * (1.0 / jnp.maximum(jnp.sum(e, axis=-1, keepdims=True), TINY))


def _top_rows_mask(s, row_id, k):
    sel = jnp.zeros(s.shape, f32)
    sentinel = s.shape[0]
    for _ in range(k):
        m = jnp.max(s, axis=0, keepdims=True)
        first = jnp.min(jnp.where(s == m, row_id, sentinel), axis=0, keepdims=True)
        hit = row_id == first
        sel = jnp.where(hit, 1.0, sel)
        s = jnp.where(hit, -jnp.inf, s)
    return sel


def _nsa_prompt_body(q_ref, gate_ref, kct_ref, vc_ref, mt_ref, kts_ref, vs_ref, ktw_ref, vw_ref, o_ref):
    j = pl.program_id(1)
    Q, R, d = Q_BLOCK, NSA_GROUP, NSA_HEAD_DIM
    n_cmp = kct_ref.shape[-1]
    n_blk = mt_ref.shape[0]
    bf16 = jnp.bfloat16
    q_all = q_ref[0] * (d ** -0.5)
    gates = gate_ref[0]
    q_pos = j * Q + lax.broadcasted_iota(jnp.int32, (Q, 1), 0)
    q_pos_l = j * Q + lax.broadcasted_iota(jnp.int32, (1, Q), 1)
    blk_r = lax.broadcasted_iota(jnp.int32, (n_blk, 1), 0)
    blk_id = lax.broadcasted_iota(jnp.int32, (n_blk, Q), 0)
    nt = (((1,), (1,)), ((), ()))

    for g in range(NSA_KV_HEADS):
        lanes = slice(g * d, (g + 1) * d)
        qg = jnp.concatenate([q_all[:, (g * R + r) * d:(g * R + r + 1) * d] for r in range(R)], axis=0).astype(bf16)

        s = jnp.dot(qg, kct_ref[0, g], preferred_element_type=f32).reshape(R, Q, n_cmp)
        n_id = lax.broadcasted_iota(jnp.int32, (1, n_cmp), 1)
        cmask = (n_id * CMP_STRIDE + (CMP_BLOCK - 1) <= q_pos) & (n_id < n_cmp - 1)
        p = _softmax_masked(s, cmask[None])
        o_c = jnp.dot(p.reshape(R * Q, n_cmp).astype(bf16), vc_ref[0], preferred_element_type=f32)[:, lanes]

        p_sum = p[0] + p[1] + p[2] + p[3]
        hi = p_sum.astype(bf16)
        lo = (p_sum - hi.astype(f32)).astype(bf16)
        mt = mt_ref[...]
        imp_t = (lax.dot_general(mt, hi, nt, preferred_element_type=f32)
                 + lax.dot_general(mt, lo, nt, preferred_element_type=f32))
        valid = blk_r * SEL_BLOCK <= q_pos_l
        forced = (blk_r == 0) | (blk_r == jnp.right_shift(q_pos_l, 6))
        score = jnp.where(forced, FORCED_SCORE, jnp.where(valid, imp_t, INVALID_SCORE))
        sel = _top_rows_mask(score, blk_id, min(N_SEL, n_blk)).T.astype(bf16)

        kc = kts_ref.shape[-1]

        def chunk(c, carry):
            m, l, acc = carry
            kt = kts_ref[0, c, g * d:(g + 1) * d, :]
            sc = jnp.dot(qg, kt, preferred_element_type=f32).reshape(R, Q, kc)
            key = c * kc + lax.broadcasted_iota(jnp.int32, (1, kc), 1)
            expand = jnp.where(blk_r == jnp.right_shift(key, 6), 1.0, 0.0).astype(bf16)
            picked = jnp.dot(sel, expand, preferred_element_type=f32) > 0.5
            mask = picked & (key <= q_pos)
            sm = jnp.where(mask[None], sc, -jnp.inf)
            m_new = jnp.maximum(m, jnp.max(sm, axis=-1, keepdims=True))
            m_safe = jnp.where(m_new == -jnp.inf, 0.0, m_new)
            alpha = jnp.exp(m - m_safe)
            pe = jnp.exp(sm - m_safe)
            l = l * alpha + jnp.sum(pe, axis=-1, keepdims=True)
            pv = jnp.dot(pe.reshape(R * Q, kc).astype(bf16), vs_ref[0, c], preferred_element_type=f32)
            acc = acc * alpha.reshape(R * Q, 1) + pv
            return m_new, l, acc

        n_chunks = (j * Q + Q + kc - 1) // kc
        init = (jnp.full((R, Q, 1), -jnp.inf, f32), jnp.zeros((R, Q, 1), f32),
                jnp.zeros((R * Q, NSA_KV_LANES), f32))
        _, l_s, acc_s = lax.fori_loop(0, n_chunks, chunk, init)
        o_s = acc_s[:, lanes] * (1.0 / jnp.maximum(l_s.reshape(R * Q, 1), TINY))

        ss, vv = [], []
        for i in range(NSA_WIN_CHUNKS):
            cc = jnp.maximum(j - (NSA_WIN_CHUNKS - 1) + i, 0)
            ss.append(jnp.dot(qg, ktw_ref[0, cc, g * d:(g + 1) * d, :], preferred_element_type=f32))
            vv.append(vw_ref[0, cc])
        span = NSA_WIN_CHUNKS * Q
        sw = jnp.concatenate(ss, axis=1).reshape(R, Q, span)
        k_pos = (j - (NSA_WIN_CHUNKS - 1)) * Q + lax.broadcasted_iota(jnp.int32, (1, span), 1)
        wmask = (k_pos >= 0) & (k_pos <= q_pos) & (k_pos >= q_pos - WINDOW)
        pw = _softmax_masked(sw, wmask[None])
        o_w = jnp.dot(pw.reshape(R * Q, span).astype(bf16), jnp.concatenate(vv, axis=0),
                      preferred_element_type=f32)[:, lanes]

        for r in range(R):
            h = g * R + r
            rows = slice(r * Q, (r + 1) * Q)
            o_ref[0, :, h * d:(h + 1) * d] = (gates[:, 3 * h:3 * h + 1] * o_c[rows]
                                              + gates[:, 3 * h + 1:3 * h + 2] * o_s[rows]
                                              + gates[:, 3 * h + 2:3 * h + 3] * o_w[rows])


def _cmp_to_sel_matrix(n_cmp, n_blk):
    r = SEL_BLOCK // CMP_STRIDE
    n = np.arange(n_cmp)
    b = np.arange(n_blk)[:, None]
    m = 0.5 * ((n // r == b).astype(np.float32) + ((n + 1) // r == b).astype(np.float32))
    m[:, n_cmp - 1] = 0.0
    return jnp.asarray(m, jnp.bfloat16)


def nsa_prompt(nq, gates, nkv, kc, vc):
    B, T, _ = nq.shape
    bf16 = jnp.bfloat16
    G, d, w = NSA_KV_HEADS, NSA_HEAD_DIM, NSA_KV_LANES
    n_cmp, n_blk = T // CMP_STRIDE, T // SEL_BLOCK
    kc_s, kc_w = NSA_SEL_CHUNK, Q_BLOCK
    pad = ((0, 0), (0, 0), (0, 1), (0, 0))
    kct = jnp.pad(kc, pad).transpose(0, 1, 3, 2).astype(bf16)
    vc2 = jnp.pad(vc, pad).transpose(0, 2, 1, 3).reshape(B, n_cmp, w).astype(bf16)
    chunks = lambda a, c: a.reshape(B, T // c, c, w).astype(bf16)
    kts = chunks(nkv[..., 2 * w:3 * w], kc_s).transpose(0, 1, 3, 2)
    vs = chunks(nkv[..., 3 * w:4 * w], kc_s)
    ktw = chunks(nkv[..., 4 * w:5 * w], kc_w).transpose(0, 1, 3, 2)
    vw = chunks(nkv[..., 5 * w:6 * w], kc_w)
    whole = lambda a: pl.BlockSpec((1,) + a.shape[1:], lambda b, j: (b,) + (0,) * (a.ndim - 1))
    mt = _cmp_to_sel_matrix(n_cmp, n_blk)
    return pl.pallas_call(
        _nsa_prompt_body,
        grid=(B, T // Q_BLOCK),
        in_specs=[
            pl.BlockSpec((1, Q_BLOCK, nq.shape[-1]), lambda b, j: (b, j, 0)),
            pl.BlockSpec((1, Q_BLOCK, gates.shape[-1]), lambda b, j: (b, j, 0)),
            whole(kct), whole(vc2),
            pl.BlockSpec(mt.shape, lambda b, j: (0, 0)),
            whole(kts), whole(vs), whole(ktw), whole(vw),
        ],
        out_specs=pl.BlockSpec((1, Q_BLOCK, nq.shape[-1]), lambda b, j: (b, j, 0)),
        out_shape=jax.ShapeDtypeStruct(nq.shape, f32),
        compiler_params=pltpu.CompilerParams(dimension_semantics=("arbitrary", "arbitrary"),
                                             vmem_limit_bytes=48 * 1024 * 1024),
        name="nsa_prompt",
    )(nq, gates, kct, vc2, mt, kts, vs, ktw, vw)


def gather_pages(pool, page_table):
    g = pool[page_table]
    return g.reshape((g.shape[0], g.shape[1] * g.shape[2]) + g.shape[3:])


def memory_kv(mem, g_mem, w_k, w_v):
    mn = rmsnorm(mem, g_mem)
    B, M = mem.shape[:2]
    k = (mn @ w_k).reshape(B, M, MEM_HEADS, MEM_HEAD_DIM)
    v = (mn @ w_v).reshape(B, M, MEM_HEADS, MEM_HEAD_DIM)
    return jnp.stack([k, v], axis=2)


def cross_attend(xn, mem_kv, w_q, w_o):
    B, T = xn.shape[:2]
    q = (xn @ w_q).reshape(B, T, MEM_HEADS, MEM_HEAD_DIM)
    s = jnp.einsum('bthd,bmhd->bhtm', q, mem_kv[:, :, 0]) * MEM_HEAD_DIM ** -0.5
    p = jax.nn.softmax(s.astype(f32), axis=-1)
    o = jnp.einsum('bhtm,bmhd->bthd', p, mem_kv[:, :, 1].astype(f32)).reshape(B, T, -1)
    return (o.astype(xn.dtype) @ w_o).astype(xn.dtype)


PEER_PICKS = PEER_HEADS * PEER_TOPK
PEER_TOKENS_PER_STEP = 64
PEER_ROW_BUFFERS = 4


def _gelu_tanh(x):
    return 0.5 * x * (1.0 + jnp.tanh(0.7978845608028654 * (x + 0.044715 * x * x * x)))


def _peer_expert_body(ids_ref, x_ref, g_ref, uv_ref, o_ref, rows, sems):
    tokens = x_ref.shape[0]
    depth = PEER_ROW_BUFFERS
    nt = (((1,), (1,)), ((), ()))

    def issue(t, slot):
        for k in range(PEER_PICKS):
            pltpu.make_async_copy(uv_ref.at[pl.ds(ids_ref[t, k], 1)], rows.at[slot, pl.ds(k, 1)],
                                  sems.at[slot]).start(priority=k % 2)

    def wait(slot):
        pltpu.make_async_copy(uv_ref.at[pl.ds(0, PEER_PICKS)], rows.at[slot], sems.at[slot]).wait()

    def mix(t, slot):
        x = x_ref[pl.ds(t, 1), :]
        act = lax.dot_general(x, rows[slot, :, :D_MODEL], nt, preferred_element_type=f32)
        w = g_ref[pl.ds(t, 1), :] * _gelu_tanh(act)
        o_ref[pl.ds(t, 1), :] = jnp.dot(w, rows[slot, :, D_MODEL:], preferred_element_type=f32)

    for t in range(depth - 1):
        issue(t, t)

    def step(i, carry):
        for slot in range(depth):
            t = i * depth + slot
            ahead = t + depth - 1

            @pl.when(ahead < tokens)
            def _():
                issue(ahead, (slot + depth - 1) % depth)

            wait(slot)
            mix(t, slot)
        return carry

    lax.fori_loop(0, tokens // depth, step, 0)


def peer_experts(xn, ids, gates, uv):
    n, d = xn.shape
    tb = PEER_TOKENS_PER_STEP
    assert n % tb == 0 and tb % PEER_ROW_BUFFERS == 0
    return pl.pallas_call(
        _peer_expert_body,
        grid=(n // tb,),
        in_specs=[
            pl.BlockSpec((tb, PEER_PICKS), lambda i: (i, 0), memory_space=pltpu.SMEM),
            pl.BlockSpec((tb, d), lambda i: (i, 0)),
            pl.BlockSpec((tb, PEER_PICKS), lambda i: (i, 0)),
            pl.BlockSpec(memory_space=pl.ANY),
        ],
        out_specs=pl.BlockSpec((tb, d), lambda i: (i, 0)),
        out_shape=jax.ShapeDtypeStruct((n, d), f32),
        scratch_shapes=[
            pltpu.VMEM((PEER_ROW_BUFFERS, PEER_PICKS, 2 * d), f32),
            pltpu.SemaphoreType.DMA((PEER_ROW_BUFFERS,)),
        ],
        compiler_params=pltpu.CompilerParams(dimension_semantics=("arbitrary",)),
        name="peer_experts",
    )(ids, xn, gates, uv)


PEER_ROUTE_TOKENS = 128
PEER_HALF_DIM = PEER_QUERY_DIM // 2


def _top_rows(s, row_id, k, payload=None):
    vals, picks = [], []
    sentinel = s.shape[0]
    for _ in range(k):
        m = jnp.max(s, axis=0, keepdims=True)
        first = jnp.min(jnp.where(s == m, row_id, sentinel), axis=0, keepdims=True)
        hit = row_id == first
        vals.append(m)
        if payload is None:
            picks.append(first)
        else:
            picks.append(jnp.max(jnp.where(hit, payload, -1), axis=0, keepdims=True))
        s = jnp.where(hit, -jnp.inf, s)
    return jnp.concatenate(vals, 0), jnp.concatenate(picks, 0)


def _peer_route_body(x_ref, wq_ref, sk_ref, ids_ref, gate_ref, q_scr):
    tn = x_ref.shape[0]
    q = jnp.dot(x_ref[...].astype(jnp.bfloat16), wq_ref[...], preferred_element_type=f32)
    for j in range(2 * PEER_HEADS):
        q_scr[j] = q[:, j * PEER_HALF_DIM:(j + 1) * PEER_HALF_DIM].astype(jnp.bfloat16)
    key_id = lax.broadcasted_iota(jnp.int32, (PEER_N_KEYS, tn), 0)
    cand_id = lax.broadcasted_iota(jnp.int32, (PEER_TOPK * PEER_TOPK, tn), 0)

    def head(h, carry):
        tops = []
        for p in range(2):
            s = lax.dot_general(sk_ref[2 * h + p], q_scr[2 * h + p], (((1,), (1,)), ((), ())),
                                preferred_element_type=f32)
            tops.append(_top_rows(s, key_id, PEER_TOPK))
        (v0, i0), (v1, i1) = tops
        cand_s = (v0[:, None, :] + v1[None, :, :]).reshape(PEER_TOPK * PEER_TOPK, tn)
        cand_e = (i0[:, None, :] * PEER_N_KEYS + i1[None, :, :]).reshape(PEER_TOPK * PEER_TOPK, tn)
        top_s, top_e = _top_rows(cand_s, cand_id, PEER_TOPK, payload=cand_e)
        e = jnp.exp(top_s - top_s[0:1])
        ids_ref[h] = top_e
        gate_ref[h] = e / jnp.sum(e, axis=0, keepdims=True)
        return carry

    lax.fori_loop(0, PEER_HEADS, head, 0)


def peer_route(xn, wq, sub_keys):
    n, d = xn.shape
    tn = PEER_ROUTE_TOKENS
    assert n % tn == 0
    n_q = 2 * PEER_HEADS * PEER_HALF_DIM
    sk = sub_keys.reshape(2 * PEER_HEADS, PEER_N_KEYS, PEER_HALF_DIM).astype(jnp.bfloat16)
    ids_t, gates_t = pl.pallas_call(
        _peer_route_body,
        grid=(n // tn,),
        in_specs=[
            pl.BlockSpec((tn, d), lambda i: (i, 0)),
            pl.BlockSpec((d, n_q), lambda i: (0, 0)),
            pl.BlockSpec((2 * PEER_HEADS, PEER_N_KEYS, PEER_HALF_DIM), lambda i: (0, 0, 0)),
        ],
        out_specs=[
            pl.BlockSpec((PEER_HEADS, PEER_TOPK, tn), lambda i: (0, 0, i)),
            pl.BlockSpec((PEER_HEADS, PEER_TOPK, tn), lambda i: (0, 0, i)),
        ],
        out_shape=[
            jax.ShapeDtypeStruct((PEER_HEADS, PEER_TOPK, n), jnp.int32),
            jax.ShapeDtypeStruct((PEER_HEADS, PEER_TOPK, n), f32),
        ],
        scratch_shapes=[pltpu.VMEM((2 * PEER_HEADS, tn, PEER_HALF_DIM), jnp.bfloat16)],
        compiler_params=pltpu.CompilerParams(dimension_semantics=("arbitrary",)),
        name="peer_route",
    )(xn, wq.astype(jnp.bfloat16), sk)
    to_rows = lambda a: a.reshape(PEER_PICKS, n).T
    return to_rows(ids_t), to_rows(gates_t)


def peer_ffn_tokens(xn, wq, sub_keys, uv):
    ids, gates = peer_route(xn, wq, sub_keys)
    return peer_experts(xn, ids, gates, uv)


def kernel(x_prompt, x_sample, cache_cmp_kv, cache_slc_kv, cache_win_kv, state_mlstm_c, state_mlstm_n,
           state_mlstm_m, cache_mem_kv, page_table, mem_prompt, g_mix, w_in, b_in, b_forget, ml_head_gain,
           cmp_pe, cmp_w1, cmp_b1, cmp_w2, w_out, g_xattn, g_mem, w_xq, w_xk, w_xv, w_xo, g_ffn, peer_wq,
           peer_sub_keys, peer_u, peer_v, g_final):
    B, T = x_prompt.shape[:2]
    S = x_sample.shape[1]
    past_len = page_table.shape[1] * PAGE_SIZE
    xp, xs = x_prompt, x_sample
    l = 0
    xn = rmsnorm(xp, g_mix[l])
    mq, mk, mv, mo, mi, mf, nq, nkv, ng = mixer_projections(xn, w_in[l], b_in[l], b_forget[l])
    h_ml, p_c, p_n, p_m = mlstm_mixer(
        mq, mk, mv, mi, mf, mo, ml_head_gain[l],
        jnp.zeros((B, ML_HEADS, ML_HEAD_DIM, ML_HEAD_DIM), f32),
        jnp.zeros((B, ML_HEADS, ML_HEAD_DIM), f32),
        jnp.full((B, ML_HEADS), -jnp.inf, f32))
    p_cmp, p_slc, win_rows = nkv[:, :, 0:2], nkv[:, :, 2:4], nkv[:, :, 4:6]
    bg = lambda a: jnp.swapaxes(a, 1, 2)
    kc, _ = nsa_compress(bg(p_cmp[:, :, 0]), cmp_pe[l, 0], cmp_w1[l, 0], cmp_b1[l, 0], cmp_w2[l, 0])
    vc, _ = nsa_compress(bg(p_cmp[:, :, 1]), cmp_pe[l, 1], cmp_w1[l, 1], cmp_b1[l, 1], cmp_w2[l, 1])
    h_nsa = nsa_prompt(nq.reshape(B, T, NSA_WIDTH), ng.reshape(B, T, 3 * NSA_HEADS),
                       nkv.reshape(B, T, 6 * NSA_KV_LANES), kc, vc)
    xp = xp + (jnp.concatenate([h_ml, h_nsa.astype(h_ml.dtype)], -1) @ w_out[l]).astype(xp.dtype)
    p_mem = memory_kv(mem_prompt, g_mem[l], w_xk[l], w_xv[l])
    xp = xp + cross_attend(rmsnorm(xp, g_xattn[l]), p_mem, w_xq[l], w_xo[l])
    p_win = win_rows[:, T - min(WINDOW, T):]

    xn = rmsnorm(xs, g_mix[l])
    mq, mk, mv, mo, mi, mf, nq, nkv, ng = mixer_projections(xn, w_in[l], b_in[l], b_forget[l])
    h_ml, s_c, s_n, s_m = mlstm_mixer(mq, mk, mv, mi, mf, mo, ml_head_gain[l],
                                      state_mlstm_c[l], state_mlstm_n[l], state_mlstm_m[l])
    s_cmp, s_slc, win_rows = nkv[:, :, 0:2], nkv[:, :, 2:4], nkv[:, :, 4:6]
    cmp_full = jnp.concatenate([gather_pages(cache_cmp_kv[l], page_table), s_cmp], axis=1)
    slc_full = jnp.concatenate([gather_pages(cache_slc_kv[l], page_table), s_slc], axis=1)
    win_buf = cache_win_kv[l]
    win_ext = jnp.concatenate([win_buf, win_rows], axis=1)
    h_nsa = nsa_mixer(nq, ng, past_len, cmp_full, slc_full, win_ext, past_len - win_buf.shape[1],
                      cmp_pe[l], cmp_w1[l], cmp_b1[l], cmp_w2[l])
    xs = xs + (jnp.concatenate([h_ml, h_nsa.astype(h_ml.dtype)], -1) @ w_out[l]).astype(xs.dtype)
    xs = xs + cross_attend(rmsnorm(xs, g_xattn[l]), cache_mem_kv[l], w_xq[l], w_xo[l])
    w_keep = min(WINDOW, past_len + S)
    s_win = win_ext[:, win_ext.shape[1] - w_keep:]

    n_p, n_s = B * T, xs.shape[0] * S
    x_all = jnp.concatenate([xp.reshape(n_p, D_MODEL), xs.reshape(n_s, D_MODEL)], 0)
    uv = jnp.concatenate([peer_u[l], peer_v[l]], axis=1)
    ffn = peer_ffn_tokens(rmsnorm(x_all, g_ffn[l]), peer_wq[l], peer_sub_keys[l], uv)
    xp = xp + ffn[:n_p].reshape(xp.shape)
    xs = xs + ffn[n_p:].reshape(xs.shape)

    y_prompt = rmsnorm_pallas(xp, g_final)
    y_sample = rmsnorm_pallas(xs, g_final)
    st = lambda a: a[None]
    return (y_prompt, y_sample,
            st(p_cmp), st(p_slc), st(p_win), st(p_c), st(p_n), st(p_m), st(p_mem),
            st(s_cmp), st(s_slc), st(s_win), st(s_c), st(s_n), st(s_m))
```

```python
import functools

import jax
import jax.numpy as jnp
from jax import lax
import numpy as np
from jax.experimental import pallas as pl
from jax.experimental.pallas import tpu as pltpu

D_MODEL = 1024
DEPTH = 1
PAGE_SIZE = 128

ML_WIDTH = D_MODEL // 2
ML_HEADS = 4
ML_HEAD_DIM = ML_WIDTH // ML_HEADS
ML_CHUNK = 128
NSA_WIDTH = D_MODEL - ML_WIDTH
NSA_HEADS = 8
NSA_HEAD_DIM = NSA_WIDTH // NSA_HEADS
NSA_KV_HEADS = 2
NSA_GROUP = NSA_HEADS // NSA_KV_HEADS
CMP_BLOCK = 32
CMP_STRIDE = 16
SEL_BLOCK = 64
N_SEL = 16
WINDOW = 512
Q_BLOCK = 128
FORCED_SCORE = 1.0e4
INVALID_SCORE = -1.0
MEM_LEN = 256
MEM_HEADS = 4
MEM_HEAD_DIM = D_MODEL // MEM_HEADS
PEER_HEADS = 8
PEER_N_KEYS = 128
PEER_N_EXPERTS = PEER_N_KEYS * PEER_N_KEYS
PEER_TOPK = 16
PEER_QUERY_DIM = 256
PEER_TOKEN_BLOCK = 128
EPS = 1e-6
TINY = 1e-30
IN_SPLITS = (ML_WIDTH, ML_WIDTH, ML_WIDTH, ML_WIDTH, ML_HEADS, ML_HEADS,
             NSA_WIDTH, 6 * NSA_KV_HEADS * NSA_HEAD_DIM, 3 * NSA_HEADS)
IN_COLS = sum(IN_SPLITS)
IN_OFFSETS = tuple(int(o) for o in np.cumsum(IN_SPLITS)[:-1])

f32 = jnp.float32


def rmsnorm(x, g):
    xf = x.astype(f32)
    return (xf * lax.rsqrt(jnp.mean(xf * xf, -1, keepdims=True) + EPS) * g).astype(x.dtype)


def _rmsnorm_body(x_ref, g_ref, o_ref):
    x = x_ref[...]
    o_ref[...] = x * lax.rsqrt(jnp.mean(x * x, -1, keepdims=True) + EPS) * g_ref[...]


def rmsnorm_pallas(x, g):
    shape = x.shape
    x2 = x.reshape(-1, shape[-1])
    n, d = x2.shape
    tm = min(n, 512)
    out = pl.pallas_call(
        _rmsnorm_body,
        grid=(n // tm,),
        in_specs=[pl.BlockSpec((tm, d), lambda i: (i, 0)), pl.BlockSpec((1, d), lambda i: (0, 0))],
        out_specs=pl.BlockSpec((tm, d), lambda i: (i, 0)),
        out_shape=jax.ShapeDtypeStruct((n, d), f32),
    )(x2, g.reshape(1, d))
    return out.reshape(shape)


def masked_softmax(s, mask):
    s = jnp.where(mask, s.astype(f32), -jnp.inf)
    mx = jnp.max(s, -1, keepdims=True)
    mx = jnp.where(jnp.isfinite(mx), mx, 0.0)
    e = jnp.exp(s - mx)
    return e / jnp.maximum(e.sum(-1, keepdims=True), TINY)


def mixer_projections(xn, w_in, b_in, b_forget):
    B, T = xn.shape[:2]
    p = xn @ w_in + b_in
    ml_q, ml_k, ml_v, ml_o, ml_i, ml_f, nq, nkv, ng = jnp.split(p, IN_OFFSETS, axis=-1)
    hd = (B, T, ML_HEADS, ML_HEAD_DIM)
    return (ml_q.reshape(hd), ml_k.reshape(hd), ml_v.reshape(hd), ml_o, ml_i, ml_f + b_forget,
            nq.reshape(B, T, NSA_HEADS, NSA_HEAD_DIM),
            nkv.reshape(B, T, 6, NSA_KV_HEADS, NSA_HEAD_DIM),
            jax.nn.sigmoid(ng).reshape(B, T, NSA_HEADS, 3))


def mlstm_chunk(carry, inp):
    c, n, m = carry
    q, k, v, ig, lf = inp
    L = q.shape[2]
    b = jnp.cumsum(lf, axis=-1)
    causal = jnp.tril(jnp.ones((L, L), bool))
    log_d = jnp.where(causal, b[..., :, None] - b[..., None, :] + ig[..., None, :], -jnp.inf)
    inter = b + m[..., None]
    m_t = jnp.maximum(inter, log_d.max(-1))
    d_mat = jnp.exp(log_d - m_t[..., None])
    a = jnp.exp(inter - m_t)
    qk = jnp.einsum('bhtd,bhsd->bhts', q, k) * d_mat
    num = a[..., None] * jnp.einsum('bhtd,bhde->bhte', q, c) + jnp.einsum('bhts,bhse->bhte', qk, v)
    den = a * jnp.einsum('bhtd,bhd->bht', q, n) + qk.sum(-1)
    h = num / jnp.maximum(jnp.abs(den), jnp.exp(-m_t))[..., None]
    m_new = m_t[..., -1]
    w = jnp.exp(b[..., -1:] - b + ig - m_new[..., None])
    decay = jnp.exp(b[..., -1] + m - m_new)
    c_new = decay[..., None, None] * c + jnp.einsum('bhs,bhsd,bhse->bhde', w, k, v)
    n_new = decay[..., None] * n + jnp.einsum('bhs,bhsd->bhd', w, k)
    return (c_new, n_new, m_new), h


def mlstm_mixer(q, k, v, i_pre, f_pre, o_pre, head_gain, c0, n0, m0):
    B, T, H, d = q.shape
    L = ML_CHUNK if T % ML_CHUNK == 0 else T
    nc = T // L

    def chunks(a):
        a = a.astype(f32).reshape((B, nc, L) + a.shape[2:])
        return jnp.transpose(a, (1, 0, 3, 2) + tuple(range(4, a.ndim)))

    xs = (chunks(q), chunks(k * d ** -0.5), chunks(v), chunks(i_pre),
          chunks(jax.nn.log_sigmoid(f_pre.astype(f32))))
    (c, n, m), h = lax.scan(mlstm_chunk, (c0.astype(f32), n0.astype(f32), m0.astype(f32)), xs)
    h = jnp.transpose(h, (1, 0, 3, 2, 4)).reshape(B, T, H, d)
    h = jax.nn.sigmoid(o_pre.astype(f32)).reshape(B, T, H, d) * h
    h = h * lax.rsqrt(jnp.mean(h * h, -1, keepdims=True) + EPS) * head_gain
    return h.reshape(B, T, H * d).astype(q.dtype), c, n, m


def nsa_compress(rows, pe, w1, b1, w2):
    B, G, T, d = rows.shape
    n_seg = T // CMP_STRIDE
    seg = rows[:, :, :n_seg * CMP_STRIDE].reshape(B, G, n_seg, CMP_STRIDE * d)
    half = CMP_STRIDE * d
    pre = seg[:, :, :-1] @ w1[:half] + seg[:, :, 1:] @ w1[half:] + (pe.reshape(-1) @ w1 + b1)
    ends = jnp.arange(n_seg - 1) * CMP_STRIDE + (CMP_BLOCK - 1)
    return jax.nn.gelu(pre) @ w2, ends


def to_sel_blocks(rows):
    B, G, T, d = rows.shape
    nsb = -(-T // SEL_BLOCK)
    rows = jnp.pad(rows, ((0, 0), (0, 0), (0, nsb * SEL_BLOCK - T), (0, 0)))
    return rows.reshape(B, G, nsb, SEL_BLOCK, d)


def cmp_to_sel(imp, nsb):
    r = SEL_BLOCK // CMP_STRIDE
    nc = imp.shape[-1]
    lead = imp.shape[:-1]
    tot = r * nsb
    padw = [(0, 0)] * len(lead)
    first = jnp.pad(imp, padw + [(0, tot - nc)]).reshape(lead + (nsb, r)).sum(-1)
    second = jnp.pad(imp, padw + [(1, tot - nc - 1)]).reshape(lead + (nsb, r)).sum(-1)
    return 0.5 * (first + second)


def nsa_attend(q, gates, q_pos, kc, vc, c_end, ks, vs, kw, vw, w_pos):
    B, G, R, Q, d = q.shape
    scale = d ** -0.5
    p_c = masked_softmax(jnp.einsum('bgrqd,bgnd->bgrqn', q, kc) * scale, c_end[None, :] <= q_pos[:, None])
    o_c = jnp.einsum('bgrqn,bgnd->bgrqd', p_c, vc)
    nsb = ks.shape[2]
    imp = cmp_to_sel(p_c.sum(2), nsb)
    blk = jnp.arange(nsb)
    valid = blk[None, :] * SEL_BLOCK <= q_pos[:, None]
    forced = (blk[None, :] == 0) | (blk[None, :] == q_pos[:, None] // SEL_BLOCK)
    score = jnp.where(forced, FORCED_SCORE, jnp.where(valid, imp, INVALID_SCORE))
    _, idx = lax.top_k(score, min(N_SEL, nsb))
    bi = jnp.arange(B)[:, None, None, None]
    gi = jnp.arange(G)[None, :, None, None]
    kg = ks[bi, gi, idx].reshape(B, G, Q, -1, d)
    vg = vs[bi, gi, idx].reshape(B, G, Q, -1, d)
    k_pos = (idx[..., None] * SEL_BLOCK + jnp.arange(SEL_BLOCK)).reshape(B, G, Q, -1)
    p_s = masked_softmax(jnp.einsum('bgrqd,bgqsd->bgrqs', q, kg) * scale,
                         (k_pos <= q_pos[None, None, :, None])[:, :, None])
    o_s = jnp.einsum('bgrqs,bgqsd->bgrqd', p_s, vg)
    wmask = ((w_pos[None, :] <= q_pos[:, None]) & (w_pos[None, :] >= q_pos[:, None] - WINDOW)
             & (w_pos[None, :] >= 0))
    p_w = masked_softmax(jnp.einsum('bgrqd,bgkd->bgrqk', q, kw) * scale, wmask)
    o_w = jnp.einsum('bgrqk,bgkd->bgrqd', p_w, vw)
    return gates[..., 0:1] * o_c + gates[..., 1:2] * o_s + gates[..., 2:3] * o_w


def nsa_mixer(q, gates, q_pos0, cmp_rows, slc_rows, win_rows, win_pos0, cmp_pe, cmp_w1, cmp_b1, cmp_w2):
    B, Tq, _, d = q.shape
    G, R = NSA_KV_HEADS, NSA_GROUP
    bg = lambda a: jnp.swapaxes(a, 1, 2)
    kc, c_end = nsa_compress(bg(cmp_rows[:, :, 0]), cmp_pe[0], cmp_w1[0], cmp_b1[0], cmp_w2[0])
    vc, _ = nsa_compress(bg(cmp_rows[:, :, 1]), cmp_pe[1], cmp_w1[1], cmp_b1[1], cmp_w2[1])
    ks = to_sel_blocks(bg(slc_rows[:, :, 0]))
    vs = to_sel_blocks(bg(slc_rows[:, :, 1]))
    kw = bg(win_rows[:, :, 0])
    vw = bg(win_rows[:, :, 1])
    qg = q.reshape(B, Tq, G, R, d).transpose(0, 2, 3, 1, 4)
    gg = gates.reshape(B, Tq, G, R, 3).transpose(0, 2, 3, 1, 4)
    qb = Q_BLOCK if Tq % Q_BLOCK == 0 else Tq
    nb = Tq // qb
    if nb == 1:
        w_pos = win_pos0 + jnp.arange(kw.shape[2])
        o = nsa_attend(qg, gg, q_pos0 + jnp.arange(Tq), kc, vc, c_end, ks, vs, kw, vw, w_pos)
    else:
        off = q_pos0 - WINDOW - win_pos0
        span = WINDOW + qb

        def block(args):
            q_blk, g_blk, j = args
            start = j * qb
            kw_b = lax.dynamic_slice_in_dim(kw, start + off, span, axis=2)
            vw_b = lax.dynamic_slice_in_dim(vw, start + off, span, axis=2)
            w_pos = q_pos0 + start - WINDOW + jnp.arange(span)
            return nsa_attend(q_blk, g_blk, q_pos0 + start + jnp.arange(qb), kc, vc, c_end,
                              ks, vs, kw_b, vw_b, w_pos)

        qs = qg.reshape(B, G, R, nb, qb, d).transpose(3, 0, 1, 2, 4, 5)
        gs = gg.reshape(B, G, R, nb, qb, 3).transpose(3, 0, 1, 2, 4, 5)
        o = lax.map(block, (qs, gs, jnp.arange(nb)))
        o = o.transpose(1, 2, 3, 0, 4, 5).reshape(B, G, R, Tq, d)
    return o.transpose(0, 3, 1, 2, 4).reshape(B, Tq, NSA_HEADS * d)


NSA_SEL_CHUNK = 512
NSA_WIN_CHUNKS = WINDOW // Q_BLOCK + 1
NSA_KV_LANES = NSA_KV_HEADS * NSA_HEAD_DIM


def _softmax_masked(s, mask):
    sm = jnp.where(mask, s, -jnp.inf)
    mx = jnp.max(sm, axis=-1, keepdims=True)
    mx = jnp.where(mx == -jnp.inf, 0.0, mx)
    e = jnp.exp(sm - mx)
    return e * (1.0 / jnp.maximum(jnp.sum(e, axis=-1, keepdims=True), TINY))


def _top_rows_mask(s, row_id, k):
    sel = jnp.zeros(s.shape, f32)
    sentinel = s.shape[0]
    for _ in range(k):
        m = jnp.max(s, axis=0, keepdims=True)
        first = jnp.min(jnp.where(s == m, row_id, sentinel), axis=0, keepdims=True)
        hit = row_id == first
        sel = jnp.where(hit, 1.0, sel)
        s = jnp.where(hit, -jnp.inf, s)
    return sel


def _nsa_prompt_body(q_ref, gate_ref, kct_ref, vc_ref, mt_ref, kts_ref, vs_ref, ktw_ref, vw_ref, o_ref):
    j = pl.program_id(1)
    Q, R, d = Q_BLOCK, NSA_GROUP, NSA_HEAD_DIM
    n_cmp = kct_ref.shape[-1]
    n_blk = mt_ref.shape[0]
    bf16 = jnp.bfloat16
    q_all = q_ref[0] * (d ** -0.5)
    gates = gate_ref[0]
    q_pos = j * Q + lax.broadcasted_iota(jnp.int32, (Q, 1), 0)
    q_pos_l = j * Q + lax.broadcasted_iota(jnp.int32, (1, Q), 1)
    blk_r = lax.broadcasted_iota(jnp.int32, (n_blk, 1), 0)
    blk_id = lax.broadcasted_iota(jnp.int32, (n_blk, Q), 0)
    nt = (((1,), (1,)), ((), ()))

    for g in range(NSA_KV_HEADS):
        lanes = slice(g * d, (g + 1) * d)
        qg = jnp.concatenate([q_all[:, (g * R + r) * d:(g * R + r + 1) * d] for r in range(R)], axis=0).astype(bf16)

        s = jnp.dot(qg, kct_ref[0, g], preferred_element_type=f32).reshape(R, Q, n_cmp)
        n_id = lax.broadcasted_iota(jnp.int32, (1, n_cmp), 1)
        cmask = (n_id * CMP_STRIDE + (CMP_BLOCK - 1) <= q_pos) & (n_id < n_cmp - 1)
        p = _softmax_masked(s, cmask[None])
        o_c = jnp.dot(p.reshape(R * Q, n_cmp).astype(bf16), vc_ref[0], preferred_element_type=f32)[:, lanes]

        p_sum = p[0] + p[1] + p[2] + p[3]
        hi = p_sum.astype(bf16)
        lo = (p_sum - hi.astype(f32)).astype(bf16)
        mt = mt_ref[...]
        imp_t = (lax.dot_general(mt, hi, nt, preferred_element_type=f32)
                 + lax.dot_general(mt, lo, nt, preferred_element_type=f32))
        valid = blk_r * SEL_BLOCK <= q_pos_l
        forced = (blk_r == 0) | (blk_r == jnp.right_shift(q_pos_l, 6))
        score = jnp.where(forced, FORCED_SCORE, jnp.where(valid, imp_t, INVALID_SCORE))
        sel = _top_rows_mask(score, blk_id, min(N_SEL, n_blk)).T.astype(bf16)

        kc = kts_ref.shape[-1]

        def chunk(c, carry):
            m, l, acc = carry
            kt = kts_ref[0, c, g * d:(g + 1) * d, :]
            sc = jnp.dot(qg, kt, preferred_element_type=f32).reshape(R, Q, kc)
            key = c * kc + lax.broadcasted_iota(jnp.int32, (1, kc), 1)
            expand = jnp.where(blk_r == jnp.right_shift(key, 6), 1.0, 0.0).astype(bf16)
            picked = jnp.dot(sel, expand, preferred_element_type=f32) > 0.5
            mask = picked & (key <= q_pos)
            sm = jnp.where(mask[None], sc, -jnp.inf)
            m_new = jnp.maximum(m, jnp.max(sm, axis=-1, keepdims=True))
            m_safe = jnp.where(m_new == -jnp.inf, 0.0, m_new)
            alpha = jnp.exp(m - m_safe)
            pe = jnp.exp(sm - m_safe)
            l = l * alpha + jnp.sum(pe, axis=-1, keepdims=True)
            pv = jnp.dot(pe.reshape(R * Q, kc).astype(bf16), vs_ref[0, c], preferred_element_type=f32)
            acc = acc * alpha.reshape(R * Q, 1) + pv
            return m_new, l, acc

        n_chunks = (j * Q + Q + kc - 1) // kc
        init = (jnp.full((R, Q, 1), -jnp.inf, f32), jnp.zeros((R, Q, 1), f32),
                jnp.zeros((R * Q, NSA_KV_LANES), f32))
        _, l_s, acc_s = lax.fori_loop(0, n_chunks, chunk, init)
        o_s = acc_s[:, lanes] * (1.0 / jnp.maximum(l_s.reshape(R * Q, 1), TINY))

        ss, vv = [], []
        for i in range(NSA_WIN_CHUNKS):
            cc = jnp.maximum(j - (NSA_WIN_CHUNKS - 1) + i, 0)
            ss.append(jnp.dot(qg, ktw_ref[0, cc, g * d:(g + 1) * d, :], preferred_element_type=f32))
            vv.append(vw_ref[0, cc])
        span = NSA_WIN_CHUNKS * Q
        sw = jnp.concatenate(ss, axis=1).reshape(R, Q, span)
        k_pos = (j - (NSA_WIN_CHUNKS - 1)) * Q + lax.broadcasted_iota(jnp.int32, (1, span), 1)
        wmask = (k_pos >= 0) & (k_pos <= q_pos) & (k_pos >= q_pos - WINDOW)
        pw = _softmax_masked(sw, wmask[None])
        o_w = jnp.dot(pw.reshape(R * Q, span).astype(bf16), jnp.concatenate(vv, axis=0),
                      preferred_element_type=f32)[:, lanes]

        for r in range(R):
            h = g * R + r
            rows = slice(r * Q, (r + 1) * Q)
            o_ref[0, :, h * d:(h + 1) * d] = (gates[:, 3 * h:3 * h + 1] * o_c[rows]
                                              + gates[:, 3 * h + 1:3 * h + 2] * o_s[rows]
                                              + gates[:, 3 * h + 2:3 * h + 3] * o_w[rows])


def _cmp_to_sel_matrix(n_cmp, n_blk):
    r = SEL_BLOCK // CMP_STRIDE
    n = np.arange(n_cmp)
    b = np.arange(n_blk)[:, None]
    m = 0.5 * ((n // r == b).astype(np.float32) + ((n + 1) // r == b).astype(np.float32))
    m[:, n_cmp - 1] = 0.0
    return jnp.asarray(m, jnp.bfloat16)


def nsa_prompt(nq, gates, nkv, kc, vc):
    B, T, _ = nq.shape
    bf16 = jnp.bfloat16
    G, d, w = NSA_KV_HEADS, NSA_HEAD_DIM, NSA_KV_LANES
    n_cmp, n_blk = T // CMP_STRIDE, T // SEL_BLOCK
    kc_s, kc_w = NSA_SEL_CHUNK, Q_BLOCK
    pad = ((0, 0), (0, 0), (0, 1), (0, 0))
    kct = jnp.pad(kc, pad).transpose(0, 1, 3, 2).astype(bf16)
    vc2 = jnp.pad(vc, pad).transpose(0, 2, 1, 3).reshape(B, n_cmp, w).astype(bf16)
    chunks = lambda a, c: a.reshape(B, T // c, c, w).astype(bf16)
    kts = chunks(nkv[..., 2 * w:3 * w], kc_s).transpose(0, 1, 3, 2)
    vs = chunks(nkv[..., 3 * w:4 * w], kc_s)
    ktw = chunks(nkv[..., 4 * w:5 * w], kc_w).transpose(0, 1, 3, 2)
    vw = chunks(nkv[..., 5 * w:6 * w], kc_w)
    whole = lambda a: pl.BlockSpec((1,) + a.shape[1:], lambda b, j: (b,) + (0,) * (a.ndim - 1))
    mt = _cmp_to_sel_matrix(n_cmp, n_blk)
    return pl.pallas_call(
        _nsa_prompt_body,
        grid=(B, T // Q_BLOCK),
        in_specs=[
            pl.BlockSpec((1, Q_BLOCK, nq.shape[-1]), lambda b, j: (b, j, 0)),
            pl.BlockSpec((1, Q_BLOCK, gates.shape[-1]), lambda b, j: (b, j, 0)),
            whole(kct), whole(vc2),
            pl.BlockSpec(mt.shape, lambda b, j: (0, 0)),
            whole(kts), whole(vs), whole(ktw), whole(vw),
        ],
        out_specs=pl.BlockSpec((1, Q_BLOCK, nq.shape[-1]), lambda b, j: (b, j, 0)),
        out_shape=jax.ShapeDtypeStruct(nq.shape, f32),
        compiler_params=pltpu.CompilerParams(dimension_semantics=("arbitrary", "arbitrary"),
                                             vmem_limit_bytes=48 * 1024 * 1024),
        name="nsa_prompt",
    )(nq, gates, kct, vc2, mt, kts, vs, ktw, vw)


def gather_pages(pool, page_table):
    g = pool[page_table]
    return g.reshape((g.shape[0], g.shape[1] * g.shape[2]) + g.shape[3:])


def memory_kv(mem, g_mem, w_k, w_v):
    mn = rmsnorm(mem, g_mem)
    B, M = mem.shape[:2]
    k = (mn @ w_k).reshape(B, M, MEM_HEADS, MEM_HEAD_DIM)
    v = (mn @ w_v).reshape(B, M, MEM_HEADS, MEM_HEAD_DIM)
    return jnp.stack([k, v], axis=2)


def cross_attend(xn, mem_kv, w_q, w_o):
    B, T = xn.shape[:2]
    q = (xn @ w_q).reshape(B, T, MEM_HEADS, MEM_HEAD_DIM)
    s = jnp.einsum('bthd,bmhd->bhtm', q, mem_kv[:, :, 0]) * MEM_HEAD_DIM ** -0.5
    p = jax.nn.softmax(s.astype(f32), axis=-1)
    o = jnp.einsum('bhtm,bmhd->bthd', p, mem_kv[:, :, 1].astype(f32)).reshape(B, T, -1)
    return (o.astype(xn.dtype) @ w_o).astype(xn.dtype)


PEER_PICKS = PEER_HEADS * PEER_TOPK
PEER_TOKENS_PER_STEP = 64
PEER_ROW_BUFFERS = 8
PEER_SLAB_LANES = 128
PEER_SLAB_ROWS = 2 * D_MODEL // PEER_SLAB_LANES


def _gelu_tanh(x):
    return 0.5 * x * (1.0 + jnp.tanh(0.7978845608028654 * (x + 0.044715 * x * x * x)))


def _peer_expert_body(ids_ref, x_ref, g_ref, seg_ref, uv_ref, o_ref, rows, sems):
    tokens = x_ref.shape[0]
    depth = PEER_ROW_BUFFERS
    half, lanes = PEER_SLAB_ROWS // 2, PEER_SLAB_LANES
    cols = PEER_PICKS * half
    seg = seg_ref.shape[0]
    nt = (((1,), (1,)), ((), ()))
    bf16 = jnp.bfloat16

    col_row = lax.broadcasted_iota(jnp.int32, (half, cols), 1) & (half - 1)
    sub = lax.broadcasted_iota(jnp.int32, (half, cols), 0)
    diag = jnp.where(col_row == sub, 1.0, 0.0)

    def issue(t, slot):
        for k in range(PEER_PICKS):
            pltpu.make_async_copy(uv_ref.at[ids_ref[t, k]], rows.at[slot, k], sems.at[slot]).start(priority=k % 2)

    def wait(slot):
        pltpu.make_async_copy(uv_ref.at[pl.ds(0, PEER_PICKS)], rows.at[slot], sems.at[slot]).wait()

    group = depth // 2
    groups = tokens // group
    n_seg = cols // seg

    def mix_group(t0, slot0):
        parts = []
        for j in range(group):
            u_rows = rows[slot0 + j, :, :half, :].reshape(cols, lanes).astype(bf16)
            prod = lax.dot_general(x_ref[t0 + j].astype(bf16), u_rows, nt, preferred_element_type=f32)
            part = jnp.sum(prod * diag, axis=0, keepdims=True)
            parts += [part[:, i * seg:(i + 1) * seg] for i in range(n_seg)]
        part = jnp.concatenate(parts, axis=0)
        hi = part.astype(bf16)
        lo = (part - hi.astype(f32)).astype(bf16)
        ones = seg_ref[...]
        act = jnp.dot(hi, ones, preferred_element_type=f32) + jnp.dot(lo, ones, preferred_element_type=f32)
        for j in range(group):
            w = g_ref[t0 + j] * _gelu_tanh(act[j * n_seg:(j + 1) * n_seg])
            w = jnp.concatenate([jnp.broadcast_to(w[i:i + 1, :], (half, seg)) for i in range(n_seg)], axis=1)
            v_rows = rows[slot0 + j, :, half:, :].reshape(cols, lanes).astype(bf16)
            o_ref[t0 + j] = jnp.dot((w * diag).astype(bf16), v_rows, preferred_element_type=f32)


    def issue_group(g, side):
        for j in range(group):
            issue(g * group + j, side * group + j)

    issue_group(0, 0)

    def step(i, carry):
        for side in range(2):
            g = 2 * i + side

            @pl.when(g + 1 < groups)
            def _():
                issue_group(g + 1, 1 - side)

            for j in range(group):
                wait(side * group + j)
            mix_group(g * group, side * group)
        return carry

    lax.fori_loop(0, groups // 2, step, 0)


def peer_experts(xn, ids, gates, uv):
    n, d = xn.shape
    tb = PEER_TOKENS_PER_STEP
    slab, half, lanes = PEER_SLAB_ROWS, PEER_SLAB_ROWS // 2, PEER_SLAB_LANES
    seg = 2 * lanes
    cols = PEER_PICKS * half
    assert n % tb == 0 and tb % PEER_ROW_BUFFERS == 0 and half * lanes == d and cols % seg == 0
    same_pick = np.arange(seg)[:, None] // half == np.arange(seg)[None, :] // half
    out = pl.pallas_call(
        _peer_expert_body,
        grid=(n // tb,),
        in_specs=[
            pl.BlockSpec((tb, PEER_PICKS), lambda i: (i, 0), memory_space=pltpu.SMEM),
            pl.BlockSpec((tb, half, lanes), lambda i: (i, 0, 0)),
            pl.BlockSpec((tb, cols // seg, seg), lambda i: (i, 0, 0)),
            pl.BlockSpec((seg, seg), lambda i: (0, 0)),
            pl.BlockSpec(memory_space=pl.ANY),
        ],
        out_specs=pl.BlockSpec((tb, half, lanes), lambda i: (i, 0, 0)),
        out_shape=jax.ShapeDtypeStruct((n, half, lanes), f32),
        scratch_shapes=[
            pltpu.VMEM((PEER_ROW_BUFFERS, PEER_PICKS, slab, lanes), f32),
            pltpu.SemaphoreType.DMA((PEER_ROW_BUFFERS,)),
        ],
        compiler_params=pltpu.CompilerParams(dimension_semantics=("arbitrary",)),
        name="peer_experts",
    )(ids, xn.reshape(n, half, lanes), jnp.repeat(gates, half, axis=1).reshape(n, cols // seg, seg),
      jnp.asarray(same_pick, jnp.bfloat16), uv)
    return out.reshape(n, d)


PEER_ROUTE_TOKENS = 128
PEER_HALF_DIM = PEER_QUERY_DIM // 2


def _top_rows(s, row_id, k, payload=None):
    vals, picks = [], []
    sentinel = s.shape[0]
    for _ in range(k):
        m = jnp.max(s, axis=0, keepdims=True)
        first = jnp.min(jnp.where(s == m, row_id, sentinel), axis=0, keepdims=True)
        hit = row_id == first
        vals.append(m)
        if payload is None:
            picks.append(first)
        else:
            picks.append(jnp.max(jnp.where(hit, payload, -1), axis=0, keepdims=True))
        s = jnp.where(hit, -jnp.inf, s)
    return jnp.concatenate(vals, 0), jnp.concatenate(picks, 0)


def _peer_route_body(x_ref, wq_ref, sk_ref, ids_ref, gate_ref, q_scr):
    tn = x_ref.shape[0]
    q = jnp.dot(x_ref[...].astype(jnp.bfloat16), wq_ref[...], preferred_element_type=f32)
    for j in range(2 * PEER_HEADS):
        q_scr[j] = q[:, j * PEER_HALF_DIM:(j + 1) * PEER_HALF_DIM].astype(jnp.bfloat16)
    key_id = lax.broadcasted_iota(jnp.int32, (PEER_N_KEYS, tn), 0)
    cand_id = lax.broadcasted_iota(jnp.int32, (PEER_TOPK * PEER_TOPK, tn), 0)

    def head(h, carry):
        tops = []
        for p in range(2):
            s = lax.dot_general(sk_ref[2 * h + p], q_scr[2 * h + p], (((1,), (1,)), ((), ())),
                                preferred_element_type=f32)
            tops.append(_top_rows(s, key_id, PEER_TOPK))
        (v0, i0), (v1, i1) = tops
        cand_s = (v0[:, None, :] + v1[None, :, :]).reshape(PEER_TOPK * PEER_TOPK, tn)
        cand_e = (i0[:, None, :] * PEER_N_KEYS + i1[None, :, :]).reshape(PEER_TOPK * PEER_TOPK, tn)
        top_s, top_e = _top_rows(cand_s, cand_id, PEER_TOPK, payload=cand_e)
        e = jnp.exp(top_s - top_s[0:1])
        ids_ref[h] = top_e
        gate_ref[h] = e / jnp.sum(e, axis=0, keepdims=True)
        return carry

    lax.fori_loop(0, PEER_HEADS, head, 0)


def peer_route(xn, wq, sub_keys):
    n, d = xn.shape
    tn = PEER_ROUTE_TOKENS
    assert n % tn == 0
    n_q = 2 * PEER_HEADS * PEER_HALF_DIM
    sk = sub_keys.reshape(2 * PEER_HEADS, PEER_N_KEYS, PEER_HALF_DIM).astype(jnp.bfloat16)
    ids_t, gates_t = pl.pallas_call(
        _peer_route_body,
        grid=(n // tn,),
        in_specs=[
            pl.BlockSpec((tn, d), lambda i: (i, 0)),
            pl.BlockSpec((d, n_q), lambda i: (0, 0)),
            pl.BlockSpec((2 * PEER_HEADS, PEER_N_KEYS, PEER_HALF_DIM), lambda i: (0, 0, 0)),
        ],
        out_specs=[
            pl.BlockSpec((PEER_HEADS, PEER_TOPK, tn), lambda i: (0, 0, i)),
            pl.BlockSpec((PEER_HEADS, PEER_TOPK, tn), lambda i: (0, 0, i)),
        ],
        out_shape=[
            jax.ShapeDtypeStruct((PEER_HEADS, PEER_TOPK, n), jnp.int32),
            jax.ShapeDtypeStruct((PEER_HEADS, PEER_TOPK, n), f32),
        ],
        scratch_shapes=[pltpu.VMEM((2 * PEER_HEADS, tn, PEER_HALF_DIM), jnp.bfloat16)],
        compiler_params=pltpu.CompilerParams(dimension_semantics=("arbitrary",)),
        name="peer_route",
    )(xn, wq.astype(jnp.bfloat16), sk)
    to_rows = lambda a: a.reshape(PEER_PICKS, n).T
    return to_rows(ids_t), to_rows(gates_t)


def peer_ffn_tokens(xn, wq, sub_keys, uv):
    ids, gates = peer_route(xn, wq, sub_keys)
    return peer_experts(xn, ids, gates, uv)


def kernel(x_prompt, x_sample, cache_cmp_kv, cache_slc_kv, cache_win_kv, state_mlstm_c, state_mlstm_n,
           state_mlstm_m, cache_mem_kv, page_table, mem_prompt, g_mix, w_in, b_in, b_forget, ml_head_gain,
           cmp_pe, cmp_w1, cmp_b1, cmp_w2, w_out, g_xattn, g_mem, w_xq, w_xk, w_xv, w_xo, g_ffn, peer_wq,
           peer_sub_keys, peer_u, peer_v, g_final):
    B, T = x_prompt.shape[:2]
    S = x_sample.shape[1]
    past_len = page_table.shape[1] * PAGE_SIZE
    xp, xs = x_prompt, x_sample
    l = 0
    xn = rmsnorm(xp, g_mix[l])
    mq, mk, mv, mo, mi, mf, nq, nkv, ng = mixer_projections(xn, w_in[l], b_in[l], b_forget[l])
    h_ml, p_c, p_n, p_m = mlstm_mixer(
        mq, mk, mv, mi, mf, mo, ml_head_gain[l],
        jnp.zeros((B, ML_HEADS, ML_HEAD_DIM, ML_HEAD_DIM), f32),
        jnp.zeros((B, ML_HEADS, ML_HEAD_DIM), f32),
        jnp.full((B, ML_HEADS), -jnp.inf, f32))
    p_cmp, p_slc, win_rows = nkv[:, :, 0:2], nkv[:, :, 2:4], nkv[:, :, 4:6]
    bg = lambda a: jnp.swapaxes(a, 1, 2)
    kc, _ = nsa_compress(bg(p_cmp[:, :, 0]), cmp_pe[l, 0], cmp_w1[l, 0], cmp_b1[l, 0], cmp_w2[l, 0])
    vc, _ = nsa_compress(bg(p_cmp[:, :, 1]), cmp_pe[l, 1], cmp_w1[l, 1], cmp_b1[l, 1], cmp_w2[l, 1])
    h_nsa = nsa_prompt(nq.reshape(B, T, NSA_WIDTH), ng.reshape(B, T, 3 * NSA_HEADS),
                       nkv.reshape(B, T, 6 * NSA_KV_LANES), kc, vc)
    xp = xp + (jnp.concatenate([h_ml, h_nsa.astype(h_ml.dtype)], -1) @ w_out[l]).astype(xp.dtype)
    p_mem = memory_kv(mem_prompt, g_mem[l], w_xk[l], w_xv[l])
    xp = xp + cross_attend(rmsnorm(xp, g_xattn[l]), p_mem, w_xq[l], w_xo[l])
    p_win = win_rows[:, T - min(WINDOW, T):]

    xn = rmsnorm(xs, g_mix[l])
    mq, mk, mv, mo, mi, mf, nq, nkv, ng = mixer_projections(xn, w_in[l], b_in[l], b_forget[l])
    h_ml, s_c, s_n, s_m = mlstm_mixer(mq, mk, mv, mi, mf, mo, ml_head_gain[l],
                                      state_mlstm_c[l], state_mlstm_n[l], state_mlstm_m[l])
    s_cmp, s_slc, win_rows = nkv[:, :, 0:2], nkv[:, :, 2:4], nkv[:, :, 4:6]
    cmp_full = jnp.concatenate([gather_pages(cache_cmp_kv[l], page_table), s_cmp], axis=1)
    slc_full = jnp.concatenate([gather_pages(cache_slc_kv[l], page_table), s_slc], axis=1)
    win_buf = cache_win_kv[l]
    win_ext = jnp.concatenate([win_buf, win_rows], axis=1)
    h_nsa = nsa_mixer(nq, ng, past_len, cmp_full, slc_full, win_ext, past_len - win_buf.shape[1],
                      cmp_pe[l], cmp_w1[l], cmp_b1[l], cmp_w2[l])
    xs = xs + (jnp.concatenate([h_ml, h_nsa.astype(h_ml.dtype)], -1) @ w_out[l]).astype(xs.dtype)
    xs = xs + cross_attend(rmsnorm(xs, g_xattn[l]), cache_mem_kv[l], w_xq[l], w_xo[l])
    w_keep = min(WINDOW, past_len + S)
    s_win = win_ext[:, win_ext.shape[1] - w_keep:]

    n_p, n_s = B * T, xs.shape[0] * S
    x_all = jnp.concatenate([xp.reshape(n_p, D_MODEL), xs.reshape(n_s, D_MODEL)], 0)
    uv = jnp.concatenate([peer_u[l], peer_v[l]], axis=1).reshape(-1, PEER_SLAB_ROWS, PEER_SLAB_LANES)
    ffn = peer_ffn_tokens(rmsnorm(x_all, g_ffn[l]), peer_wq[l], peer_sub_keys[l], uv)
    xp = xp + ffn[:n_p].reshape(xp.shape)
    xs = xs + ffn[n_p:].reshape(xs.shape)

    y_prompt = rmsnorm_pallas(xp, g_final)
    y_sample = rmsnorm_pallas(xs, g_final)
    st = lambda a: a[None]
    return (y_prompt, y_sample,
            st(p_cmp), st(p_slc), st(p_win), st(p_c), st(p_n), st(p_m), st(p_mem),
            st(s_cmp), st(s_slc), st(s_win), st(s_c), st(s_n), st(s_m))
```

```python
import functools

import jax
import jax.numpy as jnp
from jax import lax
import numpy as np
from jax.experimental import pallas as pl
from jax.experimental.pallas import tpu as pltpu

D_MODEL = 1024
DEPTH = 1
PAGE_SIZE = 128

ML_WIDTH = D_MODEL // 2
ML_HEADS = 4
ML_HEAD_DIM = ML_WIDTH // ML_HEADS
ML_CHUNK = 128
NSA_WIDTH = D_MODEL - ML_WIDTH
NSA_HEADS = 8
NSA_HEAD_DIM = NSA_WIDTH // NSA_HEADS
NSA_KV_HEADS = 2
NSA_GROUP = NSA_HEADS // NSA_KV_HEADS
CMP_BLOCK = 32
CMP_STRIDE = 16
SEL_BLOCK = 64
N_SEL = 16
WINDOW = 512
Q_BLOCK = 128
FORCED_SCORE = 1.0e4
INVALID_SCORE = -1.0
MEM_LEN = 256
MEM_HEADS = 4
MEM_HEAD_DIM = D_MODEL // MEM_HEADS
PEER_HEADS = 8
PEER_N_KEYS = 128
PEER_N_EXPERTS = PEER_N_KEYS * PEER_N_KEYS
PEER_TOPK = 16
PEER_QUERY_DIM = 256
PEER_TOKEN_BLOCK = 128
EPS = 1e-6
TINY = 1e-30
IN_SPLITS = (ML_WIDTH, ML_WIDTH, ML_WIDTH, ML_WIDTH, ML_HEADS, ML_HEADS,
             NSA_WIDTH, 6 * NSA_KV_HEADS * NSA_HEAD_DIM, 3 * NSA_HEADS)
IN_COLS = sum(IN_SPLITS)
IN_OFFSETS = tuple(int(o) for o in np.cumsum(IN_SPLITS)[:-1])

f32 = jnp.float32


def rmsnorm(x, g):
    xf = x.astype(f32)
    return (xf * lax.rsqrt(jnp.mean(xf * xf, -1, keepdims=True) + EPS) * g).astype(x.dtype)


ROW_TILE = 256
DENSE_VMEM_BYTES = 56 * 1024 * 1024


def _rms(x, g):
    return x * lax.rsqrt(jnp.mean(x * x, -1, keepdims=True) + EPS) * g


def _add_norm_body(x_ref, r_ref, g_ref, o_ref):
    o_ref[...] = _rms(x_ref[...] + r_ref[...], g_ref[...])


def add_rmsnorm(x, r, g):
    shape = x.shape
    d = shape[-1]
    x2, r2 = x.reshape(-1, d), r.reshape(-1, d)
    n = x2.shape[0]
    tm = min(n, ROW_TILE)
    rows = pl.BlockSpec((tm, d), lambda i: (i, 0))
    out = pl.pallas_call(
        _add_norm_body,
        grid=(n // tm,),
        in_specs=[rows, rows, pl.BlockSpec((1, d), lambda i: (0, 0))],
        out_specs=rows,
        out_shape=jax.ShapeDtypeStruct((n, d), f32),
        name="add_rmsnorm",
    )(x2, r2, g.reshape(1, d))
    return out.reshape(shape)


def _linear_body(*refs, pre_norm, has_bias, has_res, post_norm, splits):
    it = iter(refs)
    x_ref, w_ref = next(it), next(it)
    x = x_ref[...]
    if pre_norm:
        x = _rms(x, next(it)[...])
    y = jnp.dot(x.astype(jnp.bfloat16), w_ref[...], preferred_element_type=f32)
    if has_bias:
        y = y + next(it)[...]
    if has_res:
        y = y + next(it)[...]
    post_gain = next(it)[...] if post_norm else None
    off = 0
    for m in splits:
        next(it)[...] = y[:, off:off + m]
        off += m
    if post_norm:
        next(it)[...] = _rms(y, post_gain)


def fused_linear(x, w, *, pre_gain=None, bias=None, residual=None, post_gain=None, splits=None):
    n, k = x.shape
    m = w.shape[1]
    splits = (m,) if splits is None else tuple(splits)
    assert sum(splits) == m and (post_gain is None or len(splits) == 1)
    tm = min(n, ROW_TILE)
    assert n % tm == 0
    row = lambda c: pl.BlockSpec((tm, c), lambda i: (i, 0))
    const = lambda r, c: pl.BlockSpec((r, c), lambda i: (0, 0))
    args, specs = [x, w.astype(jnp.bfloat16)], [row(k), const(k, m)]
    if pre_gain is not None:
        args.append(pre_gain.reshape(1, k)); specs.append(const(1, k))
    if bias is not None:
        args.append(bias.reshape(1, m)); specs.append(const(1, m))
    if residual is not None:
        args.append(residual); specs.append(row(m))
    if post_gain is not None:
        args.append(post_gain.reshape(1, m)); specs.append(const(1, m))
    out_cols = splits + ((m,) if post_gain is not None else ())
    outs = pl.pallas_call(
        functools.partial(_linear_body, pre_norm=pre_gain is not None, has_bias=bias is not None,
                          has_res=residual is not None, post_norm=post_gain is not None, splits=splits),
        grid=(n // tm,),
        in_specs=specs,
        out_specs=[row(c) for c in out_cols],
        out_shape=[jax.ShapeDtypeStruct((n, c), f32) for c in out_cols],
        compiler_params=pltpu.CompilerParams(dimension_semantics=("arbitrary",),
                                             vmem_limit_bytes=DENSE_VMEM_BYTES),
        name="fused_linear",
    )(*args)
    return outs[0] if len(outs) == 1 else tuple(outs)


def _xattn_body(q_ref, kv_ref, o_ref):
    bf16 = jnp.bfloat16
    d = MEM_HEAD_DIM
    q = q_ref[0]
    kv = kv_ref[0].astype(bf16)
    for h in range(MEM_HEADS):
        k_h = kv[:, h * d:(h + 1) * d]
        v_h = kv[:, (MEM_HEADS + h) * d:(MEM_HEADS + h + 1) * d]
        s = lax.dot_general(q[:, h * d:(h + 1) * d].astype(bf16), k_h, (((1,), (1,)), ((), ())),
                            preferred_element_type=f32) * (d ** -0.5)
        e = jnp.exp(s - jnp.max(s, axis=-1, keepdims=True))
        p = e / jnp.sum(e, axis=-1, keepdims=True)
        o_ref[0, :, h * d:(h + 1) * d] = jnp.dot(p.astype(bf16), v_h, preferred_element_type=f32)


def cross_attention(q, mem_kv):
    B, T, w = q.shape
    M = mem_kv.shape[1]
    kv = mem_kv.reshape(B, M, 2 * w)
    tm = min(T, ROW_TILE)
    assert T % tm == 0 and tm % 8 == 0
    return pl.pallas_call(
        _xattn_body,
        grid=(B, T // tm),
        in_specs=[pl.BlockSpec((1, tm, w), lambda b, i: (b, i, 0)),
                  pl.BlockSpec((1, M, 2 * w), lambda b, i: (b, 0, 0))],
        out_specs=pl.BlockSpec((1, tm, w), lambda b, i: (b, i, 0)),
        out_shape=jax.ShapeDtypeStruct((B, T, w), f32),
        compiler_params=pltpu.CompilerParams(dimension_semantics=("arbitrary", "arbitrary")),
        name="cross_attention",
    )(q, kv)


def masked_softmax(s, mask):
    s = jnp.where(mask, s.astype(f32), -jnp.inf)
    mx = jnp.max(s, -1, keepdims=True)
    mx = jnp.where(jnp.isfinite(mx), mx, 0.0)
    e = jnp.exp(s - mx)
    return e / jnp.maximum(e.sum(-1, keepdims=True), TINY)


def mixer_projections(xn, w_in, b_in, b_forget):
    B, T = xn.shape[:2]
    p = xn @ w_in + b_in
    ml_q, ml_k, ml_v, ml_o, ml_i, ml_f, nq, nkv, ng = jnp.split(p, IN_OFFSETS, axis=-1)
    hd = (B, T, ML_HEADS, ML_HEAD_DIM)
    return (ml_q.reshape(hd), ml_k.reshape(hd), ml_v.reshape(hd), ml_o, ml_i, ml_f + b_forget,
            nq.reshape(B, T, NSA_HEADS, NSA_HEAD_DIM),
            nkv.reshape(B, T, 6, NSA_KV_HEADS, NSA_HEAD_DIM),
            jax.nn.sigmoid(ng).reshape(B, T, NSA_HEADS, 3))


def mlstm_chunk(carry, inp):
    c, n, m = carry
    q, k, v, ig, lf = inp
    L = q.shape[2]
    b = jnp.cumsum(lf, axis=-1)
    causal = jnp.tril(jnp.ones((L, L), bool))
    log_d = jnp.where(causal, b[..., :, None] - b[..., None, :] + ig[..., None, :], -jnp.inf)
    inter = b + m[..., None]
    m_t = jnp.maximum(inter, log_d.max(-1))
    d_mat = jnp.exp(log_d - m_t[..., None])
    a = jnp.exp(inter - m_t)
    qk = jnp.einsum('bhtd,bhsd->bhts', q, k) * d_mat
    num = a[..., None] * jnp.einsum('bhtd,bhde->bhte', q, c) + jnp.einsum('bhts,bhse->bhte', qk, v)
    den = a * jnp.einsum('bhtd,bhd->bht', q, n) + qk.sum(-1)
    h = num / jnp.maximum(jnp.abs(den), jnp.exp(-m_t))[..., None]
    m_new = m_t[..., -1]
    w = jnp.exp(b[..., -1:] - b + ig - m_new[..., None])
    decay = jnp.exp(b[..., -1] + m - m_new)
    c_new = decay[..., None, None] * c + jnp.einsum('bhs,bhsd,bhse->bhde', w, k, v)
    n_new = decay[..., None] * n + jnp.einsum('bhs,bhsd->bhd', w, k)
    return (c_new, n_new, m_new), h


def mlstm_mixer(q, k, v, i_pre, f_pre, o_pre, head_gain, c0, n0, m0):
    B, T, H, d = q.shape
    L = ML_CHUNK if T % ML_CHUNK == 0 else T
    nc = T // L

    def chunks(a):
        a = a.astype(f32).reshape((B, nc, L) + a.shape[2:])
        return jnp.transpose(a, (1, 0, 3, 2) + tuple(range(4, a.ndim)))

    xs = (chunks(q), chunks(k * d ** -0.5), chunks(v), chunks(i_pre),
          chunks(jax.nn.log_sigmoid(f_pre.astype(f32))))
    (c, n, m), h = lax.scan(mlstm_chunk, (c0.astype(f32), n0.astype(f32), m0.astype(f32)), xs)
    h = jnp.transpose(h, (1, 0, 3, 2, 4)).reshape(B, T, H, d)
    h = jax.nn.sigmoid(o_pre.astype(f32)).reshape(B, T, H, d) * h
    h = h * lax.rsqrt(jnp.mean(h * h, -1, keepdims=True) + EPS) * head_gain
    return h.reshape(B, T, H * d).astype(q.dtype), c, n, m


def nsa_compress(rows, pe, w1, b1, w2):
    B, G, T, d = rows.shape
    n_seg = T // CMP_STRIDE
    seg = rows[:, :, :n_seg * CMP_STRIDE].reshape(B, G, n_seg, CMP_STRIDE * d)
    half = CMP_STRIDE * d
    pre = seg[:, :, :-1] @ w1[:half] + seg[:, :, 1:] @ w1[half:] + (pe.reshape(-1) @ w1 + b1)
    ends = jnp.arange(n_seg - 1) * CMP_STRIDE + (CMP_BLOCK - 1)
    return jax.nn.gelu(pre) @ w2, ends


def to_sel_blocks(rows):
    B, G, T, d = rows.shape
    nsb = -(-T // SEL_BLOCK)
    rows = jnp.pad(rows, ((0, 0), (0, 0), (0, nsb * SEL_BLOCK - T), (0, 0)))
    return rows.reshape(B, G, nsb, SEL_BLOCK, d)


def cmp_to_sel(imp, nsb):
    r = SEL_BLOCK // CMP_STRIDE
    nc = imp.shape[-1]
    lead = imp.shape[:-1]
    tot = r * nsb
    padw = [(0, 0)] * len(lead)
    first = jnp.pad(imp, padw + [(0, tot - nc)]).reshape(lead + (nsb, r)).sum(-1)
    second = jnp.pad(imp, padw + [(1, tot - nc - 1)]).reshape(lead + (nsb, r)).sum(-1)
    return 0.5 * (first + second)


def nsa_attend(q, gates, q_pos, kc, vc, c_end, ks, vs, kw, vw, w_pos):
    B, G, R, Q, d = q.shape
    scale = d ** -0.5
    p_c = masked_softmax(jnp.einsum('bgrqd,bgnd->bgrqn', q, kc) * scale, c_end[None, :] <= q_pos[:, None])
    o_c = jnp.einsum('bgrqn,bgnd->bgrqd', p_c, vc)
    nsb = ks.shape[2]
    imp = cmp_to_sel(p_c.sum(2), nsb)
    blk = jnp.arange(nsb)
    valid = blk[None, :] * SEL_BLOCK <= q_pos[:, None]
    forced = (blk[None, :] == 0) | (blk[None, :] == q_pos[:, None] // SEL_BLOCK)
    score = jnp.where(forced, FORCED_SCORE, jnp.where(valid, imp, INVALID_SCORE))
    _, idx = lax.top_k(score, min(N_SEL, nsb))
    bi = jnp.arange(B)[:, None, None, None]
    gi = jnp.arange(G)[None, :, None, None]
    kg = ks[bi, gi, idx].reshape(B, G, Q, -1, d)
    vg = vs[bi, gi, idx].reshape(B, G, Q, -1, d)
    k_pos = (idx[..., None] * SEL_BLOCK + jnp.arange(SEL_BLOCK)).reshape(B, G, Q, -1)
    p_s = masked_softmax(jnp.einsum('bgrqd,bgqsd->bgrqs', q, kg) * scale,
                         (k_pos <= q_pos[None, None, :, None])[:, :, None])
    o_s = jnp.einsum('bgrqs,bgqsd->bgrqd', p_s, vg)
    wmask = ((w_pos[None, :] <= q_pos[:, None]) & (w_pos[None, :] >= q_pos[:, None] - WINDOW)
             & (w_pos[None, :] >= 0))
    p_w = masked_softmax(jnp.einsum('bgrqd,bgkd->bgrqk', q, kw) * scale, wmask)
    o_w = jnp.einsum('bgrqk,bgkd->bgrqd', p_w, vw)
    return gates[..., 0:1] * o_c + gates[..., 1:2] * o_s + gates[..., 2:3] * o_w


def nsa_mixer(q, gates, q_pos0, cmp_rows, slc_rows, win_rows, win_pos0, cmp_pe, cmp_w1, cmp_b1, cmp_w2):
    B, Tq, _, d = q.shape
    G, R = NSA_KV_HEADS, NSA_GROUP
    bg = lambda a: jnp.swapaxes(a, 1, 2)
    kc, c_end = nsa_compress(bg(cmp_rows[:, :, 0]), cmp_pe[0], cmp_w1[0], cmp_b1[0], cmp_w2[0])
    vc, _ = nsa_compress(bg(cmp_rows[:, :, 1]), cmp_pe[1], cmp_w1[1], cmp_b1[1], cmp_w2[1])
    ks = to_sel_blocks(bg(slc_rows[:, :, 0]))
    vs = to_sel_blocks(bg(slc_rows[:, :, 1]))
    kw = bg(win_rows[:, :, 0])
    vw = bg(win_rows[:, :, 1])
    qg = q.reshape(B, Tq, G, R, d).transpose(0, 2, 3, 1, 4)
    gg = gates.reshape(B, Tq, G, R, 3).transpose(0, 2, 3, 1, 4)
    qb = Q_BLOCK if Tq % Q_BLOCK == 0 else Tq
    nb = Tq // qb
    if nb == 1:
        w_pos = win_pos0 + jnp.arange(kw.shape[2])
        o = nsa_attend(qg, gg, q_pos0 + jnp.arange(Tq), kc, vc, c_end, ks, vs, kw, vw, w_pos)
    else:
        off = q_pos0 - WINDOW - win_pos0
        span = WINDOW + qb

        def block(args):
            q_blk, g_blk, j = args
            start = j * qb
            kw_b = lax.dynamic_slice_in_dim(kw, start + off, span, axis=2)
            vw_b = lax.dynamic_slice_in_dim(vw, start + off, span, axis=2)
            w_pos = q_pos0 + start - WINDOW + jnp.arange(span)
            return nsa_attend(q_blk, g_blk, q_pos0 + start + jnp.arange(qb), kc, vc, c_end,
                              ks, vs, kw_b, vw_b, w_pos)

        qs = qg.reshape(B, G, R, nb, qb, d).transpose(3, 0, 1, 2, 4, 5)
        gs = gg.reshape(B, G, R, nb, qb, 3).transpose(3, 0, 1, 2, 4, 5)
        o = lax.map(block, (qs, gs, jnp.arange(nb)))
        o = o.transpose(1, 2, 3, 0, 4, 5).reshape(B, G, R, Tq, d)
    return o.transpose(0, 3, 1, 2, 4).reshape(B, Tq, NSA_HEADS * d)


NSA_SEL_CHUNK = 512
NSA_WIN_CHUNKS = WINDOW // Q_BLOCK + 1
NSA_KV_LANES = NSA_KV_HEADS * NSA_HEAD_DIM


def _softmax_masked(s, mask):
    sm = jnp.where(mask, s, -jnp.inf)
    mx = jnp.max(sm, axis=-1, keepdims=True)
    mx = jnp.where(mx == -jnp.inf, 0.0, mx)
    e = jnp.exp(sm - mx)
    return e * (1.0 / jnp.maximum(jnp.sum(e, axis=-1, keepdims=True), TINY))


def _top_rows_mask(s, row_id, k):
    sel = jnp.zeros(s.shape, f32)
    sentinel = s.shape[0]
    for _ in range(k):
        m = jnp.max(s, axis=0, keepdims=True)
        first = jnp.min(jnp.where(s == m, row_id, sentinel), axis=0, keepdims=True)
        hit = row_id == first
        sel = jnp.where(hit, 1.0, sel)
        s = jnp.where(hit, -jnp.inf, s)
    return sel


def _nsa_prompt_body(q_ref, gate_ref, kct_ref, vc_ref, mt_ref, kts_ref, vs_ref, ktw_ref, vw_ref, o_ref):
    j = pl.program_id(1)
    Q, R, d = Q_BLOCK, NSA_GROUP, NSA_HEAD_DIM
    n_cmp = kct_ref.shape[-1]
    n_blk = mt_ref.shape[0]
    bf16 = jnp.bfloat16
    q_all = q_ref[0] * (d ** -0.5)
    gates = gate_ref[0]
    q_pos = j * Q + lax.broadcasted_iota(jnp.int32, (Q, 1), 0)
    q_pos_l = j * Q + lax.broadcasted_iota(jnp.int32, (1, Q), 1)
    blk_r = lax.broadcasted_iota(jnp.int32, (n_blk, 1), 0)
    blk_id = lax.broadcasted_iota(jnp.int32, (n_blk, Q), 0)
    nt = (((1,), (1,)), ((), ()))

    for g in range(NSA_KV_HEADS):
        lanes = slice(g * d, (g + 1) * d)
        qg = jnp.concatenate([q_all[:, (g * R + r) * d:(g * R + r + 1) * d] for r in range(R)], axis=0).astype(bf16)

        s = jnp.dot(qg, kct_ref[0, g], preferred_element_type=f32).reshape(R, Q, n_cmp)
        n_id = lax.broadcasted_iota(jnp.int32, (1, n_cmp), 1)
        cmask = (n_id * CMP_STRIDE + (CMP_BLOCK - 1) <= q_pos) & (n_id < n_cmp - 1)
        p = _softmax_masked(s, cmask[None])
        o_c = jnp.dot(p.reshape(R * Q, n_cmp).astype(bf16), vc_ref[0], preferred_element_type=f32)[:, lanes]

        p_sum = p[0] + p[1] + p[2] + p[3]
        hi = p_sum.astype(bf16)
        lo = (p_sum - hi.astype(f32)).astype(bf16)
        mt = mt_ref[...]
        imp_t = (lax.dot_general(mt, hi, nt, preferred_element_type=f32)
                 + lax.dot_general(mt, lo, nt, preferred_element_type=f32))
        valid = blk_r * SEL_BLOCK <= q_pos_l
        forced = (blk_r == 0) | (blk_r == jnp.right_shift(q_pos_l, 6))
        score = jnp.where(forced, FORCED_SCORE, jnp.where(valid, imp_t, INVALID_SCORE))
        sel = _top_rows_mask(score, blk_id, min(N_SEL, n_blk)).T.astype(bf16)

        kc = kts_ref.shape[-1]

        def chunk(c, carry):
            m, l, acc = carry
            kt = kts_ref[0, c, g * d:(g + 1) * d, :]
            sc = jnp.dot(qg, kt, preferred_element_type=f32).reshape(R, Q, kc)
            key = c * kc + lax.broadcasted_iota(jnp.int32, (1, kc), 1)
            expand = jnp.where(blk_r == jnp.right_shift(key, 6), 1.0, 0.0).astype(bf16)
            picked = jnp.dot(sel, expand, preferred_element_type=f32) > 0.5
            mask = picked & (key <= q_pos)
            sm = jnp.where(mask[None], sc, -jnp.inf)
            m_new = jnp.maximum(m, jnp.max(sm, axis=-1, keepdims=True))
            m_safe = jnp.where(m_new == -jnp.inf, 0.0, m_new)
            alpha = jnp.exp(m - m_safe)
            pe = jnp.exp(sm - m_safe)
            l = l * alpha + jnp.sum(pe, axis=-1, keepdims=True)
            pv = jnp.dot(pe.reshape(R * Q, kc).astype(bf16), vs_ref[0, c], preferred_element_type=f32)
            acc = acc * alpha.reshape(R * Q, 1) + pv
            return m_new, l, acc

        n_chunks = (j * Q + Q + kc - 1) // kc
        init = (jnp.full((R, Q, 1), -jnp.inf, f32), jnp.zeros((R, Q, 1), f32),
                jnp.zeros((R * Q, NSA_KV_LANES), f32))
        _, l_s, acc_s = lax.fori_loop(0, n_chunks, chunk, init)
        o_s = acc_s[:, lanes] * (1.0 / jnp.maximum(l_s.reshape(R * Q, 1), TINY))

        ss, vv = [], []
        for i in range(NSA_WIN_CHUNKS):
            cc = jnp.maximum(j - (NSA_WIN_CHUNKS - 1) + i, 0)
            ss.append(jnp.dot(qg, ktw_ref[0, cc, g * d:(g + 1) * d, :], preferred_element_type=f32))
            vv.append(vw_ref[0, cc])
        span = NSA_WIN_CHUNKS * Q
        sw = jnp.concatenate(ss, axis=1).reshape(R, Q, span)
        k_pos = (j - (NSA_WIN_CHUNKS - 1)) * Q + lax.broadcasted_iota(jnp.int32, (1, span), 1)
        wmask = (k_pos >= 0) & (k_pos <= q_pos) & (k_pos >= q_pos - WINDOW)
        pw = _softmax_masked(sw, wmask[None])
        o_w = jnp.dot(pw.reshape(R * Q, span).astype(bf16), jnp.concatenate(vv, axis=0),
                      preferred_element_type=f32)[:, lanes]

        for r in range(R):
            h = g * R + r
            rows = slice(r * Q, (r + 1) * Q)
            o_ref[0, :, h * d:(h + 1) * d] = (gates[:, 3 * h:3 * h + 1] * o_c[rows]
                                              + gates[:, 3 * h + 1:3 * h + 2] * o_s[rows]
                                              + gates[:, 3 * h + 2:3 * h + 3] * o_w[rows])


def _cmp_to_sel_matrix(n_cmp, n_blk):
    r = SEL_BLOCK // CMP_STRIDE
    n = np.arange(n_cmp)
    b = np.arange(n_blk)[:, None]
    m = 0.5 * ((n // r == b).astype(np.float32) + ((n + 1) // r == b).astype(np.float32))
    m[:, n_cmp - 1] = 0.0
    return jnp.asarray(m, jnp.bfloat16)


def nsa_prompt(nq, gates, nkv, kc, vc):
    B, T, _ = nq.shape
    bf16 = jnp.bfloat16
    G, d, w = NSA_KV_HEADS, NSA_HEAD_DIM, NSA_KV_LANES
    n_cmp, n_blk = T // CMP_STRIDE, T // SEL_BLOCK
    kc_s, kc_w = NSA_SEL_CHUNK, Q_BLOCK
    pad = ((0, 0), (0, 0), (0, 1), (0, 0))
    kct = jnp.pad(kc, pad).transpose(0, 1, 3, 2).astype(bf16)
    vc2 = jnp.pad(vc, pad).transpose(0, 2, 1, 3).reshape(B, n_cmp, w).astype(bf16)
    chunks = lambda a, c: a.reshape(B, T // c, c, w).astype(bf16)
    kts = chunks(nkv[..., 2 * w:3 * w], kc_s).transpose(0, 1, 3, 2)
    vs = chunks(nkv[..., 3 * w:4 * w], kc_s)
    ktw = chunks(nkv[..., 4 * w:5 * w], kc_w).transpose(0, 1, 3, 2)
    vw = chunks(nkv[..., 5 * w:6 * w], kc_w)
    whole = lambda a: pl.BlockSpec((1,) + a.shape[1:], lambda b, j: (b,) + (0,) * (a.ndim - 1))
    mt = _cmp_to_sel_matrix(n_cmp, n_blk)
    return pl.pallas_call(
        _nsa_prompt_body,
        grid=(B, T // Q_BLOCK),
        in_specs=[
            pl.BlockSpec((1, Q_BLOCK, nq.shape[-1]), lambda b, j: (b, j, 0)),
            pl.BlockSpec((1, Q_BLOCK, gates.shape[-1]), lambda b, j: (b, j, 0)),
            whole(kct), whole(vc2),
            pl.BlockSpec(mt.shape, lambda b, j: (0, 0)),
            whole(kts), whole(vs), whole(ktw), whole(vw),
        ],
        out_specs=pl.BlockSpec((1, Q_BLOCK, nq.shape[-1]), lambda b, j: (b, j, 0)),
        out_shape=jax.ShapeDtypeStruct(nq.shape, f32),
        compiler_params=pltpu.CompilerParams(dimension_semantics=("arbitrary", "arbitrary"),
                                             vmem_limit_bytes=48 * 1024 * 1024),
        name="nsa_prompt",
    )(nq, gates, kct, vc2, mt, kts, vs, ktw, vw)


def gather_pages(pool, page_table):
    g = pool[page_table]
    return g.reshape((g.shape[0], g.shape[1] * g.shape[2]) + g.shape[3:])


def memory_kv(mem, g_mem, w_k, w_v):
    mn = rmsnorm(mem, g_mem)
    B, M = mem.shape[:2]
    k = (mn @ w_k).reshape(B, M, MEM_HEADS, MEM_HEAD_DIM)
    v = (mn @ w_v).reshape(B, M, MEM_HEADS, MEM_HEAD_DIM)
    return jnp.stack([k, v], axis=2)


def cross_attend(xn, mem_kv, w_q, w_o):
    B, T = xn.shape[:2]
    q = (xn @ w_q).reshape(B, T, MEM_HEADS, MEM_HEAD_DIM)
    s = jnp.einsum('bthd,bmhd->bhtm', q, mem_kv[:, :, 0]) * MEM_HEAD_DIM ** -0.5
    p = jax.nn.softmax(s.astype(f32), axis=-1)
    o = jnp.einsum('bhtm,bmhd->bthd', p, mem_kv[:, :, 1].astype(f32)).reshape(B, T, -1)
    return (o.astype(xn.dtype) @ w_o).astype(xn.dtype)


PEER_PICKS = PEER_HEADS * PEER_TOPK
PEER_TOKENS_PER_STEP = 64
PEER_ROW_BUFFERS = 8
PEER_SLAB_LANES = 128
PEER_SLAB_ROWS = 2 * D_MODEL // PEER_SLAB_LANES


def _gelu_tanh(x):
    return 0.5 * x * (1.0 + jnp.tanh(0.7978845608028654 * (x + 0.044715 * x * x * x)))


def _peer_expert_body(ids_ref, ids_next_ref, x_ref, g_ref, seg_ref, uv_ref, o_ref, rows, sems):
    tokens = x_ref.shape[0]
    depth = PEER_ROW_BUFFERS
    half, lanes = PEER_SLAB_ROWS // 2, PEER_SLAB_LANES
    cols = PEER_PICKS * half
    seg = seg_ref.shape[0]
    nt = (((1,), (1,)), ((), ()))
    bf16 = jnp.bfloat16

    col_row = lax.broadcasted_iota(jnp.int32, (half, cols), 1) & (half - 1)
    sub = lax.broadcasted_iota(jnp.int32, (half, cols), 0)
    diag = jnp.where(col_row == sub, 1.0, 0.0)

    group = depth // 2
    groups = tokens // group
    n_seg = cols // seg
    step_id = pl.program_id(0)
    last_step = pl.num_programs(0) - 1

    def wait(slot):
        pltpu.make_async_copy(uv_ref.at[pl.ds(0, PEER_PICKS)], rows.at[slot], sems.at[slot]).wait()

    def fetcher(ids, t0, slot0):
        per = PEER_PICKS // 2

        def fetch(c):
            j, h = divmod(c, 2)
            for k in range(h * per, (h + 1) * per):
                pltpu.make_async_copy(uv_ref.at[ids[t0 + j, k]], rows.at[slot0 + j, k],
                                      sems.at[slot0 + j]).start(priority=k % 2)
        return fetch

    def mix_group(t0, slot0, fetch):
        parts = []
        for j in range(group):
            u_rows = rows[slot0 + j, :, :half, :].reshape(cols, lanes).astype(bf16)
            prod = lax.dot_general(x_ref[t0 + j].astype(bf16), u_rows, nt, preferred_element_type=f32)
            part = jnp.sum(prod * diag, axis=0, keepdims=True)
            parts += [part[:, i * seg:(i + 1) * seg] for i in range(n_seg)]
            fetch(j)
        part = jnp.concatenate(parts, axis=0)
        hi = part.astype(bf16)
        lo = (part - hi.astype(f32)).astype(bf16)
        ones = seg_ref[...]
        act = jnp.dot(hi, ones, preferred_element_type=f32) + jnp.dot(lo, ones, preferred_element_type=f32)
        for j in range(group):
            w = g_ref[t0 + j] * _gelu_tanh(act[j * n_seg:(j + 1) * n_seg])
            w = jnp.concatenate([jnp.broadcast_to(w[i:i + 1, :], (half, seg)) for i in range(n_seg)], axis=1)
            v_rows = rows[slot0 + j, :, half:, :].reshape(cols, lanes).astype(bf16)
            o_ref[t0 + j] = jnp.dot((w * diag).astype(bf16), v_rows, preferred_element_type=f32)
            fetch(group + j)

    @pl.when(step_id == 0)
    def _():
        first = fetcher(ids_ref, 0, 0)
        for c in range(2 * group):
            first(c)

    def pair(it, last):
        for side in range(2):
            g = 2 * it + side
            for j in range(group):
                wait(side * group + j)
            if side == 1 and last:
                fetch = fetcher(ids_next_ref, 0, 0)
            else:
                fetch = fetcher(ids_ref, (g + 1) * group, (1 - side) * group)
            mix_group(g * group, side * group, fetch)

    def body(it, carry):
        pair(it, False)
        return carry

    lax.fori_loop(0, groups // 2 - 1, body, 0)
    pair(groups // 2 - 1, True)

    @pl.when(step_id == last_step)
    def _():
        for j in range(group):
            wait(j)


def peer_experts(xn, ids, gates, uv):
    n, d = xn.shape
    tb = PEER_TOKENS_PER_STEP
    slab, half, lanes = PEER_SLAB_ROWS, PEER_SLAB_ROWS // 2, PEER_SLAB_LANES
    seg = 2 * lanes
    cols = PEER_PICKS * half
    assert n % tb == 0 and tb % PEER_ROW_BUFFERS == 0 and half * lanes == d and cols % seg == 0
    same_pick = np.arange(seg)[:, None] // half == np.arange(seg)[None, :] // half
    out = pl.pallas_call(
        _peer_expert_body,
        grid=(n // tb,),
        in_specs=[
            pl.BlockSpec((tb, PEER_PICKS), lambda i: (i, 0), memory_space=pltpu.SMEM),
            pl.BlockSpec((tb, PEER_PICKS), lambda i: (jnp.minimum(i + 1, n // tb - 1), 0), memory_space=pltpu.SMEM),
            pl.BlockSpec((tb, half, lanes), lambda i: (i, 0, 0)),
            pl.BlockSpec((tb, cols // seg, seg), lambda i: (i, 0, 0)),
            pl.BlockSpec((seg, seg), lambda i: (0, 0)),
            pl.BlockSpec(memory_space=pl.ANY),
        ],
        out_specs=pl.BlockSpec((tb, half, lanes), lambda i: (i, 0, 0)),
        out_shape=jax.ShapeDtypeStruct((n, half, lanes), f32),
        scratch_shapes=[
            pltpu.VMEM((PEER_ROW_BUFFERS, PEER_PICKS, slab, lanes), f32),
            pltpu.SemaphoreType.DMA((PEER_ROW_BUFFERS,)),
        ],
        compiler_params=pltpu.CompilerParams(dimension_semantics=("arbitrary",)),
        name="peer_experts",
    )(ids, ids, xn.reshape(n, half, lanes), jnp.repeat(gates, half, axis=1).reshape(n, cols // seg, seg),
      jnp.asarray(same_pick, jnp.bfloat16), uv)
    return out.reshape(n, d)


PEER_ROUTE_TOKENS = 128
PEER_HALF_DIM = PEER_QUERY_DIM // 2


def _top_rows(s, row_id, k, payload=None):
    vals, picks = [], []
    sentinel = s.shape[0]
    for _ in range(k):
        m = jnp.max(s, axis=0, keepdims=True)
        first = jnp.min(jnp.where(s == m, row_id, sentinel), axis=0, keepdims=True)
        hit = row_id == first
        vals.append(m)
        if payload is None:
            picks.append(first)
        else:
            picks.append(jnp.max(jnp.where(hit, payload, -1), axis=0, keepdims=True))
        s = jnp.where(hit, -jnp.inf, s)
    return jnp.concatenate(vals, 0), jnp.concatenate(picks, 0)


def _peer_route_body(x_ref, wq_ref, sk_ref, ids_ref, gate_ref, q_scr):
    tn = x_ref.shape[0]
    q = jnp.dot(x_ref[...].astype(jnp.bfloat16), wq_ref[...], preferred_element_type=f32)
    for j in range(2 * PEER_HEADS):
        q_scr[j] = q[:, j * PEER_HALF_DIM:(j + 1) * PEER_HALF_DIM].astype(jnp.bfloat16)
    key_id = lax.broadcasted_iota(jnp.int32, (PEER_N_KEYS, tn), 0)
    cand_id = lax.broadcasted_iota(jnp.int32, (PEER_TOPK * PEER_TOPK, tn), 0)

    def head(h, carry):
        tops = []
        for p in range(2):
            s = lax.dot_general(sk_ref[2 * h + p], q_scr[2 * h + p], (((1,), (1,)), ((), ())),
                                preferred_element_type=f32)
            tops.append(_top_rows(s, key_id, PEER_TOPK))
        (v0, i0), (v1, i1) = tops
        cand_s = (v0[:, None, :] + v1[None, :, :]).reshape(PEER_TOPK * PEER_TOPK, tn)
        cand_e = (i0[:, None, :] * PEER_N_KEYS + i1[None, :, :]).reshape(PEER_TOPK * PEER_TOPK, tn)
        top_s, top_e = _top_rows(cand_s, cand_id, PEER_TOPK, payload=cand_e)
        e = jnp.exp(top_s - top_s[0:1])
        ids_ref[h] = top_e
        gate_ref[h] = e / jnp.sum(e, axis=0, keepdims=True)
        return carry

    lax.fori_loop(0, PEER_HEADS, head, 0)


def peer_route(xn, wq, sub_keys):
    n, d = xn.shape
    tn = PEER_ROUTE_TOKENS
    assert n % tn == 0
    n_q = 2 * PEER_HEADS * PEER_HALF_DIM
    sk = sub_keys.reshape(2 * PEER_HEADS, PEER_N_KEYS, PEER_HALF_DIM).astype(jnp.bfloat16)
    ids_t, gates_t = pl.pallas_call(
        _peer_route_body,
        grid=(n // tn,),
        in_specs=[
            pl.BlockSpec((tn, d), lambda i: (i, 0)),
            pl.BlockSpec((d, n_q), lambda i: (0, 0)),
            pl.BlockSpec((2 * PEER_HEADS, PEER_N_KEYS, PEER_HALF_DIM), lambda i: (0, 0, 0)),
        ],
        out_specs=[
            pl.BlockSpec((PEER_HEADS, PEER_TOPK, tn), lambda i: (0, 0, i)),
            pl.BlockSpec((PEER_HEADS, PEER_TOPK, tn), lambda i: (0, 0, i)),
        ],
        out_shape=[
            jax.ShapeDtypeStruct((PEER_HEADS, PEER_TOPK, n), jnp.int32),
            jax.ShapeDtypeStruct((PEER_HEADS, PEER_TOPK, n), f32),
        ],
        scratch_shapes=[pltpu.VMEM((2 * PEER_HEADS, tn, PEER_HALF_DIM), jnp.bfloat16)],
        compiler_params=pltpu.CompilerParams(dimension_semantics=("arbitrary",)),
        name="peer_route",
    )(xn, wq.astype(jnp.bfloat16), sk)
    to_rows = lambda a: a.reshape(PEER_PICKS, n).T
    return to_rows(ids_t), to_rows(gates_t)


def peer_ffn_tokens(xn, wq, sub_keys, uv):
    ids, gates = peer_route(xn, wq, sub_keys)
    return peer_experts(xn, ids, gates, uv)


def kernel(x_prompt, x_sample, cache_cmp_kv, cache_slc_kv, cache_win_kv, state_mlstm_c, state_mlstm_n,
           state_mlstm_m, cache_mem_kv, page_table, mem_prompt, g_mix, w_in, b_in, b_forget, ml_head_gain,
           cmp_pe, cmp_w1, cmp_b1, cmp_w2, w_out, g_xattn, g_mem, w_xq, w_xk, w_xv, w_xo, g_ffn, peer_wq,
           peer_sub_keys, peer_u, peer_v, g_final):
    B, T = x_prompt.shape[:2]
    DB, S = x_sample.shape[:2]
    past_len = page_table.shape[1] * PAGE_SIZE
    l = 0
    bg = lambda a: jnp.swapaxes(a, 1, 2)

    o_i, o_nq, o_nkv, o_ng = IN_OFFSETS[3], IN_OFFSETS[5], IN_OFFSETS[6], IN_OFFSETS[7]
    n_small = 2 * ML_HEADS + 3 * NSA_HEADS
    small_pad = 128 - n_small
    regroup = lambda a: jnp.concatenate(
        [a[..., :o_i], a[..., o_nq:o_nkv], a[..., o_nkv:o_ng], a[..., o_i:o_nq], a[..., o_ng:],
         jnp.zeros(a.shape[:-1] + (small_pad,), a.dtype)], axis=-1)
    w_in_g, b_in_g = regroup(w_in[l]), regroup(b_in[l])
    in_splits = (4 * ML_WIDTH, NSA_WIDTH, 6 * NSA_KV_LANES, n_small + small_pad)

    def project(x):
        b, t, _ = x.shape
        ml, nq, nkv, small = fused_linear(x.reshape(b * t, D_MODEL), w_in_g, pre_gain=g_mix[l], bias=b_in_g,
                                          splits=in_splits)
        hd = (b, t, ML_HEADS, ML_HEAD_DIM)
        part = lambda i: ml[:, i * ML_WIDTH:(i + 1) * ML_WIDTH]
        mi = small[:, :ML_HEADS].reshape(b, t, ML_HEADS)
        mf = small[:, ML_HEADS:2 * ML_HEADS].reshape(b, t, ML_HEADS) + b_forget[l]
        ng = jax.nn.sigmoid(small[:, 2 * ML_HEADS:n_small]).reshape(b, t, 3 * NSA_HEADS)
        return (part(0).reshape(hd), part(1).reshape(hd), part(2).reshape(hd), part(3).reshape(b, t, ML_WIDTH),
                mi, mf, nq.reshape(b, t, NSA_WIDTH), nkv.reshape(b, t, 6 * NSA_KV_LANES), ng)

    def after_mixers(x, h_ml, h_nsa, mem_kv):
        b, t, _ = x.shape
        h = jnp.concatenate([h_ml, h_nsa], -1).reshape(b * t, D_MODEL)
        x1, xn = fused_linear(h, w_out[l], residual=x.reshape(b * t, D_MODEL), post_gain=g_xattn[l])
        q = fused_linear(xn, w_xq[l]).reshape(b, t, D_MODEL)
        t_pad = -t % 8
        o = cross_attention(jnp.pad(q, ((0, 0), (0, t_pad), (0, 0))), mem_kv)[:, :t]
        return fused_linear(o.reshape(b * t, D_MODEL), w_xo[l], residual=x1, post_gain=g_ffn[l])

    mq, mk, mv, mo, mi, mf, nq, nkv, ng = project(x_prompt)
    h_ml, p_c, p_n, p_m = mlstm_mixer(
        mq, mk, mv, mi, mf, mo, ml_head_gain[l],
        jnp.zeros((B, ML_HEADS, ML_HEAD_DIM, ML_HEAD_DIM), f32),
        jnp.zeros((B, ML_HEADS, ML_HEAD_DIM), f32),
        jnp.full((B, ML_HEADS), -jnp.inf, f32))
    rows6 = nkv.reshape(B, T, 6, NSA_KV_HEADS, NSA_HEAD_DIM)
    p_cmp, p_slc, win_rows = rows6[:, :, 0:2], rows6[:, :, 2:4], rows6[:, :, 4:6]
    kc, _ = nsa_compress(bg(p_cmp[:, :, 0]), cmp_pe[l, 0], cmp_w1[l, 0], cmp_b1[l, 0], cmp_w2[l, 0])
    vc, _ = nsa_compress(bg(p_cmp[:, :, 1]), cmp_pe[l, 1], cmp_w1[l, 1], cmp_b1[l, 1], cmp_w2[l, 1])
    h_nsa = nsa_prompt(nq, ng, nkv, kc, vc)
    p_mem = fused_linear(mem_prompt.reshape(-1, D_MODEL), jnp.concatenate([w_xk[l], w_xv[l]], axis=1),
                         pre_gain=g_mem[l]).reshape(B, MEM_LEN, 2, MEM_HEADS, MEM_HEAD_DIM)
    xp, xp_ffn_in = after_mixers(x_prompt, h_ml, h_nsa, p_mem)
    p_win = win_rows[:, T - min(WINDOW, T):]

    mq, mk, mv, mo, mi, mf, nq, nkv, ng = project(x_sample)
    h_ml, s_c, s_n, s_m = mlstm_mixer(mq, mk, mv, mi, mf, mo, ml_head_gain[l],
                                      state_mlstm_c[l], state_mlstm_n[l], state_mlstm_m[l])
    rows6 = nkv.reshape(DB, S, 6, NSA_KV_HEADS, NSA_HEAD_DIM)
    s_cmp, s_slc, win_rows = rows6[:, :, 0:2], rows6[:, :, 2:4], rows6[:, :, 4:6]
    cmp_full = jnp.concatenate([gather_pages(cache_cmp_kv[l], page_table), s_cmp], axis=1)
    slc_full = jnp.concatenate([gather_pages(cache_slc_kv[l], page_table), s_slc], axis=1)
    win_buf = cache_win_kv[l]
    win_ext = jnp.concatenate([win_buf, win_rows], axis=1)
    h_nsa = nsa_mixer(nq.reshape(DB, S, NSA_HEADS, NSA_HEAD_DIM), ng.reshape(DB, S, NSA_HEADS, 3), past_len,
                      cmp_full, slc_full, win_ext, past_len - win_buf.shape[1],
                      cmp_pe[l], cmp_w1[l], cmp_b1[l], cmp_w2[l])
    xs, xs_ffn_in = after_mixers(x_sample, h_ml, h_nsa, cache_mem_kv[l])
    w_keep = min(WINDOW, past_len + S)
    s_win = win_ext[:, win_ext.shape[1] - w_keep:]

    n_p = B * T
    uv = jnp.concatenate([peer_u[l], peer_v[l]], axis=1).reshape(-1, PEER_SLAB_ROWS, PEER_SLAB_LANES)
    ffn = peer_ffn_tokens(jnp.concatenate([xp_ffn_in, xs_ffn_in], 0), peer_wq[l], peer_sub_keys[l], uv)
    y_prompt = add_rmsnorm(xp, ffn[:n_p], g_final).reshape(x_prompt.shape)
    y_sample = add_rmsnorm(xs, ffn[n_p:], g_final).reshape(x_sample.shape)
    st = lambda a: a[None]
    return (y_prompt, y_sample,
            st(p_cmp), st(p_slc), st(p_win), st(p_c), st(p_n), st(p_m), st(p_mem),
            st(s_cmp), st(s_slc), st(s_win), st(s_c), st(s_n), st(s_m))
```

```python
import functools

import jax
import jax.numpy as jnp
from jax import lax
import numpy as np
from jax.experimental import pallas as pl
from jax.experimental.pallas import tpu as pltpu

D_MODEL = 1024
DEPTH = 1
PAGE_SIZE = 128

ML_WIDTH = D_MODEL // 2
ML_HEADS = 4
ML_HEAD_DIM = ML_WIDTH // ML_HEADS
ML_CHUNK = 128
NSA_WIDTH = D_MODEL - ML_WIDTH
NSA_HEADS = 8
NSA_HEAD_DIM = NSA_WIDTH // NSA_HEADS
NSA_KV_HEADS = 2
NSA_GROUP = NSA_HEADS // NSA_KV_HEADS
CMP_BLOCK = 32
CMP_STRIDE = 16
SEL_BLOCK = 64
N_SEL = 16
WINDOW = 512
Q_BLOCK = 128
FORCED_SCORE = 1.0e4
INVALID_SCORE = -1.0
MEM_LEN = 256
MEM_HEADS = 4
MEM_HEAD_DIM = D_MODEL // MEM_HEADS
PEER_HEADS = 8
PEER_N_KEYS = 128
PEER_N_EXPERTS = PEER_N_KEYS * PEER_N_KEYS
PEER_TOPK = 16
PEER_QUERY_DIM = 256
PEER_TOKEN_BLOCK = 128
EPS = 1e-6
TINY = 1e-30
IN_SPLITS = (ML_WIDTH, ML_WIDTH, ML_WIDTH, ML_WIDTH, ML_HEADS, ML_HEADS,
             NSA_WIDTH, 6 * NSA_KV_HEADS * NSA_HEAD_DIM, 3 * NSA_HEADS)
IN_COLS = sum(IN_SPLITS)
IN_OFFSETS = tuple(int(o) for o in np.cumsum(IN_SPLITS)[:-1])

f32 = jnp.float32


def rmsnorm(x, g):
    xf = x.astype(f32)
    return (xf * lax.rsqrt(jnp.mean(xf * xf, -1, keepdims=True) + EPS) * g).astype(x.dtype)


ROW_TILE = 256
DENSE_VMEM_BYTES = 56 * 1024 * 1024


def _rms(x, g):
    return x * lax.rsqrt(jnp.mean(x * x, -1, keepdims=True) + EPS) * g


def _add_norm_body(x_ref, r_ref, g_ref, o_ref):
    o_ref[...] = _rms(x_ref[...] + r_ref[...], g_ref[...])


def add_rmsnorm(x, r, g):
    shape = x.shape
    d = shape[-1]
    x2, r2 = x.reshape(-1, d), r.reshape(-1, d)
    n = x2.shape[0]
    tm = min(n, ROW_TILE)
    rows = pl.BlockSpec((tm, d), lambda i: (i, 0))
    out = pl.pallas_call(
        _add_norm_body,
        grid=(n // tm,),
        in_specs=[rows, rows, pl.BlockSpec((1, d), lambda i: (0, 0))],
        out_specs=rows,
        out_shape=jax.ShapeDtypeStruct((n, d), f32),
        name="add_rmsnorm",
    )(x2, r2, g.reshape(1, d))
    return out.reshape(shape)


def _linear_body(*refs, pre_norm, has_bias, has_res, post_norm, splits):
    it = iter(refs)
    x_ref, w_ref = next(it), next(it)
    x = x_ref[...]
    if pre_norm:
        x = _rms(x, next(it)[...])
    y = jnp.dot(x.astype(jnp.bfloat16), w_ref[...], preferred_element_type=f32)
    if has_bias:
        y = y + next(it)[...]
    if has_res:
        y = y + next(it)[...]
    post_gain = next(it)[...] if post_norm else None
    off = 0
    for m in splits:
        next(it)[...] = y[:, off:off + m]
        off += m
    if post_norm:
        next(it)[...] = _rms(y, post_gain)


def fused_linear(x, w, *, pre_gain=None, bias=None, residual=None, post_gain=None, splits=None):
    n, k = x.shape
    m = w.shape[1]
    splits = (m,) if splits is None else tuple(splits)
    assert sum(splits) == m and (post_gain is None or len(splits) == 1)
    tm = min(n, ROW_TILE)
    assert n % tm == 0
    row = lambda c: pl.BlockSpec((tm, c), lambda i: (i, 0))
    const = lambda r, c: pl.BlockSpec((r, c), lambda i: (0, 0))
    args, specs = [x, w.astype(jnp.bfloat16)], [row(k), const(k, m)]
    if pre_gain is not None:
        args.append(pre_gain.reshape(1, k)); specs.append(const(1, k))
    if bias is not None:
        args.append(bias.reshape(1, m)); specs.append(const(1, m))
    if residual is not None:
        args.append(residual); specs.append(row(m))
    if post_gain is not None:
        args.append(post_gain.reshape(1, m)); specs.append(const(1, m))
    out_cols = splits + ((m,) if post_gain is not None else ())
    outs = pl.pallas_call(
        functools.partial(_linear_body, pre_norm=pre_gain is not None, has_bias=bias is not None,
                          has_res=residual is not None, post_norm=post_gain is not None, splits=splits),
        grid=(n // tm,),
        in_specs=specs,
        out_specs=[row(c) for c in out_cols],
        out_shape=[jax.ShapeDtypeStruct((n, c), f32) for c in out_cols],
        compiler_params=pltpu.CompilerParams(dimension_semantics=("arbitrary",),
                                             vmem_limit_bytes=DENSE_VMEM_BYTES),
        name="fused_linear",
    )(*args)
    return outs[0] if len(outs) == 1 else tuple(outs)


def _xattn_body(q_ref, kv_ref, o_ref):
    bf16 = jnp.bfloat16
    d = MEM_HEAD_DIM
    q = q_ref[0]
    kv = kv_ref[0].astype(bf16)
    for h in range(MEM_HEADS):
        k_h = kv[:, h * d:(h + 1) * d]
        v_h = kv[:, (MEM_HEADS + h) * d:(MEM_HEADS + h + 1) * d]
        s = lax.dot_general(q[:, h * d:(h + 1) * d].astype(bf16), k_h, (((1,), (1,)), ((), ())),
                            preferred_element_type=f32) * (d ** -0.5)
        e = jnp.exp(s - jnp.max(s, axis=-1, keepdims=True))
        p = e / jnp.sum(e, axis=-1, keepdims=True)
        o_ref[0, :, h * d:(h + 1) * d] = jnp.dot(p.astype(bf16), v_h, preferred_element_type=f32)


def cross_attention(q, mem_kv):
    B, T, w = q.shape
    M = mem_kv.shape[1]
    kv = mem_kv.reshape(B, M, 2 * w)
    tm = min(T, ROW_TILE)
    assert T % tm == 0 and tm % 8 == 0
    return pl.pallas_call(
        _xattn_body,
        grid=(B, T // tm),
        in_specs=[pl.BlockSpec((1, tm, w), lambda b, i: (b, i, 0)),
                  pl.BlockSpec((1, M, 2 * w), lambda b, i: (b, 0, 0))],
        out_specs=pl.BlockSpec((1, tm, w), lambda b, i: (b, i, 0)),
        out_shape=jax.ShapeDtypeStruct((B, T, w), f32),
        compiler_params=pltpu.CompilerParams(dimension_semantics=("arbitrary", "arbitrary")),
        name="cross_attention",
    )(q, kv)


def masked_softmax(s, mask):
    s = jnp.where(mask, s.astype(f32), -jnp.inf)
    mx = jnp.max(s, -1, keepdims=True)
    mx = jnp.where(jnp.isfinite(mx), mx, 0.0)
    e = jnp.exp(s - mx)
    return e / jnp.maximum(e.sum(-1, keepdims=True), TINY)


def mixer_projections(xn, w_in, b_in, b_forget):
    B, T = xn.shape[:2]
    p = xn @ w_in + b_in
    ml_q, ml_k, ml_v, ml_o, ml_i, ml_f, nq, nkv, ng = jnp.split(p, IN_OFFSETS, axis=-1)
    hd = (B, T, ML_HEADS, ML_HEAD_DIM)
    return (ml_q.reshape(hd), ml_k.reshape(hd), ml_v.reshape(hd), ml_o, ml_i, ml_f + b_forget,
            nq.reshape(B, T, NSA_HEADS, NSA_HEAD_DIM),
            nkv.reshape(B, T, 6, NSA_KV_HEADS, NSA_HEAD_DIM),
            jax.nn.sigmoid(ng).reshape(B, T, NSA_HEADS, 3))


def mlstm_chunk(carry, inp):
    c, n, m = carry
    q, k, v, ig, lf = inp
    L = q.shape[2]
    b = jnp.cumsum(lf, axis=-1)
    causal = jnp.tril(jnp.ones((L, L), bool))
    log_d = jnp.where(causal, b[..., :, None] - b[..., None, :] + ig[..., None, :], -jnp.inf)
    inter = b + m[..., None]
    m_t = jnp.maximum(inter, log_d.max(-1))
    d_mat = jnp.exp(log_d - m_t[..., None])
    a = jnp.exp(inter - m_t)
    qk = jnp.einsum('bhtd,bhsd->bhts', q, k) * d_mat
    num = a[..., None] * jnp.einsum('bhtd,bhde->bhte', q, c) + jnp.einsum('bhts,bhse->bhte', qk, v)
    den = a * jnp.einsum('bhtd,bhd->bht', q, n) + qk.sum(-1)
    h = num / jnp.maximum(jnp.abs(den), jnp.exp(-m_t))[..., None]
    m_new = m_t[..., -1]
    w = jnp.exp(b[..., -1:] - b + ig - m_new[..., None])
    decay = jnp.exp(b[..., -1] + m - m_new)
    c_new = decay[..., None, None] * c + jnp.einsum('bhs,bhsd,bhse->bhde', w, k, v)
    n_new = decay[..., None] * n + jnp.einsum('bhs,bhsd->bhd', w, k)
    return (c_new, n_new, m_new), h


def mlstm_mixer(q, k, v, i_pre, f_pre, o_pre, head_gain, c0, n0, m0):
    B, T, H, d = q.shape
    L = ML_CHUNK if T % ML_CHUNK == 0 else T
    nc = T // L

    def chunks(a):
        a = a.astype(f32).reshape((B, nc, L) + a.shape[2:])
        return jnp.transpose(a, (1, 0, 3, 2) + tuple(range(4, a.ndim)))

    xs = (chunks(q), chunks(k * d ** -0.5), chunks(v), chunks(i_pre),
          chunks(jax.nn.log_sigmoid(f_pre.astype(f32))))
    (c, n, m), h = lax.scan(mlstm_chunk, (c0.astype(f32), n0.astype(f32), m0.astype(f32)), xs)
    h = jnp.transpose(h, (1, 0, 3, 2, 4)).reshape(B, T, H, d)
    h = jax.nn.sigmoid(o_pre.astype(f32)).reshape(B, T, H, d) * h
    h = h * lax.rsqrt(jnp.mean(h * h, -1, keepdims=True) + EPS) * head_gain
    return h.reshape(B, T, H * d).astype(q.dtype), c, n, m


def nsa_compress(rows, pe, w1, b1, w2):
    B, G, T, d = rows.shape
    n_seg = T // CMP_STRIDE
    seg = rows[:, :, :n_seg * CMP_STRIDE].reshape(B, G, n_seg, CMP_STRIDE * d)
    half = CMP_STRIDE * d
    pre = seg[:, :, :-1] @ w1[:half] + seg[:, :, 1:] @ w1[half:] + (pe.reshape(-1) @ w1 + b1)
    ends = jnp.arange(n_seg - 1) * CMP_STRIDE + (CMP_BLOCK - 1)
    return jax.nn.gelu(pre) @ w2, ends


def to_sel_blocks(rows):
    B, G, T, d = rows.shape
    nsb = -(-T // SEL_BLOCK)
    rows = jnp.pad(rows, ((0, 0), (0, 0), (0, nsb * SEL_BLOCK - T), (0, 0)))
    return rows.reshape(B, G, nsb, SEL_BLOCK, d)


def cmp_to_sel(imp, nsb):
    r = SEL_BLOCK // CMP_STRIDE
    nc = imp.shape[-1]
    lead = imp.shape[:-1]
    tot = r * nsb
    padw = [(0, 0)] * len(lead)
    first = jnp.pad(imp, padw + [(0, tot - nc)]).reshape(lead + (nsb, r)).sum(-1)
    second = jnp.pad(imp, padw + [(1, tot - nc - 1)]).reshape(lead + (nsb, r)).sum(-1)
    return 0.5 * (first + second)


def nsa_attend(q, gates, q_pos, kc, vc, c_end, ks, vs, kw, vw, w_pos):
    B, G, R, Q, d = q.shape
    scale = d ** -0.5
    p_c = masked_softmax(jnp.einsum('bgrqd,bgnd->bgrqn', q, kc) * scale, c_end[None, :] <= q_pos[:, None])
    o_c = jnp.einsum('bgrqn,bgnd->bgrqd', p_c, vc)
    nsb = ks.shape[2]
    imp = cmp_to_sel(p_c.sum(2), nsb)
    blk = jnp.arange(nsb)
    valid = blk[None, :] * SEL_BLOCK <= q_pos[:, None]
    forced = (blk[None, :] == 0) | (blk[None, :] == q_pos[:, None] // SEL_BLOCK)
    score = jnp.where(forced, FORCED_SCORE, jnp.where(valid, imp, INVALID_SCORE))
    _, idx = lax.top_k(score, min(N_SEL, nsb))
    bi = jnp.arange(B)[:, None, None, None]
    gi = jnp.arange(G)[None, :, None, None]
    kg = ks[bi, gi, idx].reshape(B, G, Q, -1, d)
    vg = vs[bi, gi, idx].reshape(B, G, Q, -1, d)
    k_pos = (idx[..., None] * SEL_BLOCK + jnp.arange(SEL_BLOCK)).reshape(B, G, Q, -1)
    p_s = masked_softmax(jnp.einsum('bgrqd,bgqsd->bgrqs', q, kg) * scale,
                         (k_pos <= q_pos[None, None, :, None])[:, :, None])
    o_s = jnp.einsum('bgrqs,bgqsd->bgrqd', p_s, vg)
    wmask = ((w_pos[None, :] <= q_pos[:, None]) & (w_pos[None, :] >= q_pos[:, None] - WINDOW)
             & (w_pos[None, :] >= 0))
    p_w = masked_softmax(jnp.einsum('bgrqd,bgkd->bgrqk', q, kw) * scale, wmask)
    o_w = jnp.einsum('bgrqk,bgkd->bgrqd', p_w, vw)
    return gates[..., 0:1] * o_c + gates[..., 1:2] * o_s + gates[..., 2:3] * o_w


def nsa_mixer(q, gates, q_pos0, cmp_rows, slc_rows, win_rows, win_pos0, cmp_pe, cmp_w1, cmp_b1, cmp_w2):
    B, Tq, _, d = q.shape
    G, R = NSA_KV_HEADS, NSA_GROUP
    bg = lambda a: jnp.swapaxes(a, 1, 2)
    kc, c_end = nsa_compress(bg(cmp_rows[:, :, 0]), cmp_pe[0], cmp_w1[0], cmp_b1[0], cmp_w2[0])
    vc, _ = nsa_compress(bg(cmp_rows[:, :, 1]), cmp_pe[1], cmp_w1[1], cmp_b1[1], cmp_w2[1])
    ks = to_sel_blocks(bg(slc_rows[:, :, 0]))
    vs = to_sel_blocks(bg(slc_rows[:, :, 1]))
    kw = bg(win_rows[:, :, 0])
    vw = bg(win_rows[:, :, 1])
    qg = q.reshape(B, Tq, G, R, d).transpose(0, 2, 3, 1, 4)
    gg = gates.reshape(B, Tq, G, R, 3).transpose(0, 2, 3, 1, 4)
    qb = Q_BLOCK if Tq % Q_BLOCK == 0 else Tq
    nb = Tq // qb
    if nb == 1:
        w_pos = win_pos0 + jnp.arange(kw.shape[2])
        o = nsa_attend(qg, gg, q_pos0 + jnp.arange(Tq), kc, vc, c_end, ks, vs, kw, vw, w_pos)
    else:
        off = q_pos0 - WINDOW - win_pos0
        span = WINDOW + qb

        def block(args):
            q_blk, g_blk, j = args
            start = j * qb
            kw_b = lax.dynamic_slice_in_dim(kw, start + off, span, axis=2)
            vw_b = lax.dynamic_slice_in_dim(vw, start + off, span, axis=2)
            w_pos = q_pos0 + start - WINDOW + jnp.arange(span)
            return nsa_attend(q_blk, g_blk, q_pos0 + start + jnp.arange(qb), kc, vc, c_end,
                              ks, vs, kw_b, vw_b, w_pos)

        qs = qg.reshape(B, G, R, nb, qb, d).transpose(3, 0, 1, 2, 4, 5)
        gs = gg.reshape(B, G, R, nb, qb, 3).transpose(3, 0, 1, 2, 4, 5)
        o = lax.map(block, (qs, gs, jnp.arange(nb)))
        o = o.transpose(1, 2, 3, 0, 4, 5).reshape(B, G, R, Tq, d)
    return o.transpose(0, 3, 1, 2, 4).reshape(B, Tq, NSA_HEADS * d)


NSA_SEL_CHUNK = 512
NSA_WIN_CHUNKS = WINDOW // Q_BLOCK + 1
NSA_KV_LANES = NSA_KV_HEADS * NSA_HEAD_DIM


def _softmax_masked(s, mask):
    sm = jnp.where(mask, s, -jnp.inf)
    mx = jnp.max(sm, axis=-1, keepdims=True)
    mx = jnp.where(mx == -jnp.inf, 0.0, mx)
    e = jnp.exp(sm - mx)
    return e * (1.0 / jnp.maximum(jnp.sum(e, axis=-1, keepdims=True), TINY))


def _top_rows_mask(s, row_id, k):
    sel = jnp.zeros(s.shape, f32)
    sentinel = s.shape[0]
    for _ in range(k):
        m = jnp.max(s, axis=0, keepdims=True)
        first = jnp.min(jnp.where(s == m, row_id, sentinel), axis=0, keepdims=True)
        hit = row_id == first
        sel = jnp.where(hit, 1.0, sel)
        s = jnp.where(hit, -jnp.inf, s)
    return sel


def _nsa_prompt_body(q_ref, gate_ref, kct_ref, vc_ref, mt_ref, kts_ref, vs_ref, ktw_ref, vw_ref, o_ref):
    j = pl.program_id(1)
    Q, R, d = Q_BLOCK, NSA_GROUP, NSA_HEAD_DIM
    n_cmp = kct_ref.shape[-1]
    n_blk = mt_ref.shape[0]
    bf16 = jnp.bfloat16
    q_all = q_ref[0] * (d ** -0.5)
    gates = gate_ref[0]
    q_pos = j * Q + lax.broadcasted_iota(jnp.int32, (Q, 1), 0)
    q_pos_l = j * Q + lax.broadcasted_iota(jnp.int32, (1, Q), 1)
    blk_r = lax.broadcasted_iota(jnp.int32, (n_blk, 1), 0)
    blk_id = lax.broadcasted_iota(jnp.int32, (n_blk, Q), 0)
    nt = (((1,), (1,)), ((), ()))

    for g in range(NSA_KV_HEADS):
        lanes = slice(g * d, (g + 1) * d)
        qg = jnp.concatenate([q_all[:, (g * R + r) * d:(g * R + r + 1) * d] for r in range(R)], axis=0).astype(bf16)

        s = jnp.dot(qg, kct_ref[0, g], preferred_element_type=f32).reshape(R, Q, n_cmp)
        n_id = lax.broadcasted_iota(jnp.int32, (1, n_cmp), 1)
        cmask = (n_id * CMP_STRIDE + (CMP_BLOCK - 1) <= q_pos) & (n_id < n_cmp - 1)
        p = _softmax_masked(s, cmask[None])
        o_c = jnp.dot(p.reshape(R * Q, n_cmp).astype(bf16), vc_ref[0], preferred_element_type=f32)[:, lanes]

        p_sum = p[0] + p[1] + p[2] + p[3]
        hi = p_sum.astype(bf16)
        lo = (p_sum - hi.astype(f32)).astype(bf16)
        mt = mt_ref[...]
        imp_t = (lax.dot_general(mt, hi, nt, preferred_element_type=f32)
                 + lax.dot_general(mt, lo, nt, preferred_element_type=f32))
        valid = blk_r * SEL_BLOCK <= q_pos_l
        forced = (blk_r == 0) | (blk_r == jnp.right_shift(q_pos_l, 6))
        score = jnp.where(forced, FORCED_SCORE, jnp.where(valid, imp_t, INVALID_SCORE))
        sel = _top_rows_mask(score, blk_id, min(N_SEL, n_blk)).T.astype(bf16)

        kc = kts_ref.shape[-1]

        def chunk(c, carry):
            m, l, acc = carry
            kt = kts_ref[0, c, g * d:(g + 1) * d, :]
            sc = jnp.dot(qg, kt, preferred_element_type=f32).reshape(R, Q, kc)
            key = c * kc + lax.broadcasted_iota(jnp.int32, (1, kc), 1)
            expand = jnp.where(blk_r == jnp.right_shift(key, 6), 1.0, 0.0).astype(bf16)
            picked = jnp.dot(sel, expand, preferred_element_type=f32) > 0.5
            mask = picked & (key <= q_pos)
            sm = jnp.where(mask[None], sc, -jnp.inf)
            m_new = jnp.maximum(m, jnp.max(sm, axis=-1, keepdims=True))
            m_safe = jnp.where(m_new == -jnp.inf, 0.0, m_new)
            alpha = jnp.exp(m - m_safe)
            pe = jnp.exp(sm - m_safe)
            l = l * alpha + jnp.sum(pe, axis=-1, keepdims=True)
            pv = jnp.dot(pe.reshape(R * Q, kc).astype(bf16), vs_ref[0, c], preferred_element_type=f32)
            acc = acc * alpha.reshape(R * Q, 1) + pv
            return m_new, l, acc

        n_chunks = (j * Q + Q + kc - 1) // kc
        init = (jnp.full((R, Q, 1), -jnp.inf, f32), jnp.zeros((R, Q, 1), f32),
                jnp.zeros((R * Q, NSA_KV_LANES), f32))
        _, l_s, acc_s = lax.fori_loop(0, n_chunks, chunk, init)
        o_s = acc_s[:, lanes] * (1.0 / jnp.maximum(l_s.reshape(R * Q, 1), TINY))

        ss, vv = [], []
        for i in range(NSA_WIN_CHUNKS):
            cc = jnp.maximum(j - (NSA_WIN_CHUNKS - 1) + i, 0)
            ss.append(jnp.dot(qg, ktw_ref[0, cc, g * d:(g + 1) * d, :], preferred_element_type=f32))
            vv.append(vw_ref[0, cc])
        span = NSA_WIN_CHUNKS * Q
        sw = jnp.concatenate(ss, axis=1).reshape(R, Q, span)
        k_pos = (j - (NSA_WIN_CHUNKS - 1)) * Q + lax.broadcasted_iota(jnp.int32, (1, span), 1)
        wmask = (k_pos >= 0) & (k_pos <= q_pos) & (k_pos >= q_pos - WINDOW)
        pw = _softmax_masked(sw, wmask[None])
        o_w = jnp.dot(pw.reshape(R * Q, span).astype(bf16), jnp.concatenate(vv, axis=0),
                      preferred_element_type=f32)[:, lanes]

        for r in range(R):
            h = g * R + r
            rows = slice(r * Q, (r + 1) * Q)
            o_ref[0, :, h * d:(h + 1) * d] = (gates[:, 3 * h:3 * h + 1] * o_c[rows]
                                              + gates[:, 3 * h + 1:3 * h + 2] * o_s[rows]
                                              + gates[:, 3 * h + 2:3 * h + 3] * o_w[rows])


def _cmp_to_sel_matrix(n_cmp, n_blk):
    r = SEL_BLOCK // CMP_STRIDE
    n = np.arange(n_cmp)
    b = np.arange(n_blk)[:, None]
    m = 0.5 * ((n // r == b).astype(np.float32) + ((n + 1) // r == b).astype(np.float32))
    m[:, n_cmp - 1] = 0.0
    return jnp.asarray(m, jnp.bfloat16)


def nsa_prompt(nq, gates, nkv, kc, vc):
    B, T, _ = nq.shape
    bf16 = jnp.bfloat16
    G, d, w = NSA_KV_HEADS, NSA_HEAD_DIM, NSA_KV_LANES
    n_cmp, n_blk = T // CMP_STRIDE, T // SEL_BLOCK
    kc_s, kc_w = NSA_SEL_CHUNK, Q_BLOCK
    pad = ((0, 0), (0, 0), (0, 1), (0, 0))
    kct = jnp.pad(kc, pad).transpose(0, 1, 3, 2).astype(bf16)
    vc2 = jnp.pad(vc, pad).transpose(0, 2, 1, 3).reshape(B, n_cmp, w).astype(bf16)
    chunks = lambda a, c: a.reshape(B, T // c, c, w).astype(bf16)
    kts = chunks(nkv[..., 2 * w:3 * w], kc_s).transpose(0, 1, 3, 2)
    vs = chunks(nkv[..., 3 * w:4 * w], kc_s)
    ktw = chunks(nkv[..., 4 * w:5 * w], kc_w).transpose(0, 1, 3, 2)
    vw = chunks(nkv[..., 5 * w:6 * w], kc_w)
    whole = lambda a: pl.BlockSpec((1,) + a.shape[1:], lambda b, j: (b,) + (0,) * (a.ndim - 1))
    mt = _cmp_to_sel_matrix(n_cmp, n_blk)
    return pl.pallas_call(
        _nsa_prompt_body,
        grid=(B, T // Q_BLOCK),
        in_specs=[
            pl.BlockSpec((1, Q_BLOCK, nq.shape[-1]), lambda b, j: (b, j, 0)),
            pl.BlockSpec((1, Q_BLOCK, gates.shape[-1]), lambda b, j: (b, j, 0)),
            whole(kct), whole(vc2),
            pl.BlockSpec(mt.shape, lambda b, j: (0, 0)),
            whole(kts), whole(vs), whole(ktw), whole(vw),
        ],
        out_specs=pl.BlockSpec((1, Q_BLOCK, nq.shape[-1]), lambda b, j: (b, j, 0)),
        out_shape=jax.ShapeDtypeStruct(nq.shape, f32),
        compiler_params=pltpu.CompilerParams(dimension_semantics=("arbitrary", "arbitrary"),
                                             vmem_limit_bytes=48 * 1024 * 1024),
        name="nsa_prompt",
    )(nq, gates, kct, vc2, mt, kts, vs, ktw, vw)


def gather_pages(pool, page_table):
    g = pool[page_table]
    return g.reshape((g.shape[0], g.shape[1] * g.shape[2]) + g.shape[3:])


def memory_kv(mem, g_mem, w_k, w_v):
    mn = rmsnorm(mem, g_mem)
    B, M = mem.shape[:2]
    k = (mn @ w_k).reshape(B, M, MEM_HEADS, MEM_HEAD_DIM)
    v = (mn @ w_v).reshape(B, M, MEM_HEADS, MEM_HEAD_DIM)
    return jnp.stack([k, v], axis=2)


def cross_attend(xn, mem_kv, w_q, w_o):
    B, T = xn.shape[:2]
    q = (xn @ w_q).reshape(B, T, MEM_HEADS, MEM_HEAD_DIM)
    s = jnp.einsum('bthd,bmhd->bhtm', q, mem_kv[:, :, 0]) * MEM_HEAD_DIM ** -0.5
    p = jax.nn.softmax(s.astype(f32), axis=-1)
    o = jnp.einsum('bhtm,bmhd->bthd', p, mem_kv[:, :, 1].astype(f32)).reshape(B, T, -1)
    return (o.astype(xn.dtype) @ w_o).astype(xn.dtype)


PEER_PICKS = PEER_HEADS * PEER_TOPK
PEER_TOKENS_PER_STEP = 64
PEER_GROUP = 4
PEER_FETCH_AHEAD = 2
PEER_ROW_BUFFERS = 4 * PEER_GROUP
PEER_SLAB_LANES = 128
PEER_SLAB_ROWS = 2 * D_MODEL // PEER_SLAB_LANES


def _gelu_tanh(x):
    return 0.5 * x * (1.0 + jnp.tanh(0.7978845608028654 * (x + 0.044715 * x * x * x)))


def _peer_expert_body(ids_ref, ids_next_ref, x_ref, g_ref, seg_ref, uv_ref, o_ref, rows, sems):
    tokens = x_ref.shape[0]
    depth = PEER_ROW_BUFFERS
    half, lanes = PEER_SLAB_ROWS // 2, PEER_SLAB_LANES
    cols = PEER_PICKS * half
    seg = seg_ref.shape[0]
    nt = (((1,), (1,)), ((), ()))
    bf16 = jnp.bfloat16

    col_row = lax.broadcasted_iota(jnp.int32, (half, cols), 1) & (half - 1)
    sub = lax.broadcasted_iota(jnp.int32, (half, cols), 0)
    diag = jnp.where(col_row == sub, 1.0, 0.0)

    group = PEER_GROUP
    groups = tokens // group
    n_seg = cols // seg
    step_id = pl.program_id(0)
    last_step = pl.num_programs(0) - 1

    def wait(slot):
        pltpu.make_async_copy(uv_ref.at[pl.ds(0, PEER_PICKS)], rows.at[slot], sems.at[slot]).wait()

    def fetcher(ids, t0, slot0):
        per = PEER_PICKS // 2

        def fetch(c):
            j, h = divmod(c, 2)
            for k in range(h * per, (h + 1) * per):
                pltpu.make_async_copy(uv_ref.at[ids[t0 + j, k]], rows.at[slot0 + j, k],
                                      sems.at[slot0 + j]).start(priority=k % 2)
        return fetch

    def mix_group(t0, slot0, fetch):
        parts = []
        for j in range(group):
            u_rows = rows[slot0 + j, :, :half, :].reshape(cols, lanes).astype(bf16)
            prod = lax.dot_general(x_ref[t0 + j].astype(bf16), u_rows, nt, preferred_element_type=f32)
            part = jnp.sum(prod * diag, axis=0, keepdims=True)
            parts += [part[:, i * seg:(i + 1) * seg] for i in range(n_seg)]
            fetch(j)
        part = jnp.concatenate(parts, axis=0)
        hi = part.astype(bf16)
        lo = (part - hi.astype(f32)).astype(bf16)
        ones = seg_ref[...]
        act = jnp.dot(hi, ones, preferred_element_type=f32) + jnp.dot(lo, ones, preferred_element_type=f32)
        for j in range(group):
            w = g_ref[t0 + j] * _gelu_tanh(act[j * n_seg:(j + 1) * n_seg])
            w = jnp.concatenate([jnp.broadcast_to(w[i:i + 1, :], (half, seg)) for i in range(n_seg)], axis=1)
            v_rows = rows[slot0 + j, :, half:, :].reshape(cols, lanes).astype(bf16)
            o_ref[t0 + j] = jnp.dot((w * diag).astype(bf16), v_rows, preferred_element_type=f32)
            fetch(group + j)

    sets, ahead = depth // group, PEER_FETCH_AHEAD

    @pl.when(step_id == 0)
    def _():
        for a in range(ahead):
            first = fetcher(ids_ref, a * group, a * group)
            for c in range(2 * group):
                first(c)

    def sweep(it, last):
        for q in range(sets):
            g = sets * it + q
            for j in range(group):
                wait(q * group + j)
            into = ((q + ahead) % sets) * group
            if last and q + ahead >= sets:
                fetch = fetcher(ids_next_ref, (q + ahead - sets) * group, into)
            else:
                fetch = fetcher(ids_ref, (g + ahead) * group, into)
            mix_group(g * group, q * group, fetch)

    def body(it, carry):
        sweep(it, False)
        return carry

    lax.fori_loop(0, groups // sets - 1, body, 0)
    sweep(groups // sets - 1, True)

    @pl.when(step_id == last_step)
    def _():
        for j in range(ahead * group):
            wait(j)


def peer_experts(xn, ids, gates, uv):
    n, d = xn.shape
    tb = PEER_TOKENS_PER_STEP
    slab, half, lanes = PEER_SLAB_ROWS, PEER_SLAB_ROWS // 2, PEER_SLAB_LANES
    seg = 2 * lanes
    cols = PEER_PICKS * half
    assert n % tb == 0 and tb % PEER_ROW_BUFFERS == 0 and half * lanes == d and cols % seg == 0
    assert 0 < PEER_FETCH_AHEAD < PEER_ROW_BUFFERS // PEER_GROUP
    same_pick = np.arange(seg)[:, None] // half == np.arange(seg)[None, :] // half
    out = pl.pallas_call(
        _peer_expert_body,
        grid=(n // tb,),
        in_specs=[
            pl.BlockSpec((tb, PEER_PICKS), lambda i: (i, 0), memory_space=pltpu.SMEM),
            pl.BlockSpec((tb, PEER_PICKS), lambda i: (jnp.minimum(i + 1, n // tb - 1), 0), memory_space=pltpu.SMEM),
            pl.BlockSpec((tb, half, lanes), lambda i: (i, 0, 0)),
            pl.BlockSpec((tb, cols // seg, seg), lambda i: (i, 0, 0)),
            pl.BlockSpec((seg, seg), lambda i: (0, 0)),
            pl.BlockSpec(memory_space=pl.ANY),
        ],
        out_specs=pl.BlockSpec((tb, half, lanes), lambda i: (i, 0, 0)),
        out_shape=jax.ShapeDtypeStruct((n, half, lanes), f32),
        scratch_shapes=[
            pltpu.VMEM((PEER_ROW_BUFFERS, PEER_PICKS, slab, lanes), f32),
            pltpu.SemaphoreType.DMA((PEER_ROW_BUFFERS,)),
        ],
        compiler_params=pltpu.CompilerParams(dimension_semantics=("arbitrary",)),
        name="peer_experts",
    )(ids, ids, xn.reshape(n, half, lanes), jnp.repeat(gates, half, axis=1).reshape(n, cols // seg, seg),
      jnp.asarray(same_pick, jnp.bfloat16), uv)
    return out.reshape(n, d)


PEER_ROUTE_TOKENS = 128
PEER_HALF_DIM = PEER_QUERY_DIM // 2


def _top_rows(s, row_id, k, payload=None):
    vals, picks = [], []
    sentinel = s.shape[0]
    for _ in range(k):
        m = jnp.max(s, axis=0, keepdims=True)
        first = jnp.min(jnp.where(s == m, row_id, sentinel), axis=0, keepdims=True)
        hit = row_id == first
        vals.append(m)
        if payload is None:
            picks.append(first)
        else:
            picks.append(jnp.max(jnp.where(hit, payload, -1), axis=0, keepdims=True))
        s = jnp.where(hit, -jnp.inf, s)
    return jnp.concatenate(vals, 0), jnp.concatenate(picks, 0)


def _peer_route_body(x_ref, wq_ref, sk_ref, ids_ref, gate_ref, q_scr):
    tn = x_ref.shape[0]
    q = jnp.dot(x_ref[...].astype(jnp.bfloat16), wq_ref[...], preferred_element_type=f32)
    for j in range(2 * PEER_HEADS):
        q_scr[j] = q[:, j * PEER_HALF_DIM:(j + 1) * PEER_HALF_DIM].astype(jnp.bfloat16)
    key_id = lax.broadcasted_iota(jnp.int32, (PEER_N_KEYS, tn), 0)
    n_cand = -(-sum(PEER_TOPK // (a + 1) for a in range(PEER_TOPK)) // 8) * 8
    cand_id = lax.broadcasted_iota(jnp.int32, (n_cand, tn), 0)

    def head(h, carry):
        tops = []
        for p in range(2):
            s = lax.dot_general(sk_ref[2 * h + p], q_scr[2 * h + p], (((1,), (1,)), ((), ())),
                                preferred_element_type=f32)
            tops.append(_top_rows(s, key_id, PEER_TOPK))
        (v0, i0), (v1, i1) = tops
        cs, ce = [], []
        for a in range(PEER_TOPK):
            nb = PEER_TOPK // (a + 1)
            cs.append(v0[a:a + 1] + v1[:nb])
            ce.append(i0[a:a + 1] * PEER_N_KEYS + i1[:nb])
        pad = n_cand - sum(c.shape[0] for c in cs)
        cand_s = jnp.concatenate(cs + [jnp.full((pad, tn), -jnp.inf, f32)], axis=0)
        cand_e = jnp.concatenate(ce + [jnp.zeros((pad, tn), jnp.int32)], axis=0)
        top_s, top_e = _top_rows(cand_s, cand_id, PEER_TOPK, payload=cand_e)
        e = jnp.exp(top_s - top_s[0:1])
        ids_ref[h] = top_e
        gate_ref[h] = e / jnp.sum(e, axis=0, keepdims=True)
        return carry

    lax.fori_loop(0, PEER_HEADS, head, 0)


def peer_route(xn, wq, sub_keys):
    n, d = xn.shape
    tn = PEER_ROUTE_TOKENS
    assert n % tn == 0
    n_q = 2 * PEER_HEADS * PEER_HALF_DIM
    sk = sub_keys.reshape(2 * PEER_HEADS, PEER_N_KEYS, PEER_HALF_DIM).astype(jnp.bfloat16)
    ids_t, gates_t = pl.pallas_call(
        _peer_route_body,
        grid=(n // tn,),
        in_specs=[
            pl.BlockSpec((tn, d), lambda i: (i, 0)),
            pl.BlockSpec((d, n_q), lambda i: (0, 0)),
            pl.BlockSpec((2 * PEER_HEADS, PEER_N_KEYS, PEER_HALF_DIM), lambda i: (0, 0, 0)),
        ],
        out_specs=[
            pl.BlockSpec((PEER_HEADS, PEER_TOPK, tn), lambda i: (0, 0, i)),
            pl.BlockSpec((PEER_HEADS, PEER_TOPK, tn), lambda i: (0, 0, i)),
        ],
        out_shape=[
            jax.ShapeDtypeStruct((PEER_HEADS, PEER_TOPK, n), jnp.int32),
            jax.ShapeDtypeStruct((PEER_HEADS, PEER_TOPK, n), f32),
        ],
        scratch_shapes=[pltpu.VMEM((2 * PEER_HEADS, tn, PEER_HALF_DIM), jnp.bfloat16)],
        compiler_params=pltpu.CompilerParams(dimension_semantics=("arbitrary",)),
        name="peer_route",
    )(xn, wq.astype(jnp.bfloat16), sk)
    to_rows = lambda a: a.reshape(PEER_PICKS, n).T
    return to_rows(ids_t), to_rows(gates_t)


def peer_ffn_tokens(xn, wq, sub_keys, uv):
    ids, gates = peer_route(xn, wq, sub_keys)
    return peer_experts(xn, ids, gates, uv)


def kernel(x_prompt, x_sample, cache_cmp_kv, cache_slc_kv, cache_win_kv, state_mlstm_c, state_mlstm_n,
           state_mlstm_m, cache_mem_kv, page_table, mem_prompt, g_mix, w_in, b_in, b_forget, ml_head_gain,
           cmp_pe, cmp_w1, cmp_b1, cmp_w2, w_out, g_xattn, g_mem, w_xq, w_xk, w_xv, w_xo, g_ffn, peer_wq,
           peer_sub_keys, peer_u, peer_v, g_final):
    B, T = x_prompt.shape[:2]
    DB, S = x_sample.shape[:2]
    past_len = page_table.shape[1] * PAGE_SIZE
    l = 0
    bg = lambda a: jnp.swapaxes(a, 1, 2)

    o_i, o_nq, o_nkv, o_ng = IN_OFFSETS[3], IN_OFFSETS[5], IN_OFFSETS[6], IN_OFFSETS[7]
    n_small = 2 * ML_HEADS + 3 * NSA_HEADS
    small_pad = 128 - n_small
    regroup = lambda a: jnp.concatenate(
        [a[..., :o_i], a[..., o_nq:o_nkv], a[..., o_nkv:o_ng], a[..., o_i:o_nq], a[..., o_ng:],
         jnp.zeros(a.shape[:-1] + (small_pad,), a.dtype)], axis=-1)
    w_in_g, b_in_g = regroup(w_in[l]), regroup(b_in[l])
    in_splits = (4 * ML_WIDTH, NSA_WIDTH, 6 * NSA_KV_LANES, n_small + small_pad)

    def project(x):
        b, t, _ = x.shape
        ml, nq, nkv, small = fused_linear(x.reshape(b * t, D_MODEL), w_in_g, pre_gain=g_mix[l], bias=b_in_g,
                                          splits=in_splits)
        hd = (b, t, ML_HEADS, ML_HEAD_DIM)
        part = lambda i: ml[:, i * ML_WIDTH:(i + 1) * ML_WIDTH]
        mi = small[:, :ML_HEADS].reshape(b, t, ML_HEADS)
        mf = small[:, ML_HEADS:2 * ML_HEADS].reshape(b, t, ML_HEADS) + b_forget[l]
        ng = jax.nn.sigmoid(small[:, 2 * ML_HEADS:n_small]).reshape(b, t, 3 * NSA_HEADS)
        return (part(0).reshape(hd), part(1).reshape(hd), part(2).reshape(hd), part(3).reshape(b, t, ML_WIDTH),
                mi, mf, nq.reshape(b, t, NSA_WIDTH), nkv.reshape(b, t, 6 * NSA_KV_LANES), ng)

    def after_mixers(x, h_ml, h_nsa, mem_kv):
        b, t, _ = x.shape
        h = jnp.concatenate([h_ml, h_nsa], -1).reshape(b * t, D_MODEL)
        x1, xn = fused_linear(h, w_out[l], residual=x.reshape(b * t, D_MODEL), post_gain=g_xattn[l])
        q = fused_linear(xn, w_xq[l]).reshape(b, t, D_MODEL)
        t_pad = -t % 8
        o = cross_attention(jnp.pad(q, ((0, 0), (0, t_pad), (0, 0))), mem_kv)[:, :t]
        return fused_linear(o.reshape(b * t, D_MODEL), w_xo[l], residual=x1, post_gain=g_ffn[l])

    mq, mk, mv, mo, mi, mf, nq, nkv, ng = project(x_prompt)
    h_ml, p_c, p_n, p_m = mlstm_mixer(
        mq, mk, mv, mi, mf, mo, ml_head_gain[l],
        jnp.zeros((B, ML_HEADS, ML_HEAD_DIM, ML_HEAD_DIM), f32),
        jnp.zeros((B, ML_HEADS, ML_HEAD_DIM), f32),
        jnp.full((B, ML_HEADS), -jnp.inf, f32))
    rows6 = nkv.reshape(B, T, 6, NSA_KV_HEADS, NSA_HEAD_DIM)
    p_cmp, p_slc, win_rows = rows6[:, :, 0:2], rows6[:, :, 2:4], rows6[:, :, 4:6]
    kc, _ = nsa_compress(bg(p_cmp[:, :, 0]), cmp_pe[l, 0], cmp_w1[l, 0], cmp_b1[l, 0], cmp_w2[l, 0])
    vc, _ = nsa_compress(bg(p_cmp[:, :, 1]), cmp_pe[l, 1], cmp_w1[l, 1], cmp_b1[l, 1], cmp_w2[l, 1])
    h_nsa = nsa_prompt(nq, ng, nkv, kc, vc)
    p_mem = fused_linear(mem_prompt.reshape(-1, D_MODEL), jnp.concatenate([w_xk[l], w_xv[l]], axis=1),
                         pre_gain=g_mem[l]).reshape(B, MEM_LEN, 2, MEM_HEADS, MEM_HEAD_DIM)
    xp, xp_ffn_in = after_mixers(x_prompt, h_ml, h_nsa, p_mem)
    p_win = win_rows[:, T - min(WINDOW, T):]

    mq, mk, mv, mo, mi, mf, nq, nkv, ng = project(x_sample)
    h_ml, s_c, s_n, s_m = mlstm_mixer(mq, mk, mv, mi, mf, mo, ml_head_gain[l],
                                      state_mlstm_c[l], state_mlstm_n[l], state_mlstm_m[l])
    rows6 = nkv.reshape(DB, S, 6, NSA_KV_HEADS, NSA_HEAD_DIM)
    s_cmp, s_slc, win_rows = rows6[:, :, 0:2], rows6[:, :, 2:4], rows6[:, :, 4:6]
    cmp_full = jnp.concatenate([gather_pages(cache_cmp_kv[l], page_table), s_cmp], axis=1)
    slc_full = jnp.concatenate([gather_pages(cache_slc_kv[l], page_table), s_slc], axis=1)
    win_buf = cache_win_kv[l]
    win_ext = jnp.concatenate([win_buf, win_rows], axis=1)
    h_nsa = nsa_mixer(nq.reshape(DB, S, NSA_HEADS, NSA_HEAD_DIM), ng.reshape(DB, S, NSA_HEADS, 3), past_len,
                      cmp_full, slc_full, win_ext, past_len - win_buf.shape[1],
                      cmp_pe[l], cmp_w1[l], cmp_b1[l], cmp_w2[l])
    xs, xs_ffn_in = after_mixers(x_sample, h_ml, h_nsa, cache_mem_kv[l])
    w_keep = min(WINDOW, past_len + S)
    s_win = win_ext[:, win_ext.shape[1] - w_keep:]

    n_p = B * T
    uv = jnp.concatenate([peer_u[l], peer_v[l]], axis=1).reshape(-1, PEER_SLAB_ROWS, PEER_SLAB_LANES)
    ffn = peer_ffn_tokens(jnp.concatenate([xp_ffn_in, xs_ffn_in], 0), peer_wq[l], peer_sub_keys[l], uv)
    y_prompt = add_rmsnorm(xp, ffn[:n_p], g_final).reshape(x_prompt.shape)
    y_sample = add_rmsnorm(xs, ffn[n_p:], g_final).reshape(x_sample.shape)
    st = lambda a: a[None]
    return (y_prompt, y_sample,
            st(p_cmp), st(p_slc), st(p_win), st(p_c), st(p_n), st(p_m), st(p_mem),
            st(s_cmp), st(s_slc), st(s_win), st(s_c), st(s_n), st(s_m))
```

```python
import functools

import jax
import jax.numpy as jnp
from jax import lax
import numpy as np
from jax.experimental import pallas as pl
from jax.experimental.pallas import tpu as pltpu

D_MODEL = 1024
DEPTH = 1
PAGE_SIZE = 128

ML_WIDTH = D_MODEL // 2
ML_HEADS = 4
ML_HEAD_DIM = ML_WIDTH // ML_HEADS
ML_CHUNK = 128
NSA_WIDTH = D_MODEL - ML_WIDTH
NSA_HEADS = 8
NSA_HEAD_DIM = NSA_WIDTH // NSA_HEADS
NSA_KV_HEADS = 2
NSA_GROUP = NSA_HEADS // NSA_KV_HEADS
CMP_BLOCK = 32
CMP_STRIDE = 16
SEL_BLOCK = 64
N_SEL = 16
WINDOW = 512
Q_BLOCK = 128
FORCED_SCORE = 1.0e4
INVALID_SCORE = -1.0
MEM_LEN = 256
MEM_HEADS = 4
MEM_HEAD_DIM = D_MODEL // MEM_HEADS
PEER_HEADS = 8
PEER_N_KEYS = 128
PEER_N_EXPERTS = PEER_N_KEYS * PEER_N_KEYS
PEER_TOPK = 16
PEER_QUERY_DIM = 256
PEER_TOKEN_BLOCK = 128
EPS = 1e-6
TINY = 1e-30
IN_SPLITS = (ML_WIDTH, ML_WIDTH, ML_WIDTH, ML_WIDTH, ML_HEADS, ML_HEADS,
             NSA_WIDTH, 6 * NSA_KV_HEADS * NSA_HEAD_DIM, 3 * NSA_HEADS)
IN_COLS = sum(IN_SPLITS)
IN_OFFSETS = tuple(int(o) for o in np.cumsum(IN_SPLITS)[:-1])

f32 = jnp.float32


def rmsnorm(x, g):
    xf = x.astype(f32)
    return (xf * lax.rsqrt(jnp.mean(xf * xf, -1, keepdims=True) + EPS) * g).astype(x.dtype)


ROW_TILE = 256
DENSE_VMEM_BYTES = 56 * 1024 * 1024


def _rms(x, g):
    return x * lax.rsqrt(jnp.mean(x * x, -1, keepdims=True) + EPS) * g


def _add_norm_body(x_ref, r_ref, g_ref, o_ref):
    o_ref[...] = _rms(x_ref[...] + r_ref[...], g_ref[...])


def add_rmsnorm(x, r, g):
    shape = x.shape
    d = shape[-1]
    x2, r2 = x.reshape(-1, d), r.reshape(-1, d)
    n = x2.shape[0]
    tm = min(n, ROW_TILE)
    rows = pl.BlockSpec((tm, d), lambda i: (i, 0))
    out = pl.pallas_call(
        _add_norm_body,
        grid=(n // tm,),
        in_specs=[rows, rows, pl.BlockSpec((1, d), lambda i: (0, 0))],
        out_specs=rows,
        out_shape=jax.ShapeDtypeStruct((n, d), f32),
        name="add_rmsnorm",
    )(x2, r2, g.reshape(1, d))
    return out.reshape(shape)


def _linear_body(*refs, pre_norm, has_bias, has_res, post_norm, splits):
    it = iter(refs)
    x_ref, w_ref = next(it), next(it)
    x = x_ref[...]
    if pre_norm:
        x = _rms(x, next(it)[...])
    y = jnp.dot(x.astype(jnp.bfloat16), w_ref[...], preferred_element_type=f32)
    if has_bias:
        y = y + next(it)[...]
    if has_res:
        y = y + next(it)[...]
    post_gain = next(it)[...] if post_norm else None
    off = 0
    for m in splits:
        next(it)[...] = y[:, off:off + m]
        off += m
    if post_norm:
        next(it)[...] = _rms(y, post_gain)


def fused_linear(x, w, *, pre_gain=None, bias=None, residual=None, post_gain=None, splits=None):
    n, k = x.shape
    m = w.shape[1]
    splits = (m,) if splits is None else tuple(splits)
    assert sum(splits) == m and (post_gain is None or len(splits) == 1)
    tm = min(n, ROW_TILE)
    assert n % tm == 0
    row = lambda c: pl.BlockSpec((tm, c), lambda i: (i, 0))
    const = lambda r, c: pl.BlockSpec((r, c), lambda i: (0, 0))
    args, specs = [x, w.astype(jnp.bfloat16)], [row(k), const(k, m)]
    if pre_gain is not None:
        args.append(pre_gain.reshape(1, k)); specs.append(const(1, k))
    if bias is not None:
        args.append(bias.reshape(1, m)); specs.append(const(1, m))
    if residual is not None:
        args.append(residual); specs.append(row(m))
    if post_gain is not None:
        args.append(post_gain.reshape(1, m)); specs.append(const(1, m))
    out_cols = splits + ((m,) if post_gain is not None else ())
    outs = pl.pallas_call(
        functools.partial(_linear_body, pre_norm=pre_gain is not None, has_bias=bias is not None,
                          has_res=residual is not None, post_norm=post_gain is not None, splits=splits),
        grid=(n // tm,),
        in_specs=specs,
        out_specs=[row(c) for c in out_cols],
        out_shape=[jax.ShapeDtypeStruct((n, c), f32) for c in out_cols],
        compiler_params=pltpu.CompilerParams(dimension_semantics=("arbitrary",),
                                             vmem_limit_bytes=DENSE_VMEM_BYTES),
        name="fused_linear",
    )(*args)
    return outs[0] if len(outs) == 1 else tuple(outs)


def _xattn_body(q_ref, kv_ref, o_ref):
    bf16 = jnp.bfloat16
    d = MEM_HEAD_DIM
    q = q_ref[0]
    kv = kv_ref[0].astype(bf16)
    for h in range(MEM_HEADS):
        k_h = kv[:, h * d:(h + 1) * d]
        v_h = kv[:, (MEM_HEADS + h) * d:(MEM_HEADS + h + 1) * d]
        s = lax.dot_general(q[:, h * d:(h + 1) * d].astype(bf16), k_h, (((1,), (1,)), ((), ())),
                            preferred_element_type=f32) * (d ** -0.5)
        e = jnp.exp(s - jnp.max(s, axis=-1, keepdims=True))
        p = e / jnp.sum(e, axis=-1, keepdims=True)
        o_ref[0, :, h * d:(h + 1) * d] = jnp.dot(p.astype(bf16), v_h, preferred_element_type=f32)


def cross_attention(q, mem_kv):
    B, T, w = q.shape
    M = mem_kv.shape[1]
    kv = mem_kv.reshape(B, M, 2 * w)
    tm = min(T, ROW_TILE)
    assert T % tm == 0 and tm % 8 == 0
    return pl.pallas_call(
        _xattn_body,
        grid=(B, T // tm),
        in_specs=[pl.BlockSpec((1, tm, w), lambda b, i: (b, i, 0)),
                  pl.BlockSpec((1, M, 2 * w), lambda b, i: (b, 0, 0))],
        out_specs=pl.BlockSpec((1, tm, w), lambda b, i: (b, i, 0)),
        out_shape=jax.ShapeDtypeStruct((B, T, w), f32),
        compiler_params=pltpu.CompilerParams(dimension_semantics=("arbitrary", "arbitrary")),
        name="cross_attention",
    )(q, kv)


def masked_softmax(s, mask):
    s = jnp.where(mask, s.astype(f32), -jnp.inf)
    mx = jnp.max(s, -1, keepdims=True)
    mx = jnp.where(jnp.isfinite(mx), mx, 0.0)
    e = jnp.exp(s - mx)
    return e / jnp.maximum(e.sum(-1, keepdims=True), TINY)


def mixer_projections(xn, w_in, b_in, b_forget):
    B, T = xn.shape[:2]
    p = xn @ w_in + b_in
    ml_q, ml_k, ml_v, ml_o, ml_i, ml_f, nq, nkv, ng = jnp.split(p, IN_OFFSETS, axis=-1)
    hd = (B, T, ML_HEADS, ML_HEAD_DIM)
    return (ml_q.reshape(hd), ml_k.reshape(hd), ml_v.reshape(hd), ml_o, ml_i, ml_f + b_forget,
            nq.reshape(B, T, NSA_HEADS, NSA_HEAD_DIM),
            nkv.reshape(B, T, 6, NSA_KV_HEADS, NSA_HEAD_DIM),
            jax.nn.sigmoid(ng).reshape(B, T, NSA_HEADS, 3))


def mlstm_chunk(carry, inp):
    c, n, m = carry
    q, k, v, ig, lf = inp
    L = q.shape[2]
    b = jnp.cumsum(lf, axis=-1)
    causal = jnp.tril(jnp.ones((L, L), bool))
    log_d = jnp.where(causal, b[..., :, None] - b[..., None, :] + ig[..., None, :], -jnp.inf)
    inter = b + m[..., None]
    m_t = jnp.maximum(inter, log_d.max(-1))
    d_mat = jnp.exp(log_d - m_t[..., None])
    a = jnp.exp(inter - m_t)
    qk = jnp.einsum('bhtd,bhsd->bhts', q, k) * d_mat
    num = a[..., None] * jnp.einsum('bhtd,bhde->bhte', q, c) + jnp.einsum('bhts,bhse->bhte', qk, v)
    den = a * jnp.einsum('bhtd,bhd->bht', q, n) + qk.sum(-1)
    h = num / jnp.maximum(jnp.abs(den), jnp.exp(-m_t))[..., None]
    m_new = m_t[..., -1]
    w = jnp.exp(b[..., -1:] - b + ig - m_new[..., None])
    decay = jnp.exp(b[..., -1] + m - m_new)
    c_new = decay[..., None, None] * c + jnp.einsum('bhs,bhsd,bhse->bhde', w, k, v)
    n_new = decay[..., None] * n + jnp.einsum('bhs,bhsd->bhd', w, k)
    return (c_new, n_new, m_new), h


def mlstm_mixer(q, k, v, i_pre, f_pre, o_pre, head_gain, c0, n0, m0):
    B, T, H, d = q.shape
    L = ML_CHUNK if T % ML_CHUNK == 0 else T
    nc = T // L

    def chunks(a):
        a = a.astype(f32).reshape((B, nc, L) + a.shape[2:])
        return jnp.transpose(a, (1, 0, 3, 2) + tuple(range(4, a.ndim)))

    xs = (chunks(q), chunks(k * d ** -0.5), chunks(v), chunks(i_pre),
          chunks(jax.nn.log_sigmoid(f_pre.astype(f32))))
    (c, n, m), h = lax.scan(mlstm_chunk, (c0.astype(f32), n0.astype(f32), m0.astype(f32)), xs)
    h = jnp.transpose(h, (1, 0, 3, 2, 4)).reshape(B, T, H, d)
    h = jax.nn.sigmoid(o_pre.astype(f32)).reshape(B, T, H, d) * h
    h = h * lax.rsqrt(jnp.mean(h * h, -1, keepdims=True) + EPS) * head_gain
    return h.reshape(B, T, H * d).astype(q.dtype), c, n, m


def _log_sigmoid(x):
    return jnp.minimum(x, 0.0) - jnp.log1p(jnp.exp(-jnp.abs(x)))


def _cumsum_lanes(x):
    lane = lax.broadcasted_iota(jnp.int32, x.shape, 1)
    shift = 1
    while shift < x.shape[1]:
        x = x + jnp.where(lane >= shift, pltpu.roll(x, shift, axis=1), 0.0)
        shift *= 2
    return x


def _mlstm_body(ml_ref, small_ref, bias_ref, gain_ref, c0_ref, n0_ref, m0_ref, h_ref, c_ref, n_ref, m_ref):
    H, d, L = ML_HEADS, ML_HEAD_DIM, ml_ref.shape[1]
    bf16 = jnp.bfloat16
    nt = (((1,), (1,)), ((), ()))
    tn = (((0,), (0,)), ((), ()))

    @pl.when(pl.program_id(1) == 0)
    def _():
        c_ref[...] = c0_ref[...]
        n_ref[...] = n0_ref[...]
        m_ref[...] = m0_ref[...]

    blk = ml_ref[0]
    small = small_ref[0] + bias_ref[...]
    small_t = small.T
    causal = (lax.broadcasted_iota(jnp.int32, (L, L), 0) >= lax.broadcasted_iota(jnp.int32, (L, L), 1))
    for h in range(H):
        q = blk[:, h * d:(h + 1) * d]
        k = blk[:, (H + h) * d:(H + h + 1) * d] * (d ** -0.5)
        v = blk[:, (2 * H + h) * d:(2 * H + h + 1) * d]
        o_pre = blk[:, (3 * H + h) * d:(3 * H + h + 1) * d]
        ig_row, ig_col = small_t[h:h + 1, :], small[:, h:h + 1]
        b_row = _cumsum_lanes(_log_sigmoid(small_t[H + h:H + h + 1, :]))
        b_col = jnp.broadcast_to(b_row, (L, L)).T
        b_t, b_last = b_col[:, 0:1], b_row[:, L - 1:L]
        c_prev, n_prev, m_prev = c_ref[0, h], n_ref[0, h], m_ref[0, h]
        log_d = jnp.where(causal, b_col - b_row + ig_row, -jnp.inf)
        inter = b_t + m_prev
        m_t = jnp.maximum(inter, jnp.max(log_d, axis=1, keepdims=True))
        a = jnp.exp(inter - m_t)
        qb, vb = q.astype(bf16), v.astype(bf16)
        qk = lax.dot_general(qb, k.astype(bf16), nt, preferred_element_type=f32) * jnp.exp(log_d - m_t)
        num = (a * jnp.dot(qb, c_prev.astype(bf16), preferred_element_type=f32)
               + jnp.dot(qk.astype(bf16), vb, preferred_element_type=f32))
        den = a * jnp.sum(q * n_prev, axis=1, keepdims=True) + jnp.sum(qk, axis=1, keepdims=True)
        hid = num / jnp.maximum(jnp.abs(den), jnp.exp(-m_t))
        m_new = m_t[L - 1:L, :]
        kw = k * jnp.exp(b_last - b_t + ig_col - m_new)
        decay = jnp.exp(b_last + m_prev - m_new)
        c_ref[0, h] = decay * c_prev + lax.dot_general(kw.astype(bf16), vb, tn, preferred_element_type=f32)
        n_ref[0, h] = decay * n_prev + jnp.sum(kw, axis=0, keepdims=True)
        m_ref[0, h] = m_new
        gated = jax.nn.sigmoid(o_pre) * hid
        h_ref[0, :, h * d:(h + 1) * d] = _rms(gated, gain_ref[:, h * d:(h + 1) * d])


def mlstm_pallas(ml, small, b_forget, head_gain, c0, n0, m0):
    B, T, _ = ml.shape
    H, d, L = ML_HEADS, ML_HEAD_DIM, ML_CHUNK
    assert T % L == 0
    bias = jnp.zeros((1, small.shape[-1]), f32).at[0, H:2 * H].set(b_forget)
    state = lambda *s: pl.BlockSpec((1,) + s, lambda b, i: (b,) + (0,) * len(s))
    h, c, n, m = pl.pallas_call(
        _mlstm_body,
        grid=(B, T // L),
        in_specs=[
            pl.BlockSpec((1, L, ml.shape[-1]), lambda b, i: (b, i, 0)),
            pl.BlockSpec((1, L, small.shape[-1]), lambda b, i: (b, i, 0)),
            pl.BlockSpec((1, small.shape[-1]), lambda b, i: (0, 0)),
            pl.BlockSpec((1, H * d), lambda b, i: (0, 0)),
            state(H, d, d), state(H, 1, d), state(H, 1, 1),
        ],
        out_specs=[pl.BlockSpec((1, L, H * d), lambda b, i: (b, i, 0)),
                   state(H, d, d), state(H, 1, d), state(H, 1, 1)],
        out_shape=[jax.ShapeDtypeStruct((B, T, H * d), f32), jax.ShapeDtypeStruct((B, H, d, d), f32),
                   jax.ShapeDtypeStruct((B, H, 1, d), f32), jax.ShapeDtypeStruct((B, H, 1, 1), f32)],
        compiler_params=pltpu.CompilerParams(dimension_semantics=("arbitrary", "arbitrary")),
        name="mlstm",
    )(ml, small, bias, head_gain.reshape(1, H * d), c0, n0.reshape(B, H, 1, d), m0.reshape(B, H, 1, 1))
    return h, c, n.reshape(B, H, d), m.reshape(B, H)


def nsa_compress(rows, pe, w1, b1, w2):
    B, G, T, d = rows.shape
    n_seg = T // CMP_STRIDE
    seg = rows[:, :, :n_seg * CMP_STRIDE].reshape(B, G, n_seg, CMP_STRIDE * d)
    half = CMP_STRIDE * d
    pre = seg[:, :, :-1] @ w1[:half] + seg[:, :, 1:] @ w1[half:] + (pe.reshape(-1) @ w1 + b1)
    ends = jnp.arange(n_seg - 1) * CMP_STRIDE + (CMP_BLOCK - 1)
    return jax.nn.gelu(pre) @ w2, ends


def to_sel_blocks(rows):
    B, G, T, d = rows.shape
    nsb = -(-T // SEL_BLOCK)
    rows = jnp.pad(rows, ((0, 0), (0, 0), (0, nsb * SEL_BLOCK - T), (0, 0)))
    return rows.reshape(B, G, nsb, SEL_BLOCK, d)


def cmp_to_sel(imp, nsb):
    r = SEL_BLOCK // CMP_STRIDE
    nc = imp.shape[-1]
    lead = imp.shape[:-1]
    tot = r * nsb
    padw = [(0, 0)] * len(lead)
    first = jnp.pad(imp, padw + [(0, tot - nc)]).reshape(lead + (nsb, r)).sum(-1)
    second = jnp.pad(imp, padw + [(1, tot - nc - 1)]).reshape(lead + (nsb, r)).sum(-1)
    return 0.5 * (first + second)


def nsa_attend(q, gates, q_pos, kc, vc, c_end, ks, vs, kw, vw, w_pos):
    B, G, R, Q, d = q.shape
    scale = d ** -0.5
    p_c = masked_softmax(jnp.einsum('bgrqd,bgnd->bgrqn', q, kc) * scale, c_end[None, :] <= q_pos[:, None])
    o_c = jnp.einsum('bgrqn,bgnd->bgrqd', p_c, vc)
    nsb = ks.shape[2]
    imp = cmp_to_sel(p_c.sum(2), nsb)
    blk = jnp.arange(nsb)
    valid = blk[None, :] * SEL_BLOCK <= q_pos[:, None]
    forced = (blk[None, :] == 0) | (blk[None, :] == q_pos[:, None] // SEL_BLOCK)
    score = jnp.where(forced, FORCED_SCORE, jnp.where(valid, imp, INVALID_SCORE))
    _, idx = lax.top_k(score, min(N_SEL, nsb))
    bi = jnp.arange(B)[:, None, None, None]
    gi = jnp.arange(G)[None, :, None, None]
    kg = ks[bi, gi, idx].reshape(B, G, Q, -1, d)
    vg = vs[bi, gi, idx].reshape(B, G, Q, -1, d)
    k_pos = (idx[..., None] * SEL_BLOCK + jnp.arange(SEL_BLOCK)).reshape(B, G, Q, -1)
    p_s = masked_softmax(jnp.einsum('bgrqd,bgqsd->bgrqs', q, kg) * scale,
                         (k_pos <= q_pos[None, None, :, None])[:, :, None])
    o_s = jnp.einsum('bgrqs,bgqsd->bgrqd', p_s, vg)
    wmask = ((w_pos[None, :] <= q_pos[:, None]) & (w_pos[None, :] >= q_pos[:, None] - WINDOW)
             & (w_pos[None, :] >= 0))
    p_w = masked_softmax(jnp.einsum('bgrqd,bgkd->bgrqk', q, kw) * scale, wmask)
    o_w = jnp.einsum('bgrqk,bgkd->bgrqd', p_w, vw)
    return gates[..., 0:1] * o_c + gates[..., 1:2] * o_s + gates[..., 2:3] * o_w


def nsa_mixer(q, gates, q_pos0, cmp_rows, slc_rows, win_rows, win_pos0, cmp_pe, cmp_w1, cmp_b1, cmp_w2):
    B, Tq, _, d = q.shape
    G, R = NSA_KV_HEADS, NSA_GROUP
    bg = lambda a: jnp.swapaxes(a, 1, 2)
    kc, c_end = nsa_compress(bg(cmp_rows[:, :, 0]), cmp_pe[0], cmp_w1[0], cmp_b1[0], cmp_w2[0])
    vc, _ = nsa_compress(bg(cmp_rows[:, :, 1]), cmp_pe[1], cmp_w1[1], cmp_b1[1], cmp_w2[1])
    ks = to_sel_blocks(bg(slc_rows[:, :, 0]))
    vs = to_sel_blocks(bg(slc_rows[:, :, 1]))
    kw = bg(win_rows[:, :, 0])
    vw = bg(win_rows[:, :, 1])
    qg = q.reshape(B, Tq, G, R, d).transpose(0, 2, 3, 1, 4)
    gg = gates.reshape(B, Tq, G, R, 3).transpose(0, 2, 3, 1, 4)
    qb = Q_BLOCK if Tq % Q_BLOCK == 0 else Tq
    nb = Tq // qb
    if nb == 1:
        w_pos = win_pos0 + jnp.arange(kw.shape[2])
        o = nsa_attend(qg, gg, q_pos0 + jnp.arange(Tq), kc, vc, c_end, ks, vs, kw, vw, w_pos)
    else:
        off = q_pos0 - WINDOW - win_pos0
        span = WINDOW + qb

        def block(args):
            q_blk, g_blk, j = args
            start = j * qb
            kw_b = lax.dynamic_slice_in_dim(kw, start + off, span, axis=2)
            vw_b = lax.dynamic_slice_in_dim(vw, start + off, span, axis=2)
            w_pos = q_pos0 + start - WINDOW + jnp.arange(span)
            return nsa_attend(q_blk, g_blk, q_pos0 + start + jnp.arange(qb), kc, vc, c_end,
                              ks, vs, kw_b, vw_b, w_pos)

        qs = qg.reshape(B, G, R, nb, qb, d).transpose(3, 0, 1, 2, 4, 5)
        gs = gg.reshape(B, G, R, nb, qb, 3).transpose(3, 0, 1, 2, 4, 5)
        o = lax.map(block, (qs, gs, jnp.arange(nb)))
        o = o.transpose(1, 2, 3, 0, 4, 5).reshape(B, G, R, Tq, d)
    return o.transpose(0, 3, 1, 2, 4).reshape(B, Tq, NSA_HEADS * d)


NSA_SEL_CHUNK = 512
NSA_WIN_CHUNKS = WINDOW // Q_BLOCK + 1
NSA_KV_LANES = NSA_KV_HEADS * NSA_HEAD_DIM


def _softmax_masked(s, mask):
    sm = jnp.where(mask, s, -jnp.inf)
    mx = jnp.max(sm, axis=-1, keepdims=True)
    mx = jnp.where(mx == -jnp.inf, 0.0, mx)
    e = jnp.exp(sm - mx)
    return e * (1.0 / jnp.maximum(jnp.sum(e, axis=-1, keepdims=True), TINY))


def _top_rows_mask(s, row_id, k):
    sel = jnp.zeros(s.shape, f32)
    sentinel = s.shape[0]
    for _ in range(k):
        m = jnp.max(s, axis=0, keepdims=True)
        first = jnp.min(jnp.where(s == m, row_id, sentinel), axis=0, keepdims=True)
        hit = row_id == first
        sel = jnp.where(hit, 1.0, sel)
        s = jnp.where(hit, -jnp.inf, s)
    return sel


def _nsa_prompt_body(q_ref, gate_ref, kct_ref, vc_ref, mt_ref, kts_ref, vs_ref, ktw_ref, vw_ref, o_ref):
    j = pl.program_id(1)
    Q, R, d = Q_BLOCK, NSA_GROUP, NSA_HEAD_DIM
    n_cmp = kct_ref.shape[-1]
    n_blk = mt_ref.shape[0]
    bf16 = jnp.bfloat16
    q_all = q_ref[0] * (d ** -0.5)
    gates = gate_ref[0]
    q_pos = j * Q + lax.broadcasted_iota(jnp.int32, (Q, 1), 0)
    q_pos_l = j * Q + lax.broadcasted_iota(jnp.int32, (1, Q), 1)
    blk_r = lax.broadcasted_iota(jnp.int32, (n_blk, 1), 0)
    blk_id = lax.broadcasted_iota(jnp.int32, (n_blk, Q), 0)
    nt = (((1,), (1,)), ((), ()))

    for g in range(NSA_KV_HEADS):
        lanes = slice(g * d, (g + 1) * d)
        qg = jnp.concatenate([q_all[:, (g * R + r) * d:(g * R + r + 1) * d] for r in range(R)], axis=0).astype(bf16)

        s = jnp.dot(qg, kct_ref[0, g], preferred_element_type=f32).reshape(R, Q, n_cmp)
        n_id = lax.broadcasted_iota(jnp.int32, (1, n_cmp), 1)
        cmask = (n_id * CMP_STRIDE + (CMP_BLOCK - 1) <= q_pos) & (n_id < n_cmp - 1)
        p = _softmax_masked(s, cmask[None])
        o_c = jnp.dot(p.reshape(R * Q, n_cmp).astype(bf16), vc_ref[0], preferred_element_type=f32)[:, lanes]

        p_sum = p[0] + p[1] + p[2] + p[3]
        hi = p_sum.astype(bf16)
        lo = (p_sum - hi.astype(f32)).astype(bf16)
        mt = mt_ref[...]
        imp_t = (lax.dot_general(mt, hi, nt, preferred_element_type=f32)
                 + lax.dot_general(mt, lo, nt, preferred_element_type=f32))
        valid = blk_r * SEL_BLOCK <= q_pos_l
        forced = (blk_r == 0) | (blk_r == jnp.right_shift(q_pos_l, 6))
        score = jnp.where(forced, FORCED_SCORE, jnp.where(valid, imp_t, INVALID_SCORE))
        sel = _top_rows_mask(score, blk_id, min(N_SEL, n_blk)).T.astype(bf16)

        kc = kts_ref.shape[-1]

        def chunk(c, carry):
            m, l, acc = carry
            kt = kts_ref[0, c, g * d:(g + 1) * d, :]
            sc = jnp.dot(qg, kt, preferred_element_type=f32).reshape(R, Q, kc)
            key = c * kc + lax.broadcasted_iota(jnp.int32, (1, kc), 1)
            expand = jnp.where(blk_r == jnp.right_shift(key, 6), 1.0, 0.0).astype(bf16)
            picked = jnp.dot(sel, expand, preferred_element_type=f32) > 0.5
            mask = picked & (key <= q_pos)
            sm = jnp.where(mask[None], sc, -jnp.inf)
            m_new = jnp.maximum(m, jnp.max(sm, axis=-1, keepdims=True))
            m_safe = jnp.where(m_new == -jnp.inf, 0.0, m_new)
            alpha = jnp.exp(m - m_safe)
            pe = jnp.exp(sm - m_safe)
            l = l * alpha + jnp.sum(pe, axis=-1, keepdims=True)
            pv = jnp.dot(pe.reshape(R * Q, kc).astype(bf16), vs_ref[0, c], preferred_element_type=f32)
            acc = acc * alpha.reshape(R * Q, 1) + pv
            return m_new, l, acc

        n_chunks = (j * Q + Q + kc - 1) // kc
        init = (jnp.full((R, Q, 1), -jnp.inf, f32), jnp.zeros((R, Q, 1), f32),
                jnp.zeros((R * Q, NSA_KV_LANES), f32))
        _, l_s, acc_s = lax.fori_loop(0, n_chunks, chunk, init)
        o_s = acc_s[:, lanes] * (1.0 / jnp.maximum(l_s.reshape(R * Q, 1), TINY))

        ss, vv = [], []
        for i in range(NSA_WIN_CHUNKS):
            cc = jnp.maximum(j - (NSA_WIN_CHUNKS - 1) + i, 0)
            ss.append(jnp.dot(qg, ktw_ref[0, cc, g * d:(g + 1) * d, :], preferred_element_type=f32))
            vv.append(vw_ref[0, cc])
        span = NSA_WIN_CHUNKS * Q
        sw = jnp.concatenate(ss, axis=1).reshape(R, Q, span)
        k_pos = (j - (NSA_WIN_CHUNKS - 1)) * Q + lax.broadcasted_iota(jnp.int32, (1, span), 1)
        wmask = (k_pos >= 0) & (k_pos <= q_pos) & (k_pos >= q_pos - WINDOW)
        pw = _softmax_masked(sw, wmask[None])
        o_w = jnp.dot(pw.reshape(R * Q, span).astype(bf16), jnp.concatenate(vv, axis=0),
                      preferred_element_type=f32)[:, lanes]

        for r in range(R):
            h = g * R + r
            rows = slice(r * Q, (r + 1) * Q)
            o_ref[0, :, h * d:(h + 1) * d] = (gates[:, 3 * h:3 * h + 1] * o_c[rows]
                                              + gates[:, 3 * h + 1:3 * h + 2] * o_s[rows]
                                              + gates[:, 3 * h + 2:3 * h + 3] * o_w[rows])


def _cmp_to_sel_matrix(n_cmp, n_blk):
    r = SEL_BLOCK // CMP_STRIDE
    n = np.arange(n_cmp)
    b = np.arange(n_blk)[:, None]
    m = 0.5 * ((n // r == b).astype(np.float32) + ((n + 1) // r == b).astype(np.float32))
    m[:, n_cmp - 1] = 0.0
    return jnp.asarray(m, jnp.bfloat16)


def nsa_prompt(nq, gates, nkv, kc, vc):
    B, T, _ = nq.shape
    bf16 = jnp.bfloat16
    G, d, w = NSA_KV_HEADS, NSA_HEAD_DIM, NSA_KV_LANES
    n_cmp, n_blk = T // CMP_STRIDE, T // SEL_BLOCK
    kc_s, kc_w = NSA_SEL_CHUNK, Q_BLOCK
    pad = ((0, 0), (0, 0), (0, 1), (0, 0))
    kct = jnp.pad(kc, pad).transpose(0, 1, 3, 2).astype(bf16)
    vc2 = jnp.pad(vc, pad).transpose(0, 2, 1, 3).reshape(B, n_cmp, w).astype(bf16)
    chunks = lambda a, c: a.reshape(B, T // c, c, w).astype(bf16)
    kts = chunks(nkv[..., 2 * w:3 * w], kc_s).transpose(0, 1, 3, 2)
    vs = chunks(nkv[..., 3 * w:4 * w], kc_s)
    ktw = chunks(nkv[..., 4 * w:5 * w], kc_w).transpose(0, 1, 3, 2)
    vw = chunks(nkv[..., 5 * w:6 * w], kc_w)
    whole = lambda a: pl.BlockSpec((1,) + a.shape[1:], lambda b, j: (b,) + (0,) * (a.ndim - 1))
    mt = _cmp_to_sel_matrix(n_cmp, n_blk)
    return pl.pallas_call(
        _nsa_prompt_body,
        grid=(B, T // Q_BLOCK),
        in_specs=[
            pl.BlockSpec((1, Q_BLOCK, nq.shape[-1]), lambda b, j: (b, j, 0)),
            pl.BlockSpec((1, Q_BLOCK, gates.shape[-1]), lambda b, j: (b, j, 0)),
            whole(kct), whole(vc2),
            pl.BlockSpec(mt.shape, lambda b, j: (0, 0)),
            whole(kts), whole(vs), whole(ktw), whole(vw),
        ],
        out_specs=pl.BlockSpec((1, Q_BLOCK, nq.shape[-1]), lambda b, j: (b, j, 0)),
        out_shape=jax.ShapeDtypeStruct(nq.shape, f32),
        compiler_params=pltpu.CompilerParams(dimension_semantics=("arbitrary", "arbitrary"),
                                             vmem_limit_bytes=48 * 1024 * 1024),
        name="nsa_prompt",
    )(nq, gates, kct, vc2, mt, kts, vs, ktw, vw)


def gather_pages(pool, page_table):
    g = pool[page_table]
    return g.reshape((g.shape[0], g.shape[1] * g.shape[2]) + g.shape[3:])


def memory_kv(mem, g_mem, w_k, w_v):
    mn = rmsnorm(mem, g_mem)
    B, M = mem.shape[:2]
    k = (mn @ w_k).reshape(B, M, MEM_HEADS, MEM_HEAD_DIM)
    v = (mn @ w_v).reshape(B, M, MEM_HEADS, MEM_HEAD_DIM)
    return jnp.stack([k, v], axis=2)


def cross_attend(xn, mem_kv, w_q, w_o):
    B, T = xn.shape[:2]
    q = (xn @ w_q).reshape(B, T, MEM_HEADS, MEM_HEAD_DIM)
    s = jnp.einsum('bthd,bmhd->bhtm', q, mem_kv[:, :, 0]) * MEM_HEAD_DIM ** -0.5
    p = jax.nn.softmax(s.astype(f32), axis=-1)
    o = jnp.einsum('bhtm,bmhd->bthd', p, mem_kv[:, :, 1].astype(f32)).reshape(B, T, -1)
    return (o.astype(xn.dtype) @ w_o).astype(xn.dtype)


PEER_PICKS = PEER_HEADS * PEER_TOPK
PEER_TOKENS_PER_STEP = 64
PEER_GROUP = 4
PEER_FETCH_AHEAD = 2
PEER_ROW_BUFFERS = 4 * PEER_GROUP
PEER_SLAB_LANES = 128
PEER_SLAB_ROWS = 2 * D_MODEL // PEER_SLAB_LANES


def _gelu_tanh(x):
    return 0.5 * x * (1.0 + jnp.tanh(0.7978845608028654 * (x + 0.044715 * x * x * x)))


def _peer_expert_body(ids_ref, ids_next_ref, x_ref, g_ref, seg_ref, uv_ref, o_ref, rows, sems):
    tokens = x_ref.shape[0]
    depth = PEER_ROW_BUFFERS
    half, lanes = PEER_SLAB_ROWS // 2, PEER_SLAB_LANES
    cols = PEER_PICKS * half
    seg = seg_ref.shape[0]
    nt = (((1,), (1,)), ((), ()))
    bf16 = jnp.bfloat16

    col_row = lax.broadcasted_iota(jnp.int32, (half, cols), 1) & (half - 1)
    sub = lax.broadcasted_iota(jnp.int32, (half, cols), 0)
    diag = jnp.where(col_row == sub, 1.0, 0.0)

    group = PEER_GROUP
    groups = tokens // group
    n_seg = cols // seg
    step_id = pl.program_id(0)
    last_step = pl.num_programs(0) - 1

    def wait(slot):
        pltpu.make_async_copy(uv_ref.at[pl.ds(0, PEER_PICKS)], rows.at[slot], sems.at[slot]).wait()

    def fetcher(ids, t0, slot0):
        per = PEER_PICKS // 2

        def fetch(c):
            j, h = divmod(c, 2)
            for k in range(h * per, (h + 1) * per):
                pltpu.make_async_copy(uv_ref.at[ids[t0 + j, k]], rows.at[slot0 + j, k],
                                      sems.at[slot0 + j]).start(priority=k % 2)
        return fetch

    def mix_group(t0, slot0, fetch):
        parts = []
        for j in range(group):
            u_rows = rows[slot0 + j, :, :half, :].reshape(cols, lanes).astype(bf16)
            prod = lax.dot_general(x_ref[t0 + j].astype(bf16), u_rows, nt, preferred_element_type=f32)
            part = jnp.sum(prod * diag, axis=0, keepdims=True)
            parts += [part[:, i * seg:(i + 1) * seg] for i in range(n_seg)]
            fetch(j)
        part = jnp.concatenate(parts, axis=0)
        hi = part.astype(bf16)
        lo = (part - hi.astype(f32)).astype(bf16)
        ones = seg_ref[...]
        act = jnp.dot(hi, ones, preferred_element_type=f32) + jnp.dot(lo, ones, preferred_element_type=f32)
        for j in range(group):
            w = g_ref[t0 + j] * _gelu_tanh(act[j * n_seg:(j + 1) * n_seg])
            w = jnp.concatenate([jnp.broadcast_to(w[i:i + 1, :], (half, seg)) for i in range(n_seg)], axis=1)
            v_rows = rows[slot0 + j, :, half:, :].reshape(cols, lanes).astype(bf16)
            o_ref[t0 + j] = jnp.dot((w * diag).astype(bf16), v_rows, preferred_element_type=f32)
            fetch(group + j)

    sets, ahead = depth // group, PEER_FETCH_AHEAD

    @pl.when(step_id == 0)
    def _():
        for a in range(ahead):
            first = fetcher(ids_ref, a * group, a * group)
            for c in range(2 * group):
                first(c)

    def sweep(it, last):
        for q in range(sets):
            g = sets * it + q
            for j in range(group):
                wait(q * group + j)
            into = ((q + ahead) % sets) * group
            if last and q + ahead >= sets:
                fetch = fetcher(ids_next_ref, (q + ahead - sets) * group, into)
            else:
                fetch = fetcher(ids_ref, (g + ahead) * group, into)
            mix_group(g * group, q * group, fetch)

    def body(it, carry):
        sweep(it, False)
        return carry

    lax.fori_loop(0, groups // sets - 1, body, 0)
    sweep(groups // sets - 1, True)

    @pl.when(step_id == last_step)
    def _():
        for j in range(ahead * group):
            wait(j)


def peer_experts(xn, ids, gates, uv):
    n, d = xn.shape
    tb = PEER_TOKENS_PER_STEP
    slab, half, lanes = PEER_SLAB_ROWS, PEER_SLAB_ROWS // 2, PEER_SLAB_LANES
    seg = 2 * lanes
    cols = PEER_PICKS * half
    assert n % tb == 0 and tb % PEER_ROW_BUFFERS == 0 and half * lanes == d and cols % seg == 0
    assert 0 < PEER_FETCH_AHEAD < PEER_ROW_BUFFERS // PEER_GROUP
    same_pick = np.arange(seg)[:, None] // half == np.arange(seg)[None, :] // half
    out = pl.pallas_call(
        _peer_expert_body,
        grid=(n // tb,),
        in_specs=[
            pl.BlockSpec((tb, PEER_PICKS), lambda i: (i, 0), memory_space=pltpu.SMEM),
            pl.BlockSpec((tb, PEER_PICKS), lambda i: (jnp.minimum(i + 1, n // tb - 1), 0), memory_space=pltpu.SMEM),
            pl.BlockSpec((tb, half, lanes), lambda i: (i, 0, 0)),
            pl.BlockSpec((tb, cols // seg, seg), lambda i: (i, 0, 0)),
            pl.BlockSpec((seg, seg), lambda i: (0, 0)),
            pl.BlockSpec(memory_space=pl.ANY),
        ],
        out_specs=pl.BlockSpec((tb, half, lanes), lambda i: (i, 0, 0)),
        out_shape=jax.ShapeDtypeStruct((n, half, lanes), f32),
        scratch_shapes=[
            pltpu.VMEM((PEER_ROW_BUFFERS, PEER_PICKS, slab, lanes), f32),
            pltpu.SemaphoreType.DMA((PEER_ROW_BUFFERS,)),
        ],
        compiler_params=pltpu.CompilerParams(dimension_semantics=("arbitrary",)),
        name="peer_experts",
    )(ids, ids, xn.reshape(n, half, lanes), jnp.repeat(gates, half, axis=1).reshape(n, cols // seg, seg),
      jnp.asarray(same_pick, jnp.bfloat16), uv)
    return out.reshape(n, d)


PEER_ROUTE_TOKENS = 128
PEER_HALF_DIM = PEER_QUERY_DIM // 2


def _top_rows(s, row_id, k, payload=None):
    vals, picks = [], []
    sentinel = s.shape[0]
    for _ in range(k):
        m = jnp.max(s, axis=0, keepdims=True)
        first = jnp.min(jnp.where(s == m, row_id, sentinel), axis=0, keepdims=True)
        hit = row_id == first
        vals.append(m)
        if payload is None:
            picks.append(first)
        else:
            picks.append(jnp.max(jnp.where(hit, payload, -1), axis=0, keepdims=True))
        s = jnp.where(hit, -jnp.inf, s)
    return jnp.concatenate(vals, 0), jnp.concatenate(picks, 0)


def _peer_route_body(x_ref, wq_ref, sk_ref, ids_ref, gate_ref, q_scr):
    tn = x_ref.shape[0]
    q = jnp.dot(x_ref[...].astype(jnp.bfloat16), wq_ref[...], preferred_element_type=f32)
    for j in range(2 * PEER_HEADS):
        q_scr[j] = q[:, j * PEER_HALF_DIM:(j + 1) * PEER_HALF_DIM].astype(jnp.bfloat16)
    key_id = lax.broadcasted_iota(jnp.int32, (PEER_N_KEYS, tn), 0)
    n_cand = -(-sum(PEER_TOPK // (a + 1) for a in range(PEER_TOPK)) // 8) * 8
    cand_id = lax.broadcasted_iota(jnp.int32, (n_cand, tn), 0)

    def head(h, carry):
        tops = []
        for p in range(2):
            s = lax.dot_general(sk_ref[2 * h + p], q_scr[2 * h + p], (((1,), (1,)), ((), ())),
                                preferred_element_type=f32)
            tops.append(_top_rows(s, key_id, PEER_TOPK))
        (v0, i0), (v1, i1) = tops
        cs, ce = [], []
        for a in range(PEER_TOPK):
            nb = PEER_TOPK // (a + 1)
            cs.append(v0[a:a + 1] + v1[:nb])
            ce.append(i0[a:a + 1] * PEER_N_KEYS + i1[:nb])
        pad = n_cand - sum(c.shape[0] for c in cs)
        cand_s = jnp.concatenate(cs + [jnp.full((pad, tn), -jnp.inf, f32)], axis=0)
        cand_e = jnp.concatenate(ce + [jnp.zeros((pad, tn), jnp.int32)], axis=0)
        top_s, top_e = _top_rows(cand_s, cand_id, PEER_TOPK, payload=cand_e)
        e = jnp.exp(top_s - top_s[0:1])
        ids_ref[h] = top_e
        gate_ref[h] = e / jnp.sum(e, axis=0, keepdims=True)
        return carry

    lax.fori_loop(0, PEER_HEADS, head, 0)


def peer_route(xn, wq, sub_keys):
    n, d = xn.shape
    tn = PEER_ROUTE_TOKENS
    assert n % tn == 0
    n_q = 2 * PEER_HEADS * PEER_HALF_DIM
    sk = sub_keys.reshape(2 * PEER_HEADS, PEER_N_KEYS, PEER_HALF_DIM).astype(jnp.bfloat16)
    ids_t, gates_t = pl.pallas_call(
        _peer_route_body,
        grid=(n // tn,),
        in_specs=[
            pl.BlockSpec((tn, d), lambda i: (i, 0)),
            pl.BlockSpec((d, n_q), lambda i: (0, 0)),
            pl.BlockSpec((2 * PEER_HEADS, PEER_N_KEYS, PEER_HALF_DIM), lambda i: (0, 0, 0)),
        ],
        out_specs=[
            pl.BlockSpec((PEER_HEADS, PEER_TOPK, tn), lambda i: (0, 0, i)),
            pl.BlockSpec((PEER_HEADS, PEER_TOPK, tn), lambda i: (0, 0, i)),
        ],
        out_shape=[
            jax.ShapeDtypeStruct((PEER_HEADS, PEER_TOPK, n), jnp.int32),
            jax.ShapeDtypeStruct((PEER_HEADS, PEER_TOPK, n), f32),
        ],
        scratch_shapes=[pltpu.VMEM((2 * PEER_HEADS, tn, PEER_HALF_DIM), jnp.bfloat16)],
        compiler_params=pltpu.CompilerParams(dimension_semantics=("arbitrary",)),
        name="peer_route",
    )(xn, wq.astype(jnp.bfloat16), sk)
    to_rows = lambda a: a.reshape(PEER_PICKS, n).T
    return to_rows(ids_t), to_rows(gates_t)


def peer_ffn_tokens(xn, wq, sub_keys, uv):
    ids, gates = peer_route(xn, wq, sub_keys)
    return peer_experts(xn, ids, gates, uv)


def kernel(x_prompt, x_sample, cache_cmp_kv, cache_slc_kv, cache_win_kv, state_mlstm_c, state_mlstm_n,
           state_mlstm_m, cache_mem_kv, page_table, mem_prompt, g_mix, w_in, b_in, b_forget, ml_head_gain,
           cmp_pe, cmp_w1, cmp_b1, cmp_w2, w_out, g_xattn, g_mem, w_xq, w_xk, w_xv, w_xo, g_ffn, peer_wq,
           peer_sub_keys, peer_u, peer_v, g_final):
    B, T = x_prompt.shape[:2]
    DB, S = x_sample.shape[:2]
    past_len = page_table.shape[1] * PAGE_SIZE
    l = 0
    bg = lambda a: jnp.swapaxes(a, 1, 2)

    o_i, o_nq, o_nkv, o_ng = IN_OFFSETS[3], IN_OFFSETS[5], IN_OFFSETS[6], IN_OFFSETS[7]
    n_small = 2 * ML_HEADS + 3 * NSA_HEADS
    small_pad = 128 - n_small
    regroup = lambda a: jnp.concatenate(
        [a[..., :o_i], a[..., o_nq:o_nkv], a[..., o_nkv:o_ng], a[..., o_i:o_nq], a[..., o_ng:],
         jnp.zeros(a.shape[:-1] + (small_pad,), a.dtype)], axis=-1)
    w_in_g, b_in_g = regroup(w_in[l]), regroup(b_in[l])
    in_splits = (4 * ML_WIDTH, NSA_WIDTH, 6 * NSA_KV_LANES, n_small + small_pad)

    def project(x):
        b, t, _ = x.shape
        ml, nq, nkv, small = fused_linear(x.reshape(b * t, D_MODEL), w_in_g, pre_gain=g_mix[l], bias=b_in_g,
                                          splits=in_splits)
        ng = jax.nn.sigmoid(small[:, 2 * ML_HEADS:n_small]).reshape(b, t, 3 * NSA_HEADS)
        return (ml.reshape(b, t, 4 * ML_WIDTH), small.reshape(b, t, -1), nq.reshape(b, t, NSA_WIDTH),
                nkv.reshape(b, t, 6 * NSA_KV_LANES), ng)

    def after_mixers(x, h_ml, h_nsa, mem_kv):
        b, t, _ = x.shape
        h = jnp.concatenate([h_ml, h_nsa], -1).reshape(b * t, D_MODEL)
        x1, xn = fused_linear(h, w_out[l], residual=x.reshape(b * t, D_MODEL), post_gain=g_xattn[l])
        q = fused_linear(xn, w_xq[l]).reshape(b, t, D_MODEL)
        t_pad = -t % 8
        o = cross_attention(jnp.pad(q, ((0, 0), (0, t_pad), (0, 0))), mem_kv)[:, :t]
        return fused_linear(o.reshape(b * t, D_MODEL), w_xo[l], residual=x1, post_gain=g_ffn[l])

    ml, small, nq, nkv, ng = project(x_prompt)
    h_ml, p_c, p_n, p_m = mlstm_pallas(
        ml, small, b_forget[l], ml_head_gain[l],
        jnp.zeros((B, ML_HEADS, ML_HEAD_DIM, ML_HEAD_DIM), f32),
        jnp.zeros((B, ML_HEADS, ML_HEAD_DIM), f32),
        jnp.full((B, ML_HEADS), -jnp.inf, f32))
    rows6 = nkv.reshape(B, T, 6, NSA_KV_HEADS, NSA_HEAD_DIM)
    p_cmp, p_slc, win_rows = rows6[:, :, 0:2], rows6[:, :, 2:4], rows6[:, :, 4:6]
    kc, _ = nsa_compress(bg(p_cmp[:, :, 0]), cmp_pe[l, 0], cmp_w1[l, 0], cmp_b1[l, 0], cmp_w2[l, 0])
    vc, _ = nsa_compress(bg(p_cmp[:, :, 1]), cmp_pe[l, 1], cmp_w1[l, 1], cmp_b1[l, 1], cmp_w2[l, 1])
    h_nsa = nsa_prompt(nq, ng, nkv, kc, vc)
    p_mem = fused_linear(mem_prompt.reshape(-1, D_MODEL), jnp.concatenate([w_xk[l], w_xv[l]], axis=1),
                         pre_gain=g_mem[l]).reshape(B, MEM_LEN, 2, MEM_HEADS, MEM_HEAD_DIM)
    xp, xp_ffn_in = after_mixers(x_prompt, h_ml, h_nsa, p_mem)
    p_win = win_rows[:, T - min(WINDOW, T):]

    ml, small, nq, nkv, ng = project(x_sample)
    t_pad = ML_CHUNK - S
    ml_p = jnp.pad(ml, ((0, 0), (0, t_pad), (0, 0)))
    small_p = jnp.pad(small, ((0, 0), (0, t_pad), (0, 0)))
    small_p = small_p.at[:, S:, :ML_HEADS].set(-jnp.inf).at[:, S:, ML_HEADS:2 * ML_HEADS].set(jnp.inf)
    h_ml, s_c, s_n, s_m = mlstm_pallas(ml_p, small_p, b_forget[l], ml_head_gain[l],
                                       state_mlstm_c[l], state_mlstm_n[l], state_mlstm_m[l])
    h_ml = h_ml[:, :S]
    rows6 = nkv.reshape(DB, S, 6, NSA_KV_HEADS, NSA_HEAD_DIM)
    s_cmp, s_slc, win_rows = rows6[:, :, 0:2], rows6[:, :, 2:4], rows6[:, :, 4:6]
    cmp_full = jnp.concatenate([gather_pages(cache_cmp_kv[l], page_table), s_cmp], axis=1)
    slc_full = jnp.concatenate([gather_pages(cache_slc_kv[l], page_table), s_slc], axis=1)
    win_buf = cache_win_kv[l]
    win_ext = jnp.concatenate([win_buf, win_rows], axis=1)
    h_nsa = nsa_mixer(nq.reshape(DB, S, NSA_HEADS, NSA_HEAD_DIM), ng.reshape(DB, S, NSA_HEADS, 3), past_len,
                      cmp_full, slc_full, win_ext, past_len - win_buf.shape[1],
                      cmp_pe[l], cmp_w1[l], cmp_b1[l], cmp_w2[l])
    xs, xs_ffn_in = after_mixers(x_sample, h_ml, h_nsa, cache_mem_kv[l])
    w_keep = min(WINDOW, past_len + S)
    s_win = win_ext[:, win_ext.shape[1] - w_keep:]

    n_p = B * T
    uv = jnp.concatenate([peer_u[l], peer_v[l]], axis=1).reshape(-1, PEER_SLAB_ROWS, PEER_SLAB_LANES)
    ffn = peer_ffn_tokens(jnp.concatenate([xp_ffn_in, xs_ffn_in], 0), peer_wq[l], peer_sub_keys[l], uv)
    y_prompt = add_rmsnorm(xp, ffn[:n_p], g_final).reshape(x_prompt.shape)
    y_sample = add_rmsnorm(xs, ffn[n_p:], g_final).reshape(x_sample.shape)
    st = lambda a: a[None]
    return (y_prompt, y_sample,
            st(p_cmp), st(p_slc), st(p_win), st(p_c), st(p_n), st(p_m), st(p_mem),
            st(s_cmp), st(s_slc), st(s_win), st(s_c), st(s_n), st(s_m))
```

```python
import functools

import jax
import jax.numpy as jnp
from jax import lax
import numpy as np
from jax.experimental import pallas as pl
from jax.experimental.pallas import tpu as pltpu

D_MODEL = 1024
DEPTH = 1
PAGE_SIZE = 128

ML_WIDTH = D_MODEL // 2
ML_HEADS = 4
ML_HEAD_DIM = ML_WIDTH // ML_HEADS
ML_CHUNK = 128
NSA_WIDTH = D_MODEL - ML_WIDTH
NSA_HEADS = 8
NSA_HEAD_DIM = NSA_WIDTH // NSA_HEADS
NSA_KV_HEADS = 2
NSA_GROUP = NSA_HEADS // NSA_KV_HEADS
CMP_BLOCK = 32
CMP_STRIDE = 16
SEL_BLOCK = 64
N_SEL = 16
WINDOW = 512
Q_BLOCK = 128
FORCED_SCORE = 1.0e4
INVALID_SCORE = -1.0
MEM_LEN = 256
MEM_HEADS = 4
MEM_HEAD_DIM = D_MODEL // MEM_HEADS
PEER_HEADS = 8
PEER_N_KEYS = 128
PEER_N_EXPERTS = PEER_N_KEYS * PEER_N_KEYS
PEER_TOPK = 16
PEER_QUERY_DIM = 256
PEER_TOKEN_BLOCK = 128
EPS = 1e-6
TINY = 1e-30
IN_SPLITS = (ML_WIDTH, ML_WIDTH, ML_WIDTH, ML_WIDTH, ML_HEADS, ML_HEADS,
             NSA_WIDTH, 6 * NSA_KV_HEADS * NSA_HEAD_DIM, 3 * NSA_HEADS)
IN_COLS = sum(IN_SPLITS)
IN_OFFSETS = tuple(int(o) for o in np.cumsum(IN_SPLITS)[:-1])

f32 = jnp.float32


def rmsnorm(x, g):
    xf = x.astype(f32)
    return (xf * lax.rsqrt(jnp.mean(xf * xf, -1, keepdims=True) + EPS) * g).astype(x.dtype)


ROW_TILE = 256
DENSE_VMEM_BYTES = 56 * 1024 * 1024


def _rms(x, g):
    return x * lax.rsqrt(jnp.mean(x * x, -1, keepdims=True) + EPS) * g


def _add_norm_body(x_ref, r_ref, g_ref, o_ref):
    o_ref[...] = _rms(x_ref[...] + r_ref[...], g_ref[...])


def add_rmsnorm(x, r, g):
    shape = x.shape
    d = shape[-1]
    x2, r2 = x.reshape(-1, d), r.reshape(-1, d)
    n = x2.shape[0]
    tm = min(n, ROW_TILE)
    rows = pl.BlockSpec((tm, d), lambda i: (i, 0))
    out = pl.pallas_call(
        _add_norm_body,
        grid=(n // tm,),
        in_specs=[rows, rows, pl.BlockSpec((1, d), lambda i: (0, 0))],
        out_specs=rows,
        out_shape=jax.ShapeDtypeStruct((n, d), f32),
        name="add_rmsnorm",
    )(x2, r2, g.reshape(1, d))
    return out.reshape(shape)


def _linear_body(*refs, pre_norm, has_bias, has_res, post_norm, splits):
    it = iter(refs)
    x_ref, w_ref = next(it), next(it)
    x = x_ref[...]
    if pre_norm:
        x = _rms(x, next(it)[...])
    y = jnp.dot(x.astype(jnp.bfloat16), w_ref[...], preferred_element_type=f32)
    if has_bias:
        y = y + next(it)[...]
    if has_res:
        y = y + next(it)[...]
    post_gain = next(it)[...] if post_norm else None
    off = 0
    for m in splits:
        next(it)[...] = y[:, off:off + m]
        off += m
    if post_norm:
        next(it)[...] = _rms(y, post_gain)


def fused_linear(x, w, *, pre_gain=None, bias=None, residual=None, post_gain=None, splits=None):
    n, k = x.shape
    m = w.shape[1]
    splits = (m,) if splits is None else tuple(splits)
    assert sum(splits) == m and (post_gain is None or len(splits) == 1)
    tm = min(n, ROW_TILE)
    assert n % tm == 0
    row = lambda c: pl.BlockSpec((tm, c), lambda i: (i, 0))
    const = lambda r, c: pl.BlockSpec((r, c), lambda i: (0, 0))
    args, specs = [x, w.astype(jnp.bfloat16)], [row(k), const(k, m)]
    if pre_gain is not None:
        args.append(pre_gain.reshape(1, k)); specs.append(const(1, k))
    if bias is not None:
        args.append(bias.reshape(1, m)); specs.append(const(1, m))
    if residual is not None:
        args.append(residual); specs.append(row(m))
    if post_gain is not None:
        args.append(post_gain.reshape(1, m)); specs.append(const(1, m))
    out_cols = splits + ((m,) if post_gain is not None else ())
    outs = pl.pallas_call(
        functools.partial(_linear_body, pre_norm=pre_gain is not None, has_bias=bias is not None,
                          has_res=residual is not None, post_norm=post_gain is not None, splits=splits),
        grid=(n // tm,),
        in_specs=specs,
        out_specs=[row(c) for c in out_cols],
        out_shape=[jax.ShapeDtypeStruct((n, c), f32) for c in out_cols],
        compiler_params=pltpu.CompilerParams(dimension_semantics=("arbitrary",),
                                             vmem_limit_bytes=DENSE_VMEM_BYTES),
        name="fused_linear",
    )(*args)
    return outs[0] if len(outs) == 1 else tuple(outs)


def _xattn_body(q_ref, kv_ref, o_ref):
    bf16 = jnp.bfloat16
    d = MEM_HEAD_DIM
    q = q_ref[0]
    kv = kv_ref[0].astype(bf16)
    for h in range(MEM_HEADS):
        k_h = kv[:, h * d:(h + 1) * d]
        v_h = kv[:, (MEM_HEADS + h) * d:(MEM_HEADS + h + 1) * d]
        s = lax.dot_general(q[:, h * d:(h + 1) * d].astype(bf16), k_h, (((1,), (1,)), ((), ())),
                            preferred_element_type=f32) * (d ** -0.5)
        e = jnp.exp(s - jnp.max(s, axis=-1, keepdims=True))
        p = e / jnp.sum(e, axis=-1, keepdims=True)
        o_ref[0, :, h * d:(h + 1) * d] = jnp.dot(p.astype(bf16), v_h, preferred_element_type=f32)


def cross_attention(q, mem_kv):
    B, T, w = q.shape
    M = mem_kv.shape[1]
    kv = mem_kv.reshape(B, M, 2 * w)
    tm = min(T, ROW_TILE)
    assert T % tm == 0 and tm % 8 == 0
    return pl.pallas_call(
        _xattn_body,
        grid=(B, T // tm),
        in_specs=[pl.BlockSpec((1, tm, w), lambda b, i: (b, i, 0)),
                  pl.BlockSpec((1, M, 2 * w), lambda b, i: (b, 0, 0))],
        out_specs=pl.BlockSpec((1, tm, w), lambda b, i: (b, i, 0)),
        out_shape=jax.ShapeDtypeStruct((B, T, w), f32),
        compiler_params=pltpu.CompilerParams(dimension_semantics=("arbitrary", "arbitrary")),
        name="cross_attention",
    )(q, kv)


def masked_softmax(s, mask):
    s = jnp.where(mask, s.astype(f32), -jnp.inf)
    mx = jnp.max(s, -1, keepdims=True)
    mx = jnp.where(jnp.isfinite(mx), mx, 0.0)
    e = jnp.exp(s - mx)
    return e / jnp.maximum(e.sum(-1, keepdims=True), TINY)


def mixer_projections(xn, w_in, b_in, b_forget):
    B, T = xn.shape[:2]
    p = xn @ w_in + b_in
    ml_q, ml_k, ml_v, ml_o, ml_i, ml_f, nq, nkv, ng = jnp.split(p, IN_OFFSETS, axis=-1)
    hd = (B, T, ML_HEADS, ML_HEAD_DIM)
    return (ml_q.reshape(hd), ml_k.reshape(hd), ml_v.reshape(hd), ml_o, ml_i, ml_f + b_forget,
            nq.reshape(B, T, NSA_HEADS, NSA_HEAD_DIM),
            nkv.reshape(B, T, 6, NSA_KV_HEADS, NSA_HEAD_DIM),
            jax.nn.sigmoid(ng).reshape(B, T, NSA_HEADS, 3))


def mlstm_chunk(carry, inp):
    c, n, m = carry
    q, k, v, ig, lf = inp
    L = q.shape[2]
    b = jnp.cumsum(lf, axis=-1)
    causal = jnp.tril(jnp.ones((L, L), bool))
    log_d = jnp.where(causal, b[..., :, None] - b[..., None, :] + ig[..., None, :], -jnp.inf)
    inter = b + m[..., None]
    m_t = jnp.maximum(inter, log_d.max(-1))
    d_mat = jnp.exp(log_d - m_t[..., None])
    a = jnp.exp(inter - m_t)
    qk = jnp.einsum('bhtd,bhsd->bhts', q, k) * d_mat
    num = a[..., None] * jnp.einsum('bhtd,bhde->bhte', q, c) + jnp.einsum('bhts,bhse->bhte', qk, v)
    den = a * jnp.einsum('bhtd,bhd->bht', q, n) + qk.sum(-1)
    h = num / jnp.maximum(jnp.abs(den), jnp.exp(-m_t))[..., None]
    m_new = m_t[..., -1]
    w = jnp.exp(b[..., -1:] - b + ig - m_new[..., None])
    decay = jnp.exp(b[..., -1] + m - m_new)
    c_new = decay[..., None, None] * c + jnp.einsum('bhs,bhsd,bhse->bhde', w, k, v)
    n_new = decay[..., None] * n + jnp.einsum('bhs,bhsd->bhd', w, k)
    return (c_new, n_new, m_new), h


def mlstm_mixer(q, k, v, i_pre, f_pre, o_pre, head_gain, c0, n0, m0):
    B, T, H, d = q.shape
    L = ML_CHUNK if T % ML_CHUNK == 0 else T
    nc = T // L

    def chunks(a):
        a = a.astype(f32).reshape((B, nc, L) + a.shape[2:])
        return jnp.transpose(a, (1, 0, 3, 2) + tuple(range(4, a.ndim)))

    xs = (chunks(q), chunks(k * d ** -0.5), chunks(v), chunks(i_pre),
          chunks(jax.nn.log_sigmoid(f_pre.astype(f32))))
    (c, n, m), h = lax.scan(mlstm_chunk, (c0.astype(f32), n0.astype(f32), m0.astype(f32)), xs)
    h = jnp.transpose(h, (1, 0, 3, 2, 4)).reshape(B, T, H, d)
    h = jax.nn.sigmoid(o_pre.astype(f32)).reshape(B, T, H, d) * h
    h = h * lax.rsqrt(jnp.mean(h * h, -1, keepdims=True) + EPS) * head_gain
    return h.reshape(B, T, H * d).astype(q.dtype), c, n, m


def _log_sigmoid(x):
    return jnp.minimum(x, 0.0) - jnp.log1p(jnp.exp(-jnp.abs(x)))


def _cumsum_lanes(x):
    lane = lax.broadcasted_iota(jnp.int32, x.shape, 1)
    shift = 1
    while shift < x.shape[1]:
        x = x + jnp.where(lane >= shift, pltpu.roll(x, shift, axis=1), 0.0)
        shift *= 2
    return x


def _mlstm_body(ml_ref, small_ref, bias_ref, gain_ref, c0_ref, n0_ref, m0_ref, h_ref, c_ref, n_ref, m_ref):
    H, d, L = ML_HEADS, ML_HEAD_DIM, ml_ref.shape[1]
    bf16 = jnp.bfloat16
    nt = (((1,), (1,)), ((), ()))
    tn = (((0,), (0,)), ((), ()))

    @pl.when(pl.program_id(1) == 0)
    def _():
        c_ref[...] = c0_ref[...]
        n_ref[...] = n0_ref[...]
        m_ref[...] = m0_ref[...]

    blk = ml_ref[0]
    small = small_ref[0] + bias_ref[...]
    small_t = small.T
    causal = (lax.broadcasted_iota(jnp.int32, (L, L), 0) >= lax.broadcasted_iota(jnp.int32, (L, L), 1))
    for h in range(H):
        q = blk[:, h * d:(h + 1) * d]
        k = blk[:, (H + h) * d:(H + h + 1) * d] * (d ** -0.5)
        v = blk[:, (2 * H + h) * d:(2 * H + h + 1) * d]
        o_pre = blk[:, (3 * H + h) * d:(3 * H + h + 1) * d]
        ig_row, ig_col = small_t[h:h + 1, :], small[:, h:h + 1]
        b_row = _cumsum_lanes(_log_sigmoid(small_t[H + h:H + h + 1, :]))
        b_col = jnp.broadcast_to(b_row, (L, L)).T
        b_t, b_last = b_col[:, 0:1], b_row[:, L - 1:L]
        c_prev, n_prev, m_prev = c_ref[0, h], n_ref[0, h], m_ref[0, h]
        log_d = jnp.where(causal, b_col - b_row + ig_row, -jnp.inf)
        inter = b_t + m_prev
        m_t = jnp.maximum(inter, jnp.max(log_d, axis=1, keepdims=True))
        a = jnp.exp(inter - m_t)
        qb, vb = q.astype(bf16), v.astype(bf16)
        qk = lax.dot_general(qb, k.astype(bf16), nt, preferred_element_type=f32) * jnp.exp(log_d - m_t)
        num = (a * jnp.dot(qb, c_prev.astype(bf16), preferred_element_type=f32)
               + jnp.dot(qk.astype(bf16), vb, preferred_element_type=f32))
        den = a * jnp.sum(q * n_prev, axis=1, keepdims=True) + jnp.sum(qk, axis=1, keepdims=True)
        hid = num / jnp.maximum(jnp.abs(den), jnp.exp(-m_t))
        m_new = m_t[L - 1:L, :]
        kw = k * jnp.exp(b_last - b_t + ig_col - m_new)
        decay = jnp.exp(b_last + m_prev - m_new)
        c_ref[0, h] = decay * c_prev + lax.dot_general(kw.astype(bf16), vb, tn, preferred_element_type=f32)
        n_ref[0, h] = decay * n_prev + jnp.sum(kw, axis=0, keepdims=True)
        m_ref[0, h] = m_new
        gated = jax.nn.sigmoid(o_pre) * hid
        h_ref[0, :, h * d:(h + 1) * d] = _rms(gated, gain_ref[:, h * d:(h + 1) * d])


def mlstm_pallas(ml, small, b_forget, head_gain, c0, n0, m0):
    B, T, _ = ml.shape
    H, d, L = ML_HEADS, ML_HEAD_DIM, ML_CHUNK
    assert T % L == 0
    bias = jnp.zeros((1, small.shape[-1]), f32).at[0, H:2 * H].set(b_forget)
    state = lambda *s: pl.BlockSpec((1,) + s, lambda b, i: (b,) + (0,) * len(s))
    h, c, n, m = pl.pallas_call(
        _mlstm_body,
        grid=(B, T // L),
        in_specs=[
            pl.BlockSpec((1, L, ml.shape[-1]), lambda b, i: (b, i, 0)),
            pl.BlockSpec((1, L, small.shape[-1]), lambda b, i: (b, i, 0)),
            pl.BlockSpec((1, small.shape[-1]), lambda b, i: (0, 0)),
            pl.BlockSpec((1, H * d), lambda b, i: (0, 0)),
            state(H, d, d), state(H, 1, d), state(H, 1, 1),
        ],
        out_specs=[pl.BlockSpec((1, L, H * d), lambda b, i: (b, i, 0)),
                   state(H, d, d), state(H, 1, d), state(H, 1, 1)],
        out_shape=[jax.ShapeDtypeStruct((B, T, H * d), f32), jax.ShapeDtypeStruct((B, H, d, d), f32),
                   jax.ShapeDtypeStruct((B, H, 1, d), f32), jax.ShapeDtypeStruct((B, H, 1, 1), f32)],
        compiler_params=pltpu.CompilerParams(dimension_semantics=("arbitrary", "arbitrary")),
        name="mlstm",
    )(ml, small, bias, head_gain.reshape(1, H * d), c0, n0.reshape(B, H, 1, d), m0.reshape(B, H, 1, 1))
    return h, c, n.reshape(B, H, d), m.reshape(B, H)


def nsa_compress(rows, pe, w1, b1, w2):
    B, G, T, d = rows.shape
    n_seg = T // CMP_STRIDE
    seg = rows[:, :, :n_seg * CMP_STRIDE].reshape(B, G, n_seg, CMP_STRIDE * d)
    half = CMP_STRIDE * d
    pre = seg[:, :, :-1] @ w1[:half] + seg[:, :, 1:] @ w1[half:] + (pe.reshape(-1) @ w1 + b1)
    ends = jnp.arange(n_seg - 1) * CMP_STRIDE + (CMP_BLOCK - 1)
    return jax.nn.gelu(pre) @ w2, ends


def to_sel_blocks(rows):
    B, G, T, d = rows.shape
    nsb = -(-T // SEL_BLOCK)
    rows = jnp.pad(rows, ((0, 0), (0, 0), (0, nsb * SEL_BLOCK - T), (0, 0)))
    return rows.reshape(B, G, nsb, SEL_BLOCK, d)


def cmp_to_sel(imp, nsb):
    r = SEL_BLOCK // CMP_STRIDE
    nc = imp.shape[-1]
    lead = imp.shape[:-1]
    tot = r * nsb
    padw = [(0, 0)] * len(lead)
    first = jnp.pad(imp, padw + [(0, tot - nc)]).reshape(lead + (nsb, r)).sum(-1)
    second = jnp.pad(imp, padw + [(1, tot - nc - 1)]).reshape(lead + (nsb, r)).sum(-1)
    return 0.5 * (first + second)


def nsa_attend(q, gates, q_pos, kc, vc, c_end, ks, vs, kw, vw, w_pos):
    B, G, R, Q, d = q.shape
    scale = d ** -0.5
    p_c = masked_softmax(jnp.einsum('bgrqd,bgnd->bgrqn', q, kc) * scale, c_end[None, :] <= q_pos[:, None])
    o_c = jnp.einsum('bgrqn,bgnd->bgrqd', p_c, vc)
    nsb = ks.shape[2]
    imp = cmp_to_sel(p_c.sum(2), nsb)
    blk = jnp.arange(nsb)
    valid = blk[None, :] * SEL_BLOCK <= q_pos[:, None]
    forced = (blk[None, :] == 0) | (blk[None, :] == q_pos[:, None] // SEL_BLOCK)
    score = jnp.where(forced, FORCED_SCORE, jnp.where(valid, imp, INVALID_SCORE))
    _, idx = lax.top_k(score, min(N_SEL, nsb))
    bi = jnp.arange(B)[:, None, None, None]
    gi = jnp.arange(G)[None, :, None, None]
    kg = ks[bi, gi, idx].reshape(B, G, Q, -1, d)
    vg = vs[bi, gi, idx].reshape(B, G, Q, -1, d)
    k_pos = (idx[..., None] * SEL_BLOCK + jnp.arange(SEL_BLOCK)).reshape(B, G, Q, -1)
    p_s = masked_softmax(jnp.einsum('bgrqd,bgqsd->bgrqs', q, kg) * scale,
                         (k_pos <= q_pos[None, None, :, None])[:, :, None])
    o_s = jnp.einsum('bgrqs,bgqsd->bgrqd', p_s, vg)
    wmask = ((w_pos[None, :] <= q_pos[:, None]) & (w_pos[None, :] >= q_pos[:, None] - WINDOW)
             & (w_pos[None, :] >= 0))
    p_w = masked_softmax(jnp.einsum('bgrqd,bgkd->bgrqk', q, kw) * scale, wmask)
    o_w = jnp.einsum('bgrqk,bgkd->bgrqd', p_w, vw)
    return gates[..., 0:1] * o_c + gates[..., 1:2] * o_s + gates[..., 2:3] * o_w


def nsa_mixer(q, gates, q_pos0, cmp_rows, slc_rows, win_rows, win_pos0, cmp_pe, cmp_w1, cmp_b1, cmp_w2):
    B, Tq, _, d = q.shape
    G, R = NSA_KV_HEADS, NSA_GROUP
    bg = lambda a: jnp.swapaxes(a, 1, 2)
    kc, c_end = nsa_compress(bg(cmp_rows[:, :, 0]), cmp_pe[0], cmp_w1[0], cmp_b1[0], cmp_w2[0])
    vc, _ = nsa_compress(bg(cmp_rows[:, :, 1]), cmp_pe[1], cmp_w1[1], cmp_b1[1], cmp_w2[1])
    ks = to_sel_blocks(bg(slc_rows[:, :, 0]))
    vs = to_sel_blocks(bg(slc_rows[:, :, 1]))
    kw = bg(win_rows[:, :, 0])
    vw = bg(win_rows[:, :, 1])
    qg = q.reshape(B, Tq, G, R, d).transpose(0, 2, 3, 1, 4)
    gg = gates.reshape(B, Tq, G, R, 3).transpose(0, 2, 3, 1, 4)
    qb = Q_BLOCK if Tq % Q_BLOCK == 0 else Tq
    nb = Tq // qb
    if nb == 1:
        w_pos = win_pos0 + jnp.arange(kw.shape[2])
        o = nsa_attend(qg, gg, q_pos0 + jnp.arange(Tq), kc, vc, c_end, ks, vs, kw, vw, w_pos)
    else:
        off = q_pos0 - WINDOW - win_pos0
        span = WINDOW + qb

        def block(args):
            q_blk, g_blk, j = args
            start = j * qb
            kw_b = lax.dynamic_slice_in_dim(kw, start + off, span, axis=2)
            vw_b = lax.dynamic_slice_in_dim(vw, start + off, span, axis=2)
            w_pos = q_pos0 + start - WINDOW + jnp.arange(span)
            return nsa_attend(q_blk, g_blk, q_pos0 + start + jnp.arange(qb), kc, vc, c_end,
                              ks, vs, kw_b, vw_b, w_pos)

        qs = qg.reshape(B, G, R, nb, qb, d).transpose(3, 0, 1, 2, 4, 5)
        gs = gg.reshape(B, G, R, nb, qb, 3).transpose(3, 0, 1, 2, 4, 5)
        o = lax.map(block, (qs, gs, jnp.arange(nb)))
        o = o.transpose(1, 2, 3, 0, 4, 5).reshape(B, G, R, Tq, d)
    return o.transpose(0, 3, 1, 2, 4).reshape(B, Tq, NSA_HEADS * d)


NSA_SEL_CHUNK = 512
NSA_WIN_CHUNKS = WINDOW // Q_BLOCK + 1
NSA_KV_LANES = NSA_KV_HEADS * NSA_HEAD_DIM


def _softmax_masked(s, mask):
    sm = jnp.where(mask, s, -jnp.inf)
    mx = jnp.max(sm, axis=-1, keepdims=True)
    mx = jnp.where(mx == -jnp.inf, 0.0, mx)
    e = jnp.exp(sm - mx)
    return e * (1.0 / jnp.maximum(jnp.sum(e, axis=-1, keepdims=True), TINY))


def _top_rows_mask(s, row_id, k):
    sel = jnp.zeros(s.shape, f32)
    sentinel = s.shape[0]
    for _ in range(k):
        m = jnp.max(s, axis=0, keepdims=True)
        first = jnp.min(jnp.where(s == m, row_id, sentinel), axis=0, keepdims=True)
        hit = row_id == first
        sel = jnp.where(hit, 1.0, sel)
        s = jnp.where(hit, -jnp.inf, s)
    return sel


def _nsa_prompt_body(q_ref, gate_ref, kct_ref, vc_ref, mt_ref, kts_ref, vs_ref, ktw_ref, vw_ref, o_ref):
    j = pl.program_id(1)
    Q, R, d = Q_BLOCK, NSA_GROUP, NSA_HEAD_DIM
    n_cmp = kct_ref.shape[-1]
    n_blk = mt_ref.shape[0]
    bf16 = jnp.bfloat16
    q_all = q_ref[0] * (d ** -0.5)
    gates = gate_ref[0]
    q_pos = j * Q + lax.broadcasted_iota(jnp.int32, (Q, 1), 0)
    q_pos_l = j * Q + lax.broadcasted_iota(jnp.int32, (1, Q), 1)
    blk_r = lax.broadcasted_iota(jnp.int32, (n_blk, 1), 0)
    blk_id = lax.broadcasted_iota(jnp.int32, (n_blk, Q), 0)
    nt = (((1,), (1,)), ((), ()))

    for g in range(NSA_KV_HEADS):
        lanes = slice(g * d, (g + 1) * d)
        qg = jnp.concatenate([q_all[:, (g * R + r) * d:(g * R + r + 1) * d] for r in range(R)], axis=0).astype(bf16)

        s = jnp.dot(qg, kct_ref[0, g], preferred_element_type=f32).reshape(R, Q, n_cmp)
        n_id = lax.broadcasted_iota(jnp.int32, (1, n_cmp), 1)
        cmask = (n_id * CMP_STRIDE + (CMP_BLOCK - 1) <= q_pos) & (n_id < n_cmp - 1)
        p = _softmax_masked(s, cmask[None])
        o_c = jnp.dot(p.reshape(R * Q, n_cmp).astype(bf16), vc_ref[0], preferred_element_type=f32)[:, lanes]

        p_sum = p[0] + p[1] + p[2] + p[3]
        hi = p_sum.astype(bf16)
        lo = (p_sum - hi.astype(f32)).astype(bf16)
        mt = mt_ref[...]
        imp_t = (lax.dot_general(mt, hi, nt, preferred_element_type=f32)
                 + lax.dot_general(mt, lo, nt, preferred_element_type=f32))
        valid = blk_r * SEL_BLOCK <= q_pos_l
        forced = (blk_r == 0) | (blk_r == jnp.right_shift(q_pos_l, 6))
        score = jnp.where(forced, FORCED_SCORE, jnp.where(valid, imp_t, INVALID_SCORE))
        sel = _top_rows_mask(score, blk_id, min(N_SEL, n_blk)).T.astype(bf16)

        kc = kts_ref.shape[-1]

        def chunk(c, carry):
            m, l, acc = carry
            kt = kts_ref[0, c, g * d:(g + 1) * d, :]
            sc = jnp.dot(qg, kt, preferred_element_type=f32).reshape(R, Q, kc)
            key = c * kc + lax.broadcasted_iota(jnp.int32, (1, kc), 1)
            expand = jnp.where(blk_r == jnp.right_shift(key, 6), 1.0, 0.0).astype(bf16)
            picked = jnp.dot(sel, expand, preferred_element_type=f32) > 0.5
            mask = picked & (key <= q_pos)
            sm = jnp.where(mask[None], sc, -jnp.inf)
            m_new = jnp.maximum(m, jnp.max(sm, axis=-1, keepdims=True))
            m_safe = jnp.where(m_new == -jnp.inf, 0.0, m_new)
            alpha = jnp.exp(m - m_safe)
            pe = jnp.exp(sm - m_safe)
            l = l * alpha + jnp.sum(pe, axis=-1, keepdims=True)
            pv = jnp.dot(pe.reshape(R * Q, kc).astype(bf16), vs_ref[0, c], preferred_element_type=f32)
            acc = acc * alpha.reshape(R * Q, 1) + pv
            return m_new, l, acc

        n_chunks = (j * Q + Q + kc - 1) // kc
        init = (jnp.full((R, Q, 1), -jnp.inf, f32), jnp.zeros((R, Q, 1), f32),
                jnp.zeros((R * Q, NSA_KV_LANES), f32))
        _, l_s, acc_s = lax.fori_loop(0, n_chunks, chunk, init)
        o_s = acc_s[:, lanes] * (1.0 / jnp.maximum(l_s.reshape(R * Q, 1), TINY))

        ss, vv = [], []
        for i in range(NSA_WIN_CHUNKS):
            cc = jnp.maximum(j - (NSA_WIN_CHUNKS - 1) + i, 0)
            ss.append(jnp.dot(qg, ktw_ref[0, cc, g * d:(g + 1) * d, :], preferred_element_type=f32))
            vv.append(vw_ref[0, cc])
        span = NSA_WIN_CHUNKS * Q
        sw = jnp.concatenate(ss, axis=1).reshape(R, Q, span)
        k_pos = (j - (NSA_WIN_CHUNKS - 1)) * Q + lax.broadcasted_iota(jnp.int32, (1, span), 1)
        wmask = (k_pos >= 0) & (k_pos <= q_pos) & (k_pos >= q_pos - WINDOW)
        pw = _softmax_masked(sw, wmask[None])
        o_w = jnp.dot(pw.reshape(R * Q, span).astype(bf16), jnp.concatenate(vv, axis=0),
                      preferred_element_type=f32)[:, lanes]

        for r in range(R):
            h = g * R + r
            rows = slice(r * Q, (r + 1) * Q)
            o_ref[0, :, h * d:(h + 1) * d] = (gates[:, 3 * h:3 * h + 1] * o_c[rows]
                                              + gates[:, 3 * h + 1:3 * h + 2] * o_s[rows]
                                              + gates[:, 3 * h + 2:3 * h + 3] * o_w[rows])


def _cmp_to_sel_matrix(n_cmp, n_blk):
    r = SEL_BLOCK // CMP_STRIDE
    n = np.arange(n_cmp)
    b = np.arange(n_blk)[:, None]
    m = 0.5 * ((n // r == b).astype(np.float32) + ((n + 1) // r == b).astype(np.float32))
    m[:, n_cmp - 1] = 0.0
    return jnp.asarray(m, jnp.bfloat16)


def nsa_prompt(nq, gates, nkv, kc, vc):
    B, T, _ = nq.shape
    bf16 = jnp.bfloat16
    G, d, w = NSA_KV_HEADS, NSA_HEAD_DIM, NSA_KV_LANES
    n_cmp, n_blk = T // CMP_STRIDE, T // SEL_BLOCK
    kc_s, kc_w = NSA_SEL_CHUNK, Q_BLOCK
    pad = ((0, 0), (0, 0), (0, 1), (0, 0))
    kct = jnp.pad(kc, pad).transpose(0, 1, 3, 2).astype(bf16)
    vc2 = jnp.pad(vc, pad).transpose(0, 2, 1, 3).reshape(B, n_cmp, w).astype(bf16)
    chunks = lambda a, c: a.reshape(B, T // c, c, w).astype(bf16)
    kts = chunks(nkv[..., 2 * w:3 * w], kc_s).transpose(0, 1, 3, 2)
    vs = chunks(nkv[..., 3 * w:4 * w], kc_s)
    ktw = chunks(nkv[..., 4 * w:5 * w], kc_w).transpose(0, 1, 3, 2)
    vw = chunks(nkv[..., 5 * w:6 * w], kc_w)
    whole = lambda a: pl.BlockSpec((1,) + a.shape[1:], lambda b, j: (b,) + (0,) * (a.ndim - 1))
    mt = _cmp_to_sel_matrix(n_cmp, n_blk)
    return pl.pallas_call(
        _nsa_prompt_body,
        grid=(B, T // Q_BLOCK),
        in_specs=[
            pl.BlockSpec((1, Q_BLOCK, nq.shape[-1]), lambda b, j: (b, j, 0)),
            pl.BlockSpec((1, Q_BLOCK, gates.shape[-1]), lambda b, j: (b, j, 0)),
            whole(kct), whole(vc2),
            pl.BlockSpec(mt.shape, lambda b, j: (0, 0)),
            whole(kts), whole(vs), whole(ktw), whole(vw),
        ],
        out_specs=pl.BlockSpec((1, Q_BLOCK, nq.shape[-1]), lambda b, j: (b, j, 0)),
        out_shape=jax.ShapeDtypeStruct(nq.shape, f32),
        compiler_params=pltpu.CompilerParams(dimension_semantics=("arbitrary", "arbitrary"),
                                             vmem_limit_bytes=48 * 1024 * 1024),
        name="nsa_prompt",
    )(nq, gates, kct, vc2, mt, kts, vs, ktw, vw)


def nsa_decode(q, gates, past_len, pool_cmp, pool_slc, page_table, new_cmp, new_slc, win_rows, win_pos0,
               cmp_pe, cmp_w1, cmp_b1, cmp_w2):
    B, S, _ = q.shape
    G, R, d = NSA_KV_HEADS, NSA_GROUP, NSA_HEAD_DIM
    T = past_len + S
    scale = d ** -0.5
    q_pos = past_len + jnp.arange(S)
    qg = q.reshape(B, S, G, R, d).transpose(0, 2, 3, 1, 4)
    gg = gates.reshape(B, S, G, R, 3).transpose(0, 2, 3, 1, 4)

    n_seg = T // CMP_STRIDE
    rows = pool_cmp[page_table].reshape(B, past_len, 2, G, d)
    if n_seg * CMP_STRIDE > past_len:
        rows = jnp.concatenate([rows, new_cmp], axis=1)
    seg = rows[:, :n_seg * CMP_STRIDE].reshape(B, n_seg, CMP_STRIDE, 2, G, d)
    half = CMP_STRIDE * d

    def compress(kv):
        w1 = cmp_w1[kv]
        w_lo, w_hi = w1[:half].reshape(CMP_STRIDE, d, -1), w1[half:].reshape(CMP_STRIDE, d, -1)
        x = seg[:, :, :, kv]
        pre = (jnp.einsum('bntgd,tde->bgne', x[:, :-1], w_lo) + jnp.einsum('bntgd,tde->bgne', x[:, 1:], w_hi)
               + (cmp_pe[kv].reshape(-1) @ w1 + cmp_b1[kv]))
        return jax.nn.gelu(pre) @ cmp_w2[kv]

    kc, vc = compress(0), compress(1)
    c_end = jnp.arange(n_seg - 1) * CMP_STRIDE + (CMP_BLOCK - 1)
    p_c = masked_softmax(jnp.einsum('bgrqd,bgnd->bgrqn', qg, kc) * scale, c_end[None, :] <= q_pos[:, None])
    o_c = jnp.einsum('bgrqn,bgnd->bgrqd', p_c, vc)

    nsb = -(-T // SEL_BLOCK)
    n_past = past_len // SEL_BLOCK
    assert past_len % SEL_BLOCK == 0 and PAGE_SIZE % SEL_BLOCK == 0 and nsb - n_past <= 1
    imp = cmp_to_sel(p_c.sum(2), nsb)
    blk = jnp.arange(nsb)
    valid = blk[None, :] * SEL_BLOCK <= q_pos[:, None]
    forced = (blk[None, :] == 0) | (blk[None, :] == q_pos[:, None] // SEL_BLOCK)
    score = jnp.where(forced, FORCED_SCORE, jnp.where(valid, imp, INVALID_SCORE))
    _, idx = lax.top_k(score, min(N_SEL, nsb))
    per_page = PAGE_SIZE // SEL_BLOCK
    past = jnp.minimum(idx, n_past - 1)
    bi = jnp.arange(B)[:, None, None, None]
    gi = jnp.arange(G)[None, :, None, None]
    pool_blk = page_table[bi, past // per_page] * per_page + past % per_page
    blocks = pool_slc.reshape((-1, SEL_BLOCK) + pool_slc.shape[2:])[pool_blk]
    blocks = jnp.take_along_axis(blocks, gi[..., None, None, None, None], axis=6)[..., 0, :]
    tail = jnp.pad(new_slc, ((0, 0), (0, SEL_BLOCK - S), (0, 0), (0, 0), (0, 0)))
    tail = tail.transpose(0, 3, 1, 2, 4)[:, :, None, None]
    blocks = jnp.where((idx >= n_past)[..., None, None, None], tail, blocks)
    kg = blocks[..., 0, :].reshape(B, G, S, -1, d)
    vg = blocks[..., 1, :].reshape(B, G, S, -1, d)
    k_pos = (idx[..., None] * SEL_BLOCK + jnp.arange(SEL_BLOCK)).reshape(B, G, S, -1)
    p_s = masked_softmax(jnp.einsum('bgrqd,bgqsd->bgrqs', qg, kg) * scale,
                         (k_pos <= q_pos[None, None, :, None])[:, :, None])
    o_s = jnp.einsum('bgrqs,bgqsd->bgrqd', p_s, vg)

    kw = jnp.swapaxes(win_rows[:, :, 0], 1, 2)
    vw = jnp.swapaxes(win_rows[:, :, 1], 1, 2)
    w_pos = win_pos0 + jnp.arange(kw.shape[2])
    wmask = ((w_pos[None, :] <= q_pos[:, None]) & (w_pos[None, :] >= q_pos[:, None] - WINDOW)
             & (w_pos[None, :] >= 0))
    p_w = masked_softmax(jnp.einsum('bgrqd,bgkd->bgrqk', qg, kw) * scale, wmask)
    o_w = jnp.einsum('bgrqk,bgkd->bgrqd', p_w, vw)
    o = gg[..., 0:1] * o_c + gg[..., 1:2] * o_s + gg[..., 2:3] * o_w
    return o.transpose(0, 3, 1, 2, 4).reshape(B, S, NSA_HEADS * d)


def gather_pages(pool, page_table):
    g = pool[page_table]
    return g.reshape((g.shape[0], g.shape[1] * g.shape[2]) + g.shape[3:])


def memory_kv(mem, g_mem, w_k, w_v):
    mn = rmsnorm(mem, g_mem)
    B, M = mem.shape[:2]
    k = (mn @ w_k).reshape(B, M, MEM_HEADS, MEM_HEAD_DIM)
    v = (mn @ w_v).reshape(B, M, MEM_HEADS, MEM_HEAD_DIM)
    return jnp.stack([k, v], axis=2)


def cross_attend(xn, mem_kv, w_q, w_o):
    B, T = xn.shape[:2]
    q = (xn @ w_q).reshape(B, T, MEM_HEADS, MEM_HEAD_DIM)
    s = jnp.einsum('bthd,bmhd->bhtm', q, mem_kv[:, :, 0]) * MEM_HEAD_DIM ** -0.5
    p = jax.nn.softmax(s.astype(f32), axis=-1)
    o = jnp.einsum('bhtm,bmhd->bthd', p, mem_kv[:, :, 1].astype(f32)).reshape(B, T, -1)
    return (o.astype(xn.dtype) @ w_o).astype(xn.dtype)


PEER_PICKS = PEER_HEADS * PEER_TOPK
PEER_TOKENS_PER_STEP = 64
PEER_GROUP = 4
PEER_FETCH_AHEAD = 2
PEER_ROW_BUFFERS = 4 * PEER_GROUP
PEER_SLAB_LANES = 128
PEER_SLAB_ROWS = 2 * D_MODEL // PEER_SLAB_LANES


def _gelu_tanh(x):
    return 0.5 * x * (1.0 + jnp.tanh(0.7978845608028654 * (x + 0.044715 * x * x * x)))


def _peer_expert_body(ids_ref, ids_next_ref, x_ref, g_ref, seg_ref, uv_ref, o_ref, rows, sems):
    tokens = x_ref.shape[0]
    depth = PEER_ROW_BUFFERS
    half, lanes = PEER_SLAB_ROWS // 2, PEER_SLAB_LANES
    cols = PEER_PICKS * half
    seg = seg_ref.shape[0]
    nt = (((1,), (1,)), ((), ()))
    bf16 = jnp.bfloat16

    col_row = lax.broadcasted_iota(jnp.int32, (half, cols), 1) & (half - 1)
    sub = lax.broadcasted_iota(jnp.int32, (half, cols), 0)
    diag = jnp.where(col_row == sub, 1.0, 0.0)

    group = PEER_GROUP
    groups = tokens // group
    n_seg = cols // seg
    step_id = pl.program_id(0)
    last_step = pl.num_programs(0) - 1

    def wait(slot):
        pltpu.make_async_copy(uv_ref.at[pl.ds(0, PEER_PICKS)], rows.at[slot], sems.at[slot]).wait()

    def fetcher(ids, t0, slot0):
        per = PEER_PICKS // 2

        def fetch(c):
            j, h = divmod(c, 2)
            for k in range(h * per, (h + 1) * per):
                pltpu.make_async_copy(uv_ref.at[ids[t0 + j, k]], rows.at[slot0 + j, k],
                                      sems.at[slot0 + j]).start(priority=k % 2)
        return fetch

    def mix_group(t0, slot0, fetch):
        parts = []
        for j in range(group):
            u_rows = rows[slot0 + j, :, :half, :].reshape(cols, lanes).astype(bf16)
            prod = lax.dot_general(x_ref[t0 + j].astype(bf16), u_rows, nt, preferred_element_type=f32)
            part = jnp.sum(prod * diag, axis=0, keepdims=True)
            parts += [part[:, i * seg:(i + 1) * seg] for i in range(n_seg)]
            fetch(j)
        part = jnp.concatenate(parts, axis=0)
        hi = part.astype(bf16)
        lo = (part - hi.astype(f32)).astype(bf16)
        ones = seg_ref[...]
        act = jnp.dot(hi, ones, preferred_element_type=f32) + jnp.dot(lo, ones, preferred_element_type=f32)
        for j in range(group):
            w = g_ref[t0 + j] * _gelu_tanh(act[j * n_seg:(j + 1) * n_seg])
            w = jnp.concatenate([jnp.broadcast_to(w[i:i + 1, :], (half, seg)) for i in range(n_seg)], axis=1)
            v_rows = rows[slot0 + j, :, half:, :].reshape(cols, lanes).astype(bf16)
            o_ref[t0 + j] = jnp.dot((w * diag).astype(bf16), v_rows, preferred_element_type=f32)
            fetch(group + j)

    sets, ahead = depth // group, PEER_FETCH_AHEAD

    @pl.when(step_id == 0)
    def _():
        for a in range(ahead):
            first = fetcher(ids_ref, a * group, a * group)
            for c in range(2 * group):
                first(c)

    def sweep(it, last):
        for q in range(sets):
            g = sets * it + q
            for j in range(group):
                wait(q * group + j)
            into = ((q + ahead) % sets) * group
            if last and q + ahead >= sets:
                fetch = fetcher(ids_next_ref, (q + ahead - sets) * group, into)
            else:
                fetch = fetcher(ids_ref, (g + ahead) * group, into)
            mix_group(g * group, q * group, fetch)

    def body(it, carry):
        sweep(it, False)
        return carry

    lax.fori_loop(0, groups // sets - 1, body, 0)
    sweep(groups // sets - 1, True)

    @pl.when(step_id == last_step)
    def _():
        for j in range(ahead * group):
            wait(j)


def peer_experts(xn, ids, gates, uv):
    n, d = xn.shape
    tb = PEER_TOKENS_PER_STEP
    slab, half, lanes = PEER_SLAB_ROWS, PEER_SLAB_ROWS // 2, PEER_SLAB_LANES
    seg = 2 * lanes
    cols = PEER_PICKS * half
    assert n % tb == 0 and tb % PEER_ROW_BUFFERS == 0 and half * lanes == d and cols % seg == 0
    assert 0 < PEER_FETCH_AHEAD < PEER_ROW_BUFFERS // PEER_GROUP
    same_pick = np.arange(seg)[:, None] // half == np.arange(seg)[None, :] // half
    out = pl.pallas_call(
        _peer_expert_body,
        grid=(n // tb,),
        in_specs=[
            pl.BlockSpec((tb, PEER_PICKS), lambda i: (i, 0), memory_space=pltpu.SMEM),
            pl.BlockSpec((tb, PEER_PICKS), lambda i: (jnp.minimum(i + 1, n // tb - 1), 0), memory_space=pltpu.SMEM),
            pl.BlockSpec((tb, half, lanes), lambda i: (i, 0, 0)),
            pl.BlockSpec((tb, cols // seg, seg), lambda i: (i, 0, 0)),
            pl.BlockSpec((seg, seg), lambda i: (0, 0)),
            pl.BlockSpec(memory_space=pl.ANY),
        ],
        out_specs=pl.BlockSpec((tb, half, lanes), lambda i: (i, 0, 0)),
        out_shape=jax.ShapeDtypeStruct((n, half, lanes), f32),
        scratch_shapes=[
            pltpu.VMEM((PEER_ROW_BUFFERS, PEER_PICKS, slab, lanes), f32),
            pltpu.SemaphoreType.DMA((PEER_ROW_BUFFERS,)),
        ],
        compiler_params=pltpu.CompilerParams(dimension_semantics=("arbitrary",)),
        name="peer_experts",
    )(ids, ids, xn.reshape(n, half, lanes), jnp.repeat(gates, half, axis=1).reshape(n, cols // seg, seg),
      jnp.asarray(same_pick, jnp.bfloat16), uv)
    return out.reshape(n, d)


PEER_ROUTE_TOKENS = 128
PEER_HALF_DIM = PEER_QUERY_DIM // 2


def _top_rows(s, row_id, k, payload=None):
    vals, picks = [], []
    sentinel = s.shape[0]
    for _ in range(k):
        m = jnp.max(s, axis=0, keepdims=True)
        first = jnp.min(jnp.where(s == m, row_id, sentinel), axis=0, keepdims=True)
        hit = row_id == first
        vals.append(m)
        if payload is None:
            picks.append(first)
        else:
            picks.append(jnp.max(jnp.where(hit, payload, -1), axis=0, keepdims=True))
        s = jnp.where(hit, -jnp.inf, s)
    return jnp.concatenate(vals, 0), jnp.concatenate(picks, 0)


def _peer_route_body(x_ref, wq_ref, sk_ref, ids_ref, gate_ref, q_scr):
    tn = x_ref.shape[0]
    q = jnp.dot(x_ref[...].astype(jnp.bfloat16), wq_ref[...], preferred_element_type=f32)
    for j in range(2 * PEER_HEADS):
        q_scr[j] = q[:, j * PEER_HALF_DIM:(j + 1) * PEER_HALF_DIM].astype(jnp.bfloat16)
    key_id = lax.broadcasted_iota(jnp.int32, (PEER_N_KEYS, tn), 0)
    n_cand = -(-sum(PEER_TOPK // (a + 1) for a in range(PEER_TOPK)) // 8) * 8
    cand_id = lax.broadcasted_iota(jnp.int32, (n_cand, tn), 0)

    def head(h, carry):
        tops = []
        for p in range(2):
            s = lax.dot_general(sk_ref[2 * h + p], q_scr[2 * h + p], (((1,), (1,)), ((), ())),
                                preferred_element_type=f32)
            tops.append(_top_rows(s, key_id, PEER_TOPK))
        (v0, i0), (v1, i1) = tops
        cs, ce = [], []
        for a in range(PEER_TOPK):
            nb = PEER_TOPK // (a + 1)
            cs.append(v0[a:a + 1] + v1[:nb])
            ce.append(i0[a:a + 1] * PEER_N_KEYS + i1[:nb])
        pad = n_cand - sum(c.shape[0] for c in cs)
        cand_s = jnp.concatenate(cs + [jnp.full((pad, tn), -jnp.inf, f32)], axis=0)
        cand_e = jnp.concatenate(ce + [jnp.zeros((pad, tn), jnp.int32)], axis=0)
        top_s, top_e = _top_rows(cand_s, cand_id, PEER_TOPK, payload=cand_e)
        e = jnp.exp(top_s - top_s[0:1])
        ids_ref[h] = top_e
        gate_ref[h] = e / jnp.sum(e, axis=0, keepdims=True)
        return carry

    lax.fori_loop(0, PEER_HEADS, head, 0)


def peer_route(xn, wq, sub_keys):
    n, d = xn.shape
    tn = PEER_ROUTE_TOKENS
    assert n % tn == 0
    n_q = 2 * PEER_HEADS * PEER_HALF_DIM
    sk = sub_keys.reshape(2 * PEER_HEADS, PEER_N_KEYS, PEER_HALF_DIM).astype(jnp.bfloat16)
    ids_t, gates_t = pl.pallas_call(
        _peer_route_body,
        grid=(n // tn,),
        in_specs=[
            pl.BlockSpec((tn, d), lambda i: (i, 0)),
            pl.BlockSpec((d, n_q), lambda i: (0, 0)),
            pl.BlockSpec((2 * PEER_HEADS, PEER_N_KEYS, PEER_HALF_DIM), lambda i: (0, 0, 0)),
        ],
        out_specs=[
            pl.BlockSpec((PEER_HEADS, PEER_TOPK, tn), lambda i: (0, 0, i)),
            pl.BlockSpec((PEER_HEADS, PEER_TOPK, tn), lambda i: (0, 0, i)),
        ],
        out_shape=[
            jax.ShapeDtypeStruct((PEER_HEADS, PEER_TOPK, n), jnp.int32),
            jax.ShapeDtypeStruct((PEER_HEADS, PEER_TOPK, n), f32),
        ],
        scratch_shapes=[pltpu.VMEM((2 * PEER_HEADS, tn, PEER_HALF_DIM), jnp.bfloat16)],
        compiler_params=pltpu.CompilerParams(dimension_semantics=("arbitrary",)),
        name="peer_route",
    )(xn, wq.astype(jnp.bfloat16), sk)
    to_rows = lambda a: a.reshape(PEER_PICKS, n).T
    return to_rows(ids_t), to_rows(gates_t)


def peer_ffn_tokens(xn, wq, sub_keys, uv):
    ids, gates = peer_route(xn, wq, sub_keys)
    return peer_experts(xn, ids, gates, uv)


def kernel(x_prompt, x_sample, cache_cmp_kv, cache_slc_kv, cache_win_kv, state_mlstm_c, state_mlstm_n,
           state_mlstm_m, cache_mem_kv, page_table, mem_prompt, g_mix, w_in, b_in, b_forget, ml_head_gain,
           cmp_pe, cmp_w1, cmp_b1, cmp_w2, w_out, g_xattn, g_mem, w_xq, w_xk, w_xv, w_xo, g_ffn, peer_wq,
           peer_sub_keys, peer_u, peer_v, g_final):
    B, T = x_prompt.shape[:2]
    DB, S = x_sample.shape[:2]
    past_len = page_table.shape[1] * PAGE_SIZE
    l = 0
    bg = lambda a: jnp.swapaxes(a, 1, 2)

    o_i, o_nq, o_nkv, o_ng = IN_OFFSETS[3], IN_OFFSETS[5], IN_OFFSETS[6], IN_OFFSETS[7]
    n_small = 2 * ML_HEADS + 3 * NSA_HEADS
    small_pad = 128 - n_small
    regroup = lambda a: jnp.concatenate(
        [a[..., :o_i], a[..., o_nq:o_nkv], a[..., o_nkv:o_ng], a[..., o_i:o_nq], a[..., o_ng:],
         jnp.zeros(a.shape[:-1] + (small_pad,), a.dtype)], axis=-1)
    w_in_g, b_in_g = regroup(w_in[l]), regroup(b_in[l])
    in_splits = (4 * ML_WIDTH, NSA_WIDTH, 6 * NSA_KV_LANES, n_small + small_pad)

    def project(x):
        b, t, _ = x.shape
        ml, nq, nkv, small = fused_linear(x.reshape(b * t, D_MODEL), w_in_g, pre_gain=g_mix[l], bias=b_in_g,
                                          splits=in_splits)
        ng = jax.nn.sigmoid(small[:, 2 * ML_HEADS:n_small]).reshape(b, t, 3 * NSA_HEADS)
        return (ml.reshape(b, t, 4 * ML_WIDTH), small.reshape(b, t, -1), nq.reshape(b, t, NSA_WIDTH),
                nkv.reshape(b, t, 6 * NSA_KV_LANES), ng)

    def after_mixers(x, h_ml, h_nsa, mem_kv):
        b, t, _ = x.shape
        h = jnp.concatenate([h_ml, h_nsa], -1).reshape(b * t, D_MODEL)
        x1, xn = fused_linear(h, w_out[l], residual=x.reshape(b * t, D_MODEL), post_gain=g_xattn[l])
        q = fused_linear(xn, w_xq[l]).reshape(b, t, D_MODEL)
        t_pad = -t % 8
        o = cross_attention(jnp.pad(q, ((0, 0), (0, t_pad), (0, 0))), mem_kv)[:, :t]
        return fused_linear(o.reshape(b * t, D_MODEL), w_xo[l], residual=x1, post_gain=g_ffn[l])

    ml, small, nq, nkv, ng = project(x_prompt)
    h_ml, p_c, p_n, p_m = mlstm_pallas(
        ml, small, b_forget[l], ml_head_gain[l],
        jnp.zeros((B, ML_HEADS, ML_HEAD_DIM, ML_HEAD_DIM), f32),
        jnp.zeros((B, ML_HEADS, ML_HEAD_DIM), f32),
        jnp.full((B, ML_HEADS), -jnp.inf, f32))
    rows6 = nkv.reshape(B, T, 6, NSA_KV_HEADS, NSA_HEAD_DIM)
    p_cmp, p_slc, win_rows = rows6[:, :, 0:2], rows6[:, :, 2:4], rows6[:, :, 4:6]
    kc, _ = nsa_compress(bg(p_cmp[:, :, 0]), cmp_pe[l, 0], cmp_w1[l, 0], cmp_b1[l, 0], cmp_w2[l, 0])
    vc, _ = nsa_compress(bg(p_cmp[:, :, 1]), cmp_pe[l, 1], cmp_w1[l, 1], cmp_b1[l, 1], cmp_w2[l, 1])
    h_nsa = nsa_prompt(nq, ng, nkv, kc, vc)
    p_mem = fused_linear(mem_prompt.reshape(-1, D_MODEL), jnp.concatenate([w_xk[l], w_xv[l]], axis=1),
                         pre_gain=g_mem[l]).reshape(B, MEM_LEN, 2, MEM_HEADS, MEM_HEAD_DIM)
    xp, xp_ffn_in = after_mixers(x_prompt, h_ml, h_nsa, p_mem)
    p_win = win_rows[:, T - min(WINDOW, T):]

    ml, small, nq, nkv, ng = project(x_sample)
    t_pad = ML_CHUNK - S
    ml_p = jnp.pad(ml, ((0, 0), (0, t_pad), (0, 0)))
    small_p = jnp.pad(small, ((0, 0), (0, t_pad), (0, 0)))
    small_p = small_p.at[:, S:, :ML_HEADS].set(-jnp.inf).at[:, S:, ML_HEADS:2 * ML_HEADS].set(jnp.inf)
    h_ml, s_c, s_n, s_m = mlstm_pallas(ml_p, small_p, b_forget[l], ml_head_gain[l],
                                       state_mlstm_c[l], state_mlstm_n[l], state_mlstm_m[l])
    h_ml = h_ml[:, :S]
    rows6 = nkv.reshape(DB, S, 6, NSA_KV_HEADS, NSA_HEAD_DIM)
    s_cmp, s_slc, win_rows = rows6[:, :, 0:2], rows6[:, :, 2:4], rows6[:, :, 4:6]
    win_buf = cache_win_kv[l]
    win_ext = jnp.concatenate([win_buf, win_rows], axis=1)
    h_nsa = nsa_decode(nq, ng, past_len, cache_cmp_kv[l], cache_slc_kv[l], page_table, s_cmp, s_slc,
                       win_ext, past_len - win_buf.shape[1], cmp_pe[l], cmp_w1[l], cmp_b1[l], cmp_w2[l])
    xs, xs_ffn_in = after_mixers(x_sample, h_ml, h_nsa, cache_mem_kv[l])
    w_keep = min(WINDOW, past_len + S)
    s_win = win_ext[:, win_ext.shape[1] - w_keep:]

    n_p = B * T
    uv = jnp.concatenate([peer_u[l], peer_v[l]], axis=1).reshape(-1, PEER_SLAB_ROWS, PEER_SLAB_LANES)
    ffn = peer_ffn_tokens(jnp.concatenate([xp_ffn_in, xs_ffn_in], 0), peer_wq[l], peer_sub_keys[l], uv)
    y_prompt = add_rmsnorm(xp, ffn[:n_p], g_final).reshape(x_prompt.shape)
    y_sample = add_rmsnorm(xs, ffn[n_p:], g_final).reshape(x_sample.shape)
    st = lambda a: a[None]
    return (y_prompt, y_sample,
            st(p_cmp), st(p_slc), st(p_win), st(p_c), st(p_n), st(p_m), st(p_mem),
            st(s_cmp), st(s_slc), st(s_win), st(s_c), st(s_n), st(s_m))
```

```python
import functools

import jax
import jax.numpy as jnp
from jax import lax
import numpy as np
from jax.experimental import pallas as pl
from jax.experimental.pallas import tpu as pltpu

D_MODEL = 1024
PAGE_SIZE = 128

ML_WIDTH = D_MODEL // 2
ML_HEADS = 4
ML_HEAD_DIM = ML_WIDTH // ML_HEADS
ML_CHUNK = 128
NSA_WIDTH = D_MODEL - ML_WIDTH
NSA_HEADS = 8
NSA_HEAD_DIM = NSA_WIDTH // NSA_HEADS
NSA_KV_HEADS = 2
NSA_GROUP = NSA_HEADS // NSA_KV_HEADS
CMP_BLOCK = 32
CMP_STRIDE = 16
SEL_BLOCK = 64
N_SEL = 16
WINDOW = 512
Q_BLOCK = 128
FORCED_SCORE = 1.0e4
INVALID_SCORE = -1.0
MEM_LEN = 256
MEM_HEADS = 4
MEM_HEAD_DIM = D_MODEL // MEM_HEADS
PEER_HEADS = 8
PEER_N_KEYS = 128
PEER_TOPK = 16
PEER_QUERY_DIM = 256
EPS = 1e-6
TINY = 1e-30
IN_SPLITS = (ML_WIDTH, ML_WIDTH, ML_WIDTH, ML_WIDTH, ML_HEADS, ML_HEADS,
             NSA_WIDTH, 6 * NSA_KV_HEADS * NSA_HEAD_DIM, 3 * NSA_HEADS)
IN_OFFSETS = tuple(int(o) for o in np.cumsum(IN_SPLITS)[:-1])

f32 = jnp.float32

ROW_TILE = 256
DENSE_VMEM_BYTES = 56 * 1024 * 1024


def _rms(x, g):
    return x * lax.rsqrt(jnp.mean(x * x, -1, keepdims=True) + EPS) * g


def _add_norm_body(x_ref, r_ref, g_ref, o_ref):
    o_ref[...] = _rms(x_ref[...] + r_ref[...], g_ref[...])


def add_rmsnorm(x, r, g):
    shape = x.shape
    d = shape[-1]
    x2, r2 = x.reshape(-1, d), r.reshape(-1, d)
    n = x2.shape[0]
    tm = min(n, ROW_TILE)
    rows = pl.BlockSpec((tm, d), lambda i: (i, 0))
    out = pl.pallas_call(
        _add_norm_body,
        grid=(n // tm,),
        in_specs=[rows, rows, pl.BlockSpec((1, d), lambda i: (0, 0))],
        out_specs=rows,
        out_shape=jax.ShapeDtypeStruct((n, d), f32),
        name="add_rmsnorm",
    )(x2, r2, g.reshape(1, d))
    return out.reshape(shape)


def _linear_body(*refs, pre_norm, has_bias, has_res, post_norm, splits):
    it = iter(refs)
    x_ref, w_ref = next(it), next(it)
    x = x_ref[...]
    if pre_norm:
        x = _rms(x, next(it)[...])
    y = jnp.dot(x.astype(jnp.bfloat16), w_ref[...], preferred_element_type=f32)
    if has_bias:
        y = y + next(it)[...]
    if has_res:
        y = y + next(it)[...]
    post_gain = next(it)[...] if post_norm else None
    off = 0
    for m in splits:
        next(it)[...] = y[:, off:off + m]
        off += m
    if post_norm:
        next(it)[...] = _rms(y, post_gain)


def fused_linear(x, w, *, pre_gain=None, bias=None, residual=None, post_gain=None, splits=None):
    n, k = x.shape
    m = w.shape[1]
    splits = (m,) if splits is None else tuple(splits)
    assert sum(splits) == m and (post_gain is None or len(splits) == 1)
    tm = min(n, ROW_TILE)
    assert n % tm == 0
    row = lambda c: pl.BlockSpec((tm, c), lambda i: (i, 0))
    const = lambda r, c: pl.BlockSpec((r, c), lambda i: (0, 0))
    args, specs = [x, w.astype(jnp.bfloat16)], [row(k), const(k, m)]
    if pre_gain is not None:
        args.append(pre_gain.reshape(1, k)); specs.append(const(1, k))
    if bias is not None:
        args.append(bias.reshape(1, m)); specs.append(const(1, m))
    if residual is not None:
        args.append(residual); specs.append(row(m))
    if post_gain is not None:
        args.append(post_gain.reshape(1, m)); specs.append(const(1, m))
    out_cols = splits + ((m,) if post_gain is not None else ())
    outs = pl.pallas_call(
        functools.partial(_linear_body, pre_norm=pre_gain is not None, has_bias=bias is not None,
                          has_res=residual is not None, post_norm=post_gain is not None, splits=splits),
        grid=(n // tm,),
        in_specs=specs,
        out_specs=[row(c) for c in out_cols],
        out_shape=[jax.ShapeDtypeStruct((n, c), f32) for c in out_cols],
        compiler_params=pltpu.CompilerParams(dimension_semantics=("arbitrary",),
                                             vmem_limit_bytes=DENSE_VMEM_BYTES),
        name="fused_linear",
    )(*args)
    return outs[0] if len(outs) == 1 else tuple(outs)


def _xattn_body(q_ref, kv_ref, o_ref):
    bf16 = jnp.bfloat16
    d = MEM_HEAD_DIM
    q = q_ref[0]
    kv = kv_ref[0].astype(bf16)
    for h in range(MEM_HEADS):
        k_h = kv[:, h * d:(h + 1) * d]
        v_h = kv[:, (MEM_HEADS + h) * d:(MEM_HEADS + h + 1) * d]
        s = lax.dot_general(q[:, h * d:(h + 1) * d].astype(bf16), k_h, (((1,), (1,)), ((), ())),
                            preferred_element_type=f32) * (d ** -0.5)
        e = jnp.exp(s - jnp.max(s, axis=-1, keepdims=True))
        p = e / jnp.sum(e, axis=-1, keepdims=True)
        o_ref[0, :, h * d:(h + 1) * d] = jnp.dot(p.astype(bf16), v_h, preferred_element_type=f32)


def cross_attention(q, mem_kv):
    B, T, w = q.shape
    M = mem_kv.shape[1]
    kv = mem_kv.reshape(B, M, 2 * w)
    tm = min(T, ROW_TILE)
    assert T % tm == 0 and tm % 8 == 0
    return pl.pallas_call(
        _xattn_body,
        grid=(B, T // tm),
        in_specs=[pl.BlockSpec((1, tm, w), lambda b, i: (b, i, 0)),
                  pl.BlockSpec((1, M, 2 * w), lambda b, i: (b, 0, 0))],
        out_specs=pl.BlockSpec((1, tm, w), lambda b, i: (b, i, 0)),
        out_shape=jax.ShapeDtypeStruct((B, T, w), f32),
        compiler_params=pltpu.CompilerParams(dimension_semantics=("arbitrary", "arbitrary")),
        name="cross_attention",
    )(q, kv)


def masked_softmax(s, mask):
    s = jnp.where(mask, s.astype(f32), -jnp.inf)
    mx = jnp.max(s, -1, keepdims=True)
    mx = jnp.where(jnp.isfinite(mx), mx, 0.0)
    e = jnp.exp(s - mx)
    return e / jnp.maximum(e.sum(-1, keepdims=True), TINY)


def _log_sigmoid(x):
    return jnp.minimum(x, 0.0) - jnp.log1p(jnp.exp(-jnp.abs(x)))


def _cumsum_lanes(x):
    lane = lax.broadcasted_iota(jnp.int32, x.shape, 1)
    shift = 1
    while shift < x.shape[1]:
        x = x + jnp.where(lane >= shift, pltpu.roll(x, shift, axis=1), 0.0)
        shift *= 2
    return x


def _mlstm_body(ml_ref, small_ref, bias_ref, gain_ref, c0_ref, n0_ref, m0_ref, h_ref, c_ref, n_ref, m_ref):
    H, d, L = ML_HEADS, ML_HEAD_DIM, ml_ref.shape[1]
    bf16 = jnp.bfloat16
    nt = (((1,), (1,)), ((), ()))
    tn = (((0,), (0,)), ((), ()))

    @pl.when(pl.program_id(1) == 0)
    def _():
        c_ref[...] = c0_ref[...]
        n_ref[...] = n0_ref[...]
        m_ref[...] = m0_ref[...]

    blk = ml_ref[0]
    small = small_ref[0] + bias_ref[...]
    small_t = small.T
    causal = (lax.broadcasted_iota(jnp.int32, (L, L), 0) >= lax.broadcasted_iota(jnp.int32, (L, L), 1))
    for h in range(H):
        q = blk[:, h * d:(h + 1) * d]
        k = blk[:, (H + h) * d:(H + h + 1) * d] * (d ** -0.5)
        v = blk[:, (2 * H + h) * d:(2 * H + h + 1) * d]
        o_pre = blk[:, (3 * H + h) * d:(3 * H + h + 1) * d]
        ig_row, ig_col = small_t[h:h + 1, :], small[:, h:h + 1]
        b_row = _cumsum_lanes(_log_sigmoid(small_t[H + h:H + h + 1, :]))
        b_col = jnp.broadcast_to(b_row, (L, L)).T
        b_t, b_last = b_col[:, 0:1], b_row[:, L - 1:L]
        c_prev, n_prev, m_prev = c_ref[0, h], n_ref[0, h], m_ref[0, h]
        log_d = jnp.where(causal, b_col - b_row + ig_row, -jnp.inf)
        inter = b_t + m_prev
        m_t = jnp.maximum(inter, jnp.max(log_d, axis=1, keepdims=True))
        a = jnp.exp(inter - m_t)
        qb, vb = q.astype(bf16), v.astype(bf16)
        qk = lax.dot_general(qb, k.astype(bf16), nt, preferred_element_type=f32) * jnp.exp(log_d - m_t)
        num = (a * jnp.dot(qb, c_prev.astype(bf16), preferred_element_type=f32)
               + jnp.dot(qk.astype(bf16), vb, preferred_element_type=f32))
        den = a * jnp.sum(q * n_prev, axis=1, keepdims=True) + jnp.sum(qk, axis=1, keepdims=True)
        hid = num / jnp.maximum(jnp.abs(den), jnp.exp(-m_t))
        m_new = m_t[L - 1:L, :]
        kw = k * jnp.exp(b_last - b_t + ig_col - m_new)
        decay = jnp.exp(b_last + m_prev - m_new)
        c_ref[0, h] = decay * c_prev + lax.dot_general(kw.astype(bf16), vb, tn, preferred_element_type=f32)
        n_ref[0, h] = decay * n_prev + jnp.sum(kw, axis=0, keepdims=True)
        m_ref[0, h] = m_new
        gated = jax.nn.sigmoid(o_pre) * hid
        h_ref[0, :, h * d:(h + 1) * d] = _rms(gated, gain_ref[:, h * d:(h + 1) * d])


def mlstm_pallas(ml, small, b_forget, head_gain, c0, n0, m0):
    B, T, _ = ml.shape
    H, d, L = ML_HEADS, ML_HEAD_DIM, ML_CHUNK
    assert T % L == 0
    bias = jnp.zeros((1, small.shape[-1]), f32).at[0, H:2 * H].set(b_forget)
    state = lambda *s: pl.BlockSpec((1,) + s, lambda b, i: (b,) + (0,) * len(s))
    h, c, n, m = pl.pallas_call(
        _mlstm_body,
        grid=(B, T // L),
        in_specs=[
            pl.BlockSpec((1, L, ml.shape[-1]), lambda b, i: (b, i, 0)),
            pl.BlockSpec((1, L, small.shape[-1]), lambda b, i: (b, i, 0)),
            pl.BlockSpec((1, small.shape[-1]), lambda b, i: (0, 0)),
            pl.BlockSpec((1, H * d), lambda b, i: (0, 0)),
            state(H, d, d), state(H, 1, d), state(H, 1, 1),
        ],
        out_specs=[pl.BlockSpec((1, L, H * d), lambda b, i: (b, i, 0)),
                   state(H, d, d), state(H, 1, d), state(H, 1, 1)],
        out_shape=[jax.ShapeDtypeStruct((B, T, H * d), f32), jax.ShapeDtypeStruct((B, H, d, d), f32),
                   jax.ShapeDtypeStruct((B, H, 1, d), f32), jax.ShapeDtypeStruct((B, H, 1, 1), f32)],
        compiler_params=pltpu.CompilerParams(dimension_semantics=("arbitrary", "arbitrary")),
        name="mlstm",
    )(ml, small, bias, head_gain.reshape(1, H * d), c0, n0.reshape(B, H, 1, d), m0.reshape(B, H, 1, 1))
    return h, c, n.reshape(B, H, d), m.reshape(B, H)


def compress_rows(seg, cmp_pe, cmp_w1, cmp_b1, cmp_w2):
    B, n_seg, width = seg.shape
    G, d, t = NSA_KV_HEADS, NSA_HEAD_DIM, CMP_STRIDE
    e = cmp_w1.shape[-1]
    w1 = cmp_w1.reshape(2, 2, t, d, e)
    eye_kv, eye_g = jnp.eye(2, dtype=f32), jnp.eye(G, dtype=f32)
    w_big = jnp.einsum('kptde,kK,gG->tkgdpKGe', w1, eye_kv, eye_g).reshape(width, 2 * 2 * G * e)
    ab = fused_linear(seg.reshape(B * n_seg, width), w_big).reshape(B, n_seg, 2, 2, G, e)
    const = jnp.einsum('kr,kre->ke', cmp_pe.reshape(2, -1), cmp_w1) + cmp_b1
    pre = ab[:, :-1, 0] + ab[:, 1:, 1] + const[:, None, :]
    out = jnp.einsum('bnkge,kef->kbgnf', jax.nn.gelu(pre), cmp_w2)
    return out[0], out[1]


def cmp_to_sel(imp, nsb):
    r = SEL_BLOCK // CMP_STRIDE
    nc = imp.shape[-1]
    lead = imp.shape[:-1]
    tot = r * nsb
    padw = [(0, 0)] * len(lead)
    first = jnp.pad(imp, padw + [(0, tot - nc)]).reshape(lead + (nsb, r)).sum(-1)
    second = jnp.pad(imp, padw + [(1, tot - nc - 1)]).reshape(lead + (nsb, r)).sum(-1)
    return 0.5 * (first + second)


NSA_SEL_CHUNK = 512
NSA_WIN_CHUNKS = WINDOW // Q_BLOCK + 1
NSA_KV_LANES = NSA_KV_HEADS * NSA_HEAD_DIM


def _softmax_masked(s, mask):
    sm = jnp.where(mask, s, -jnp.inf)
    mx = jnp.max(sm, axis=-1, keepdims=True)
    mx = jnp.where(mx == -jnp.inf, 0.0, mx)
    e = jnp.exp(sm - mx)
    return e * (1.0 / jnp.maximum(jnp.sum(e, axis=-1, keepdims=True), TINY))


def _top_rows_mask(s, row_id, k):
    sel = jnp.zeros(s.shape, f32)
    sentinel = s.shape[0]
    for _ in range(k):
        m = jnp.max(s, axis=0, keepdims=True)
        first = jnp.min(jnp.where(s == m, row_id, sentinel), axis=0, keepdims=True)
        hit = row_id == first
        sel = jnp.where(hit, 1.0, sel)
        s = jnp.where(hit, -jnp.inf, s)
    return sel


def _nsa_prompt_body(q_ref, gate_ref, kct_ref, vc_ref, mt_ref, kts_ref, vs_ref, ktw_ref, vw_ref, o_ref):
    j = pl.program_id(1)
    Q, R, d = Q_BLOCK, NSA_GROUP, NSA_HEAD_DIM
    n_cmp = kct_ref.shape[-1]
    n_blk = mt_ref.shape[0]
    bf16 = jnp.bfloat16
    q_all = q_ref[0] * (d ** -0.5)
    gates = gate_ref[0]
    q_pos = j * Q + lax.broadcasted_iota(jnp.int32, (Q, 1), 0)
    q_pos_l = j * Q + lax.broadcasted_iota(jnp.int32, (1, Q), 1)
    blk_r = lax.broadcasted_iota(jnp.int32, (n_blk, 1), 0)
    blk_id = lax.broadcasted_iota(jnp.int32, (n_blk, Q), 0)
    nt = (((1,), (1,)), ((), ()))

    for g in range(NSA_KV_HEADS):
        lanes = slice(g * d, (g + 1) * d)
        qg = jnp.concatenate([q_all[:, (g * R + r) * d:(g * R + r + 1) * d] for r in range(R)], axis=0).astype(bf16)

        s = jnp.dot(qg, kct_ref[0, g], preferred_element_type=f32).reshape(R, Q, n_cmp)
        n_id = lax.broadcasted_iota(jnp.int32, (1, n_cmp), 1)
        cmask = (n_id * CMP_STRIDE + (CMP_BLOCK - 1) <= q_pos) & (n_id < n_cmp - 1)
        p = _softmax_masked(s, cmask[None])
        o_c = jnp.dot(p.reshape(R * Q, n_cmp).astype(bf16), vc_ref[0], preferred_element_type=f32)[:, lanes]

        p_sum = p[0] + p[1] + p[2] + p[3]
        hi = p_sum.astype(bf16)
        lo = (p_sum - hi.astype(f32)).astype(bf16)
        mt = mt_ref[...]
        imp_t = (lax.dot_general(mt, hi, nt, preferred_element_type=f32)
                 + lax.dot_general(mt, lo, nt, preferred_element_type=f32))
        valid = blk_r * SEL_BLOCK <= q_pos_l
        forced = (blk_r == 0) | (blk_r == jnp.right_shift(q_pos_l, 6))
        score = jnp.where(forced, FORCED_SCORE, jnp.where(valid, imp_t, INVALID_SCORE))
        sel = _top_rows_mask(score, blk_id, min(N_SEL, n_blk)).T.astype(bf16)

        kc = kts_ref.shape[-1]

        def chunk(c, carry):
            m, l, acc = carry
            kt = kts_ref[0, c, g * d:(g + 1) * d, :]
            sc = jnp.dot(qg, kt, preferred_element_type=f32).reshape(R, Q, kc)
            key = c * kc + lax.broadcasted_iota(jnp.int32, (1, kc), 1)
            expand = jnp.where(blk_r == jnp.right_shift(key, 6), 1.0, 0.0).astype(bf16)
            picked = jnp.dot(sel, expand, preferred_element_type=f32) > 0.5
            mask = picked & (key <= q_pos)
            sm = jnp.where(mask[None], sc, -jnp.inf)
            m_new = jnp.maximum(m, jnp.max(sm, axis=-1, keepdims=True))
            m_safe = jnp.where(m_new == -jnp.inf, 0.0, m_new)
            alpha = jnp.exp(m - m_safe)
            pe = jnp.exp(sm - m_safe)
            l = l * alpha + jnp.sum(pe, axis=-1, keepdims=True)
            pv = jnp.dot(pe.reshape(R * Q, kc).astype(bf16), vs_ref[0, c], preferred_element_type=f32)
            acc = acc * alpha.reshape(R * Q, 1) + pv
            return m_new, l, acc

        n_chunks = (j * Q + Q + kc - 1) // kc
        init = (jnp.full((R, Q, 1), -jnp.inf, f32), jnp.zeros((R, Q, 1), f32),
                jnp.zeros((R * Q, NSA_KV_LANES), f32))
        _, l_s, acc_s = lax.fori_loop(0, n_chunks, chunk, init)
        o_s = acc_s[:, lanes] * (1.0 / jnp.maximum(l_s.reshape(R * Q, 1), TINY))

        ss, vv = [], []
        for i in range(NSA_WIN_CHUNKS):
            cc = jnp.maximum(j - (NSA_WIN_CHUNKS - 1) + i, 0)
            ss.append(jnp.dot(qg, ktw_ref[0, cc, g * d:(g + 1) * d, :], preferred_element_type=f32))
            vv.append(vw_ref[0, cc])
        span = NSA_WIN_CHUNKS * Q
        sw = jnp.concatenate(ss, axis=1).reshape(R, Q, span)
        k_pos = (j - (NSA_WIN_CHUNKS - 1)) * Q + lax.broadcasted_iota(jnp.int32, (1, span), 1)
        wmask = (k_pos >= 0) & (k_pos <= q_pos) & (k_pos >= q_pos - WINDOW)
        pw = _softmax_masked(sw, wmask[None])
        o_w = jnp.dot(pw.reshape(R * Q, span).astype(bf16), jnp.concatenate(vv, axis=0),
                      preferred_element_type=f32)[:, lanes]

        for r in range(R):
            h = g * R + r
            rows = slice(r * Q, (r + 1) * Q)
            o_ref[0, :, h * d:(h + 1) * d] = (gates[:, 3 * h:3 * h + 1] * o_c[rows]
                                              + gates[:, 3 * h + 1:3 * h + 2] * o_s[rows]
                                              + gates[:, 3 * h + 2:3 * h + 3] * o_w[rows])


def _cmp_to_sel_matrix(n_cmp, n_blk):
    r = SEL_BLOCK // CMP_STRIDE
    n = np.arange(n_cmp)
    b = np.arange(n_blk)[:, None]
    m = 0.5 * ((n // r == b).astype(np.float32) + ((n + 1) // r == b).astype(np.float32))
    m[:, n_cmp - 1] = 0.0
    return jnp.asarray(m, jnp.bfloat16)


def nsa_prompt(nq, gates, nkv, kc, vc):
    B, T, _ = nq.shape
    bf16 = jnp.bfloat16
    G, d, w = NSA_KV_HEADS, NSA_HEAD_DIM, NSA_KV_LANES
    n_cmp, n_blk = T // CMP_STRIDE, T // SEL_BLOCK
    kc_s, kc_w = NSA_SEL_CHUNK, Q_BLOCK
    pad = ((0, 0), (0, 0), (0, 1), (0, 0))
    kct = jnp.pad(kc, pad).transpose(0, 1, 3, 2).astype(bf16)
    vc2 = jnp.pad(vc, pad).transpose(0, 2, 1, 3).reshape(B, n_cmp, w).astype(bf16)
    chunks = lambda a, c: a.reshape(B, T // c, c, w).astype(bf16)
    kts = chunks(nkv[..., 2 * w:3 * w], kc_s).transpose(0, 1, 3, 2)
    vs = chunks(nkv[..., 3 * w:4 * w], kc_s)
    ktw = chunks(nkv[..., 4 * w:5 * w], kc_w).transpose(0, 1, 3, 2)
    vw = chunks(nkv[..., 5 * w:6 * w], kc_w)
    whole = lambda a: pl.BlockSpec((1,) + a.shape[1:], lambda b, j: (b,) + (0,) * (a.ndim - 1))
    mt = _cmp_to_sel_matrix(n_cmp, n_blk)
    return pl.pallas_call(
        _nsa_prompt_body,
        grid=(B, T // Q_BLOCK),
        in_specs=[
            pl.BlockSpec((1, Q_BLOCK, nq.shape[-1]), lambda b, j: (b, j, 0)),
            pl.BlockSpec((1, Q_BLOCK, gates.shape[-1]), lambda b, j: (b, j, 0)),
            whole(kct), whole(vc2),
            pl.BlockSpec(mt.shape, lambda b, j: (0, 0)),
            whole(kts), whole(vs), whole(ktw), whole(vw),
        ],
        out_specs=pl.BlockSpec((1, Q_BLOCK, nq.shape[-1]), lambda b, j: (b, j, 0)),
        out_shape=jax.ShapeDtypeStruct(nq.shape, f32),
        compiler_params=pltpu.CompilerParams(dimension_semantics=("arbitrary", "arbitrary"),
                                             vmem_limit_bytes=48 * 1024 * 1024),
        name="nsa_prompt",
    )(nq, gates, kct, vc2, mt, kts, vs, ktw, vw)


def nsa_decode(q, gates, past_len, pool_cmp, pool_slc, page_table, new_cmp, new_slc, win_rows, win_pos0,
               cmp_pe, cmp_w1, cmp_b1, cmp_w2):
    B, S, _ = q.shape
    G, R, d = NSA_KV_HEADS, NSA_GROUP, NSA_HEAD_DIM
    T = past_len + S
    scale = d ** -0.5
    q_pos = past_len + jnp.arange(S)
    qg = q.reshape(B, S, G, R, d).transpose(0, 2, 3, 1, 4)
    gg = gates.reshape(B, S, G, R, 3).transpose(0, 2, 3, 1, 4)

    n_seg = T // CMP_STRIDE
    rows = pool_cmp[page_table].reshape(B, past_len, 2, G, d)
    if n_seg * CMP_STRIDE > past_len:
        rows = jnp.concatenate([rows, new_cmp], axis=1)
    kc, vc = compress_rows(rows[:, :n_seg * CMP_STRIDE].reshape(B, n_seg, -1), cmp_pe, cmp_w1, cmp_b1, cmp_w2)
    c_end = jnp.arange(n_seg - 1) * CMP_STRIDE + (CMP_BLOCK - 1)
    p_c = masked_softmax(jnp.einsum('bgrqd,bgnd->bgrqn', qg, kc) * scale, c_end[None, :] <= q_pos[:, None])
    o_c = jnp.einsum('bgrqn,bgnd->bgrqd', p_c, vc)

    nsb = -(-T // SEL_BLOCK)
    n_past = past_len // SEL_BLOCK
    assert past_len % SEL_BLOCK == 0 and PAGE_SIZE % SEL_BLOCK == 0 and nsb - n_past <= 1
    imp = cmp_to_sel(p_c.sum(2), nsb)
    blk = jnp.arange(nsb)
    valid = blk[None, :] * SEL_BLOCK <= q_pos[:, None]
    forced = (blk[None, :] == 0) | (blk[None, :] == q_pos[:, None] // SEL_BLOCK)
    score = jnp.where(forced, FORCED_SCORE, jnp.where(valid, imp, INVALID_SCORE))
    _, idx = lax.top_k(score, min(N_SEL, nsb))
    per_page = PAGE_SIZE // SEL_BLOCK
    past = jnp.minimum(idx, n_past - 1)
    bi = jnp.arange(B)[:, None, None, None]
    gi = jnp.arange(G)[None, :, None, None]
    pool_blk = page_table[bi, past // per_page] * per_page + past % per_page
    blocks = pool_slc.reshape((-1, SEL_BLOCK) + pool_slc.shape[2:])[pool_blk]
    blocks = jnp.take_along_axis(blocks, gi[..., None, None, None, None], axis=6)[..., 0, :]
    tail = jnp.pad(new_slc, ((0, 0), (0, SEL_BLOCK - S), (0, 0), (0, 0), (0, 0)))
    tail = tail.transpose(0, 3, 1, 2, 4)[:, :, None, None]
    blocks = jnp.where((idx >= n_past)[..., None, None, None], tail, blocks)
    kg = blocks[..., 0, :].reshape(B, G, S, -1, d)
    vg = blocks[..., 1, :].reshape(B, G, S, -1, d)
    k_pos = (idx[..., None] * SEL_BLOCK + jnp.arange(SEL_BLOCK)).reshape(B, G, S, -1)
    p_s = masked_softmax(jnp.einsum('bgrqd,bgqsd->bgrqs', qg, kg) * scale,
                         (k_pos <= q_pos[None, None, :, None])[:, :, None])
    o_s = jnp.einsum('bgrqs,bgqsd->bgrqd', p_s, vg)

    kw = jnp.swapaxes(win_rows[:, :, 0], 1, 2)
    vw = jnp.swapaxes(win_rows[:, :, 1], 1, 2)
    w_pos = win_pos0 + jnp.arange(kw.shape[2])
    wmask = ((w_pos[None, :] <= q_pos[:, None]) & (w_pos[None, :] >= q_pos[:, None] - WINDOW)
             & (w_pos[None, :] >= 0))
    p_w = masked_softmax(jnp.einsum('bgrqd,bgkd->bgrqk', qg, kw) * scale, wmask)
    o_w = jnp.einsum('bgrqk,bgkd->bgrqd', p_w, vw)
    o = gg[..., 0:1] * o_c + gg[..., 1:2] * o_s + gg[..., 2:3] * o_w
    return o.transpose(0, 3, 1, 2, 4).reshape(B, S, NSA_HEADS * d)


PEER_PICKS = PEER_HEADS * PEER_TOPK
PEER_TOKENS_PER_STEP = 64
PEER_GROUP = 4
PEER_FETCH_AHEAD = 2
PEER_ROW_BUFFERS = 4 * PEER_GROUP
PEER_SLAB_LANES = 128
PEER_SLAB_ROWS = 2 * D_MODEL // PEER_SLAB_LANES


def _gelu_tanh(x):
    return 0.5 * x * (1.0 + jnp.tanh(0.7978845608028654 * (x + 0.044715 * x * x * x)))


def _peer_expert_body(ids_ref, ids_next_ref, x_ref, g_ref, seg_ref, uv_ref, o_ref, rows, sems):
    tokens = x_ref.shape[0]
    depth = PEER_ROW_BUFFERS
    half, lanes = PEER_SLAB_ROWS // 2, PEER_SLAB_LANES
    cols = PEER_PICKS * half
    seg = seg_ref.shape[0]
    nt = (((1,), (1,)), ((), ()))
    bf16 = jnp.bfloat16

    col_row = lax.broadcasted_iota(jnp.int32, (half, cols), 1) & (half - 1)
    sub = lax.broadcasted_iota(jnp.int32, (half, cols), 0)
    diag = jnp.where(col_row == sub, 1.0, 0.0)

    group = PEER_GROUP
    groups = tokens // group
    n_seg = cols // seg
    step_id = pl.program_id(0)
    last_step = pl.num_programs(0) - 1

    def wait(slot):
        pltpu.make_async_copy(uv_ref.at[pl.ds(0, PEER_PICKS)], rows.at[slot], sems.at[slot]).wait()

    def fetcher(ids, t0, slot0):
        per = PEER_PICKS // 2

        def fetch(c):
            j, h = divmod(c, 2)
            for k in range(h * per, (h + 1) * per):
                pltpu.make_async_copy(uv_ref.at[ids[t0 + j, k]], rows.at[slot0 + j, k],
                                      sems.at[slot0 + j]).start(priority=k % 2)
        return fetch

    def mix_group(t0, slot0, fetch):
        parts = []
        for j in range(group):
            u_rows = rows[slot0 + j, :, :half, :].reshape(cols, lanes).astype(bf16)
            prod = lax.dot_general(x_ref[t0 + j].astype(bf16), u_rows, nt, preferred_element_type=f32)
            part = jnp.sum(prod * diag, axis=0, keepdims=True)
            parts += [part[:, i * seg:(i + 1) * seg] for i in range(n_seg)]
            fetch(j)
        part = jnp.concatenate(parts, axis=0)
        hi = part.astype(bf16)
        lo = (part - hi.astype(f32)).astype(bf16)
        ones = seg_ref[...]
        act = jnp.dot(hi, ones, preferred_element_type=f32) + jnp.dot(lo, ones, preferred_element_type=f32)
        for j in range(group):
            w = g_ref[t0 + j] * _gelu_tanh(act[j * n_seg:(j + 1) * n_seg])
            w = jnp.concatenate([jnp.broadcast_to(w[i:i + 1, :], (half, seg)) for i in range(n_seg)], axis=1)
            v_rows = rows[slot0 + j, :, half:, :].reshape(cols, lanes).astype(bf16)
            o_ref[t0 + j] = jnp.dot((w * diag).astype(bf16), v_rows, preferred_element_type=f32)
            fetch(group + j)

    sets, ahead = depth // group, PEER_FETCH_AHEAD

    @pl.when(step_id == 0)
    def _():
        for a in range(ahead):
            first = fetcher(ids_ref, a * group, a * group)
            for c in range(2 * group):
                first(c)

    def sweep(it, last):
        for q in range(sets):
            g = sets * it + q
            for j in range(group):
                wait(q * group + j)
            into = ((q + ahead) % sets) * group
            if last and q + ahead >= sets:
                fetch = fetcher(ids_next_ref, (q + ahead - sets) * group, into)
            else:
                fetch = fetcher(ids_ref, (g + ahead) * group, into)
            mix_group(g * group, q * group, fetch)

    def body(it, carry):
        sweep(it, False)
        return carry

    lax.fori_loop(0, groups // sets - 1, body, 0)
    sweep(groups // sets - 1, True)

    @pl.when(step_id == last_step)
    def _():
        for j in range(ahead * group):
            wait(j)


def peer_experts(xn, ids, gates, uv):
    n, d = xn.shape
    tb = PEER_TOKENS_PER_STEP
    slab, half, lanes = PEER_SLAB_ROWS, PEER_SLAB_ROWS // 2, PEER_SLAB_LANES
    seg = 2 * lanes
    cols = PEER_PICKS * half
    assert n % tb == 0 and tb % PEER_ROW_BUFFERS == 0 and half * lanes == d and cols % seg == 0
    assert 0 < PEER_FETCH_AHEAD < PEER_ROW_BUFFERS // PEER_GROUP
    same_pick = np.arange(seg)[:, None] // half == np.arange(seg)[None, :] // half
    out = pl.pallas_call(
        _peer_expert_body,
        grid=(n // tb,),
        in_specs=[
            pl.BlockSpec((tb, PEER_PICKS), lambda i: (i, 0), memory_space=pltpu.SMEM),
            pl.BlockSpec((tb, PEER_PICKS), lambda i: (jnp.minimum(i + 1, n // tb - 1), 0), memory_space=pltpu.SMEM),
            pl.BlockSpec((tb, half, lanes), lambda i: (i, 0, 0)),
            pl.BlockSpec((tb, cols // seg, seg), lambda i: (i, 0, 0)),
            pl.BlockSpec((seg, seg), lambda i: (0, 0)),
            pl.BlockSpec(memory_space=pl.ANY),
        ],
        out_specs=pl.BlockSpec((tb, half, lanes), lambda i: (i, 0, 0)),
        out_shape=jax.ShapeDtypeStruct((n, half, lanes), f32),
        scratch_shapes=[
            pltpu.VMEM((PEER_ROW_BUFFERS, PEER_PICKS, slab, lanes), f32),
            pltpu.SemaphoreType.DMA((PEER_ROW_BUFFERS,)),
        ],
        compiler_params=pltpu.CompilerParams(dimension_semantics=("arbitrary",)),
        name="peer_experts",
    )(ids, ids, xn.reshape(n, half, lanes), jnp.repeat(gates, half, axis=1).reshape(n, cols // seg, seg),
      jnp.asarray(same_pick, jnp.bfloat16), uv)
    return out.reshape(n, d)


PEER_ROUTE_TOKENS = 128
PEER_HALF_DIM = PEER_QUERY_DIM // 2


def _top_rows(s, row_id, k, payload=None):
    vals, picks = [], []
    sentinel = s.shape[0]
    for _ in range(k):
        m = jnp.max(s, axis=0, keepdims=True)
        first = jnp.min(jnp.where(s == m, row_id, sentinel), axis=0, keepdims=True)
        hit = row_id == first
        vals.append(m)
        if payload is None:
            picks.append(first)
        else:
            picks.append(jnp.max(jnp.where(hit, payload, -1), axis=0, keepdims=True))
        s = jnp.where(hit, -jnp.inf, s)
    return jnp.concatenate(vals, 0), jnp.concatenate(picks, 0)


def _peer_route_body(x_ref, wq_ref, sk_ref, ids_ref, gate_ref, q_scr):
    tn = x_ref.shape[0]
    q = jnp.dot(x_ref[...].astype(jnp.bfloat16), wq_ref[...], preferred_element_type=f32)
    for j in range(2 * PEER_HEADS):
        q_scr[j] = q[:, j * PEER_HALF_DIM:(j + 1) * PEER_HALF_DIM].astype(jnp.bfloat16)
    key_id = lax.broadcasted_iota(jnp.int32, (PEER_N_KEYS, tn), 0)
    n_cand = -(-sum(PEER_TOPK // (a + 1) for a in range(PEER_TOPK)) // 8) * 8
    cand_id = lax.broadcasted_iota(jnp.int32, (n_cand, tn), 0)

    def head(h, carry):
        tops = []
        for p in range(2):
            s = lax.dot_general(sk_ref[2 * h + p], q_scr[2 * h + p], (((1,), (1,)), ((), ())),
                                preferred_element_type=f32)
            tops.append(_top_rows(s, key_id, PEER_TOPK))
        (v0, i0), (v1, i1) = tops
        cs, ce = [], []
        for a in range(PEER_TOPK):
            nb = PEER_TOPK // (a + 1)
            cs.append(v0[a:a + 1] + v1[:nb])
            ce.append(i0[a:a + 1] * PEER_N_KEYS + i1[:nb])
        pad = n_cand - sum(c.shape[0] for c in cs)
        cand_s = jnp.concatenate(cs + [jnp.full((pad, tn), -jnp.inf, f32)], axis=0)
        cand_e = jnp.concatenate(ce + [jnp.zeros((pad, tn), jnp.int32)], axis=0)
        top_s, top_e = _top_rows(cand_s, cand_id, PEER_TOPK, payload=cand_e)
        e = jnp.exp(top_s - top_s[0:1])
        ids_ref[h] = top_e
        gate_ref[h] = e / jnp.sum(e, axis=0, keepdims=True)
        return carry

    lax.fori_loop(0, PEER_HEADS, head, 0)


def peer_route(xn, wq, sub_keys):
    n, d = xn.shape
    tn = PEER_ROUTE_TOKENS
    assert n % tn == 0
    n_q = 2 * PEER_HEADS * PEER_HALF_DIM
    sk = sub_keys.reshape(2 * PEER_HEADS, PEER_N_KEYS, PEER_HALF_DIM).astype(jnp.bfloat16)
    ids_t, gates_t = pl.pallas_call(
        _peer_route_body,
        grid=(n // tn,),
        in_specs=[
            pl.BlockSpec((tn, d), lambda i: (i, 0)),
            pl.BlockSpec((d, n_q), lambda i: (0, 0)),
            pl.BlockSpec((2 * PEER_HEADS, PEER_N_KEYS, PEER_HALF_DIM), lambda i: (0, 0, 0)),
        ],
        out_specs=[
            pl.BlockSpec((PEER_HEADS, PEER_TOPK, tn), lambda i: (0, 0, i)),
            pl.BlockSpec((PEER_HEADS, PEER_TOPK, tn), lambda i: (0, 0, i)),
        ],
        out_shape=[
            jax.ShapeDtypeStruct((PEER_HEADS, PEER_TOPK, n), jnp.int32),
            jax.ShapeDtypeStruct((PEER_HEADS, PEER_TOPK, n), f32),
        ],
        scratch_shapes=[pltpu.VMEM((2 * PEER_HEADS, tn, PEER_HALF_DIM), jnp.bfloat16)],
        compiler_params=pltpu.CompilerParams(dimension_semantics=("arbitrary",)),
        name="peer_route",
    )(xn, wq.astype(jnp.bfloat16), sk)
    to_rows = lambda a: a.reshape(PEER_PICKS, n).T
    return to_rows(ids_t), to_rows(gates_t)


def peer_ffn_tokens(xn, wq, sub_keys, uv):
    ids, gates = peer_route(xn, wq, sub_keys)
    return peer_experts(xn, ids, gates, uv)


def kernel(x_prompt, x_sample, cache_cmp_kv, cache_slc_kv, cache_win_kv, state_mlstm_c, state_mlstm_n,
           state_mlstm_m, cache_mem_kv, page_table, mem_prompt, g_mix, w_in, b_in, b_forget, ml_head_gain,
           cmp_pe, cmp_w1, cmp_b1, cmp_w2, w_out, g_xattn, g_mem, w_xq, w_xk, w_xv, w_xo, g_ffn, peer_wq,
           peer_sub_keys, peer_u, peer_v, g_final):
    B, T = x_prompt.shape[:2]
    DB, S = x_sample.shape[:2]
    past_len = page_table.shape[1] * PAGE_SIZE
    l = 0

    o_i, o_nq, o_nkv, o_ng = IN_OFFSETS[3], IN_OFFSETS[5], IN_OFFSETS[6], IN_OFFSETS[7]
    n_small = 2 * ML_HEADS + 3 * NSA_HEADS
    small_pad = 128 - n_small
    regroup = lambda a: jnp.concatenate(
        [a[..., :o_i], a[..., o_nq:o_nkv], a[..., o_nkv:o_ng], a[..., o_i:o_nq], a[..., o_ng:],
         jnp.zeros(a.shape[:-1] + (small_pad,), a.dtype)], axis=-1)
    w_in_g, b_in_g = regroup(w_in[l]), regroup(b_in[l])
    in_splits = (4 * ML_WIDTH, NSA_WIDTH, 6 * NSA_KV_LANES, n_small + small_pad)

    def project(x):
        b, t, _ = x.shape
        ml, nq, nkv, small = fused_linear(x.reshape(b * t, D_MODEL), w_in_g, pre_gain=g_mix[l], bias=b_in_g,
                                          splits=in_splits)
        ng = jax.nn.sigmoid(small[:, 2 * ML_HEADS:n_small]).reshape(b, t, 3 * NSA_HEADS)
        return (ml.reshape(b, t, 4 * ML_WIDTH), small.reshape(b, t, -1), nq.reshape(b, t, NSA_WIDTH),
                nkv.reshape(b, t, 6 * NSA_KV_LANES), ng)

    def after_mixers(x, h_ml, h_nsa, mem_kv):
        b, t, _ = x.shape
        h = jnp.concatenate([h_ml, h_nsa], -1).reshape(b * t, D_MODEL)
        x1, xn = fused_linear(h, w_out[l], residual=x.reshape(b * t, D_MODEL), post_gain=g_xattn[l])
        q = fused_linear(xn, w_xq[l]).reshape(b, t, D_MODEL)
        t_pad = -t % 8
        o = cross_attention(jnp.pad(q, ((0, 0), (0, t_pad), (0, 0))), mem_kv)[:, :t]
        return fused_linear(o.reshape(b * t, D_MODEL), w_xo[l], residual=x1, post_gain=g_ffn[l])

    ml, small, nq, nkv, ng = project(x_prompt)
    h_ml, p_c, p_n, p_m = mlstm_pallas(
        ml, small, b_forget[l], ml_head_gain[l],
        jnp.zeros((B, ML_HEADS, ML_HEAD_DIM, ML_HEAD_DIM), f32),
        jnp.zeros((B, ML_HEADS, ML_HEAD_DIM), f32),
        jnp.full((B, ML_HEADS), -jnp.inf, f32))
    rows6 = nkv.reshape(B, T, 6, NSA_KV_HEADS, NSA_HEAD_DIM)
    p_cmp, p_slc, win_rows = rows6[:, :, 0:2], rows6[:, :, 2:4], rows6[:, :, 4:6]
    seg_w = CMP_STRIDE * 2 * NSA_KV_LANES
    kc, vc = compress_rows(nkv[..., :2 * NSA_KV_LANES].reshape(B, T // CMP_STRIDE, seg_w),
                           cmp_pe[l], cmp_w1[l], cmp_b1[l], cmp_w2[l])
    h_nsa = nsa_prompt(nq, ng, nkv, kc, vc)
    p_mem = fused_linear(mem_prompt.reshape(-1, D_MODEL), jnp.concatenate([w_xk[l], w_xv[l]], axis=1),
                         pre_gain=g_mem[l]).reshape(B, MEM_LEN, 2, MEM_HEADS, MEM_HEAD_DIM)
    xp, xp_ffn_in = after_mixers(x_prompt, h_ml, h_nsa, p_mem)
    p_win = win_rows[:, T - min(WINDOW, T):]

    ml, small, nq, nkv, ng = project(x_sample)
    t_pad = ML_CHUNK - S
    ml_p = jnp.pad(ml, ((0, 0), (0, t_pad), (0, 0)))
    small_p = jnp.pad(small, ((0, 0), (0, t_pad), (0, 0)))
    small_p = small_p.at[:, S:, :ML_HEADS].set(-jnp.inf).at[:, S:, ML_HEADS:2 * ML_HEADS].set(jnp.inf)
    h_ml, s_c, s_n, s_m = mlstm_pallas(ml_p, small_p, b_forget[l], ml_head_gain[l],
                                       state_mlstm_c[l], state_mlstm_n[l], state_mlstm_m[l])
    h_ml = h_ml[:, :S]
    rows6 = nkv.reshape(DB, S, 6, NSA_KV_HEADS, NSA_HEAD_DIM)
    s_cmp, s_slc, win_rows = rows6[:, :, 0:2], rows6[:, :, 2:4], rows6[:, :, 4:6]
    win_buf = cache_win_kv[l]
    win_ext = jnp.concatenate([win_buf, win_rows], axis=1)
    h_nsa = nsa_decode(nq, ng, past_len, cache_cmp_kv[l], cache_slc_kv[l], page_table, s_cmp, s_slc,
                       win_ext, past_len - win_buf.shape[1], cmp_pe[l], cmp_w1[l], cmp_b1[l], cmp_w2[l])
    xs, xs_ffn_in = after_mixers(x_sample, h_ml, h_nsa, cache_mem_kv[l])
    w_keep = min(WINDOW, past_len + S)
    s_win = win_ext[:, win_ext.shape[1] - w_keep:]

    n_p = B * T
    uv = jnp.concatenate([peer_u[l], peer_v[l]], axis=1).reshape(-1, PEER_SLAB_ROWS, PEER_SLAB_LANES)
    ffn = peer_ffn_tokens(jnp.concatenate([xp_ffn_in, xs_ffn_in], 0), peer_wq[l], peer_sub_keys[l], uv)
    y_prompt = add_rmsnorm(xp, ffn[:n_p], g_final).reshape(x_prompt.shape)
    y_sample = add_rmsnorm(xs, ffn[n_p:], g_final).reshape(x_sample.shape)
    st = lambda a: a[None]
    return (y_prompt, y_sample,
            st(p_cmp), st(p_slc), st(p_win), st(p_c), st(p_n), st(p_m), st(p_mem),
            st(s_cmp), st(s_slc), st(s_win), st(s_c), st(s_n), st(s_m))
```

```python
import functools

import jax
import jax.numpy as jnp
from jax import lax
import numpy as np
from jax.experimental import pallas as pl
from jax.experimental.pallas import tpu as pltpu

D_MODEL = 1024
PAGE_SIZE = 128

ML_WIDTH = D_MODEL // 2
ML_HEADS = 4
ML_HEAD_DIM = ML_WIDTH // ML_HEADS
ML_CHUNK = 128
NSA_WIDTH = D_MODEL - ML_WIDTH
NSA_HEADS = 8
NSA_HEAD_DIM = NSA_WIDTH // NSA_HEADS
NSA_KV_HEADS = 2
NSA_GROUP = NSA_HEADS // NSA_KV_HEADS
CMP_BLOCK = 32
CMP_STRIDE = 16
SEL_BLOCK = 64
N_SEL = 16
WINDOW = 512
Q_BLOCK = 128
FORCED_SCORE = 1.0e4
INVALID_SCORE = -1.0
MEM_LEN = 256
MEM_HEADS = 4
MEM_HEAD_DIM = D_MODEL // MEM_HEADS
PEER_HEADS = 8
PEER_N_KEYS = 128
PEER_TOPK = 16
PEER_QUERY_DIM = 256
EPS = 1e-6
TINY = 1e-30
IN_SPLITS = (ML_WIDTH, ML_WIDTH, ML_WIDTH, ML_WIDTH, ML_HEADS, ML_HEADS,
             NSA_WIDTH, 6 * NSA_KV_HEADS * NSA_HEAD_DIM, 3 * NSA_HEADS)
IN_OFFSETS = tuple(int(o) for o in np.cumsum(IN_SPLITS)[:-1])

f32 = jnp.float32

ROW_TILE = 512
DENSE_VMEM_BYTES = 56 * 1024 * 1024


def _rms(x, g):
    return x * lax.rsqrt(jnp.mean(x * x, -1, keepdims=True) + EPS) * g


def _add_norm_body(x_ref, r_ref, g_ref, o_ref):
    o_ref[...] = _rms(x_ref[...] + r_ref[...], g_ref[...])


def add_rmsnorm(x, r, g):
    shape = x.shape
    d = shape[-1]
    x2, r2 = x.reshape(-1, d), r.reshape(-1, d)
    n = x2.shape[0]
    tm = min(n, ROW_TILE)
    rows = pl.BlockSpec((tm, d), lambda i: (i, 0))
    out = pl.pallas_call(
        _add_norm_body,
        grid=(n // tm,),
        in_specs=[rows, rows, pl.BlockSpec((1, d), lambda i: (0, 0))],
        out_specs=rows,
        out_shape=jax.ShapeDtypeStruct((n, d), f32),
        name="add_rmsnorm",
    )(x2, r2, g.reshape(1, d))
    return out.reshape(shape)


def _linear_body(*refs, pre_norm, has_bias, has_res, post_norm, splits):
    it = iter(refs)
    x_ref, w_ref = next(it), next(it)
    x = x_ref[...]
    if pre_norm:
        x = _rms(x, next(it)[...])
    y = jnp.dot(x.astype(jnp.bfloat16), w_ref[...], preferred_element_type=f32)
    if has_bias:
        y = y + next(it)[...]
    if has_res:
        y = y + next(it)[...]
    post_gain = next(it)[...] if post_norm else None
    off = 0
    for m in splits:
        next(it)[...] = y[:, off:off + m]
        off += m
    if post_norm:
        next(it)[...] = _rms(y, post_gain)


def fused_linear(x, w, *, pre_gain=None, bias=None, residual=None, post_gain=None, splits=None):
    n, k = x.shape
    m = w.shape[1]
    splits = (m,) if splits is None else tuple(splits)
    assert sum(splits) == m and (post_gain is None or len(splits) == 1)
    tm = min(n, ROW_TILE)
    assert n % tm == 0
    row = lambda c: pl.BlockSpec((tm, c), lambda i: (i, 0))
    const = lambda r, c: pl.BlockSpec((r, c), lambda i: (0, 0))
    args, specs = [x, w.astype(jnp.bfloat16)], [row(k), const(k, m)]
    if pre_gain is not None:
        args.append(pre_gain.reshape(1, k)); specs.append(const(1, k))
    if bias is not None:
        args.append(bias.reshape(1, m)); specs.append(const(1, m))
    if residual is not None:
        args.append(residual); specs.append(row(m))
    if post_gain is not None:
        args.append(post_gain.reshape(1, m)); specs.append(const(1, m))
    out_cols = splits + ((m,) if post_gain is not None else ())
    outs = pl.pallas_call(
        functools.partial(_linear_body, pre_norm=pre_gain is not None, has_bias=bias is not None,
                          has_res=residual is not None, post_norm=post_gain is not None, splits=splits),
        grid=(n // tm,),
        in_specs=specs,
        out_specs=[row(c) for c in out_cols],
        out_shape=[jax.ShapeDtypeStruct((n, c), f32) for c in out_cols],
        compiler_params=pltpu.CompilerParams(dimension_semantics=("arbitrary",),
                                             vmem_limit_bytes=DENSE_VMEM_BYTES),
        name="fused_linear",
    )(*args)
    return outs[0] if len(outs) == 1 else tuple(outs)


def _xattn_body(q_ref, kv_ref, o_ref):
    bf16 = jnp.bfloat16
    d = MEM_HEAD_DIM
    q = q_ref[0]
    kv = kv_ref[0].astype(bf16)
    for h in range(MEM_HEADS):
        k_h = kv[:, h * d:(h + 1) * d]
        v_h = kv[:, (MEM_HEADS + h) * d:(MEM_HEADS + h + 1) * d]
        s = lax.dot_general(q[:, h * d:(h + 1) * d].astype(bf16), k_h, (((1,), (1,)), ((), ())),
                            preferred_element_type=f32) * (d ** -0.5)
        e = jnp.exp(s - jnp.max(s, axis=-1, keepdims=True))
        p = e / jnp.sum(e, axis=-1, keepdims=True)
        o_ref[0, :, h * d:(h + 1) * d] = jnp.dot(p.astype(bf16), v_h, preferred_element_type=f32)


def cross_attention(q, mem_kv):
    B, T, w = q.shape
    M = mem_kv.shape[1]
    kv = mem_kv.reshape(B, M, 2 * w)
    tm = min(T, ROW_TILE)
    assert T % tm == 0 and tm % 8 == 0
    return pl.pallas_call(
        _xattn_body,
        grid=(B, T // tm),
        in_specs=[pl.BlockSpec((1, tm, w), lambda b, i: (b, i, 0)),
                  pl.BlockSpec((1, M, 2 * w), lambda b, i: (b, 0, 0))],
        out_specs=pl.BlockSpec((1, tm, w), lambda b, i: (b, i, 0)),
        out_shape=jax.ShapeDtypeStruct((B, T, w), f32),
        compiler_params=pltpu.CompilerParams(dimension_semantics=("arbitrary", "arbitrary")),
        name="cross_attention",
    )(q, kv)


def masked_softmax(s, mask):
    s = jnp.where(mask, s.astype(f32), -jnp.inf)
    mx = jnp.max(s, -1, keepdims=True)
    mx = jnp.where(jnp.isfinite(mx), mx, 0.0)
    e = jnp.exp(s - mx)
    return e / jnp.maximum(e.sum(-1, keepdims=True), TINY)


def _log_sigmoid(x):
    return jnp.minimum(x, 0.0) - jnp.log1p(jnp.exp(-jnp.abs(x)))


def _cumsum_lanes(x):
    lane = lax.broadcasted_iota(jnp.int32, x.shape, 1)
    shift = 1
    while shift < x.shape[1]:
        x = x + jnp.where(lane >= shift, pltpu.roll(x, shift, axis=1), 0.0)
        shift *= 2
    return x


def _mlstm_body(ml_ref, small_ref, bias_ref, gain_ref, c0_ref, n0_ref, m0_ref, h_ref, c_ref, n_ref, m_ref):
    H, d, L = ML_HEADS, ML_HEAD_DIM, ml_ref.shape[1]
    bf16 = jnp.bfloat16
    nt = (((1,), (1,)), ((), ()))
    tn = (((0,), (0,)), ((), ()))

    @pl.when(pl.program_id(1) == 0)
    def _():
        c_ref[...] = c0_ref[...]
        n_ref[...] = n0_ref[...]
        m_ref[...] = m0_ref[...]

    blk = ml_ref[0]
    small = small_ref[0] + bias_ref[...]
    small_t = small.T
    causal = (lax.broadcasted_iota(jnp.int32, (L, L), 0) >= lax.broadcasted_iota(jnp.int32, (L, L), 1))
    for h in range(H):
        q = blk[:, h * d:(h + 1) * d]
        k = blk[:, (H + h) * d:(H + h + 1) * d] * (d ** -0.5)
        v = blk[:, (2 * H + h) * d:(2 * H + h + 1) * d]
        o_pre = blk[:, (3 * H + h) * d:(3 * H + h + 1) * d]
        ig_row, ig_col = small_t[h:h + 1, :], small[:, h:h + 1]
        b_row = _cumsum_lanes(_log_sigmoid(small_t[H + h:H + h + 1, :]))
        b_col = jnp.broadcast_to(b_row, (L, L)).T
        b_t, b_last = b_col[:, 0:1], b_row[:, L - 1:L]
        c_prev, n_prev, m_prev = c_ref[0, h], n_ref[0, h], m_ref[0, h]
        log_d = jnp.where(causal, b_col - b_row + ig_row, -jnp.inf)
        inter = b_t + m_prev
        m_t = jnp.maximum(inter, jnp.max(log_d, axis=1, keepdims=True))
        a = jnp.exp(inter - m_t)
        qb, vb = q.astype(bf16), v.astype(bf16)
        qk = lax.dot_general(qb, k.astype(bf16), nt, preferred_element_type=f32) * jnp.exp(log_d - m_t)
        num = (a * jnp.dot(qb, c_prev.astype(bf16), preferred_element_type=f32)
               + jnp.dot(qk.astype(bf16), vb, preferred_element_type=f32))
        den = a * jnp.sum(q * n_prev, axis=1, keepdims=True) + jnp.sum(qk, axis=1, keepdims=True)
        hid = num / jnp.maximum(jnp.abs(den), jnp.exp(-m_t))
        m_new = m_t[L - 1:L, :]
        kw = k * jnp.exp(b_last - b_t + ig_col - m_new)
        decay = jnp.exp(b_last + m_prev - m_new)
        c_ref[0, h] = decay * c_prev + lax.dot_general(kw.astype(bf16), vb, tn, preferred_element_type=f32)
        n_ref[0, h] = decay * n_prev + jnp.sum(kw, axis=0, keepdims=True)
        m_ref[0, h] = m_new
        gated = jax.nn.sigmoid(o_pre) * hid
        h_ref[0, :, h * d:(h + 1) * d] = _rms(gated, gain_ref[:, h * d:(h + 1) * d])


def mlstm_pallas(ml, small, b_forget, head_gain, c0, n0, m0):
    B, T, _ = ml.shape
    H, d, L = ML_HEADS, ML_HEAD_DIM, ML_CHUNK
    assert T % L == 0
    bias = jnp.zeros((1, small.shape[-1]), f32).at[0, H:2 * H].set(b_forget)
    state = lambda *s: pl.BlockSpec((1,) + s, lambda b, i: (b,) + (0,) * len(s))
    h, c, n, m = pl.pallas_call(
        _mlstm_body,
        grid=(B, T // L),
        in_specs=[
            pl.BlockSpec((1, L, ml.shape[-1]), lambda b, i: (b, i, 0)),
            pl.BlockSpec((1, L, small.shape[-1]), lambda b, i: (b, i, 0)),
            pl.BlockSpec((1, small.shape[-1]), lambda b, i: (0, 0)),
            pl.BlockSpec((1, H * d), lambda b, i: (0, 0)),
            state(H, d, d), state(H, 1, d), state(H, 1, 1),
        ],
        out_specs=[pl.BlockSpec((1, L, H * d), lambda b, i: (b, i, 0)),
                   state(H, d, d), state(H, 1, d), state(H, 1, 1)],
        out_shape=[jax.ShapeDtypeStruct((B, T, H * d), f32), jax.ShapeDtypeStruct((B, H, d, d), f32),
                   jax.ShapeDtypeStruct((B, H, 1, d), f32), jax.ShapeDtypeStruct((B, H, 1, 1), f32)],
        compiler_params=pltpu.CompilerParams(dimension_semantics=("arbitrary", "arbitrary")),
        name="mlstm",
    )(ml, small, bias, head_gain.reshape(1, H * d), c0, n0.reshape(B, H, 1, d), m0.reshape(B, H, 1, 1))
    return h, c, n.reshape(B, H, d), m.reshape(B, H)


def compress_rows(seg, cmp_pe, cmp_w1, cmp_b1, cmp_w2):
    B, n_seg, width = seg.shape
    G, d, t = NSA_KV_HEADS, NSA_HEAD_DIM, CMP_STRIDE
    e = cmp_w1.shape[-1]
    w1 = cmp_w1.reshape(2, 2, t, d, e)
    eye_kv, eye_g = jnp.eye(2, dtype=f32), jnp.eye(G, dtype=f32)
    w_big = jnp.einsum('kptde,kK,gG->tkgdpKGe', w1, eye_kv, eye_g).reshape(width, 2 * 2 * G * e)
    ab = fused_linear(seg.reshape(B * n_seg, width), w_big).reshape(B, n_seg, 2, 2, G, e)
    const = jnp.einsum('kr,kre->ke', cmp_pe.reshape(2, -1), cmp_w1) + cmp_b1
    pre = ab[:, :-1, 0] + ab[:, 1:, 1] + const[:, None, :]
    out = jnp.einsum('bnkge,kef->kbgnf', jax.nn.gelu(pre), cmp_w2)
    return out[0], out[1]


def cmp_to_sel(imp, nsb):
    r = SEL_BLOCK // CMP_STRIDE
    nc = imp.shape[-1]
    lead = imp.shape[:-1]
    tot = r * nsb
    padw = [(0, 0)] * len(lead)
    first = jnp.pad(imp, padw + [(0, tot - nc)]).reshape(lead + (nsb, r)).sum(-1)
    second = jnp.pad(imp, padw + [(1, tot - nc - 1)]).reshape(lead + (nsb, r)).sum(-1)
    return 0.5 * (first + second)


NSA_SEL_CHUNK = 512
NSA_WIN_CHUNKS = WINDOW // Q_BLOCK + 1
NSA_KV_LANES = NSA_KV_HEADS * NSA_HEAD_DIM


def _softmax_masked(s, mask):
    sm = jnp.where(mask, s, -jnp.inf)
    mx = jnp.max(sm, axis=-1, keepdims=True)
    mx = jnp.where(mx == -jnp.inf, 0.0, mx)
    e = jnp.exp(sm - mx)
    return e * (1.0 / jnp.maximum(jnp.sum(e, axis=-1, keepdims=True), TINY))


def _top_rows_mask(s, row_id, k):
    sel = jnp.zeros(s.shape, f32)
    sentinel = s.shape[0]
    for _ in range(k):
        m = jnp.max(s, axis=0, keepdims=True)
        first = jnp.min(jnp.where(s == m, row_id, sentinel), axis=0, keepdims=True)
        hit = row_id == first
        sel = jnp.where(hit, 1.0, sel)
        s = jnp.where(hit, -jnp.inf, s)
    return sel


def _nsa_prompt_body(q_ref, gate_ref, kct_ref, vc_ref, mt_ref, kts_ref, vs_ref, ktw_ref, vw_ref, o_ref):
    j = pl.program_id(1)
    Q, R, d = Q_BLOCK, NSA_GROUP, NSA_HEAD_DIM
    n_cmp = kct_ref.shape[-1]
    n_blk = mt_ref.shape[0]
    bf16 = jnp.bfloat16
    q_all = q_ref[0] * (d ** -0.5)
    gates = gate_ref[0]
    q_pos = j * Q + lax.broadcasted_iota(jnp.int32, (Q, 1), 0)
    q_pos_l = j * Q + lax.broadcasted_iota(jnp.int32, (1, Q), 1)
    blk_r = lax.broadcasted_iota(jnp.int32, (n_blk, 1), 0)
    blk_id = lax.broadcasted_iota(jnp.int32, (n_blk, Q), 0)
    nt = (((1,), (1,)), ((), ()))

    for g in range(NSA_KV_HEADS):
        lanes = slice(g * d, (g + 1) * d)
        qg = jnp.concatenate([q_all[:, (g * R + r) * d:(g * R + r + 1) * d] for r in range(R)], axis=0).astype(bf16)

        s = jnp.dot(qg, kct_ref[0, g], preferred_element_type=f32).reshape(R, Q, n_cmp)
        n_id = lax.broadcasted_iota(jnp.int32, (1, n_cmp), 1)
        cmask = (n_id * CMP_STRIDE + (CMP_BLOCK - 1) <= q_pos) & (n_id < n_cmp - 1)
        p = _softmax_masked(s, cmask[None])
        o_c = jnp.dot(p.reshape(R * Q, n_cmp).astype(bf16), vc_ref[0], preferred_element_type=f32)[:, lanes]

        p_sum = p[0] + p[1] + p[2] + p[3]
        hi = p_sum.astype(bf16)
        lo = (p_sum - hi.astype(f32)).astype(bf16)
        mt = mt_ref[...]
        imp_t = (lax.dot_general(mt, hi, nt, preferred_element_type=f32)
                 + lax.dot_general(mt, lo, nt, preferred_element_type=f32))
        valid = blk_r * SEL_BLOCK <= q_pos_l
        forced = (blk_r == 0) | (blk_r == jnp.right_shift(q_pos_l, 6))
        score = jnp.where(forced, FORCED_SCORE, jnp.where(valid, imp_t, INVALID_SCORE))
        sel = _top_rows_mask(score, blk_id, min(N_SEL, n_blk)).T.astype(bf16)

        kc = kts_ref.shape[-1]

        def chunk(c, carry):
            m, l, acc = carry
            kt = kts_ref[0, c, g * d:(g + 1) * d, :]
            sc = jnp.dot(qg, kt, preferred_element_type=f32).reshape(R, Q, kc)
            key = c * kc + lax.broadcasted_iota(jnp.int32, (1, kc), 1)
            expand = jnp.where(blk_r == jnp.right_shift(key, 6), 1.0, 0.0).astype(bf16)
            picked = jnp.dot(sel, expand, preferred_element_type=f32) > 0.5
            mask = picked & (key <= q_pos)
            sm = jnp.where(mask[None], sc, -jnp.inf)
            m_new = jnp.maximum(m, jnp.max(sm, axis=-1, keepdims=True))
            m_safe = jnp.where(m_new == -jnp.inf, 0.0, m_new)
            alpha = jnp.exp(m - m_safe)
            pe = jnp.exp(sm - m_safe)
            l = l * alpha + jnp.sum(pe, axis=-1, keepdims=True)
            pv = jnp.dot(pe.reshape(R * Q, kc).astype(bf16), vs_ref[0, c], preferred_element_type=f32)
            acc = acc * alpha.reshape(R * Q, 1) + pv
            return m_new, l, acc

        n_chunks = (j * Q + Q + kc - 1) // kc
        init = (jnp.full((R, Q, 1), -jnp.inf, f32), jnp.zeros((R, Q, 1), f32),
                jnp.zeros((R * Q, NSA_KV_LANES), f32))
        _, l_s, acc_s = lax.fori_loop(0, n_chunks, chunk, init)
        o_s = acc_s[:, lanes] * (1.0 / jnp.maximum(l_s.reshape(R * Q, 1), TINY))

        ss, vv = [], []
        for i in range(NSA_WIN_CHUNKS):
            cc = jnp.maximum(j - (NSA_WIN_CHUNKS - 1) + i, 0)
            ss.append(jnp.dot(qg, ktw_ref[0, cc, g * d:(g + 1) * d, :], preferred_element_type=f32))
            vv.append(vw_ref[0, cc])
        span = NSA_WIN_CHUNKS * Q
        sw = jnp.concatenate(ss, axis=1).reshape(R, Q, span)
        k_pos = (j - (NSA_WIN_CHUNKS - 1)) * Q + lax.broadcasted_iota(jnp.int32, (1, span), 1)
        wmask = (k_pos >= 0) & (k_pos <= q_pos) & (k_pos >= q_pos - WINDOW)
        pw = _softmax_masked(sw, wmask[None])
        o_w = jnp.dot(pw.reshape(R * Q, span).astype(bf16), jnp.concatenate(vv, axis=0),
                      preferred_element_type=f32)[:, lanes]

        for r in range(R):
            h = g * R + r
            rows = slice(r * Q, (r + 1) * Q)
            o_ref[0, :, h * d:(h + 1) * d] = (gates[:, 3 * h:3 * h + 1] * o_c[rows]
                                              + gates[:, 3 * h + 1:3 * h + 2] * o_s[rows]
                                              + gates[:, 3 * h + 2:3 * h + 3] * o_w[rows])


def _cmp_to_sel_matrix(n_cmp, n_blk):
    r = SEL_BLOCK // CMP_STRIDE
    n = np.arange(n_cmp)
    b = np.arange(n_blk)[:, None]
    m = 0.5 * ((n // r == b).astype(np.float32) + ((n + 1) // r == b).astype(np.float32))
    m[:, n_cmp - 1] = 0.0
    return jnp.asarray(m, jnp.bfloat16)


def nsa_prompt(nq, gates, nkv, kc, vc):
    B, T, _ = nq.shape
    bf16 = jnp.bfloat16
    G, d, w = NSA_KV_HEADS, NSA_HEAD_DIM, NSA_KV_LANES
    n_cmp, n_blk = T // CMP_STRIDE, T // SEL_BLOCK
    kc_s, kc_w = NSA_SEL_CHUNK, Q_BLOCK
    pad = ((0, 0), (0, 0), (0, 1), (0, 0))
    kct = jnp.pad(kc, pad).transpose(0, 1, 3, 2).astype(bf16)
    vc2 = jnp.pad(vc, pad).transpose(0, 2, 1, 3).reshape(B, n_cmp, w).astype(bf16)
    chunks = lambda a, c: a.reshape(B, T // c, c, w).astype(bf16)
    kts = chunks(nkv[..., 2 * w:3 * w], kc_s).transpose(0, 1, 3, 2)
    vs = chunks(nkv[..., 3 * w:4 * w], kc_s)
    ktw = chunks(nkv[..., 4 * w:5 * w], kc_w).transpose(0, 1, 3, 2)
    vw = chunks(nkv[..., 5 * w:6 * w], kc_w)
    whole = lambda a: pl.BlockSpec((1,) + a.shape[1:], lambda b, j: (b,) + (0,) * (a.ndim - 1))
    mt = _cmp_to_sel_matrix(n_cmp, n_blk)
    return pl.pallas_call(
        _nsa_prompt_body,
        grid=(B, T // Q_BLOCK),
        in_specs=[
            pl.BlockSpec((1, Q_BLOCK, nq.shape[-1]), lambda b, j: (b, j, 0)),
            pl.BlockSpec((1, Q_BLOCK, gates.shape[-1]), lambda b, j: (b, j, 0)),
            whole(kct), whole(vc2),
            pl.BlockSpec(mt.shape, lambda b, j: (0, 0)),
            whole(kts), whole(vs), whole(ktw), whole(vw),
        ],
        out_specs=pl.BlockSpec((1, Q_BLOCK, nq.shape[-1]), lambda b, j: (b, j, 0)),
        out_shape=jax.ShapeDtypeStruct(nq.shape, f32),
        compiler_params=pltpu.CompilerParams(dimension_semantics=("arbitrary", "arbitrary"),
                                             vmem_limit_bytes=48 * 1024 * 1024),
        name="nsa_prompt",
    )(nq, gates, kct, vc2, mt, kts, vs, ktw, vw)


def nsa_decode(q, gates, past_len, pool_cmp, pool_slc, page_table, new_cmp, new_slc, win_rows, win_pos0,
               cmp_pe, cmp_w1, cmp_b1, cmp_w2):
    B, S, _ = q.shape
    G, R, d = NSA_KV_HEADS, NSA_GROUP, NSA_HEAD_DIM
    T = past_len + S
    scale = d ** -0.5
    q_pos = past_len + jnp.arange(S)
    qg = q.reshape(B, S, G, R, d).transpose(0, 2, 3, 1, 4)
    gg = gates.reshape(B, S, G, R, 3).transpose(0, 2, 3, 1, 4)

    n_seg = T // CMP_STRIDE
    rows = pool_cmp[page_table].reshape(B, past_len, 2, G, d)
    if n_seg * CMP_STRIDE > past_len:
        rows = jnp.concatenate([rows, new_cmp], axis=1)
    seg = rows[:, :n_seg * CMP_STRIDE].reshape(B, n_seg, CMP_STRIDE, 2, G, d)
    half = CMP_STRIDE * d

    def compress(kv):
        w1 = cmp_w1[kv]
        w_lo, w_hi = w1[:half].reshape(CMP_STRIDE, d, -1), w1[half:].reshape(CMP_STRIDE, d, -1)
        x = seg[:, :, :, kv]
        pre = (jnp.einsum('bntgd,tde->bgne', x[:, :-1], w_lo) + jnp.einsum('bntgd,tde->bgne', x[:, 1:], w_hi)
               + (cmp_pe[kv].reshape(-1) @ w1 + cmp_b1[kv]))
        return jax.nn.gelu(pre) @ cmp_w2[kv]

    kc, vc = compress(0), compress(1)
    c_end = jnp.arange(n_seg - 1) * CMP_STRIDE + (CMP_BLOCK - 1)
    p_c = masked_softmax(jnp.einsum('bgrqd,bgnd->bgrqn', qg, kc) * scale, c_end[None, :] <= q_pos[:, None])
    o_c = jnp.einsum('bgrqn,bgnd->bgrqd', p_c, vc)

    nsb = -(-T // SEL_BLOCK)
    n_past = past_len // SEL_BLOCK
    assert past_len % SEL_BLOCK == 0 and PAGE_SIZE % SEL_BLOCK == 0 and nsb - n_past <= 1
    imp = cmp_to_sel(p_c.sum(2), nsb)
    blk = jnp.arange(nsb)
    valid = blk[None, :] * SEL_BLOCK <= q_pos[:, None]
    forced = (blk[None, :] == 0) | (blk[None, :] == q_pos[:, None] // SEL_BLOCK)
    score = jnp.where(forced, FORCED_SCORE, jnp.where(valid, imp, INVALID_SCORE))
    _, idx = lax.top_k(score, min(N_SEL, nsb))
    per_page = PAGE_SIZE // SEL_BLOCK
    past = jnp.minimum(idx, n_past - 1)
    bi = jnp.arange(B)[:, None, None, None]
    gi = jnp.arange(G)[None, :, None, None]
    pool_blk = page_table[bi, past // per_page] * per_page + past % per_page
    blocks = pool_slc.reshape((-1, SEL_BLOCK) + pool_slc.shape[2:])[pool_blk]
    blocks = jnp.take_along_axis(blocks, gi[..., None, None, None, None], axis=6)[..., 0, :]
    tail = jnp.pad(new_slc, ((0, 0), (0, SEL_BLOCK - S), (0, 0), (0, 0), (0, 0)))
    tail = tail.transpose(0, 3, 1, 2, 4)[:, :, None, None]
    blocks = jnp.where((idx >= n_past)[..., None, None, None], tail, blocks)
    kg = blocks[..., 0, :].reshape(B, G, S, -1, d)
    vg = blocks[..., 1, :].reshape(B, G, S, -1, d)
    k_pos = (idx[..., None] * SEL_BLOCK + jnp.arange(SEL_BLOCK)).reshape(B, G, S, -1)
    p_s = masked_softmax(jnp.einsum('bgrqd,bgqsd->bgrqs', qg, kg) * scale,
                         (k_pos <= q_pos[None, None, :, None])[:, :, None])
    o_s = jnp.einsum('bgrqs,bgqsd->bgrqd', p_s, vg)

    kw = jnp.swapaxes(win_rows[:, :, 0], 1, 2)
    vw = jnp.swapaxes(win_rows[:, :, 1], 1, 2)
    w_pos = win_pos0 + jnp.arange(kw.shape[2])
    wmask = ((w_pos[None, :] <= q_pos[:, None]) & (w_pos[None, :] >= q_pos[:, None] - WINDOW)
             & (w_pos[None, :] >= 0))
    p_w = masked_softmax(jnp.einsum('bgrqd,bgkd->bgrqk', qg, kw) * scale, wmask)
    o_w = jnp.einsum('bgrqk,bgkd->bgrqd', p_w, vw)
    o = gg[..., 0:1] * o_c + gg[..., 1:2] * o_s + gg[..., 2:3] * o_w
    return o.transpose(0, 3, 1, 2, 4).reshape(B, S, NSA_HEADS * d)


PEER_PICKS = PEER_HEADS * PEER_TOPK
PEER_TOKENS_PER_STEP = 64
PEER_GROUP = 4
PEER_FETCH_AHEAD = 2
PEER_ROW_BUFFERS = 4 * PEER_GROUP
PEER_SLAB_LANES = 128
PEER_SLAB_ROWS = 2 * D_MODEL // PEER_SLAB_LANES


def _gelu_tanh(x):
    return 0.5 * x * (1.0 + jnp.tanh(0.7978845608028654 * (x + 0.044715 * x * x * x)))


def _peer_expert_body(ids_ref, ids_next_ref, x_ref, g_ref, seg_ref, uv_ref, o_ref, rows, sems):
    tokens = x_ref.shape[0]
    depth = PEER_ROW_BUFFERS
    half, lanes = PEER_SLAB_ROWS // 2, PEER_SLAB_LANES
    cols = PEER_PICKS * half
    seg = seg_ref.shape[0]
    nt = (((1,), (1,)), ((), ()))
    bf16 = jnp.bfloat16

    col_row = lax.broadcasted_iota(jnp.int32, (half, cols), 1) & (half - 1)
    sub = lax.broadcasted_iota(jnp.int32, (half, cols), 0)
    diag = jnp.where(col_row == sub, 1.0, 0.0)

    group = PEER_GROUP
    groups = tokens // group
    n_seg = cols // seg
    step_id = pl.program_id(0)
    last_step = pl.num_programs(0) - 1

    def wait(slot):
        pltpu.make_async_copy(uv_ref.at[pl.ds(0, PEER_PICKS)], rows.at[slot], sems.at[slot]).wait()

    def fetcher(ids, t0, slot0):
        per = PEER_PICKS // 2

        def fetch(c):
            j, h = divmod(c, 2)
            for k in range(h * per, (h + 1) * per):
                pltpu.make_async_copy(uv_ref.at[ids[t0 + j, k]], rows.at[slot0 + j, k],
                                      sems.at[slot0 + j]).start(priority=k % 2)
        return fetch

    def mix_group(t0, slot0, fetch):
        parts = []
        for j in range(group):
            u_rows = rows[slot0 + j, :, :half, :].reshape(cols, lanes).astype(bf16)
            prod = lax.dot_general(x_ref[t0 + j].astype(bf16), u_rows, nt, preferred_element_type=f32)
            part = jnp.sum(prod * diag, axis=0, keepdims=True)
            parts += [part[:, i * seg:(i + 1) * seg] for i in range(n_seg)]
            fetch(j)
        part = jnp.concatenate(parts, axis=0)
        hi = part.astype(bf16)
        lo = (part - hi.astype(f32)).astype(bf16)
        ones = seg_ref[...]
        act = jnp.dot(hi, ones, preferred_element_type=f32) + jnp.dot(lo, ones, preferred_element_type=f32)
        for j in range(group):
            w = g_ref[t0 + j] * _gelu_tanh(act[j * n_seg:(j + 1) * n_seg])
            w = jnp.concatenate([jnp.broadcast_to(w[i:i + 1, :], (half, seg)) for i in range(n_seg)], axis=1)
            v_rows = rows[slot0 + j, :, half:, :].reshape(cols, lanes).astype(bf16)
            o_ref[t0 + j] = jnp.dot((w * diag).astype(bf16), v_rows, preferred_element_type=f32)
            fetch(group + j)

    sets, ahead = depth // group, PEER_FETCH_AHEAD

    @pl.when(step_id == 0)
    def _():
        for a in range(ahead):
            first = fetcher(ids_ref, a * group, a * group)
            for c in range(2 * group):
                first(c)

    def sweep(it, last):
        for q in range(sets):
            g = sets * it + q
            for j in range(group):
                wait(q * group + j)
            into = ((q + ahead) % sets) * group
            if last and q + ahead >= sets:
                fetch = fetcher(ids_next_ref, (q + ahead - sets) * group, into)
            else:
                fetch = fetcher(ids_ref, (g + ahead) * group, into)
            mix_group(g * group, q * group, fetch)

    def body(it, carry):
        sweep(it, False)
        return carry

    lax.fori_loop(0, groups // sets - 1, body, 0)
    sweep(groups // sets - 1, True)

    @pl.when(step_id == last_step)
    def _():
        for j in range(ahead * group):
            wait(j)


def peer_experts(xn, ids, gates, uv):
    n, d = xn.shape
    tb = PEER_TOKENS_PER_STEP
    slab, half, lanes = PEER_SLAB_ROWS, PEER_SLAB_ROWS // 2, PEER_SLAB_LANES
    seg = 2 * lanes
    cols = PEER_PICKS * half
    assert n % tb == 0 and tb % PEER_ROW_BUFFERS == 0 and half * lanes == d and cols % seg == 0
    assert 0 < PEER_FETCH_AHEAD < PEER_ROW_BUFFERS // PEER_GROUP
    same_pick = np.arange(seg)[:, None] // half == np.arange(seg)[None, :] // half
    out = pl.pallas_call(
        _peer_expert_body,
        grid=(n // tb,),
        in_specs=[
            pl.BlockSpec((tb, PEER_PICKS), lambda i: (i, 0), memory_space=pltpu.SMEM),
            pl.BlockSpec((tb, PEER_PICKS), lambda i: (jnp.minimum(i + 1, n // tb - 1), 0), memory_space=pltpu.SMEM),
            pl.BlockSpec((tb, half, lanes), lambda i: (i, 0, 0)),
            pl.BlockSpec((tb, cols // seg, seg), lambda i: (i, 0, 0)),
            pl.BlockSpec((seg, seg), lambda i: (0, 0)),
            pl.BlockSpec(memory_space=pl.ANY),
        ],
        out_specs=pl.BlockSpec((tb, half, lanes), lambda i: (i, 0, 0)),
        out_shape=jax.ShapeDtypeStruct((n, half, lanes), f32),
        scratch_shapes=[
            pltpu.VMEM((PEER_ROW_BUFFERS, PEER_PICKS, slab, lanes), f32),
            pltpu.SemaphoreType.DMA((PEER_ROW_BUFFERS,)),
        ],
        compiler_params=pltpu.CompilerParams(dimension_semantics=("arbitrary",)),
        name="peer_experts",
    )(ids, ids, xn.reshape(n, half, lanes), jnp.repeat(gates, half, axis=1).reshape(n, cols // seg, seg),
      jnp.asarray(same_pick, jnp.bfloat16), uv)
    return out.reshape(n, d)


PEER_ROUTE_TOKENS = 256
PEER_HALF_DIM = PEER_QUERY_DIM // 2


def _top_rows(s, row_id, k, payload=None):
    vals, picks = [], []
    sentinel = s.shape[0]
    for _ in range(k):
        m = jnp.max(s, axis=0, keepdims=True)
        first = jnp.min(jnp.where(s == m, row_id, sentinel), axis=0, keepdims=True)
        hit = row_id == first
        vals.append(m)
        if payload is None:
            picks.append(first)
        else:
            picks.append(jnp.max(jnp.where(hit, payload, -1), axis=0, keepdims=True))
        s = jnp.where(hit, -jnp.inf, s)
    return jnp.concatenate(vals, 0), jnp.concatenate(picks, 0)


def _peer_route_body(x_ref, wq_ref, sk_ref, ids_ref, gate_ref, q_scr):
    tn = x_ref.shape[0]
    q = jnp.dot(x_ref[...].astype(jnp.bfloat16), wq_ref[...], preferred_element_type=f32)
    for j in range(2 * PEER_HEADS):
        q_scr[j] = q[:, j * PEER_HALF_DIM:(j + 1) * PEER_HALF_DIM].astype(jnp.bfloat16)
    key_id = lax.broadcasted_iota(jnp.int32, (PEER_N_KEYS, tn), 0)
    n_cand = -(-sum(PEER_TOPK // (a + 1) for a in range(PEER_TOPK)) // 8) * 8
    cand_id = lax.broadcasted_iota(jnp.int32, (n_cand, tn), 0)

    def head(h, carry):
        tops = []
        for p in range(2):
            s = lax.dot_general(sk_ref[2 * h + p], q_scr[2 * h + p], (((1,), (1,)), ((), ())),
                                preferred_element_type=f32)
            tops.append(_top_rows(s, key_id, PEER_TOPK))
        (v0, i0), (v1, i1) = tops
        cs, ce = [], []
        for a in range(PEER_TOPK):
            nb = PEER_TOPK // (a + 1)
            cs.append(v0[a:a + 1] + v1[:nb])
            ce.append(i0[a:a + 1] * PEER_N_KEYS + i1[:nb])
        pad = n_cand - sum(c.shape[0] for c in cs)
        cand_s = jnp.concatenate(cs + [jnp.full((pad, tn), -jnp.inf, f32)], axis=0)
        cand_e = jnp.concatenate(ce + [jnp.zeros((pad, tn), jnp.int32)], axis=0)
        top_s, top_e = _top_rows(cand_s, cand_id, PEER_TOPK, payload=cand_e)
        e = jnp.exp(top_s - top_s[0:1])
        ids_ref[h] = top_e
        gate_ref[h] = e / jnp.sum(e, axis=0, keepdims=True)
        return carry

    lax.fori_loop(0, PEER_HEADS, head, 0)


def peer_route(xn, wq, sub_keys):
    n, d = xn.shape
    tn = PEER_ROUTE_TOKENS if n % PEER_ROUTE_TOKENS == 0 else PEER_ROUTE_TOKENS // 2
    assert n % tn == 0
    n_q = 2 * PEER_HEADS * PEER_HALF_DIM
    sk = sub_keys.reshape(2 * PEER_HEADS, PEER_N_KEYS, PEER_HALF_DIM).astype(jnp.bfloat16)
    ids_t, gates_t = pl.pallas_call(
        _peer_route_body,
        grid=(n // tn,),
        in_specs=[
            pl.BlockSpec((tn, d), lambda i: (i, 0)),
            pl.BlockSpec((d, n_q), lambda i: (0, 0)),
            pl.BlockSpec((2 * PEER_HEADS, PEER_N_KEYS, PEER_HALF_DIM), lambda i: (0, 0, 0)),
        ],
        out_specs=[
            pl.BlockSpec((PEER_HEADS, PEER_TOPK, tn), lambda i: (0, 0, i)),
            pl.BlockSpec((PEER_HEADS, PEER_TOPK, tn), lambda i: (0, 0, i)),
        ],
        out_shape=[
            jax.ShapeDtypeStruct((PEER_HEADS, PEER_TOPK, n), jnp.int32),
            jax.ShapeDtypeStruct((PEER_HEADS, PEER_TOPK, n), f32),
        ],
        scratch_shapes=[pltpu.VMEM((2 * PEER_HEADS, tn, PEER_HALF_DIM), jnp.bfloat16)],
        compiler_params=pltpu.CompilerParams(dimension_semantics=("arbitrary",)),
        name="peer_route",
    )(xn, wq.astype(jnp.bfloat16), sk)
    to_rows = lambda a: a.reshape(PEER_PICKS, n).T
    return to_rows(ids_t), to_rows(gates_t)


def kernel(x_prompt, x_sample, cache_cmp_kv, cache_slc_kv, cache_win_kv, state_mlstm_c, state_mlstm_n,
           state_mlstm_m, cache_mem_kv, page_table, mem_prompt, g_mix, w_in, b_in, b_forget, ml_head_gain,
           cmp_pe, cmp_w1, cmp_b1, cmp_w2, w_out, g_xattn, g_mem, w_xq, w_xk, w_xv, w_xo, g_ffn, peer_wq,
           peer_sub_keys, peer_u, peer_v, g_final):
    B, T = x_prompt.shape[:2]
    DB, S = x_sample.shape[:2]
    past_len = page_table.shape[1] * PAGE_SIZE
    l = 0

    o_i, o_nq, o_nkv, o_ng = IN_OFFSETS[3], IN_OFFSETS[5], IN_OFFSETS[6], IN_OFFSETS[7]
    n_small = 2 * ML_HEADS + 3 * NSA_HEADS
    small_pad = 128 - n_small
    regroup = lambda a: jnp.concatenate(
        [a[..., :o_i], a[..., o_nq:o_nkv], a[..., o_nkv:o_ng], a[..., o_i:o_nq], a[..., o_ng:],
         jnp.zeros(a.shape[:-1] + (small_pad,), a.dtype)], axis=-1)
    w_in_g, b_in_g = regroup(w_in[l]), regroup(b_in[l])
    in_splits = (4 * ML_WIDTH, NSA_WIDTH, 6 * NSA_KV_LANES, n_small + small_pad)

    def project(x):
        b, t, _ = x.shape
        ml, nq, nkv, small = fused_linear(x.reshape(b * t, D_MODEL), w_in_g, pre_gain=g_mix[l], bias=b_in_g,
                                          splits=in_splits)
        ng = jax.nn.sigmoid(small[:, 2 * ML_HEADS:n_small]).reshape(b, t, 3 * NSA_HEADS)
        return (ml.reshape(b, t, 4 * ML_WIDTH), small.reshape(b, t, -1), nq.reshape(b, t, NSA_WIDTH),
                nkv.reshape(b, t, 6 * NSA_KV_LANES), ng)

    def after_mixers(x, h_ml, h_nsa, mem_kv):
        b, t, _ = x.shape
        h = jnp.concatenate([h_ml, h_nsa], -1).reshape(b * t, D_MODEL)
        x1, xn = fused_linear(h, w_out[l], residual=x.reshape(b * t, D_MODEL), post_gain=g_xattn[l])
        q = fused_linear(xn, w_xq[l]).reshape(b, t, D_MODEL)
        t_pad = -t % 8
        o = cross_attention(jnp.pad(q, ((0, 0), (0, t_pad), (0, 0))), mem_kv)[:, :t]
        return fused_linear(o.reshape(b * t, D_MODEL), w_xo[l], residual=x1, post_gain=g_ffn[l])

    ml, small, nq, nkv, ng = project(x_prompt)
    h_ml, p_c, p_n, p_m = mlstm_pallas(
        ml, small, b_forget[l], ml_head_gain[l],
        jnp.zeros((B, ML_HEADS, ML_HEAD_DIM, ML_HEAD_DIM), f32),
        jnp.zeros((B, ML_HEADS, ML_HEAD_DIM), f32),
        jnp.full((B, ML_HEADS), -jnp.inf, f32))
    rows6 = nkv.reshape(B, T, 6, NSA_KV_HEADS, NSA_HEAD_DIM)
    p_cmp, p_slc, win_rows = rows6[:, :, 0:2], rows6[:, :, 2:4], rows6[:, :, 4:6]
    seg_w = CMP_STRIDE * 2 * NSA_KV_LANES
    kc, vc = compress_rows(nkv[..., :2 * NSA_KV_LANES].reshape(B, T // CMP_STRIDE, seg_w),
                           cmp_pe[l], cmp_w1[l], cmp_b1[l], cmp_w2[l])
    h_nsa = nsa_prompt(nq, ng, nkv, kc, vc)
    p_mem = fused_linear(mem_prompt.reshape(-1, D_MODEL), jnp.concatenate([w_xk[l], w_xv[l]], axis=1),
                         pre_gain=g_mem[l]).reshape(B, MEM_LEN, 2, MEM_HEADS, MEM_HEAD_DIM)
    xp, xp_ffn_in = after_mixers(x_prompt, h_ml, h_nsa, p_mem)
    p_win = win_rows[:, T - min(WINDOW, T):]

    ml, small, nq, nkv, ng = project(x_sample)
    t_pad = ML_CHUNK - S
    ml_p = jnp.pad(ml, ((0, 0), (0, t_pad), (0, 0)))
    small_p = jnp.pad(small, ((0, 0), (0, t_pad), (0, 0)))
    small_p = small_p.at[:, S:, :ML_HEADS].set(-jnp.inf).at[:, S:, ML_HEADS:2 * ML_HEADS].set(jnp.inf)
    h_ml, s_c, s_n, s_m = mlstm_pallas(ml_p, small_p, b_forget[l], ml_head_gain[l],
                                       state_mlstm_c[l], state_mlstm_n[l], state_mlstm_m[l])
    h_ml = h_ml[:, :S]
    rows6 = nkv.reshape(DB, S, 6, NSA_KV_HEADS, NSA_HEAD_DIM)
    s_cmp, s_slc, win_rows = rows6[:, :, 0:2], rows6[:, :, 2:4], rows6[:, :, 4:6]
    win_buf = cache_win_kv[l]
    win_ext = jnp.concatenate([win_buf, win_rows], axis=1)
    h_nsa = nsa_decode(nq, ng, past_len, cache_cmp_kv[l], cache_slc_kv[l], page_table, s_cmp, s_slc,
                       win_ext, past_len - win_buf.shape[1], cmp_pe[l], cmp_w1[l], cmp_b1[l], cmp_w2[l])
    xs, xs_ffn_in = after_mixers(x_sample, h_ml, h_nsa, cache_mem_kv[l])
    w_keep = min(WINDOW, past_len + S)
    s_win = win_ext[:, win_ext.shape[1] - w_keep:]

    n_p = B * T
    uv = jnp.concatenate([peer_u[l], peer_v[l]], axis=1).reshape(-1, PEER_SLAB_ROWS, PEER_SLAB_LANES)
    routes = [peer_route(x, peer_wq[l], peer_sub_keys[l]) for x in (xp_ffn_in, xs_ffn_in)]
    ids, gates = (jnp.concatenate(a, 0) for a in zip(*routes))
    ffn = peer_experts(jnp.concatenate([xp_ffn_in, xs_ffn_in], 0), ids, gates, uv)
    y_prompt = add_rmsnorm(xp, ffn[:n_p], g_final).reshape(x_prompt.shape)
    y_sample = add_rmsnorm(xs, ffn[n_p:], g_final).reshape(x_sample.shape)
    st = lambda a: a[None]
    return (y_prompt, y_sample,
            st(p_cmp), st(p_slc), st(p_win), st(p_c), st(p_n), st(p_m), st(p_mem),
            st(s_cmp), st(s_slc), st(s_win), st(s_c), st(s_n), st(s_m))
```

```python
import functools

import jax
import jax.numpy as jnp
from jax import lax
import numpy as np
from jax.experimental import pallas as pl
from jax.experimental.pallas import tpu as pltpu

D_MODEL = 1024
PAGE_SIZE = 128

ML_WIDTH = D_MODEL // 2
ML_HEADS = 4
ML_HEAD_DIM = ML_WIDTH // ML_HEADS
ML_CHUNK = 128
NSA_WIDTH = D_MODEL - ML_WIDTH
NSA_HEADS = 8
NSA_HEAD_DIM = NSA_WIDTH // NSA_HEADS
NSA_KV_HEADS = 2
NSA_GROUP = NSA_HEADS // NSA_KV_HEADS
CMP_BLOCK = 32
CMP_STRIDE = 16
SEL_BLOCK = 64
N_SEL = 16
WINDOW = 512
Q_BLOCK = 128
FORCED_SCORE = 1.0e4
INVALID_SCORE = -1.0
MEM_LEN = 256
MEM_HEADS = 4
MEM_HEAD_DIM = D_MODEL // MEM_HEADS
PEER_HEADS = 8
PEER_N_KEYS = 128
PEER_TOPK = 16
PEER_QUERY_DIM = 256
EPS = 1e-6
TINY = 1e-30
IN_SPLITS = (ML_WIDTH, ML_WIDTH, ML_WIDTH, ML_WIDTH, ML_HEADS, ML_HEADS,
             NSA_WIDTH, 6 * NSA_KV_HEADS * NSA_HEAD_DIM, 3 * NSA_HEADS)
IN_OFFSETS = tuple(int(o) for o in np.cumsum(IN_SPLITS)[:-1])

f32 = jnp.float32

ROW_TILE = 512
WIDE_OUTPUT = 2048
DENSE_VMEM_BYTES = 56 * 1024 * 1024


def _rms(x, g):
    return x * lax.rsqrt(jnp.mean(x * x, -1, keepdims=True) + EPS) * g


def _add_norm_body(x_ref, r_ref, g_ref, o_ref):
    o_ref[...] = _rms(x_ref[...] + r_ref[...], g_ref[...])


def add_rmsnorm(x, r, g):
    shape = x.shape
    d = shape[-1]
    x2, r2 = x.reshape(-1, d), r.reshape(-1, d)
    n = x2.shape[0]
    tm = min(n, ROW_TILE)
    rows = pl.BlockSpec((tm, d), lambda i: (i, 0))
    out = pl.pallas_call(
        _add_norm_body,
        grid=(n // tm,),
        in_specs=[rows, rows, pl.BlockSpec((1, d), lambda i: (0, 0))],
        out_specs=rows,
        out_shape=jax.ShapeDtypeStruct((n, d), f32),
        name="add_rmsnorm",
    )(x2, r2, g.reshape(1, d))
    return out.reshape(shape)


def _linear_body(*refs, pre_norm, has_bias, has_res, post_norm, splits):
    it = iter(refs)
    x_ref, w_ref = next(it), next(it)
    x = x_ref[...]
    if pre_norm:
        x = _rms(x, next(it)[...])
    y = jnp.dot(x.astype(jnp.bfloat16), w_ref[...], preferred_element_type=f32)
    if has_bias:
        y = y + next(it)[...]
    if has_res:
        y = y + next(it)[...]
    post_gain = next(it)[...] if post_norm else None
    off = 0
    for m in splits:
        next(it)[...] = y[:, off:off + m]
        off += m
    if post_norm:
        next(it)[...] = _rms(y, post_gain)


def fused_linear(x, w, *, pre_gain=None, bias=None, residual=None, post_gain=None, splits=None):
    n, k = x.shape
    m = w.shape[1]
    splits = (m,) if splits is None else tuple(splits)
    assert sum(splits) == m and (post_gain is None or len(splits) == 1)
    tm = min(n, ROW_TILE if m <= WIDE_OUTPUT else ROW_TILE // 2)
    assert n % tm == 0
    row = lambda c: pl.BlockSpec((tm, c), lambda i: (i, 0))
    const = lambda r, c: pl.BlockSpec((r, c), lambda i: (0, 0))
    args, specs = [x, w.astype(jnp.bfloat16)], [row(k), const(k, m)]
    if pre_gain is not None:
        args.append(pre_gain.reshape(1, k)); specs.append(const(1, k))
    if bias is not None:
        args.append(bias.reshape(1, m)); specs.append(const(1, m))
    if residual is not None:
        args.append(residual); specs.append(row(m))
    if post_gain is not None:
        args.append(post_gain.reshape(1, m)); specs.append(const(1, m))
    out_cols = splits + ((m,) if post_gain is not None else ())
    outs = pl.pallas_call(
        functools.partial(_linear_body, pre_norm=pre_gain is not None, has_bias=bias is not None,
                          has_res=residual is not None, post_norm=post_gain is not None, splits=splits),
        grid=(n // tm,),
        in_specs=specs,
        out_specs=[row(c) for c in out_cols],
        out_shape=[jax.ShapeDtypeStruct((n, c), f32) for c in out_cols],
        compiler_params=pltpu.CompilerParams(dimension_semantics=("arbitrary",),
                                             vmem_limit_bytes=DENSE_VMEM_BYTES),
        name="fused_linear",
    )(*args)
    return outs[0] if len(outs) == 1 else tuple(outs)


def _xattn_body(q_ref, kv_ref, o_ref):
    bf16 = jnp.bfloat16
    d = MEM_HEAD_DIM
    q = q_ref[0]
    kv = kv_ref[0].astype(bf16)
    for h in range(MEM_HEADS):
        k_h = kv[:, h * d:(h + 1) * d]
        v_h = kv[:, (MEM_HEADS + h) * d:(MEM_HEADS + h + 1) * d]
        s = lax.dot_general(q[:, h * d:(h + 1) * d].astype(bf16), k_h, (((1,), (1,)), ((), ())),
                            preferred_element_type=f32) * (d ** -0.5)
        e = jnp.exp(s - jnp.max(s, axis=-1, keepdims=True))
        p = e / jnp.sum(e, axis=-1, keepdims=True)
        o_ref[0, :, h * d:(h + 1) * d] = jnp.dot(p.astype(bf16), v_h, preferred_element_type=f32)


def cross_attention(q, mem_kv):
    B, T, w = q.shape
    M = mem_kv.shape[1]
    kv = mem_kv.reshape(B, M, 2 * w)
    tm = min(T, ROW_TILE)
    assert T % tm == 0 and tm % 8 == 0
    return pl.pallas_call(
        _xattn_body,
        grid=(B, T // tm),
        in_specs=[pl.BlockSpec((1, tm, w), lambda b, i: (b, i, 0)),
                  pl.BlockSpec((1, M, 2 * w), lambda b, i: (b, 0, 0))],
        out_specs=pl.BlockSpec((1, tm, w), lambda b, i: (b, i, 0)),
        out_shape=jax.ShapeDtypeStruct((B, T, w), f32),
        compiler_params=pltpu.CompilerParams(dimension_semantics=("arbitrary", "arbitrary")),
        name="cross_attention",
    )(q, kv)


def masked_softmax(s, mask):
    s = jnp.where(mask, s.astype(f32), -jnp.inf)
    mx = jnp.max(s, -1, keepdims=True)
    mx = jnp.where(jnp.isfinite(mx), mx, 0.0)
    e = jnp.exp(s - mx)
    return e / jnp.maximum(e.sum(-1, keepdims=True), TINY)


def _log_sigmoid(x):
    return jnp.minimum(x, 0.0) - jnp.log1p(jnp.exp(-jnp.abs(x)))


def _cumsum_lanes(x):
    lane = lax.broadcasted_iota(jnp.int32, x.shape, 1)
    shift = 1
    while shift < x.shape[1]:
        x = x + jnp.where(lane >= shift, pltpu.roll(x, shift, axis=1), 0.0)
        shift *= 2
    return x


def _mlstm_body(ml_ref, small_ref, bias_ref, gain_ref, c0_ref, n0_ref, m0_ref, h_ref, c_ref, n_ref, m_ref):
    H, d, L = ML_HEADS, ML_HEAD_DIM, ml_ref.shape[1]
    bf16 = jnp.bfloat16
    nt = (((1,), (1,)), ((), ()))
    tn = (((0,), (0,)), ((), ()))

    @pl.when(pl.program_id(1) == 0)
    def _():
        c_ref[...] = c0_ref[...]
        n_ref[...] = n0_ref[...]
        m_ref[...] = m0_ref[...]

    blk = ml_ref[0]
    small = small_ref[0] + bias_ref[...]
    small_t = small.T
    causal = (lax.broadcasted_iota(jnp.int32, (L, L), 0) >= lax.broadcasted_iota(jnp.int32, (L, L), 1))
    for h in range(H):
        q = blk[:, h * d:(h + 1) * d]
        k = blk[:, (H + h) * d:(H + h + 1) * d] * (d ** -0.5)
        v = blk[:, (2 * H + h) * d:(2 * H + h + 1) * d]
        o_pre = blk[:, (3 * H + h) * d:(3 * H + h + 1) * d]
        ig_row, ig_col = small_t[h:h + 1, :], small[:, h:h + 1]
        b_row = _cumsum_lanes(_log_sigmoid(small_t[H + h:H + h + 1, :]))
        b_col = jnp.broadcast_to(b_row, (L, L)).T
        b_t, b_last = b_col[:, 0:1], b_row[:, L - 1:L]
        c_prev, n_prev, m_prev = c_ref[0, h], n_ref[0, h], m_ref[0, h]
        log_d = jnp.where(causal, b_col - b_row + ig_row, -jnp.inf)
        inter = b_t + m_prev
        m_t = jnp.maximum(inter, jnp.max(log_d, axis=1, keepdims=True))
        a = jnp.exp(inter - m_t)
        qb, vb = q.astype(bf16), v.astype(bf16)
        qk = lax.dot_general(qb, k.astype(bf16), nt, preferred_element_type=f32) * jnp.exp(log_d - m_t)
        num = (a * jnp.dot(qb, c_prev.astype(bf16), preferred_element_type=f32)
               + jnp.dot(qk.astype(bf16), vb, preferred_element_type=f32))
        den = a * jnp.sum(q * n_prev, axis=1, keepdims=True) + jnp.sum(qk, axis=1, keepdims=True)
        hid = num / jnp.maximum(jnp.abs(den), jnp.exp(-m_t))
        m_new = m_t[L - 1:L, :]
        kw = k * jnp.exp(b_last - b_t + ig_col - m_new)
        decay = jnp.exp(b_last + m_prev - m_new)
        c_ref[0, h] = decay * c_prev + lax.dot_general(kw.astype(bf16), vb, tn, preferred_element_type=f32)
        n_ref[0, h] = decay * n_prev + jnp.sum(kw, axis=0, keepdims=True)
        m_ref[0, h] = m_new
        gated = jax.nn.sigmoid(o_pre) * hid
        h_ref[0, :, h * d:(h + 1) * d] = _rms(gated, gain_ref[:, h * d:(h + 1) * d])


def mlstm_pallas(ml, small, b_forget, head_gain, c0, n0, m0):
    B, T, _ = ml.shape
    H, d, L = ML_HEADS, ML_HEAD_DIM, ML_CHUNK
    assert T % L == 0
    bias = jnp.zeros((1, small.shape[-1]), f32).at[0, H:2 * H].set(b_forget)
    state = lambda *s: pl.BlockSpec((1,) + s, lambda b, i: (b,) + (0,) * len(s))
    h, c, n, m = pl.pallas_call(
        _mlstm_body,
        grid=(B, T // L),
        in_specs=[
            pl.BlockSpec((1, L, ml.shape[-1]), lambda b, i: (b, i, 0)),
            pl.BlockSpec((1, L, small.shape[-1]), lambda b, i: (b, i, 0)),
            pl.BlockSpec((1, small.shape[-1]), lambda b, i: (0, 0)),
            pl.BlockSpec((1, H * d), lambda b, i: (0, 0)),
            state(H, d, d), state(H, 1, d), state(H, 1, 1),
        ],
        out_specs=[pl.BlockSpec((1, L, H * d), lambda b, i: (b, i, 0)),
                   state(H, d, d), state(H, 1, d), state(H, 1, 1)],
        out_shape=[jax.ShapeDtypeStruct((B, T, H * d), f32), jax.ShapeDtypeStruct((B, H, d, d), f32),
                   jax.ShapeDtypeStruct((B, H, 1, d), f32), jax.ShapeDtypeStruct((B, H, 1, 1), f32)],
        compiler_params=pltpu.CompilerParams(dimension_semantics=("arbitrary", "arbitrary")),
        name="mlstm",
    )(ml, small, bias, head_gain.reshape(1, H * d), c0, n0.reshape(B, H, 1, d), m0.reshape(B, H, 1, 1))
    return h, c, n.reshape(B, H, d), m.reshape(B, H)


def compress_rows(seg, cmp_pe, cmp_w1, cmp_b1, cmp_w2):
    B, n_seg, width = seg.shape
    G, d, t = NSA_KV_HEADS, NSA_HEAD_DIM, CMP_STRIDE
    e = cmp_w1.shape[-1]
    w1 = cmp_w1.reshape(2, 2, t, d, e)
    eye_kv, eye_g = jnp.eye(2, dtype=f32), jnp.eye(G, dtype=f32)
    w_big = jnp.einsum('kptde,kK,gG->tkgdpKGe', w1, eye_kv, eye_g).reshape(width, 2 * 2 * G * e)
    ab = fused_linear(seg.reshape(B * n_seg, width), w_big).reshape(B, n_seg, 2, 2, G, e)
    const = jnp.einsum('kr,kre->ke', cmp_pe.reshape(2, -1), cmp_w1) + cmp_b1
    pre = ab[:, :-1, 0] + ab[:, 1:, 1] + const[:, None, :]
    out = jnp.einsum('bnkge,kef->kbgnf', jax.nn.gelu(pre), cmp_w2)
    return out[0], out[1]


def cmp_to_sel(imp, nsb):
    r = SEL_BLOCK // CMP_STRIDE
    nc = imp.shape[-1]
    lead = imp.shape[:-1]
    tot = r * nsb
    padw = [(0, 0)] * len(lead)
    first = jnp.pad(imp, padw + [(0, tot - nc)]).reshape(lead + (nsb, r)).sum(-1)
    second = jnp.pad(imp, padw + [(1, tot - nc - 1)]).reshape(lead + (nsb, r)).sum(-1)
    return 0.5 * (first + second)


NSA_SEL_CHUNK = 512
NSA_WIN_CHUNKS = WINDOW // Q_BLOCK + 1
NSA_KV_LANES = NSA_KV_HEADS * NSA_HEAD_DIM


def _softmax_masked(s, mask):
    sm = jnp.where(mask, s, -jnp.inf)
    mx = jnp.max(sm, axis=-1, keepdims=True)
    mx = jnp.where(mx == -jnp.inf, 0.0, mx)
    e = jnp.exp(sm - mx)
    return e * (1.0 / jnp.maximum(jnp.sum(e, axis=-1, keepdims=True), TINY))


def _top_rows_mask(s, row_id, k):
    sel = jnp.zeros(s.shape, f32)
    sentinel = s.shape[0]
    for _ in range(k):
        m = jnp.max(s, axis=0, keepdims=True)
        first = jnp.min(jnp.where(s == m, row_id, sentinel), axis=0, keepdims=True)
        hit = row_id == first
        sel = jnp.where(hit, 1.0, sel)
        s = jnp.where(hit, -jnp.inf, s)
    return sel


def _nsa_prompt_body(q_ref, gate_ref, kct_ref, vc_ref, mt_ref, kts_ref, vs_ref, ktw_ref, vw_ref, o_ref):
    j = pl.program_id(1)
    Q, R, d = Q_BLOCK, NSA_GROUP, NSA_HEAD_DIM
    n_cmp = kct_ref.shape[-1]
    n_blk = mt_ref.shape[0]
    bf16 = jnp.bfloat16
    q_all = q_ref[0] * (d ** -0.5)
    gates = gate_ref[0]
    q_pos = j * Q + lax.broadcasted_iota(jnp.int32, (Q, 1), 0)
    q_pos_l = j * Q + lax.broadcasted_iota(jnp.int32, (1, Q), 1)
    blk_r = lax.broadcasted_iota(jnp.int32, (n_blk, 1), 0)
    blk_id = lax.broadcasted_iota(jnp.int32, (n_blk, Q), 0)
    nt = (((1,), (1,)), ((), ()))

    for g in range(NSA_KV_HEADS):
        lanes = slice(g * d, (g + 1) * d)
        qg = jnp.concatenate([q_all[:, (g * R + r) * d:(g * R + r + 1) * d] for r in range(R)], axis=0).astype(bf16)

        s = jnp.dot(qg, kct_ref[0, g], preferred_element_type=f32).reshape(R, Q, n_cmp)
        n_id = lax.broadcasted_iota(jnp.int32, (1, n_cmp), 1)
        cmask = (n_id * CMP_STRIDE + (CMP_BLOCK - 1) <= q_pos) & (n_id < n_cmp - 1)
        p = _softmax_masked(s, cmask[None])
        o_c = jnp.dot(p.reshape(R * Q, n_cmp).astype(bf16), vc_ref[0], preferred_element_type=f32)[:, lanes]

        p_sum = p[0] + p[1] + p[2] + p[3]
        hi = p_sum.astype(bf16)
        lo = (p_sum - hi.astype(f32)).astype(bf16)
        mt = mt_ref[...]
        imp_t = (lax.dot_general(mt, hi, nt, preferred_element_type=f32)
                 + lax.dot_general(mt, lo, nt, preferred_element_type=f32))
        valid = blk_r * SEL_BLOCK <= q_pos_l
        forced = (blk_r == 0) | (blk_r == jnp.right_shift(q_pos_l, 6))
        score = jnp.where(forced, FORCED_SCORE, jnp.where(valid, imp_t, INVALID_SCORE))
        sel = _top_rows_mask(score, blk_id, min(N_SEL, n_blk)).T.astype(bf16)

        kc = kts_ref.shape[-1]

        def chunk(c, carry):
            m, l, acc = carry
            kt = kts_ref[0, c, g * d:(g + 1) * d, :]
            sc = jnp.dot(qg, kt, preferred_element_type=f32).reshape(R, Q, kc)
            key = c * kc + lax.broadcasted_iota(jnp.int32, (1, kc), 1)
            expand = jnp.where(blk_r == jnp.right_shift(key, 6), 1.0, 0.0).astype(bf16)
            picked = jnp.dot(sel, expand, preferred_element_type=f32) > 0.5
            mask = picked & (key <= q_pos)
            sm = jnp.where(mask[None], sc, -jnp.inf)
            m_new = jnp.maximum(m, jnp.max(sm, axis=-1, keepdims=True))
            m_safe = jnp.where(m_new == -jnp.inf, 0.0, m_new)
            alpha = jnp.exp(m - m_safe)
            pe = jnp.exp(sm - m_safe)
            l = l * alpha + jnp.sum(pe, axis=-1, keepdims=True)
            pv = jnp.dot(pe.reshape(R * Q, kc).astype(bf16), vs_ref[0, c], preferred_element_type=f32)
            acc = acc * alpha.reshape(R * Q, 1) + pv
            return m_new, l, acc

        n_chunks = (j * Q + Q + kc - 1) // kc
        init = (jnp.full((R, Q, 1), -jnp.inf, f32), jnp.zeros((R, Q, 1), f32),
                jnp.zeros((R * Q, NSA_KV_LANES), f32))
        _, l_s, acc_s = lax.fori_loop(0, n_chunks, chunk, init)
        o_s = acc_s[:, lanes] * (1.0 / jnp.maximum(l_s.reshape(R * Q, 1), TINY))

        ss, vv = [], []
        for i in range(NSA_WIN_CHUNKS):
            cc = jnp.maximum(j - (NSA_WIN_CHUNKS - 1) + i, 0)
            ss.append(jnp.dot(qg, ktw_ref[0, cc, g * d:(g + 1) * d, :], preferred_element_type=f32))
            vv.append(vw_ref[0, cc])
        span = NSA_WIN_CHUNKS * Q
        sw = jnp.concatenate(ss, axis=1).reshape(R, Q, span)
        k_pos = (j - (NSA_WIN_CHUNKS - 1)) * Q + lax.broadcasted_iota(jnp.int32, (1, span), 1)
        wmask = (k_pos >= 0) & (k_pos <= q_pos) & (k_pos >= q_pos - WINDOW)
        pw = _softmax_masked(sw, wmask[None])
        o_w = jnp.dot(pw.reshape(R * Q, span).astype(bf16), jnp.concatenate(vv, axis=0),
                      preferred_element_type=f32)[:, lanes]

        for r in range(R):
            h = g * R + r
            rows = slice(r * Q, (r + 1) * Q)
            o_ref[0, :, h * d:(h + 1) * d] = (gates[:, 3 * h:3 * h + 1] * o_c[rows]
                                              + gates[:, 3 * h + 1:3 * h + 2] * o_s[rows]
                                              + gates[:, 3 * h + 2:3 * h + 3] * o_w[rows])


def _cmp_to_sel_matrix(n_cmp, n_blk):
    r = SEL_BLOCK // CMP_STRIDE
    n = np.arange(n_cmp)
    b = np.arange(n_blk)[:, None]
    m = 0.5 * ((n // r == b).astype(np.float32) + ((n + 1) // r == b).astype(np.float32))
    m[:, n_cmp - 1] = 0.0
    return jnp.asarray(m, jnp.bfloat16)


def nsa_prompt(nq, gates, nkv, kc, vc):
    B, T, _ = nq.shape
    bf16 = jnp.bfloat16
    G, d, w = NSA_KV_HEADS, NSA_HEAD_DIM, NSA_KV_LANES
    n_cmp, n_blk = T // CMP_STRIDE, T // SEL_BLOCK
    kc_s, kc_w = NSA_SEL_CHUNK, Q_BLOCK
    pad = ((0, 0), (0, 0), (0, 1), (0, 0))
    kct = jnp.pad(kc, pad).transpose(0, 1, 3, 2).astype(bf16)
    vc2 = jnp.pad(vc, pad).transpose(0, 2, 1, 3).reshape(B, n_cmp, w).astype(bf16)
    chunks = lambda a, c: a.reshape(B, T // c, c, w).astype(bf16)
    kts = chunks(nkv[..., 2 * w:3 * w], kc_s).transpose(0, 1, 3, 2)
    vs = chunks(nkv[..., 3 * w:4 * w], kc_s)
    ktw = chunks(nkv[..., 4 * w:5 * w], kc_w).transpose(0, 1, 3, 2)
    vw = chunks(nkv[..., 5 * w:6 * w], kc_w)
    whole = lambda a: pl.BlockSpec((1,) + a.shape[1:], lambda b, j: (b,) + (0,) * (a.ndim - 1))
    mt = _cmp_to_sel_matrix(n_cmp, n_blk)
    return pl.pallas_call(
        _nsa_prompt_body,
        grid=(B, T // Q_BLOCK),
        in_specs=[
            pl.BlockSpec((1, Q_BLOCK, nq.shape[-1]), lambda b, j: (b, j, 0)),
            pl.BlockSpec((1, Q_BLOCK, gates.shape[-1]), lambda b, j: (b, j, 0)),
            whole(kct), whole(vc2),
            pl.BlockSpec(mt.shape, lambda b, j: (0, 0)),
            whole(kts), whole(vs), whole(ktw), whole(vw),
        ],
        out_specs=pl.BlockSpec((1, Q_BLOCK, nq.shape[-1]), lambda b, j: (b, j, 0)),
        out_shape=jax.ShapeDtypeStruct(nq.shape, f32),
        compiler_params=pltpu.CompilerParams(dimension_semantics=("arbitrary", "arbitrary"),
                                             vmem_limit_bytes=48 * 1024 * 1024),
        name="nsa_prompt",
    )(nq, gates, kct, vc2, mt, kts, vs, ktw, vw)


def nsa_decode(q, gates, past_len, pool_cmp, pool_slc, page_table, new_cmp, new_slc, win_rows, win_pos0,
               cmp_pe, cmp_w1, cmp_b1, cmp_w2):
    B, S, _ = q.shape
    G, R, d = NSA_KV_HEADS, NSA_GROUP, NSA_HEAD_DIM
    T = past_len + S
    scale = d ** -0.5
    q_pos = past_len + jnp.arange(S)
    qg = q.reshape(B, S, G, R, d).transpose(0, 2, 3, 1, 4)
    gg = gates.reshape(B, S, G, R, 3).transpose(0, 2, 3, 1, 4)

    n_seg = T // CMP_STRIDE
    rows = pool_cmp[page_table].reshape(B, past_len, 2, G, d)
    if n_seg * CMP_STRIDE > past_len:
        rows = jnp.concatenate([rows, new_cmp], axis=1)
    seg = rows[:, :n_seg * CMP_STRIDE].reshape(B, n_seg, CMP_STRIDE, 2, G, d)
    half = CMP_STRIDE * d

    def compress(kv):
        w1 = cmp_w1[kv]
        w_lo, w_hi = w1[:half].reshape(CMP_STRIDE, d, -1), w1[half:].reshape(CMP_STRIDE, d, -1)
        x = seg[:, :, :, kv]
        pre = (jnp.einsum('bntgd,tde->bgne', x[:, :-1], w_lo) + jnp.einsum('bntgd,tde->bgne', x[:, 1:], w_hi)
               + (cmp_pe[kv].reshape(-1) @ w1 + cmp_b1[kv]))
        return jax.nn.gelu(pre) @ cmp_w2[kv]

    kc, vc = compress(0), compress(1)
    c_end = jnp.arange(n_seg - 1) * CMP_STRIDE + (CMP_BLOCK - 1)
    p_c = masked_softmax(jnp.einsum('bgrqd,bgnd->bgrqn', qg, kc) * scale, c_end[None, :] <= q_pos[:, None])
    o_c = jnp.einsum('bgrqn,bgnd->bgrqd', p_c, vc)

    nsb = -(-T // SEL_BLOCK)
    n_past = past_len // SEL_BLOCK
    assert past_len % SEL_BLOCK == 0 and PAGE_SIZE % SEL_BLOCK == 0 and nsb - n_past <= 1
    imp = cmp_to_sel(p_c.sum(2), nsb)
    blk = jnp.arange(nsb)
    valid = blk[None, :] * SEL_BLOCK <= q_pos[:, None]
    forced = (blk[None, :] == 0) | (blk[None, :] == q_pos[:, None] // SEL_BLOCK)
    score = jnp.where(forced, FORCED_SCORE, jnp.where(valid, imp, INVALID_SCORE))
    _, idx = lax.top_k(score, min(N_SEL, nsb))
    per_page = PAGE_SIZE // SEL_BLOCK
    past = jnp.minimum(idx, n_past - 1)
    bi = jnp.arange(B)[:, None, None, None]
    gi = jnp.arange(G)[None, :, None, None]
    pages = pool_slc[page_table[bi, past // per_page]]
    pages = pages.reshape(pages.shape[:4] + (per_page, SEL_BLOCK) + pages.shape[5:])
    which = (past % per_page)[..., None, None, None, None]
    both = pages[:, :, :, :, 0]
    for h in range(1, per_page):
        both = jnp.where(which == h, pages[:, :, :, :, h], both)
    blocks = both[..., 0, :]
    for g in range(1, G):
        blocks = jnp.where(gi[..., None, None, None] == g, both[..., g, :], blocks)
    tail = jnp.pad(new_slc, ((0, 0), (0, SEL_BLOCK - S), (0, 0), (0, 0), (0, 0)))
    tail = tail.transpose(0, 3, 1, 2, 4)[:, :, None, None]
    blocks = jnp.where((idx >= n_past)[..., None, None, None], tail, blocks)
    kg = blocks[..., 0, :].reshape(B, G, S, -1, d)
    vg = blocks[..., 1, :].reshape(B, G, S, -1, d)
    k_pos = (idx[..., None] * SEL_BLOCK + jnp.arange(SEL_BLOCK)).reshape(B, G, S, -1)
    p_s = masked_softmax(jnp.einsum('bgrqd,bgqsd->bgrqs', qg, kg) * scale,
                         (k_pos <= q_pos[None, None, :, None])[:, :, None])
    o_s = jnp.einsum('bgrqs,bgqsd->bgrqd', p_s, vg)

    kw = jnp.swapaxes(win_rows[:, :, 0], 1, 2)
    vw = jnp.swapaxes(win_rows[:, :, 1], 1, 2)
    w_pos = win_pos0 + jnp.arange(kw.shape[2])
    wmask = ((w_pos[None, :] <= q_pos[:, None]) & (w_pos[None, :] >= q_pos[:, None] - WINDOW)
             & (w_pos[None, :] >= 0))
    p_w = masked_softmax(jnp.einsum('bgrqd,bgkd->bgrqk', qg, kw) * scale, wmask)
    o_w = jnp.einsum('bgrqk,bgkd->bgrqd', p_w, vw)
    o = gg[..., 0:1] * o_c + gg[..., 1:2] * o_s + gg[..., 2:3] * o_w
    return o.transpose(0, 3, 1, 2, 4).reshape(B, S, NSA_HEADS * d)


PEER_PICKS = PEER_HEADS * PEER_TOPK
PEER_TOKENS_PER_STEP = 64
PEER_GROUP = 4
PEER_FETCH_AHEAD = 2
PEER_ROW_BUFFERS = 4 * PEER_GROUP
PEER_SLAB_LANES = 128
PEER_SLAB_ROWS = 2 * D_MODEL // PEER_SLAB_LANES


def _gelu_tanh(x):
    return 0.5 * x * (1.0 + jnp.tanh(0.7978845608028654 * (x + 0.044715 * x * x * x)))


def _peer_expert_body(ids_ref, ids_next_ref, x_ref, g_ref, seg_ref, uv_ref, o_ref, rows, sems):
    tokens = x_ref.shape[0]
    depth = PEER_ROW_BUFFERS
    half, lanes = PEER_SLAB_ROWS // 2, PEER_SLAB_LANES
    cols = PEER_PICKS * half
    seg = seg_ref.shape[0]
    nt = (((1,), (1,)), ((), ()))
    bf16 = jnp.bfloat16

    col_row = lax.broadcasted_iota(jnp.int32, (half, cols), 1) & (half - 1)
    sub = lax.broadcasted_iota(jnp.int32, (half, cols), 0)
    diag = jnp.where(col_row == sub, 1.0, 0.0)

    group = PEER_GROUP
    groups = tokens // group
    n_seg = cols // seg
    step_id = pl.program_id(0)
    last_step = pl.num_programs(0) - 1

    def wait(slot):
        pltpu.make_async_copy(uv_ref.at[pl.ds(0, PEER_PICKS)], rows.at[slot], sems.at[slot]).wait()

    def fetcher(ids, t0, slot0):
        per = PEER_PICKS // 2

        def fetch(c):
            j, h = divmod(c, 2)
            for k in range(h * per, (h + 1) * per):
                pltpu.make_async_copy(uv_ref.at[ids[t0 + j, k]], rows.at[slot0 + j, k],
                                      sems.at[slot0 + j]).start(priority=k % 2)
        return fetch

    def mix_group(t0, slot0, fetch):
        parts = []
        for j in range(group):
            u_rows = rows[slot0 + j, :, :half, :].reshape(cols, lanes).astype(bf16)
            prod = lax.dot_general(x_ref[t0 + j].astype(bf16), u_rows, nt, preferred_element_type=f32)
            part = jnp.sum(prod * diag, axis=0, keepdims=True)
            parts += [part[:, i * seg:(i + 1) * seg] for i in range(n_seg)]
            fetch(j)
        part = jnp.concatenate(parts, axis=0)
        hi = part.astype(bf16)
        lo = (part - hi.astype(f32)).astype(bf16)
        ones = seg_ref[...]
        act = jnp.dot(hi, ones, preferred_element_type=f32) + jnp.dot(lo, ones, preferred_element_type=f32)
        for j in range(group):
            w = g_ref[t0 + j] * _gelu_tanh(act[j * n_seg:(j + 1) * n_seg])
            w = jnp.concatenate([jnp.broadcast_to(w[i:i + 1, :], (half, seg)) for i in range(n_seg)], axis=1)
            v_rows = rows[slot0 + j, :, half:, :].reshape(cols, lanes).astype(bf16)
            o_ref[t0 + j] = jnp.dot((w * diag).astype(bf16), v_rows, preferred_element_type=f32)
            fetch(group + j)

    sets, ahead = depth // group, PEER_FETCH_AHEAD

    @pl.when(step_id == 0)
    def _():
        for a in range(ahead):
            first = fetcher(ids_ref, a * group, a * group)
            for c in range(2 * group):
                first(c)

    def sweep(it, last):
        for q in range(sets):
            g = sets * it + q
            for j in range(group):
                wait(q * group + j)
            into = ((q + ahead) % sets) * group
            if last and q + ahead >= sets:
                fetch = fetcher(ids_next_ref, (q + ahead - sets) * group, into)
            else:
                fetch = fetcher(ids_ref, (g + ahead) * group, into)
            mix_group(g * group, q * group, fetch)

    def body(it, carry):
        sweep(it, False)
        return carry

    lax.fori_loop(0, groups // sets - 1, body, 0)
    sweep(groups // sets - 1, True)

    @pl.when(step_id == last_step)
    def _():
        for j in range(ahead * group):
            wait(j)


def peer_experts(xn, ids, gates, uv):
    n, d = xn.shape
    tb = PEER_TOKENS_PER_STEP
    slab, half, lanes = PEER_SLAB_ROWS, PEER_SLAB_ROWS // 2, PEER_SLAB_LANES
    seg = 2 * lanes
    cols = PEER_PICKS * half
    assert n % tb == 0 and tb % PEER_ROW_BUFFERS == 0 and half * lanes == d and cols % seg == 0
    assert 0 < PEER_FETCH_AHEAD < PEER_ROW_BUFFERS // PEER_GROUP
    same_pick = np.arange(seg)[:, None] // half == np.arange(seg)[None, :] // half
    out = pl.pallas_call(
        _peer_expert_body,
        grid=(n // tb,),
        in_specs=[
            pl.BlockSpec((tb, PEER_PICKS), lambda i: (i, 0), memory_space=pltpu.SMEM),
            pl.BlockSpec((tb, PEER_PICKS), lambda i: (jnp.minimum(i + 1, n // tb - 1), 0), memory_space=pltpu.SMEM),
            pl.BlockSpec((tb, half, lanes), lambda i: (i, 0, 0)),
            pl.BlockSpec((tb, cols // seg, seg), lambda i: (i, 0, 0)),
            pl.BlockSpec((seg, seg), lambda i: (0, 0)),
            pl.BlockSpec(memory_space=pl.ANY),
        ],
        out_specs=pl.BlockSpec((tb, half, lanes), lambda i: (i, 0, 0)),
        out_shape=jax.ShapeDtypeStruct((n, half, lanes), f32),
        scratch_shapes=[
            pltpu.VMEM((PEER_ROW_BUFFERS, PEER_PICKS, slab, lanes), f32),
            pltpu.SemaphoreType.DMA((PEER_ROW_BUFFERS,)),
        ],
        compiler_params=pltpu.CompilerParams(dimension_semantics=("arbitrary",)),
        name="peer_experts",
    )(ids, ids, xn.reshape(n, half, lanes), jnp.repeat(gates, half, axis=1).reshape(n, cols // seg, seg),
      jnp.asarray(same_pick, jnp.bfloat16), uv)
    return out.reshape(n, d)


PEER_ROUTE_TOKENS = 256
PEER_HALF_DIM = PEER_QUERY_DIM // 2


def _top_rows(s, row_id, k, payload=None):
    vals, picks = [], []
    sentinel = s.shape[0]
    for _ in range(k):
        m = jnp.max(s, axis=0, keepdims=True)
        first = jnp.min(jnp.where(s == m, row_id, sentinel), axis=0, keepdims=True)
        hit = row_id == first
        vals.append(m)
        if payload is None:
            picks.append(first)
        else:
            picks.append(jnp.max(jnp.where(hit, payload, -1), axis=0, keepdims=True))
        s = jnp.where(hit, -jnp.inf, s)
    return jnp.concatenate(vals, 0), jnp.concatenate(picks, 0)


def _peer_route_body(x_ref, wq_ref, sk_ref, ids_ref, gate_ref, q_scr):
    tn = x_ref.shape[0]
    q = jnp.dot(x_ref[...].astype(jnp.bfloat16), wq_ref[...], preferred_element_type=f32)
    for j in range(2 * PEER_HEADS):
        q_scr[j] = q[:, j * PEER_HALF_DIM:(j + 1) * PEER_HALF_DIM].astype(jnp.bfloat16)
    key_id = lax.broadcasted_iota(jnp.int32, (PEER_N_KEYS, tn), 0)
    n_cand = -(-sum(PEER_TOPK // (a + 1) for a in range(PEER_TOPK)) // 8) * 8
    cand_id = lax.broadcasted_iota(jnp.int32, (n_cand, tn), 0)

    def head(h, carry):
        tops = []
        for p in range(2):
            s = lax.dot_general(sk_ref[2 * h + p], q_scr[2 * h + p], (((1,), (1,)), ((), ())),
                                preferred_element_type=f32)
            tops.append(_top_rows(s, key_id, PEER_TOPK))
        (v0, i0), (v1, i1) = tops
        cs, ce = [], []
        for a in range(PEER_TOPK):
            nb = PEER_TOPK // (a + 1)
            cs.append(v0[a:a + 1] + v1[:nb])
            ce.append(i0[a:a + 1] * PEER_N_KEYS + i1[:nb])
        pad = n_cand - sum(c.shape[0] for c in cs)
        cand_s = jnp.concatenate(cs + [jnp.full((pad, tn), -jnp.inf, f32)], axis=0)
        cand_e = jnp.concatenate(ce + [jnp.zeros((pad, tn), jnp.int32)], axis=0)
        top_s, top_e = _top_rows(cand_s, cand_id, PEER_TOPK, payload=cand_e)
        e = jnp.exp(top_s - top_s[0:1])
        ids_ref[h] = top_e
        gate_ref[h] = e / jnp.sum(e, axis=0, keepdims=True)
        return carry

    lax.fori_loop(0, PEER_HEADS, head, 0)


def peer_route(xn, wq, sub_keys):
    n, d = xn.shape
    tn = PEER_ROUTE_TOKENS if n % PEER_ROUTE_TOKENS == 0 else PEER_ROUTE_TOKENS // 2
    assert n % tn == 0
    n_q = 2 * PEER_HEADS * PEER_HALF_DIM
    sk = sub_keys.reshape(2 * PEER_HEADS, PEER_N_KEYS, PEER_HALF_DIM).astype(jnp.bfloat16)
    ids_t, gates_t = pl.pallas_call(
        _peer_route_body,
        grid=(n // tn,),
        in_specs=[
            pl.BlockSpec((tn, d), lambda i: (i, 0)),
            pl.BlockSpec((d, n_q), lambda i: (0, 0)),
            pl.BlockSpec((2 * PEER_HEADS, PEER_N_KEYS, PEER_HALF_DIM), lambda i: (0, 0, 0)),
        ],
        out_specs=[
            pl.BlockSpec((PEER_HEADS, PEER_TOPK, tn), lambda i: (0, 0, i)),
            pl.BlockSpec((PEER_HEADS, PEER_TOPK, tn), lambda i: (0, 0, i)),
        ],
        out_shape=[
            jax.ShapeDtypeStruct((PEER_HEADS, PEER_TOPK, n), jnp.int32),
            jax.ShapeDtypeStruct((PEER_HEADS, PEER_TOPK, n), f32),
        ],
        scratch_shapes=[pltpu.VMEM((2 * PEER_HEADS, tn, PEER_HALF_DIM), jnp.bfloat16)],
        compiler_params=pltpu.CompilerParams(dimension_semantics=("arbitrary",)),
        name="peer_route",
    )(xn, wq.astype(jnp.bfloat16), sk)
    to_rows = lambda a: a.reshape(PEER_PICKS, n).T
    return to_rows(ids_t), to_rows(gates_t)


def kernel(x_prompt, x_sample, cache_cmp_kv, cache_slc_kv, cache_win_kv, state_mlstm_c, state_mlstm_n,
           state_mlstm_m, cache_mem_kv, page_table, mem_prompt, g_mix, w_in, b_in, b_forget, ml_head_gain,
           cmp_pe, cmp_w1, cmp_b1, cmp_w2, w_out, g_xattn, g_mem, w_xq, w_xk, w_xv, w_xo, g_ffn, peer_wq,
           peer_sub_keys, peer_u, peer_v, g_final):
    B, T = x_prompt.shape[:2]
    DB, S = x_sample.shape[:2]
    past_len = page_table.shape[1] * PAGE_SIZE
    l = 0

    o_i, o_nq, o_nkv, o_ng = IN_OFFSETS[3], IN_OFFSETS[5], IN_OFFSETS[6], IN_OFFSETS[7]
    n_small = 2 * ML_HEADS + 3 * NSA_HEADS
    small_pad = 128 - n_small
    regroup = lambda a: jnp.concatenate(
        [a[..., :o_i], a[..., o_nq:o_nkv], a[..., o_nkv:o_ng], a[..., o_i:o_nq], a[..., o_ng:],
         jnp.zeros(a.shape[:-1] + (small_pad,), a.dtype)], axis=-1)
    w_in_g, b_in_g = regroup(w_in[l]), regroup(b_in[l])
    in_splits = (4 * ML_WIDTH, NSA_WIDTH, 6 * NSA_KV_LANES, n_small + small_pad)

    def project(x):
        b, t, _ = x.shape
        ml, nq, nkv, small = fused_linear(x.reshape(b * t, D_MODEL), w_in_g, pre_gain=g_mix[l], bias=b_in_g,
                                          splits=in_splits)
        ng = jax.nn.sigmoid(small[:, 2 * ML_HEADS:n_small]).reshape(b, t, 3 * NSA_HEADS)
        return (ml.reshape(b, t, 4 * ML_WIDTH), small.reshape(b, t, -1), nq.reshape(b, t, NSA_WIDTH),
                nkv.reshape(b, t, 6 * NSA_KV_LANES), ng)

    def after_mixers(x, h_ml, h_nsa, mem_kv):
        b, t, _ = x.shape
        h = jnp.concatenate([h_ml, h_nsa], -1).reshape(b * t, D_MODEL)
        x1, xn = fused_linear(h, w_out[l], residual=x.reshape(b * t, D_MODEL), post_gain=g_xattn[l])
        q = fused_linear(xn, w_xq[l]).reshape(b, t, D_MODEL)
        t_pad = -t % 8
        o = cross_attention(jnp.pad(q, ((0, 0), (0, t_pad), (0, 0))), mem_kv)[:, :t]
        return fused_linear(o.reshape(b * t, D_MODEL), w_xo[l], residual=x1, post_gain=g_ffn[l])

    ml, small, nq, nkv, ng = project(x_prompt)
    h_ml, p_c, p_n, p_m = mlstm_pallas(
        ml, small, b_forget[l], ml_head_gain[l],
        jnp.zeros((B, ML_HEADS, ML_HEAD_DIM, ML_HEAD_DIM), f32),
        jnp.zeros((B, ML_HEADS, ML_HEAD_DIM), f32),
        jnp.full((B, ML_HEADS), -jnp.inf, f32))
    rows6 = nkv.reshape(B, T, 6, NSA_KV_HEADS, NSA_HEAD_DIM)
    p_cmp, p_slc, win_rows = rows6[:, :, 0:2], rows6[:, :, 2:4], rows6[:, :, 4:6]
    seg_w = CMP_STRIDE * 2 * NSA_KV_LANES
    kc, vc = compress_rows(nkv[..., :2 * NSA_KV_LANES].reshape(B, T // CMP_STRIDE, seg_w),
                           cmp_pe[l], cmp_w1[l], cmp_b1[l], cmp_w2[l])
    h_nsa = nsa_prompt(nq, ng, nkv, kc, vc)
    p_mem = fused_linear(mem_prompt.reshape(-1, D_MODEL), jnp.concatenate([w_xk[l], w_xv[l]], axis=1),
                         pre_gain=g_mem[l]).reshape(B, MEM_LEN, 2, MEM_HEADS, MEM_HEAD_DIM)
    xp, xp_ffn_in = after_mixers(x_prompt, h_ml, h_nsa, p_mem)
    p_win = win_rows[:, T - min(WINDOW, T):]

    ml, small, nq, nkv, ng = project(x_sample)
    t_pad = ML_CHUNK - S
    ml_p = jnp.pad(ml, ((0, 0), (0, t_pad), (0, 0)))
    small_p = jnp.pad(small, ((0, 0), (0, t_pad), (0, 0)))
    small_p = small_p.at[:, S:, :ML_HEADS].set(-jnp.inf).at[:, S:, ML_HEADS:2 * ML_HEADS].set(jnp.inf)
    h_ml, s_c, s_n, s_m = mlstm_pallas(ml_p, small_p, b_forget[l], ml_head_gain[l],
                                       state_mlstm_c[l], state_mlstm_n[l], state_mlstm_m[l])
    h_ml = h_ml[:, :S]
    rows6 = nkv.reshape(DB, S, 6, NSA_KV_HEADS, NSA_HEAD_DIM)
    s_cmp, s_slc, win_rows = rows6[:, :, 0:2], rows6[:, :, 2:4], rows6[:, :, 4:6]
    win_buf = cache_win_kv[l]
    win_ext = jnp.concatenate([win_buf, win_rows], axis=1)
    h_nsa = nsa_decode(nq, ng, past_len, cache_cmp_kv[l], cache_slc_kv[l], page_table, s_cmp, s_slc,
                       win_ext, past_len - win_buf.shape[1], cmp_pe[l], cmp_w1[l], cmp_b1[l], cmp_w2[l])
    xs, xs_ffn_in = after_mixers(x_sample, h_ml, h_nsa, cache_mem_kv[l])
    w_keep = min(WINDOW, past_len + S)
    s_win = win_ext[:, win_ext.shape[1] - w_keep:]

    n_p = B * T
    uv = jnp.concatenate([peer_u[l], peer_v[l]], axis=1).reshape(-1, PEER_SLAB_ROWS, PEER_SLAB_LANES)
    routes = [peer_route(x, peer_wq[l], peer_sub_keys[l]) for x in (xp_ffn_in, xs_ffn_in)]
    ids, gates = (jnp.concatenate(a, 0) for a in zip(*routes))
    ffn = peer_experts(jnp.concatenate([xp_ffn_in, xs_ffn_in], 0), ids, gates, uv)
    y_prompt = add_rmsnorm(xp, ffn[:n_p], g_final).reshape(x_prompt.shape)
    y_sample = add_rmsnorm(xs, ffn[n_p:], g_final).reshape(x_sample.shape)
    st = lambda a: a[None]
    return (y_prompt, y_sample,
            st(p_cmp), st(p_slc), st(p_win), st(p_c), st(p_n), st(p_m), st(p_mem),
            st(s_cmp), st(s_slc), st(s_win), st(s_c), st(s_n), st(s_m))
```

```python
import functools

import jax
import jax.numpy as jnp
from jax import lax
import numpy as np
from jax.experimental import pallas as pl
from jax.experimental.pallas import tpu as pltpu

D_MODEL = 1024
PAGE_SIZE = 128

ML_WIDTH = D_MODEL // 2
ML_HEADS = 4
ML_HEAD_DIM = ML_WIDTH // ML_HEADS
ML_CHUNK = 128
NSA_WIDTH = D_MODEL - ML_WIDTH
NSA_HEADS = 8
NSA_HEAD_DIM = NSA_WIDTH // NSA_HEADS
NSA_KV_HEADS = 2
NSA_GROUP = NSA_HEADS // NSA_KV_HEADS
CMP_BLOCK = 32
CMP_STRIDE = 16
SEL_BLOCK = 64
N_SEL = 16
WINDOW = 512
Q_BLOCK = 128
FORCED_SCORE = 1.0e4
INVALID_SCORE = -1.0
MEM_LEN = 256
MEM_HEADS = 4
MEM_HEAD_DIM = D_MODEL // MEM_HEADS
PEER_HEADS = 8
PEER_N_KEYS = 128
PEER_TOPK = 16
PEER_QUERY_DIM = 256
EPS = 1e-6
TINY = 1e-30
IN_SPLITS = (ML_WIDTH, ML_WIDTH, ML_WIDTH, ML_WIDTH, ML_HEADS, ML_HEADS,
             NSA_WIDTH, 6 * NSA_KV_HEADS * NSA_HEAD_DIM, 3 * NSA_HEADS)
IN_OFFSETS = tuple(int(o) for o in np.cumsum(IN_SPLITS)[:-1])

f32 = jnp.float32

ROW_TILE = 512
WIDE_OUTPUT = 2048
DENSE_VMEM_BYTES = 56 * 1024 * 1024


def _rms(x, g):
    return x * lax.rsqrt(jnp.mean(x * x, -1, keepdims=True) + EPS) * g


def _add_norm_body(x_ref, r_ref, g_ref, o_ref):
    o_ref[...] = _rms(x_ref[...] + r_ref[...], g_ref[...])


def add_rmsnorm(x, r, g):
    shape = x.shape
    d = shape[-1]
    x2, r2 = x.reshape(-1, d), r.reshape(-1, d)
    n = x2.shape[0]
    tm = min(n, ROW_TILE)
    rows = pl.BlockSpec((tm, d), lambda i: (i, 0))
    out = pl.pallas_call(
        _add_norm_body,
        grid=(n // tm,),
        in_specs=[rows, rows, pl.BlockSpec((1, d), lambda i: (0, 0))],
        out_specs=rows,
        out_shape=jax.ShapeDtypeStruct((n, d), f32),
        name="add_rmsnorm",
    )(x2, r2, g.reshape(1, d))
    return out.reshape(shape)


def _linear_body(*refs, pre_norm, has_bias, has_res, post_norm, splits):
    it = iter(refs)
    x_ref, w_ref = next(it), next(it)
    x = x_ref[...]
    if pre_norm:
        x = _rms(x, next(it)[...])
    y = jnp.dot(x.astype(jnp.bfloat16), w_ref[...], preferred_element_type=f32)
    if has_bias:
        y = y + next(it)[...]
    if has_res:
        y = y + next(it)[...]
    post_gain = next(it)[...] if post_norm else None
    off = 0
    for m in splits:
        next(it)[...] = y[:, off:off + m]
        off += m
    if post_norm:
        next(it)[...] = _rms(y, post_gain)


def fused_linear(x, w, *, pre_gain=None, bias=None, residual=None, post_gain=None, splits=None):
    n, k = x.shape
    m = w.shape[1]
    splits = (m,) if splits is None else tuple(splits)
    assert sum(splits) == m and (post_gain is None or len(splits) == 1)
    tm = min(n, ROW_TILE if m <= WIDE_OUTPUT else ROW_TILE // 2)
    assert n % tm == 0
    row = lambda c: pl.BlockSpec((tm, c), lambda i: (i, 0))
    const = lambda r, c: pl.BlockSpec((r, c), lambda i: (0, 0))
    args, specs = [x, w.astype(jnp.bfloat16)], [row(k), const(k, m)]
    if pre_gain is not None:
        args.append(pre_gain.reshape(1, k)); specs.append(const(1, k))
    if bias is not None:
        args.append(bias.reshape(1, m)); specs.append(const(1, m))
    if residual is not None:
        args.append(residual); specs.append(row(m))
    if post_gain is not None:
        args.append(post_gain.reshape(1, m)); specs.append(const(1, m))
    out_cols = splits + ((m,) if post_gain is not None else ())
    outs = pl.pallas_call(
        functools.partial(_linear_body, pre_norm=pre_gain is not None, has_bias=bias is not None,
                          has_res=residual is not None, post_norm=post_gain is not None, splits=splits),
        grid=(n // tm,),
        in_specs=specs,
        out_specs=[row(c) for c in out_cols],
        out_shape=[jax.ShapeDtypeStruct((n, c), f32) for c in out_cols],
        compiler_params=pltpu.CompilerParams(dimension_semantics=("arbitrary",),
                                             vmem_limit_bytes=DENSE_VMEM_BYTES),
        name="fused_linear",
    )(*args)
    return outs[0] if len(outs) == 1 else tuple(outs)


def _xattn_body(q_ref, kv_ref, o_ref):
    bf16 = jnp.bfloat16
    d = MEM_HEAD_DIM
    q = q_ref[0]
    kv = kv_ref[0].astype(bf16)
    for h in range(MEM_HEADS):
        k_h = kv[:, h * d:(h + 1) * d]
        v_h = kv[:, (MEM_HEADS + h) * d:(MEM_HEADS + h + 1) * d]
        s = lax.dot_general(q[:, h * d:(h + 1) * d].astype(bf16), k_h, (((1,), (1,)), ((), ())),
                            preferred_element_type=f32) * (d ** -0.5)
        e = jnp.exp(s - jnp.max(s, axis=-1, keepdims=True))
        p = e / jnp.sum(e, axis=-1, keepdims=True)
        o_ref[0, :, h * d:(h + 1) * d] = jnp.dot(p.astype(bf16), v_h, preferred_element_type=f32)


def cross_attention(q, mem_kv):
    B, T, w = q.shape
    M = mem_kv.shape[1]
    kv = mem_kv.reshape(B, M, 2 * w)
    tm = min(T, ROW_TILE)
    assert T % tm == 0 and tm % 8 == 0
    return pl.pallas_call(
        _xattn_body,
        grid=(B, T // tm),
        in_specs=[pl.BlockSpec((1, tm, w), lambda b, i: (b, i, 0)),
                  pl.BlockSpec((1, M, 2 * w), lambda b, i: (b, 0, 0))],
        out_specs=pl.BlockSpec((1, tm, w), lambda b, i: (b, i, 0)),
        out_shape=jax.ShapeDtypeStruct((B, T, w), f32),
        compiler_params=pltpu.CompilerParams(dimension_semantics=("arbitrary", "arbitrary")),
        name="cross_attention",
    )(q, kv)


def masked_softmax(s, mask):
    s = jnp.where(mask, s.astype(f32), -jnp.inf)
    mx = jnp.max(s, -1, keepdims=True)
    mx = jnp.where(jnp.isfinite(mx), mx, 0.0)
    e = jnp.exp(s - mx)
    return e / jnp.maximum(e.sum(-1, keepdims=True), TINY)


def _log_sigmoid(x):
    return jnp.minimum(x, 0.0) - jnp.log1p(jnp.exp(-jnp.abs(x)))


def _cumsum_lanes(x):
    lane = lax.broadcasted_iota(jnp.int32, x.shape, 1)
    shift = 1
    while shift < x.shape[1]:
        x = x + jnp.where(lane >= shift, pltpu.roll(x, shift, axis=1), 0.0)
        shift *= 2
    return x


def _mlstm_body(ml_ref, small_ref, bias_ref, gain_ref, c0_ref, n0_ref, m0_ref, h_ref, c_ref, n_ref, m_ref):
    H, d, L = ML_HEADS, ML_HEAD_DIM, ml_ref.shape[1]
    bf16 = jnp.bfloat16
    nt = (((1,), (1,)), ((), ()))
    tn = (((0,), (0,)), ((), ()))

    @pl.when(pl.program_id(1) == 0)
    def _():
        c_ref[...] = c0_ref[...]
        n_ref[...] = n0_ref[...]
        m_ref[...] = m0_ref[...]

    blk = ml_ref[0]
    small = small_ref[0] + bias_ref[...]
    small_t = small.T
    causal = (lax.broadcasted_iota(jnp.int32, (L, L), 0) >= lax.broadcasted_iota(jnp.int32, (L, L), 1))
    for h in range(H):
        q = blk[:, h * d:(h + 1) * d]
        k = blk[:, (H + h) * d:(H + h + 1) * d] * (d ** -0.5)
        v = blk[:, (2 * H + h) * d:(2 * H + h + 1) * d]
        o_pre = blk[:, (3 * H + h) * d:(3 * H + h + 1) * d]
        ig_row, ig_col = small_t[h:h + 1, :], small[:, h:h + 1]
        b_row = _cumsum_lanes(_log_sigmoid(small_t[H + h:H + h + 1, :]))
        b_col = jnp.broadcast_to(b_row, (L, L)).T
        b_t, b_last = b_col[:, 0:1], b_row[:, L - 1:L]
        c_prev, n_prev, m_prev = c_ref[0, h], n_ref[0, h], m_ref[0, h]
        log_d = jnp.where(causal, b_col - b_row + ig_row, -jnp.inf)
        inter = b_t + m_prev
        m_t = jnp.maximum(inter, jnp.max(log_d, axis=1, keepdims=True))
        a = jnp.exp(inter - m_t)
        qb, vb = q.astype(bf16), v.astype(bf16)
        qk = lax.dot_general(qb, k.astype(bf16), nt, preferred_element_type=f32) * jnp.exp(log_d - m_t)
        num = (a * jnp.dot(qb, c_prev.astype(bf16), preferred_element_type=f32)
               + jnp.dot(qk.astype(bf16), vb, preferred_element_type=f32))
        den = a * jnp.sum(q * n_prev, axis=1, keepdims=True) + jnp.sum(qk, axis=1, keepdims=True)
        hid = num / jnp.maximum(jnp.abs(den), jnp.exp(-m_t))
        m_new = m_t[L - 1:L, :]
        kw = k * jnp.exp(b_last - b_t + ig_col - m_new)
        decay = jnp.exp(b_last + m_prev - m_new)
        c_ref[0, h] = decay * c_prev + lax.dot_general(kw.astype(bf16), vb, tn, preferred_element_type=f32)
        n_ref[0, h] = decay * n_prev + jnp.sum(kw, axis=0, keepdims=True)
        m_ref[0, h] = m_new
        gated = jax.nn.sigmoid(o_pre) * hid
        h_ref[0, :, h * d:(h + 1) * d] = _rms(gated, gain_ref[:, h * d:(h + 1) * d])


def mlstm_pallas(ml, small, b_forget, head_gain, c0, n0, m0):
    B, T, _ = ml.shape
    H, d, L = ML_HEADS, ML_HEAD_DIM, ML_CHUNK
    assert T % L == 0
    bias = jnp.zeros((1, small.shape[-1]), f32).at[0, H:2 * H].set(b_forget)
    state = lambda *s: pl.BlockSpec((1,) + s, lambda b, i: (b,) + (0,) * len(s))
    h, c, n, m = pl.pallas_call(
        _mlstm_body,
        grid=(B, T // L),
        in_specs=[
            pl.BlockSpec((1, L, ml.shape[-1]), lambda b, i: (b, i, 0)),
            pl.BlockSpec((1, L, small.shape[-1]), lambda b, i: (b, i, 0)),
            pl.BlockSpec((1, small.shape[-1]), lambda b, i: (0, 0)),
            pl.BlockSpec((1, H * d), lambda b, i: (0, 0)),
            state(H, d, d), state(H, 1, d), state(H, 1, 1),
        ],
        out_specs=[pl.BlockSpec((1, L, H * d), lambda b, i: (b, i, 0)),
                   state(H, d, d), state(H, 1, d), state(H, 1, 1)],
        out_shape=[jax.ShapeDtypeStruct((B, T, H * d), f32), jax.ShapeDtypeStruct((B, H, d, d), f32),
                   jax.ShapeDtypeStruct((B, H, 1, d), f32), jax.ShapeDtypeStruct((B, H, 1, 1), f32)],
        compiler_params=pltpu.CompilerParams(dimension_semantics=("arbitrary", "arbitrary")),
        name="mlstm",
    )(ml, small, bias, head_gain.reshape(1, H * d), c0, n0.reshape(B, H, 1, d), m0.reshape(B, H, 1, 1))
    return h, c, n.reshape(B, H, d), m.reshape(B, H)


def compress_rows(seg, cmp_pe, cmp_w1, cmp_b1, cmp_w2):
    B, n_seg, width = seg.shape
    G, d, t = NSA_KV_HEADS, NSA_HEAD_DIM, CMP_STRIDE
    e = cmp_w1.shape[-1]
    w1 = cmp_w1.reshape(2, 2, t, d, e)
    eye_kv, eye_g = jnp.eye(2, dtype=f32), jnp.eye(G, dtype=f32)
    w_big = jnp.einsum('kptde,kK,gG->tkgdpKGe', w1, eye_kv, eye_g).reshape(width, 2 * 2 * G * e)
    ab = fused_linear(seg.reshape(B * n_seg, width), w_big).reshape(B, n_seg, 2, 2, G, e)
    const = jnp.einsum('kr,kre->ke', cmp_pe.reshape(2, -1), cmp_w1) + cmp_b1
    pre = ab[:, :-1, 0] + ab[:, 1:, 1] + const[:, None, :]
    out = jnp.einsum('bnkge,kef->kbgnf', jax.nn.gelu(pre), cmp_w2)
    return out[0], out[1]


def cmp_to_sel(imp, nsb):
    r = SEL_BLOCK // CMP_STRIDE
    nc = imp.shape[-1]
    lead = imp.shape[:-1]
    tot = r * nsb
    padw = [(0, 0)] * len(lead)
    first = jnp.pad(imp, padw + [(0, tot - nc)]).reshape(lead + (nsb, r)).sum(-1)
    second = jnp.pad(imp, padw + [(1, tot - nc - 1)]).reshape(lead + (nsb, r)).sum(-1)
    return 0.5 * (first + second)


NSA_SEL_CHUNK = 1024
NSA_WIN_CHUNKS = WINDOW // Q_BLOCK + 1
NSA_KV_LANES = NSA_KV_HEADS * NSA_HEAD_DIM


def _softmax_masked(s, mask):
    sm = jnp.where(mask, s, -jnp.inf)
    mx = jnp.max(sm, axis=-1, keepdims=True)
    mx = jnp.where(mx == -jnp.inf, 0.0, mx)
    e = jnp.exp(sm - mx)
    return e * (1.0 / jnp.maximum(jnp.sum(e, axis=-1, keepdims=True), TINY))


def _top_rows_mask(s, row_id, k):
    sel = jnp.zeros(s.shape, f32)
    sentinel = s.shape[0]
    for _ in range(k):
        m = jnp.max(s, axis=0, keepdims=True)
        first = jnp.min(jnp.where(s == m, row_id, sentinel), axis=0, keepdims=True)
        hit = row_id == first
        sel = jnp.where(hit, 1.0, sel)
        s = jnp.where(hit, -jnp.inf, s)
    return sel


def _nsa_prompt_body(q_ref, gate_ref, kct_ref, vc_ref, mt_ref, kts_ref, vs_ref, ktw_ref, vw_ref, o_ref):
    j = pl.program_id(1)
    Q, R, d = Q_BLOCK, NSA_GROUP, NSA_HEAD_DIM
    n_cmp = kct_ref.shape[-1]
    n_blk = mt_ref.shape[0]
    bf16 = jnp.bfloat16
    q_all = q_ref[0] * (d ** -0.5)
    gates = gate_ref[0]
    q_pos = j * Q + lax.broadcasted_iota(jnp.int32, (Q, 1), 0)
    q_pos_l = j * Q + lax.broadcasted_iota(jnp.int32, (1, Q), 1)
    blk_r = lax.broadcasted_iota(jnp.int32, (n_blk, 1), 0)
    blk_id = lax.broadcasted_iota(jnp.int32, (n_blk, Q), 0)
    nt = (((1,), (1,)), ((), ()))

    for g in range(NSA_KV_HEADS):
        lanes = slice(g * d, (g + 1) * d)
        qg = jnp.concatenate([q_all[:, (g * R + r) * d:(g * R + r + 1) * d] for r in range(R)], axis=0).astype(bf16)

        s = jnp.dot(qg, kct_ref[0, g], preferred_element_type=f32).reshape(R, Q, n_cmp)
        n_id = lax.broadcasted_iota(jnp.int32, (1, n_cmp), 1)
        cmask = (n_id * CMP_STRIDE + (CMP_BLOCK - 1) <= q_pos) & (n_id < n_cmp - 1)
        p = _softmax_masked(s, cmask[None])
        o_c = jnp.dot(p.reshape(R * Q, n_cmp).astype(bf16), vc_ref[0], preferred_element_type=f32)[:, lanes]

        p_sum = p[0] + p[1] + p[2] + p[3]
        hi = p_sum.astype(bf16)
        lo = (p_sum - hi.astype(f32)).astype(bf16)
        mt = mt_ref[...]
        imp_t = (lax.dot_general(mt, hi, nt, preferred_element_type=f32)
                 + lax.dot_general(mt, lo, nt, preferred_element_type=f32))
        valid = blk_r * SEL_BLOCK <= q_pos_l
        forced = (blk_r == 0) | (blk_r == jnp.right_shift(q_pos_l, 6))
        score = jnp.where(forced, FORCED_SCORE, jnp.where(valid, imp_t, INVALID_SCORE))
        sel = _top_rows_mask(score, blk_id, min(N_SEL, n_blk)).T.astype(bf16)

        kc = kts_ref.shape[-1]

        def chunk(c, carry):
            m, l, acc = carry
            kt = kts_ref[0, c, g * d:(g + 1) * d, :]
            sc = jnp.dot(qg, kt, preferred_element_type=f32).reshape(R, Q, kc)
            key = c * kc + lax.broadcasted_iota(jnp.int32, (1, kc), 1)
            expand = jnp.where(blk_r == jnp.right_shift(key, 6), 1.0, 0.0).astype(bf16)
            picked = jnp.dot(sel, expand, preferred_element_type=f32) > 0.5
            mask = picked & (key <= q_pos)
            sm = jnp.where(mask[None], sc, -jnp.inf)
            m_new = jnp.maximum(m, jnp.max(sm, axis=-1, keepdims=True))
            m_safe = jnp.where(m_new == -jnp.inf, 0.0, m_new)
            alpha = jnp.exp(m - m_safe)
            pe = jnp.exp(sm - m_safe)
            l = l * alpha + jnp.sum(pe, axis=-1, keepdims=True)
            pv = jnp.dot(pe.reshape(R * Q, kc).astype(bf16), vs_ref[0, c], preferred_element_type=f32)
            acc = acc * alpha.reshape(R * Q, 1) + pv
            return m_new, l, acc

        n_chunks = (j * Q + Q + kc - 1) // kc
        init = (jnp.full((R, Q, 1), -jnp.inf, f32), jnp.zeros((R, Q, 1), f32),
                jnp.zeros((R * Q, NSA_KV_LANES), f32))
        _, l_s, acc_s = lax.fori_loop(0, n_chunks, chunk, init)
        o_s = acc_s[:, lanes] * (1.0 / jnp.maximum(l_s.reshape(R * Q, 1), TINY))

        ss, vv = [], []
        for i in range(NSA_WIN_CHUNKS):
            cc = jnp.maximum(j - (NSA_WIN_CHUNKS - 1) + i, 0)
            ss.append(jnp.dot(qg, ktw_ref[0, cc, g * d:(g + 1) * d, :], preferred_element_type=f32))
            vv.append(vw_ref[0, cc])
        span = NSA_WIN_CHUNKS * Q
        sw = jnp.concatenate(ss, axis=1).reshape(R, Q, span)
        k_pos = (j - (NSA_WIN_CHUNKS - 1)) * Q + lax.broadcasted_iota(jnp.int32, (1, span), 1)
        wmask = (k_pos >= 0) & (k_pos <= q_pos) & (k_pos >= q_pos - WINDOW)
        pw = _softmax_masked(sw, wmask[None])
        o_w = jnp.dot(pw.reshape(R * Q, span).astype(bf16), jnp.concatenate(vv, axis=0),
                      preferred_element_type=f32)[:, lanes]

        for r in range(R):
            h = g * R + r
            rows = slice(r * Q, (r + 1) * Q)
            o_ref[0, :, h * d:(h + 1) * d] = (gates[:, 3 * h:3 * h + 1] * o_c[rows]
                                              + gates[:, 3 * h + 1:3 * h + 2] * o_s[rows]
                                              + gates[:, 3 * h + 2:3 * h + 3] * o_w[rows])


def _cmp_to_sel_matrix(n_cmp, n_blk):
    r = SEL_BLOCK // CMP_STRIDE
    n = np.arange(n_cmp)
    b = np.arange(n_blk)[:, None]
    m = 0.5 * ((n // r == b).astype(np.float32) + ((n + 1) // r == b).astype(np.float32))
    m[:, n_cmp - 1] = 0.0
    return jnp.asarray(m, jnp.bfloat16)


def nsa_prompt(nq, gates, nkv, kc, vc):
    B, T, _ = nq.shape
    bf16 = jnp.bfloat16
    G, d, w = NSA_KV_HEADS, NSA_HEAD_DIM, NSA_KV_LANES
    n_cmp, n_blk = T // CMP_STRIDE, T // SEL_BLOCK
    kc_s, kc_w = NSA_SEL_CHUNK, Q_BLOCK
    pad = ((0, 0), (0, 0), (0, 1), (0, 0))
    kct = jnp.pad(kc, pad).transpose(0, 1, 3, 2).astype(bf16)
    vc2 = jnp.pad(vc, pad).transpose(0, 2, 1, 3).reshape(B, n_cmp, w).astype(bf16)
    chunks = lambda a, c: a.reshape(B, T // c, c, w).astype(bf16)
    kts = chunks(nkv[..., 2 * w:3 * w], kc_s).transpose(0, 1, 3, 2)
    vs = chunks(nkv[..., 3 * w:4 * w], kc_s)
    ktw = chunks(nkv[..., 4 * w:5 * w], kc_w).transpose(0, 1, 3, 2)
    vw = chunks(nkv[..., 5 * w:6 * w], kc_w)
    whole = lambda a: pl.BlockSpec((1,) + a.shape[1:], lambda b, j: (b,) + (0,) * (a.ndim - 1))
    mt = _cmp_to_sel_matrix(n_cmp, n_blk)
    return pl.pallas_call(
        _nsa_prompt_body,
        grid=(B, T // Q_BLOCK),
        in_specs=[
            pl.BlockSpec((1, Q_BLOCK, nq.shape[-1]), lambda b, j: (b, j, 0)),
            pl.BlockSpec((1, Q_BLOCK, gates.shape[-1]), lambda b, j: (b, j, 0)),
            whole(kct), whole(vc2),
            pl.BlockSpec(mt.shape, lambda b, j: (0, 0)),
            whole(kts), whole(vs), whole(ktw), whole(vw),
        ],
        out_specs=pl.BlockSpec((1, Q_BLOCK, nq.shape[-1]), lambda b, j: (b, j, 0)),
        out_shape=jax.ShapeDtypeStruct(nq.shape, f32),
        compiler_params=pltpu.CompilerParams(dimension_semantics=("arbitrary", "arbitrary"),
                                             vmem_limit_bytes=48 * 1024 * 1024),
        name="nsa_prompt",
    )(nq, gates, kct, vc2, mt, kts, vs, ktw, vw)


def nsa_decode(q, gates, past_len, pool_cmp, pool_slc, page_table, new_cmp, new_slc, win_rows, win_pos0,
               cmp_pe, cmp_w1, cmp_b1, cmp_w2):
    B, S, _ = q.shape
    G, R, d = NSA_KV_HEADS, NSA_GROUP, NSA_HEAD_DIM
    T = past_len + S
    scale = d ** -0.5
    q_pos = past_len + jnp.arange(S)
    qg = q.reshape(B, S, G, R, d).transpose(0, 2, 3, 1, 4)
    gg = gates.reshape(B, S, G, R, 3).transpose(0, 2, 3, 1, 4)

    n_seg = T // CMP_STRIDE
    rows = pool_cmp[page_table].reshape(B, past_len, 2, G, d)
    if n_seg * CMP_STRIDE > past_len:
        rows = jnp.concatenate([rows, new_cmp], axis=1)
    seg = rows[:, :n_seg * CMP_STRIDE].reshape(B, n_seg, CMP_STRIDE, 2, G, d)
    half = CMP_STRIDE * d

    def compress(kv):
        w1 = cmp_w1[kv]
        w_lo, w_hi = w1[:half].reshape(CMP_STRIDE, d, -1), w1[half:].reshape(CMP_STRIDE, d, -1)
        x = seg[:, :, :, kv]
        pre = (jnp.einsum('bntgd,tde->bgne', x[:, :-1], w_lo) + jnp.einsum('bntgd,tde->bgne', x[:, 1:], w_hi)
               + (cmp_pe[kv].reshape(-1) @ w1 + cmp_b1[kv]))
        return jax.nn.gelu(pre) @ cmp_w2[kv]

    kc, vc = compress(0), compress(1)
    c_end = jnp.arange(n_seg - 1) * CMP_STRIDE + (CMP_BLOCK - 1)
    p_c = masked_softmax(jnp.einsum('bgrqd,bgnd->bgrqn', qg, kc) * scale, c_end[None, :] <= q_pos[:, None])
    o_c = jnp.einsum('bgrqn,bgnd->bgrqd', p_c, vc)

    nsb = -(-T // SEL_BLOCK)
    n_past = past_len // SEL_BLOCK
    assert past_len % SEL_BLOCK == 0 and PAGE_SIZE % SEL_BLOCK == 0 and nsb - n_past <= 1
    imp = cmp_to_sel(p_c.sum(2), nsb)
    blk = jnp.arange(nsb)
    valid = blk[None, :] * SEL_BLOCK <= q_pos[:, None]
    forced = (blk[None, :] == 0) | (blk[None, :] == q_pos[:, None] // SEL_BLOCK)
    score = jnp.where(forced, FORCED_SCORE, jnp.where(valid, imp, INVALID_SCORE))
    _, idx = lax.top_k(score, min(N_SEL, nsb))
    per_page = PAGE_SIZE // SEL_BLOCK
    past = jnp.minimum(idx, n_past - 1)
    bi = jnp.arange(B)[:, None, None, None]
    gi = jnp.arange(G)[None, :, None, None]
    pool_blk = page_table[bi, past // per_page] * per_page + past % per_page
    blocks = pool_slc.reshape((-1, SEL_BLOCK) + pool_slc.shape[2:])[pool_blk]
    blocks = jnp.take_along_axis(blocks, gi[..., None, None, None, None], axis=6)[..., 0, :]
    tail = jnp.pad(new_slc, ((0, 0), (0, SEL_BLOCK - S), (0, 0), (0, 0), (0, 0)))
    tail = tail.transpose(0, 3, 1, 2, 4)[:, :, None, None]
    blocks = jnp.where((idx >= n_past)[..., None, None, None], tail, blocks)
    kg = blocks[..., 0, :].reshape(B, G, S, -1, d)
    vg = blocks[..., 1, :].reshape(B, G, S, -1, d)
    k_pos = (idx[..., None] * SEL_BLOCK + jnp.arange(SEL_BLOCK)).reshape(B, G, S, -1)
    p_s = masked_softmax(jnp.einsum('bgrqd,bgqsd->bgrqs', qg, kg) * scale,
                         (k_pos <= q_pos[None, None, :, None])[:, :, None])
    o_s = jnp.einsum('bgrqs,bgqsd->bgrqd', p_s, vg)

    kw = jnp.swapaxes(win_rows[:, :, 0], 1, 2)
    vw = jnp.swapaxes(win_rows[:, :, 1], 1, 2)
    w_pos = win_pos0 + jnp.arange(kw.shape[2])
    wmask = ((w_pos[None, :] <= q_pos[:, None]) & (w_pos[None, :] >= q_pos[:, None] - WINDOW)
             & (w_pos[None, :] >= 0))
    p_w = masked_softmax(jnp.einsum('bgrqd,bgkd->bgrqk', qg, kw) * scale, wmask)
    o_w = jnp.einsum('bgrqk,bgkd->bgrqd', p_w, vw)
    o = gg[..., 0:1] * o_c + gg[..., 1:2] * o_s + gg[..., 2:3] * o_w
    return o.transpose(0, 3, 1, 2, 4).reshape(B, S, NSA_HEADS * d)


PEER_PICKS = PEER_HEADS * PEER_TOPK
PEER_TOKENS_PER_STEP = 64
PEER_GROUP = 4
PEER_FETCH_AHEAD = 2
PEER_ROW_BUFFERS = 4 * PEER_GROUP
PEER_SLAB_LANES = 128
PEER_SLAB_ROWS = 2 * D_MODEL // PEER_SLAB_LANES


def _gelu_tanh(x):
    return 0.5 * x * (1.0 + jnp.tanh(0.7978845608028654 * (x + 0.044715 * x * x * x)))


def _peer_expert_body(ids_ref, ids_next_ref, x_ref, g_ref, seg_ref, uv_ref, o_ref, rows, sems):
    tokens = x_ref.shape[0]
    depth = PEER_ROW_BUFFERS
    half, lanes = PEER_SLAB_ROWS // 2, PEER_SLAB_LANES
    cols = PEER_PICKS * half
    seg = seg_ref.shape[0]
    nt = (((1,), (1,)), ((), ()))
    bf16 = jnp.bfloat16

    col_row = lax.broadcasted_iota(jnp.int32, (half, cols), 1) & (half - 1)
    sub = lax.broadcasted_iota(jnp.int32, (half, cols), 0)
    diag = jnp.where(col_row == sub, 1.0, 0.0)

    group = PEER_GROUP
    groups = tokens // group
    n_seg = cols // seg
    step_id = pl.program_id(0)
    last_step = pl.num_programs(0) - 1

    def wait(slot):
        pltpu.make_async_copy(uv_ref.at[pl.ds(0, PEER_PICKS)], rows.at[slot], sems.at[slot]).wait()

    def fetcher(ids, t0, slot0):
        per = PEER_PICKS // 2

        def fetch(c):
            j, h = divmod(c, 2)
            for k in range(h * per, (h + 1) * per):
                pltpu.make_async_copy(uv_ref.at[ids[t0 + j, k]], rows.at[slot0 + j, k],
                                      sems.at[slot0 + j]).start(priority=k % 2)
        return fetch

    def mix_group(t0, slot0, fetch):
        parts = []
        for j in range(group):
            u_rows = rows[slot0 + j, :, :half, :].reshape(cols, lanes).astype(bf16)
            prod = lax.dot_general(x_ref[t0 + j].astype(bf16), u_rows, nt, preferred_element_type=f32)
            part = jnp.sum(prod * diag, axis=0, keepdims=True)
            parts += [part[:, i * seg:(i + 1) * seg] for i in range(n_seg)]
            fetch(j)
        part = jnp.concatenate(parts, axis=0)
        hi = part.astype(bf16)
        lo = (part - hi.astype(f32)).astype(bf16)
        ones = seg_ref[...]
        act = jnp.dot(hi, ones, preferred_element_type=f32) + jnp.dot(lo, ones, preferred_element_type=f32)
        for j in range(group):
            w = g_ref[t0 + j] * _gelu_tanh(act[j * n_seg:(j + 1) * n_seg])
            w = jnp.concatenate([jnp.broadcast_to(w[i:i + 1, :], (half, seg)) for i in range(n_seg)], axis=1)
            v_rows = rows[slot0 + j, :, half:, :].reshape(cols, lanes).astype(bf16)
            o_ref[t0 + j] = jnp.dot((w * diag).astype(bf16), v_rows, preferred_element_type=f32)
            fetch(group + j)

    sets, ahead = depth // group, PEER_FETCH_AHEAD

    @pl.when(step_id == 0)
    def _():
        for a in range(ahead):
            first = fetcher(ids_ref, a * group, a * group)
            for c in range(2 * group):
                first(c)

    def sweep(it, last):
        for q in range(sets):
            g = sets * it + q
            for j in range(group):
                wait(q * group + j)
            into = ((q + ahead) % sets) * group
            if last and q + ahead >= sets:
                fetch = fetcher(ids_next_ref, (q + ahead - sets) * group, into)
            else:
                fetch = fetcher(ids_ref, (g + ahead) * group, into)
            mix_group(g * group, q * group, fetch)

    def body(it, carry):
        sweep(it, False)
        return carry

    lax.fori_loop(0, groups // sets - 1, body, 0)
    sweep(groups // sets - 1, True)

    @pl.when(step_id == last_step)
    def _():
        for j in range(ahead * group):
            wait(j)


def peer_experts(xn, ids, gates, uv):
    n, d = xn.shape
    tb = PEER_TOKENS_PER_STEP
    slab, half, lanes = PEER_SLAB_ROWS, PEER_SLAB_ROWS // 2, PEER_SLAB_LANES
    seg = 2 * lanes
    cols = PEER_PICKS * half
    assert n % tb == 0 and tb % PEER_ROW_BUFFERS == 0 and half * lanes == d and cols % seg == 0
    assert 0 < PEER_FETCH_AHEAD < PEER_ROW_BUFFERS // PEER_GROUP
    same_pick = np.arange(seg)[:, None] // half == np.arange(seg)[None, :] // half
    out = pl.pallas_call(
        _peer_expert_body,
        grid=(n // tb,),
        in_specs=[
            pl.BlockSpec((tb, PEER_PICKS), lambda i: (i, 0), memory_space=pltpu.SMEM),
            pl.BlockSpec((tb, PEER_PICKS), lambda i: (jnp.minimum(i + 1, n // tb - 1), 0), memory_space=pltpu.SMEM),
            pl.BlockSpec((tb, half, lanes), lambda i: (i, 0, 0)),
            pl.BlockSpec((tb, cols // seg, seg), lambda i: (i, 0, 0)),
            pl.BlockSpec((seg, seg), lambda i: (0, 0)),
            pl.BlockSpec(memory_space=pl.ANY),
        ],
        out_specs=pl.BlockSpec((tb, half, lanes), lambda i: (i, 0, 0)),
        out_shape=jax.ShapeDtypeStruct((n, half, lanes), f32),
        scratch_shapes=[
            pltpu.VMEM((PEER_ROW_BUFFERS, PEER_PICKS, slab, lanes), f32),
            pltpu.SemaphoreType.DMA((PEER_ROW_BUFFERS,)),
        ],
        compiler_params=pltpu.CompilerParams(dimension_semantics=("arbitrary",)),
        name="peer_experts",
    )(ids, ids, xn.reshape(n, half, lanes), jnp.repeat(gates, half, axis=1).reshape(n, cols // seg, seg),
      jnp.asarray(same_pick, jnp.bfloat16), uv)
    return out.reshape(n, d)


PEER_ROUTE_TOKENS = 256
PEER_HALF_DIM = PEER_QUERY_DIM // 2


def _top_rows(s, row_id, k, payload=None):
    vals, picks = [], []
    sentinel = s.shape[0]
    for _ in range(k):
        m = jnp.max(s, axis=0, keepdims=True)
        first = jnp.min(jnp.where(s == m, row_id, sentinel), axis=0, keepdims=True)
        hit = row_id == first
        vals.append(m)
        if payload is None:
            picks.append(first)
        else:
            picks.append(jnp.max(jnp.where(hit, payload, -1), axis=0, keepdims=True))
        s = jnp.where(hit, -jnp.inf, s)
    return jnp.concatenate(vals, 0), jnp.concatenate(picks, 0)


def _peer_route_body(x_ref, wq_ref, sk_ref, ids_ref, gate_ref, q_scr):
    tn = x_ref.shape[0]
    q = jnp.dot(x_ref[...].astype(jnp.bfloat16), wq_ref[...], preferred_element_type=f32)
    for j in range(2 * PEER_HEADS):
        q_scr[j] = q[:, j * PEER_HALF_DIM:(j + 1) * PEER_HALF_DIM].astype(jnp.bfloat16)
    key_id = lax.broadcasted_iota(jnp.int32, (PEER_N_KEYS, tn), 0)
    n_cand = -(-sum(PEER_TOPK // (a + 1) for a in range(PEER_TOPK)) // 8) * 8
    cand_id = lax.broadcasted_iota(jnp.int32, (n_cand, tn), 0)

    def head(h, carry):
        tops = []
        for p in range(2):
            s = lax.dot_general(sk_ref[2 * h + p], q_scr[2 * h + p], (((1,), (1,)), ((), ())),
                                preferred_element_type=f32)
            tops.append(_top_rows(s, key_id, PEER_TOPK))
        (v0, i0), (v1, i1) = tops
        cs, ce = [], []
        for a in range(PEER_TOPK):
            nb = PEER_TOPK // (a + 1)
            cs.append(v0[a:a + 1] + v1[:nb])
            ce.append(i0[a:a + 1] * PEER_N_KEYS + i1[:nb])
        pad = n_cand - sum(c.shape[0] for c in cs)
        cand_s = jnp.concatenate(cs + [jnp.full((pad, tn), -jnp.inf, f32)], axis=0)
        cand_e = jnp.concatenate(ce + [jnp.zeros((pad, tn), jnp.int32)], axis=0)
        top_s, top_e = _top_rows(cand_s, cand_id, PEER_TOPK, payload=cand_e)
        e = jnp.exp(top_s - top_s[0:1])
        ids_ref[h] = top_e
        gate_ref[h] = e / jnp.sum(e, axis=0, keepdims=True)
        return carry

    lax.fori_loop(0, PEER_HEADS, head, 0)


def peer_route(xn, wq, sub_keys):
    n, d = xn.shape
    tn = PEER_ROUTE_TOKENS if n % PEER_ROUTE_TOKENS == 0 else PEER_ROUTE_TOKENS // 2
    assert n % tn == 0
    n_q = 2 * PEER_HEADS * PEER_HALF_DIM
    sk = sub_keys.reshape(2 * PEER_HEADS, PEER_N_KEYS, PEER_HALF_DIM).astype(jnp.bfloat16)
    ids_t, gates_t = pl.pallas_call(
        _peer_route_body,
        grid=(n // tn,),
        in_specs=[
            pl.BlockSpec((tn, d), lambda i: (i, 0)),
            pl.BlockSpec((d, n_q), lambda i: (0, 0)),
            pl.BlockSpec((2 * PEER_HEADS, PEER_N_KEYS, PEER_HALF_DIM), lambda i: (0, 0, 0)),
        ],
        out_specs=[
            pl.BlockSpec((PEER_HEADS, PEER_TOPK, tn), lambda i: (0, 0, i)),
            pl.BlockSpec((PEER_HEADS, PEER_TOPK, tn), lambda i: (0, 0, i)),
        ],
        out_shape=[
            jax.ShapeDtypeStruct((PEER_HEADS, PEER_TOPK, n), jnp.int32),
            jax.ShapeDtypeStruct((PEER_HEADS, PEER_TOPK, n), f32),
        ],
        scratch_shapes=[pltpu.VMEM((2 * PEER_HEADS, tn, PEER_HALF_DIM), jnp.bfloat16)],
        compiler_params=pltpu.CompilerParams(dimension_semantics=("arbitrary",)),
        name="peer_route",
    )(xn, wq.astype(jnp.bfloat16), sk)
    to_rows = lambda a: a.reshape(PEER_PICKS, n).T
    return to_rows(ids_t), to_rows(gates_t)


def kernel(x_prompt, x_sample, cache_cmp_kv, cache_slc_kv, cache_win_kv, state_mlstm_c, state_mlstm_n,
           state_mlstm_m, cache_mem_kv, page_table, mem_prompt, g_mix, w_in, b_in, b_forget, ml_head_gain,
           cmp_pe, cmp_w1, cmp_b1, cmp_w2, w_out, g_xattn, g_mem, w_xq, w_xk, w_xv, w_xo, g_ffn, peer_wq,
           peer_sub_keys, peer_u, peer_v, g_final):
    B, T = x_prompt.shape[:2]
    DB, S = x_sample.shape[:2]
    past_len = page_table.shape[1] * PAGE_SIZE
    l = 0

    o_i, o_nq, o_nkv, o_ng = IN_OFFSETS[3], IN_OFFSETS[5], IN_OFFSETS[6], IN_OFFSETS[7]
    n_small = 2 * ML_HEADS + 3 * NSA_HEADS
    small_pad = 128 - n_small
    regroup = lambda a: jnp.concatenate(
        [a[..., :o_i], a[..., o_nq:o_nkv], a[..., o_nkv:o_ng], a[..., o_i:o_nq], a[..., o_ng:],
         jnp.zeros(a.shape[:-1] + (small_pad,), a.dtype)], axis=-1)
    w_in_g, b_in_g = regroup(w_in[l]), regroup(b_in[l])
    in_splits = (4 * ML_WIDTH, NSA_WIDTH, 6 * NSA_KV_LANES, n_small + small_pad)

    def project(x):
        b, t, _ = x.shape
        ml, nq, nkv, small = fused_linear(x.reshape(b * t, D_MODEL), w_in_g, pre_gain=g_mix[l], bias=b_in_g,
                                          splits=in_splits)
        ng = jax.nn.sigmoid(small[:, 2 * ML_HEADS:n_small]).reshape(b, t, 3 * NSA_HEADS)
        return (ml.reshape(b, t, 4 * ML_WIDTH), small.reshape(b, t, -1), nq.reshape(b, t, NSA_WIDTH),
                nkv.reshape(b, t, 6 * NSA_KV_LANES), ng)

    def after_mixers(x, h_ml, h_nsa, mem_kv):
        b, t, _ = x.shape
        h = jnp.concatenate([h_ml, h_nsa], -1).reshape(b * t, D_MODEL)
        x1, xn = fused_linear(h, w_out[l], residual=x.reshape(b * t, D_MODEL), post_gain=g_xattn[l])
        q = fused_linear(xn, w_xq[l]).reshape(b, t, D_MODEL)
        t_pad = -t % 8
        o = cross_attention(jnp.pad(q, ((0, 0), (0, t_pad), (0, 0))), mem_kv)[:, :t]
        return fused_linear(o.reshape(b * t, D_MODEL), w_xo[l], residual=x1, post_gain=g_ffn[l])

    ml, small, nq, nkv, ng = project(x_prompt)
    h_ml, p_c, p_n, p_m = mlstm_pallas(
        ml, small, b_forget[l], ml_head_gain[l],
        jnp.zeros((B, ML_HEADS, ML_HEAD_DIM, ML_HEAD_DIM), f32),
        jnp.zeros((B, ML_HEADS, ML_HEAD_DIM), f32),
        jnp.full((B, ML_HEADS), -jnp.inf, f32))
    rows6 = nkv.reshape(B, T, 6, NSA_KV_HEADS, NSA_HEAD_DIM)
    p_cmp, p_slc, win_rows = rows6[:, :, 0:2], rows6[:, :, 2:4], rows6[:, :, 4:6]
    seg_w = CMP_STRIDE * 2 * NSA_KV_LANES
    kc, vc = compress_rows(nkv[..., :2 * NSA_KV_LANES].reshape(B, T // CMP_STRIDE, seg_w),
                           cmp_pe[l], cmp_w1[l], cmp_b1[l], cmp_w2[l])
    h_nsa = nsa_prompt(nq, ng, nkv, kc, vc)
    p_mem = fused_linear(mem_prompt.reshape(-1, D_MODEL), jnp.concatenate([w_xk[l], w_xv[l]], axis=1),
                         pre_gain=g_mem[l]).reshape(B, MEM_LEN, 2, MEM_HEADS, MEM_HEAD_DIM)
    xp, xp_ffn_in = after_mixers(x_prompt, h_ml, h_nsa, p_mem)
    p_win = win_rows[:, T - min(WINDOW, T):]

    ml, small, nq, nkv, ng = project(x_sample)
    t_pad = ML_CHUNK - S
    ml_p = jnp.pad(ml, ((0, 0), (0, t_pad), (0, 0)))
    small_p = jnp.pad(small, ((0, 0), (0, t_pad), (0, 0)))
    small_p = small_p.at[:, S:, :ML_HEADS].set(-jnp.inf).at[:, S:, ML_HEADS:2 * ML_HEADS].set(jnp.inf)
    h_ml, s_c, s_n, s_m = mlstm_pallas(ml_p, small_p, b_forget[l], ml_head_gain[l],
                                       state_mlstm_c[l], state_mlstm_n[l], state_mlstm_m[l])
    h_ml = h_ml[:, :S]
    rows6 = nkv.reshape(DB, S, 6, NSA_KV_HEADS, NSA_HEAD_DIM)
    s_cmp, s_slc, win_rows = rows6[:, :, 0:2], rows6[:, :, 2:4], rows6[:, :, 4:6]
    win_buf = cache_win_kv[l]
    win_ext = jnp.concatenate([win_buf, win_rows], axis=1)
    h_nsa = nsa_decode(nq, ng, past_len, cache_cmp_kv[l], cache_slc_kv[l], page_table, s_cmp, s_slc,
                       win_ext, past_len - win_buf.shape[1], cmp_pe[l], cmp_w1[l], cmp_b1[l], cmp_w2[l])
    xs, xs_ffn_in = after_mixers(x_sample, h_ml, h_nsa, cache_mem_kv[l])
    w_keep = min(WINDOW, past_len + S)
    s_win = win_ext[:, win_ext.shape[1] - w_keep:]

    n_p = B * T
    uv = jnp.concatenate([peer_u[l], peer_v[l]], axis=1).reshape(-1, PEER_SLAB_ROWS, PEER_SLAB_LANES)
    routes = [peer_route(x, peer_wq[l], peer_sub_keys[l]) for x in (xp_ffn_in, xs_ffn_in)]
    ids, gates = (jnp.concatenate(a, 0) for a in zip(*routes))
    ffn = peer_experts(jnp.concatenate([xp_ffn_in, xs_ffn_in], 0), ids, gates, uv)
    y_prompt = add_rmsnorm(xp, ffn[:n_p], g_final).reshape(x_prompt.shape)
    y_sample = add_rmsnorm(xs, ffn[n_p:], g_final).reshape(x_sample.shape)
    st = lambda a: a[None]
    return (y_prompt, y_sample,
            st(p_cmp), st(p_slc), st(p_win), st(p_c), st(p_n), st(p_m), st(p_mem),
            st(s_cmp), st(s_slc), st(s_win), st(s_c), st(s_n), st(s_m))
```

```python
import functools

import jax
import jax.numpy as jnp
from jax import lax
import numpy as np
from jax.experimental import pallas as pl
from jax.experimental.pallas import tpu as pltpu

D_MODEL = 1024
PAGE_SIZE = 128

ML_WIDTH = D_MODEL // 2
ML_HEADS = 4
ML_HEAD_DIM = ML_WIDTH // ML_HEADS
ML_CHUNK = 128
NSA_WIDTH = D_MODEL - ML_WIDTH
NSA_HEADS = 8
NSA_HEAD_DIM = NSA_WIDTH // NSA_HEADS
NSA_KV_HEADS = 2
NSA_GROUP = NSA_HEADS // NSA_KV_HEADS
CMP_BLOCK = 32
CMP_STRIDE = 16
SEL_BLOCK = 64
N_SEL = 16
WINDOW = 512
Q_BLOCK = 128
FORCED_SCORE = 1.0e4
INVALID_SCORE = -1.0
MEM_LEN = 256
MEM_HEADS = 4
MEM_HEAD_DIM = D_MODEL // MEM_HEADS
PEER_HEADS = 8
PEER_N_KEYS = 128
PEER_TOPK = 16
PEER_QUERY_DIM = 256
EPS = 1e-6
TINY = 1e-30
IN_SPLITS = (ML_WIDTH, ML_WIDTH, ML_WIDTH, ML_WIDTH, ML_HEADS, ML_HEADS,
             NSA_WIDTH, 6 * NSA_KV_HEADS * NSA_HEAD_DIM, 3 * NSA_HEADS)
IN_OFFSETS = tuple(int(o) for o in np.cumsum(IN_SPLITS)[:-1])

f32 = jnp.float32

LANES = 128
ROW_TILE = 512
WIDE_OUTPUT = 2048
DENSE_VMEM_BYTES = 56 * 1024 * 1024


def _rms(x, g):
    return x * lax.rsqrt(jnp.mean(x * x, -1, keepdims=True) + EPS) * g


def _add_norm_body(x_ref, r_ref, g_ref, o_ref):
    o_ref[...] = _rms(x_ref[...] + r_ref[...], g_ref[...])


def add_rmsnorm(x, r, g):
    shape = x.shape
    d = shape[-1]
    x2, r2 = x.reshape(-1, d), r.reshape(-1, d)
    n = x2.shape[0]
    tm = min(n, ROW_TILE)
    rows = pl.BlockSpec((tm, d), lambda i: (i, 0))
    out = pl.pallas_call(
        _add_norm_body,
        grid=(n // tm,),
        in_specs=[rows, rows, pl.BlockSpec((1, d), lambda i: (0, 0))],
        out_specs=rows,
        out_shape=jax.ShapeDtypeStruct((n, d), f32),
        name="add_rmsnorm",
    )(x2, r2, g.reshape(1, d))
    return out.reshape(shape)


def _linear_body(*refs, pre_norm, has_bias, has_res, post_norm, splits):
    it = iter(refs)
    x_ref, w_ref = next(it), next(it)
    x = x_ref[...]
    if pre_norm:
        x = _rms(x, next(it)[...])
    y = jnp.dot(x.astype(jnp.bfloat16), w_ref[...], preferred_element_type=f32)
    if has_bias:
        y = y + next(it)[...]
    if has_res:
        y = y + next(it)[...]
    post_gain = next(it)[...] if post_norm else None
    off = 0
    for m in splits:
        next(it)[...] = y[:, off:off + m]
        off += m
    if post_norm:
        next(it)[...] = _rms(y, post_gain)


def fused_linear(x, w, *, pre_gain=None, bias=None, residual=None, post_gain=None, splits=None):
    n, k = x.shape
    m = w.shape[1]
    splits = (m,) if splits is None else tuple(splits)
    assert sum(splits) == m and (post_gain is None or len(splits) == 1)
    tm = min(n, ROW_TILE if m <= WIDE_OUTPUT else ROW_TILE // 2)
    assert n % tm == 0
    row = lambda c: pl.BlockSpec((tm, c), lambda i: (i, 0))
    const = lambda r, c: pl.BlockSpec((r, c), lambda i: (0, 0))
    args, specs = [x, w.astype(jnp.bfloat16)], [row(k), const(k, m)]
    if pre_gain is not None:
        args.append(pre_gain.reshape(1, k)); specs.append(const(1, k))
    if bias is not None:
        args.append(bias.reshape(1, m)); specs.append(const(1, m))
    if residual is not None:
        args.append(residual); specs.append(row(m))
    if post_gain is not None:
        args.append(post_gain.reshape(1, m)); specs.append(const(1, m))
    out_cols = splits + ((m,) if post_gain is not None else ())
    outs = pl.pallas_call(
        functools.partial(_linear_body, pre_norm=pre_gain is not None, has_bias=bias is not None,
                          has_res=residual is not None, post_norm=post_gain is not None, splits=splits),
        grid=(n // tm,),
        in_specs=specs,
        out_specs=[row(c) for c in out_cols],
        out_shape=[jax.ShapeDtypeStruct((n, c), f32) for c in out_cols],
        compiler_params=pltpu.CompilerParams(dimension_semantics=("arbitrary",),
                                             vmem_limit_bytes=DENSE_VMEM_BYTES),
        name="fused_linear",
    )(*args)
    return outs[0] if len(outs) == 1 else tuple(outs)


def _xattn_body(q_ref, kv_ref, o_ref):
    bf16 = jnp.bfloat16
    d = MEM_HEAD_DIM
    q = q_ref[0]
    kv = kv_ref[0].astype(bf16)
    for h in range(MEM_HEADS):
        k_h = kv[:, h * d:(h + 1) * d]
        v_h = kv[:, (MEM_HEADS + h) * d:(MEM_HEADS + h + 1) * d]
        s = lax.dot_general(q[:, h * d:(h + 1) * d].astype(bf16), k_h, (((1,), (1,)), ((), ())),
                            preferred_element_type=f32) * (d ** -0.5)
        e = jnp.exp(s - jnp.max(s, axis=-1, keepdims=True))
        p = e / jnp.sum(e, axis=-1, keepdims=True)
        o_ref[0, :, h * d:(h + 1) * d] = jnp.dot(p.astype(bf16), v_h, preferred_element_type=f32)


def cross_attention(q, mem_kv):
    B, T, w = q.shape
    M = mem_kv.shape[1]
    kv = mem_kv.reshape(B, M, 2 * w)
    tm = min(T, ROW_TILE)
    assert T % tm == 0 and tm % 8 == 0
    return pl.pallas_call(
        _xattn_body,
        grid=(B, T // tm),
        in_specs=[pl.BlockSpec((1, tm, w), lambda b, i: (b, i, 0)),
                  pl.BlockSpec((1, M, 2 * w), lambda b, i: (b, 0, 0))],
        out_specs=pl.BlockSpec((1, tm, w), lambda b, i: (b, i, 0)),
        out_shape=jax.ShapeDtypeStruct((B, T, w), f32),
        compiler_params=pltpu.CompilerParams(dimension_semantics=("arbitrary", "arbitrary")),
        name="cross_attention",
    )(q, kv)


def masked_softmax(s, mask):
    s = jnp.where(mask, s.astype(f32), -jnp.inf)
    mx = jnp.max(s, -1, keepdims=True)
    mx = jnp.where(jnp.isfinite(mx), mx, 0.0)
    e = jnp.exp(s - mx)
    return e / jnp.maximum(e.sum(-1, keepdims=True), TINY)


def _log_sigmoid(x):
    return jnp.minimum(x, 0.0) - jnp.log1p(jnp.exp(-jnp.abs(x)))


def _cumsum_lanes(x):
    lane = lax.broadcasted_iota(jnp.int32, x.shape, 1)
    shift = 1
    while shift < x.shape[1]:
        x = x + jnp.where(lane >= shift, pltpu.roll(x, shift, axis=1), 0.0)
        shift *= 2
    return x


def _mlstm_body(ml_ref, small_ref, bias_ref, gain_ref, c0_ref, n0_ref, m0_ref, h_ref, c_ref, n_ref, m_ref):
    H, d, L = ML_HEADS, ML_HEAD_DIM, ml_ref.shape[1]
    bf16 = jnp.bfloat16
    nt = (((1,), (1,)), ((), ()))
    tn = (((0,), (0,)), ((), ()))

    @pl.when(pl.program_id(1) == 0)
    def _():
        c_ref[...] = c0_ref[...]
        n_ref[...] = n0_ref[...]
        m_ref[...] = m0_ref[...]

    blk = ml_ref[0]
    small = small_ref[0] + bias_ref[...]
    small_t = small.T
    causal = (lax.broadcasted_iota(jnp.int32, (L, L), 0) >= lax.broadcasted_iota(jnp.int32, (L, L), 1))
    for h in range(H):
        q = blk[:, h * d:(h + 1) * d]
        k = blk[:, (H + h) * d:(H + h + 1) * d] * (d ** -0.5)
        v = blk[:, (2 * H + h) * d:(2 * H + h + 1) * d]
        o_pre = blk[:, (3 * H + h) * d:(3 * H + h + 1) * d]
        ig_row, ig_col = small_t[h:h + 1, :], small[:, h:h + 1]
        b_row = _cumsum_lanes(_log_sigmoid(small_t[H + h:H + h + 1, :]))
        b_col = jnp.broadcast_to(b_row, (L, L)).T
        b_t, b_last = b_col[:, 0:1], b_row[:, L - 1:L]
        c_prev, n_prev, m_prev = c_ref[0, h], n_ref[0, h], m_ref[0, h]
        log_d = jnp.where(causal, b_col - b_row + ig_row, -jnp.inf)
        inter = b_t + m_prev
        m_t = jnp.maximum(inter, jnp.max(log_d, axis=1, keepdims=True))
        a = jnp.exp(inter - m_t)
        qb, vb = q.astype(bf16), v.astype(bf16)
        qk = lax.dot_general(qb, k.astype(bf16), nt, preferred_element_type=f32) * jnp.exp(log_d - m_t)
        num = (a * jnp.dot(qb, c_prev.astype(bf16), preferred_element_type=f32)
               + jnp.dot(qk.astype(bf16), vb, preferred_element_type=f32))
        den = a * jnp.sum(q * n_prev, axis=1, keepdims=True) + jnp.sum(qk, axis=1, keepdims=True)
        hid = num / jnp.maximum(jnp.abs(den), jnp.exp(-m_t))
        m_new = m_t[L - 1:L, :]
        kw = k * jnp.exp(b_last - b_t + ig_col - m_new)
        decay = jnp.exp(b_last + m_prev - m_new)
        c_ref[0, h] = decay * c_prev + lax.dot_general(kw.astype(bf16), vb, tn, preferred_element_type=f32)
        n_ref[0, h] = decay * n_prev + jnp.sum(kw, axis=0, keepdims=True)
        m_ref[0, h] = m_new
        gated = jax.nn.sigmoid(o_pre) * hid
        h_ref[0, :, h * d:(h + 1) * d] = _rms(gated, gain_ref[:, h * d:(h + 1) * d])


def mlstm_pallas(ml, small, b_forget, head_gain, c0, n0, m0):
    B, T, _ = ml.shape
    H, d, L = ML_HEADS, ML_HEAD_DIM, ML_CHUNK
    assert T % L == 0
    bias = jnp.zeros((1, small.shape[-1]), f32).at[0, H:2 * H].set(b_forget)
    state = lambda *s: pl.BlockSpec((1,) + s, lambda b, i: (b,) + (0,) * len(s))
    h, c, n, m = pl.pallas_call(
        _mlstm_body,
        grid=(B, T // L),
        in_specs=[
            pl.BlockSpec((1, L, ml.shape[-1]), lambda b, i: (b, i, 0)),
            pl.BlockSpec((1, L, small.shape[-1]), lambda b, i: (b, i, 0)),
            pl.BlockSpec((1, small.shape[-1]), lambda b, i: (0, 0)),
            pl.BlockSpec((1, H * d), lambda b, i: (0, 0)),
            state(H, d, d), state(H, 1, d), state(H, 1, 1),
        ],
        out_specs=[pl.BlockSpec((1, L, H * d), lambda b, i: (b, i, 0)),
                   state(H, d, d), state(H, 1, d), state(H, 1, 1)],
        out_shape=[jax.ShapeDtypeStruct((B, T, H * d), f32), jax.ShapeDtypeStruct((B, H, d, d), f32),
                   jax.ShapeDtypeStruct((B, H, 1, d), f32), jax.ShapeDtypeStruct((B, H, 1, 1), f32)],
        compiler_params=pltpu.CompilerParams(dimension_semantics=("arbitrary", "arbitrary")),
        name="mlstm",
    )(ml, small, bias, head_gain.reshape(1, H * d), c0, n0.reshape(B, H, 1, d), m0.reshape(B, H, 1, 1))
    return h, c, n.reshape(B, H, d), m.reshape(B, H)


def compress_rows(seg, cmp_pe, cmp_w1, cmp_b1, cmp_w2):
    B, n_seg, width = seg.shape
    G, d, t = NSA_KV_HEADS, NSA_HEAD_DIM, CMP_STRIDE
    e = cmp_w1.shape[-1]
    w1 = cmp_w1.reshape(2, 2, t, d, e)
    eye_kv, eye_g = jnp.eye(2, dtype=f32), jnp.eye(G, dtype=f32)
    w_big = jnp.einsum('kptde,kK,gG->tkgdpKGe', w1, eye_kv, eye_g).reshape(width, 2 * 2 * G * e)
    ab = fused_linear(seg.reshape(B * n_seg, width), w_big).reshape(B, n_seg, 2, 2, G, e)
    const = jnp.einsum('kr,kre->ke', cmp_pe.reshape(2, -1), cmp_w1) + cmp_b1
    pre = ab[:, :-1, 0] + ab[:, 1:, 1] + const[:, None, :]
    out = jnp.einsum('bnkge,kef->kbgnf', jax.nn.gelu(pre), cmp_w2)
    return out[0], out[1]


def cmp_to_sel(imp, nsb):
    r = SEL_BLOCK // CMP_STRIDE
    nc = imp.shape[-1]
    lead = imp.shape[:-1]
    tot = r * nsb
    padw = [(0, 0)] * len(lead)
    first = jnp.pad(imp, padw + [(0, tot - nc)]).reshape(lead + (nsb, r)).sum(-1)
    second = jnp.pad(imp, padw + [(1, tot - nc - 1)]).reshape(lead + (nsb, r)).sum(-1)
    return 0.5 * (first + second)


NSA_SEL_CHUNK = 1024
NSA_WIN_CHUNKS = WINDOW // Q_BLOCK + 1
NSA_KV_LANES = NSA_KV_HEADS * NSA_HEAD_DIM
NSA_VMEM_BYTES = 48 * 1024 * 1024
SEL_SHIFT = SEL_BLOCK.bit_length() - 1


def _softmax_masked(s, mask):
    sm = jnp.where(mask, s, -jnp.inf)
    mx = jnp.max(sm, axis=-1, keepdims=True)
    mx = jnp.where(mx == -jnp.inf, 0.0, mx)
    e = jnp.exp(sm - mx)
    return e * (1.0 / jnp.maximum(jnp.sum(e, axis=-1, keepdims=True), TINY))


def _top_rows_mask(s, row_id, k):
    sel = jnp.zeros(s.shape, f32)
    sentinel = s.shape[0]
    for _ in range(k):
        m = jnp.max(s, axis=0, keepdims=True)
        first = jnp.min(jnp.where(s == m, row_id, sentinel), axis=0, keepdims=True)
        hit = row_id == first
        sel = jnp.where(hit, 1.0, sel)
        s = jnp.where(hit, -jnp.inf, s)
    return sel


def _nsa_prompt_body(q_ref, gate_ref, kct_ref, vc_ref, mt_ref, kts_ref, vs_ref, ktw_ref, vw_ref, o_ref):
    j = pl.program_id(1)
    Q, R, d = Q_BLOCK, NSA_GROUP, NSA_HEAD_DIM
    n_cmp = kct_ref.shape[-1]
    n_blk = mt_ref.shape[0]
    bf16 = jnp.bfloat16
    q_all = q_ref[0] * (d ** -0.5)
    gates = gate_ref[0]
    q_pos = j * Q + lax.broadcasted_iota(jnp.int32, (Q, 1), 0)
    q_pos_l = j * Q + lax.broadcasted_iota(jnp.int32, (1, Q), 1)
    blk_r = lax.broadcasted_iota(jnp.int32, (n_blk, 1), 0)
    blk_id = lax.broadcasted_iota(jnp.int32, (n_blk, Q), 0)
    nt = (((1,), (1,)), ((), ()))

    for g in range(NSA_KV_HEADS):
        lanes = slice(g * d, (g + 1) * d)
        qg = jnp.concatenate([q_all[:, (g * R + r) * d:(g * R + r + 1) * d] for r in range(R)], axis=0).astype(bf16)

        s = jnp.dot(qg, kct_ref[0, g], preferred_element_type=f32).reshape(R, Q, n_cmp)
        n_id = lax.broadcasted_iota(jnp.int32, (1, n_cmp), 1)
        cmask = (n_id * CMP_STRIDE + (CMP_BLOCK - 1) <= q_pos) & (n_id < n_cmp - 1)
        p = _softmax_masked(s, cmask[None])
        o_c = jnp.dot(p.reshape(R * Q, n_cmp).astype(bf16), vc_ref[0], preferred_element_type=f32)[:, lanes]

        p_sum = p[0] + p[1] + p[2] + p[3]
        hi = p_sum.astype(bf16)
        lo = (p_sum - hi.astype(f32)).astype(bf16)
        mt = mt_ref[...]
        imp_t = (lax.dot_general(mt, hi, nt, preferred_element_type=f32)
                 + lax.dot_general(mt, lo, nt, preferred_element_type=f32))
        valid = blk_r * SEL_BLOCK <= q_pos_l
        forced = (blk_r == 0) | (blk_r == jnp.right_shift(q_pos_l, SEL_SHIFT))
        score = jnp.where(forced, FORCED_SCORE, jnp.where(valid, imp_t, INVALID_SCORE))
        sel = _top_rows_mask(score, blk_id, min(N_SEL, n_blk)).T.astype(bf16)

        kc = kts_ref.shape[-1]

        def chunk(c, carry):
            m, l, acc = carry
            kt = kts_ref[0, c, g * d:(g + 1) * d, :]
            sc = jnp.dot(qg, kt, preferred_element_type=f32).reshape(R, Q, kc)
            key = c * kc + lax.broadcasted_iota(jnp.int32, (1, kc), 1)
            expand = jnp.where(blk_r == jnp.right_shift(key, SEL_SHIFT), 1.0, 0.0).astype(bf16)
            picked = jnp.dot(sel, expand, preferred_element_type=f32) > 0.5
            mask = picked & (key <= q_pos)
            sm = jnp.where(mask[None], sc, -jnp.inf)
            m_new = jnp.maximum(m, jnp.max(sm, axis=-1, keepdims=True))
            m_safe = jnp.where(m_new == -jnp.inf, 0.0, m_new)
            alpha = jnp.exp(m - m_safe)
            pe = jnp.exp(sm - m_safe)
            l = l * alpha + jnp.sum(pe, axis=-1, keepdims=True)
            pv = jnp.dot(pe.reshape(R * Q, kc).astype(bf16), vs_ref[0, c], preferred_element_type=f32)
            acc = acc * alpha.reshape(R * Q, 1) + pv
            return m_new, l, acc

        n_chunks = (j * Q + Q + kc - 1) // kc
        init = (jnp.full((R, Q, 1), -jnp.inf, f32), jnp.zeros((R, Q, 1), f32),
                jnp.zeros((R * Q, NSA_KV_LANES), f32))
        _, l_s, acc_s = lax.fori_loop(0, n_chunks, chunk, init)
        o_s = acc_s[:, lanes] * (1.0 / jnp.maximum(l_s.reshape(R * Q, 1), TINY))

        ss, vv = [], []
        for i in range(NSA_WIN_CHUNKS):
            cc = jnp.maximum(j - (NSA_WIN_CHUNKS - 1) + i, 0)
            ss.append(jnp.dot(qg, ktw_ref[0, cc, g * d:(g + 1) * d, :], preferred_element_type=f32))
            vv.append(vw_ref[0, cc])
        span = NSA_WIN_CHUNKS * Q
        sw = jnp.concatenate(ss, axis=1).reshape(R, Q, span)
        k_pos = (j - (NSA_WIN_CHUNKS - 1)) * Q + lax.broadcasted_iota(jnp.int32, (1, span), 1)
        wmask = (k_pos >= 0) & (k_pos <= q_pos) & (k_pos >= q_pos - WINDOW)
        pw = _softmax_masked(sw, wmask[None])
        o_w = jnp.dot(pw.reshape(R * Q, span).astype(bf16), jnp.concatenate(vv, axis=0),
                      preferred_element_type=f32)[:, lanes]

        for r in range(R):
            h = g * R + r
            rows = slice(r * Q, (r + 1) * Q)
            o_ref[0, :, h * d:(h + 1) * d] = (gates[:, 3 * h:3 * h + 1] * o_c[rows]
                                              + gates[:, 3 * h + 1:3 * h + 2] * o_s[rows]
                                              + gates[:, 3 * h + 2:3 * h + 3] * o_w[rows])


def _cmp_to_sel_matrix(n_cmp, n_blk):
    r = SEL_BLOCK // CMP_STRIDE
    n = np.arange(n_cmp)
    b = np.arange(n_blk)[:, None]
    m = 0.5 * ((n // r == b).astype(np.float32) + ((n + 1) // r == b).astype(np.float32))
    m[:, n_cmp - 1] = 0.0
    return jnp.asarray(m, jnp.bfloat16)


def nsa_prompt(nq, gates, nkv, kc, vc):
    B, T, _ = nq.shape
    bf16 = jnp.bfloat16
    G, d, w = NSA_KV_HEADS, NSA_HEAD_DIM, NSA_KV_LANES
    n_cmp, n_blk = T // CMP_STRIDE, T // SEL_BLOCK
    kc_s, kc_w = NSA_SEL_CHUNK, Q_BLOCK
    pad = ((0, 0), (0, 0), (0, 1), (0, 0))
    kct = jnp.pad(kc, pad).transpose(0, 1, 3, 2).astype(bf16)
    vc2 = jnp.pad(vc, pad).transpose(0, 2, 1, 3).reshape(B, n_cmp, w).astype(bf16)
    chunks = lambda a, c: a.reshape(B, T // c, c, w).astype(bf16)
    kts = chunks(nkv[..., 2 * w:3 * w], kc_s).transpose(0, 1, 3, 2)
    vs = chunks(nkv[..., 3 * w:4 * w], kc_s)
    ktw = chunks(nkv[..., 4 * w:5 * w], kc_w).transpose(0, 1, 3, 2)
    vw = chunks(nkv[..., 5 * w:6 * w], kc_w)
    whole = lambda a: pl.BlockSpec((1,) + a.shape[1:], lambda b, j: (b,) + (0,) * (a.ndim - 1))
    mt = _cmp_to_sel_matrix(n_cmp, n_blk)
    return pl.pallas_call(
        _nsa_prompt_body,
        grid=(B, T // Q_BLOCK),
        in_specs=[
            pl.BlockSpec((1, Q_BLOCK, nq.shape[-1]), lambda b, j: (b, j, 0)),
            pl.BlockSpec((1, Q_BLOCK, gates.shape[-1]), lambda b, j: (b, j, 0)),
            whole(kct), whole(vc2),
            pl.BlockSpec(mt.shape, lambda b, j: (0, 0)),
            whole(kts), whole(vs), whole(ktw), whole(vw),
        ],
        out_specs=pl.BlockSpec((1, Q_BLOCK, nq.shape[-1]), lambda b, j: (b, j, 0)),
        out_shape=jax.ShapeDtypeStruct(nq.shape, f32),
        compiler_params=pltpu.CompilerParams(dimension_semantics=("arbitrary", "arbitrary"),
                                             vmem_limit_bytes=NSA_VMEM_BYTES),
        name="nsa_prompt",
    )(nq, gates, kct, vc2, mt, kts, vs, ktw, vw)


def nsa_decode(q, gates, past_len, pool_cmp, pool_slc, page_table, new_cmp, new_slc, win_rows, win_pos0,
               cmp_pe, cmp_w1, cmp_b1, cmp_w2):
    B, S, _ = q.shape
    G, R, d = NSA_KV_HEADS, NSA_GROUP, NSA_HEAD_DIM
    T = past_len + S
    scale = d ** -0.5
    q_pos = past_len + jnp.arange(S)
    qg = q.reshape(B, S, G, R, d).transpose(0, 2, 3, 1, 4)
    gg = gates.reshape(B, S, G, R, 3).transpose(0, 2, 3, 1, 4)

    n_seg = T // CMP_STRIDE
    rows = pool_cmp[page_table].reshape(B, past_len, 2, G, d)
    if n_seg * CMP_STRIDE > past_len:
        rows = jnp.concatenate([rows, new_cmp], axis=1)
    seg = rows[:, :n_seg * CMP_STRIDE].reshape(B, n_seg, CMP_STRIDE, 2, G, d)
    half = CMP_STRIDE * d

    def compress(kv):
        w1 = cmp_w1[kv]
        w_lo, w_hi = w1[:half].reshape(CMP_STRIDE, d, -1), w1[half:].reshape(CMP_STRIDE, d, -1)
        x = seg[:, :, :, kv]
        pre = (jnp.einsum('bntgd,tde->bgne', x[:, :-1], w_lo) + jnp.einsum('bntgd,tde->bgne', x[:, 1:], w_hi)
               + (cmp_pe[kv].reshape(-1) @ w1 + cmp_b1[kv]))
        return jax.nn.gelu(pre) @ cmp_w2[kv]

    kc, vc = compress(0), compress(1)
    c_end = jnp.arange(n_seg - 1) * CMP_STRIDE + (CMP_BLOCK - 1)
    p_c = masked_softmax(jnp.einsum('bgrqd,bgnd->bgrqn', qg, kc) * scale, c_end[None, :] <= q_pos[:, None])
    o_c = jnp.einsum('bgrqn,bgnd->bgrqd', p_c, vc)

    nsb = -(-T // SEL_BLOCK)
    n_past = past_len // SEL_BLOCK
    assert past_len % SEL_BLOCK == 0 and PAGE_SIZE % SEL_BLOCK == 0 and nsb - n_past <= 1
    imp = cmp_to_sel(p_c.sum(2), nsb)
    blk = jnp.arange(nsb)
    valid = blk[None, :] * SEL_BLOCK <= q_pos[:, None]
    forced = (blk[None, :] == 0) | (blk[None, :] == q_pos[:, None] // SEL_BLOCK)
    score = jnp.where(forced, FORCED_SCORE, jnp.where(valid, imp, INVALID_SCORE))
    _, idx = lax.top_k(score, min(N_SEL, nsb))
    per_page = PAGE_SIZE // SEL_BLOCK
    past = jnp.minimum(idx, n_past - 1)
    bi = jnp.arange(B)[:, None, None, None]
    gi = jnp.arange(G)[None, :, None, None]
    pool_blk = page_table[bi, past // per_page] * per_page + past % per_page
    blocks = pool_slc.reshape((-1, SEL_BLOCK) + pool_slc.shape[2:])[pool_blk]
    blocks = jnp.take_along_axis(blocks, gi[..., None, None, None, None], axis=6)[..., 0, :]
    tail = jnp.pad(new_slc, ((0, 0), (0, SEL_BLOCK - S), (0, 0), (0, 0), (0, 0)))
    tail = tail.transpose(0, 3, 1, 2, 4)[:, :, None, None]
    blocks = jnp.where((idx >= n_past)[..., None, None, None], tail, blocks)
    kg = blocks[..., 0, :].reshape(B, G, S, -1, d)
    vg = blocks[..., 1, :].reshape(B, G, S, -1, d)
    k_pos = (idx[..., None] * SEL_BLOCK + jnp.arange(SEL_BLOCK)).reshape(B, G, S, -1)
    p_s = masked_softmax(jnp.einsum('bgrqd,bgqsd->bgrqs', qg, kg) * scale,
                         (k_pos <= q_pos[None, None, :, None])[:, :, None])
    o_s = jnp.einsum('bgrqs,bgqsd->bgrqd', p_s, vg)

    kw = jnp.swapaxes(win_rows[:, :, 0], 1, 2)
    vw = jnp.swapaxes(win_rows[:, :, 1], 1, 2)
    w_pos = win_pos0 + jnp.arange(kw.shape[2])
    wmask = ((w_pos[None, :] <= q_pos[:, None]) & (w_pos[None, :] >= q_pos[:, None] - WINDOW)
             & (w_pos[None, :] >= 0))
    p_w = masked_softmax(jnp.einsum('bgrqd,bgkd->bgrqk', qg, kw) * scale, wmask)
    o_w = jnp.einsum('bgrqk,bgkd->bgrqd', p_w, vw)
    o = gg[..., 0:1] * o_c + gg[..., 1:2] * o_s + gg[..., 2:3] * o_w
    return o.transpose(0, 3, 1, 2, 4).reshape(B, S, NSA_HEADS * d)


PEER_PICKS = PEER_HEADS * PEER_TOPK
PEER_TOKENS_PER_STEP = 64
PEER_GROUP = 4
PEER_FETCH_AHEAD = 2
PEER_ROW_BUFFERS = 4 * PEER_GROUP
PEER_SLAB_LANES = LANES
PEER_SLAB_ROWS = 2 * D_MODEL // PEER_SLAB_LANES


def _gelu_tanh(x):
    return 0.5 * x * (1.0 + jnp.tanh(0.7978845608028654 * (x + 0.044715 * x * x * x)))


def _peer_expert_body(ids_ref, ids_next_ref, x_ref, g_ref, seg_ref, uv_ref, o_ref, rows, sems):
    tokens = x_ref.shape[0]
    depth = PEER_ROW_BUFFERS
    half, lanes = PEER_SLAB_ROWS // 2, PEER_SLAB_LANES
    cols = PEER_PICKS * half
    seg = seg_ref.shape[0]
    nt = (((1,), (1,)), ((), ()))
    bf16 = jnp.bfloat16

    col_row = lax.broadcasted_iota(jnp.int32, (half, cols), 1) & (half - 1)
    sub = lax.broadcasted_iota(jnp.int32, (half, cols), 0)
    diag = jnp.where(col_row == sub, 1.0, 0.0)

    group = PEER_GROUP
    groups = tokens // group
    n_seg = cols // seg
    step_id = pl.program_id(0)
    last_step = pl.num_programs(0) - 1

    def wait(slot):
        pltpu.make_async_copy(uv_ref.at[pl.ds(0, PEER_PICKS)], rows.at[slot], sems.at[slot]).wait()

    def fetcher(ids, t0, slot0):
        per = PEER_PICKS // 2

        def fetch(c):
            j, h = divmod(c, 2)
            for k in range(h * per, (h + 1) * per):
                pltpu.make_async_copy(uv_ref.at[ids[t0 + j, k]], rows.at[slot0 + j, k],
                                      sems.at[slot0 + j]).start(priority=k % 2)
        return fetch

    def mix_group(t0, slot0, fetch):
        parts = []
        for j in range(group):
            u_rows = rows[slot0 + j, :, :half, :].reshape(cols, lanes).astype(bf16)
            prod = lax.dot_general(x_ref[t0 + j].astype(bf16), u_rows, nt, preferred_element_type=f32)
            part = jnp.sum(prod * diag, axis=0, keepdims=True)
            parts += [part[:, i * seg:(i + 1) * seg] for i in range(n_seg)]
            fetch(j)
        part = jnp.concatenate(parts, axis=0)
        hi = part.astype(bf16)
        lo = (part - hi.astype(f32)).astype(bf16)
        ones = seg_ref[...]
        act = jnp.dot(hi, ones, preferred_element_type=f32) + jnp.dot(lo, ones, preferred_element_type=f32)
        for j in range(group):
            w = g_ref[t0 + j] * _gelu_tanh(act[j * n_seg:(j + 1) * n_seg])
            w = jnp.concatenate([jnp.broadcast_to(w[i:i + 1, :], (half, seg)) for i in range(n_seg)], axis=1)
            v_rows = rows[slot0 + j, :, half:, :].reshape(cols, lanes).astype(bf16)
            o_ref[t0 + j] = jnp.dot((w * diag).astype(bf16), v_rows, preferred_element_type=f32)
            fetch(group + j)

    sets, ahead = depth // group, PEER_FETCH_AHEAD

    @pl.when(step_id == 0)
    def _():
        for a in range(ahead):
            first = fetcher(ids_ref, a * group, a * group)
            for c in range(2 * group):
                first(c)

    def sweep(it, last):
        for q in range(sets):
            g = sets * it + q
            for j in range(group):
                wait(q * group + j)
            into = ((q + ahead) % sets) * group
            if last and q + ahead >= sets:
                fetch = fetcher(ids_next_ref, (q + ahead - sets) * group, into)
            else:
                fetch = fetcher(ids_ref, (g + ahead) * group, into)
            mix_group(g * group, q * group, fetch)

    def body(it, carry):
        sweep(it, False)
        return carry

    lax.fori_loop(0, groups // sets - 1, body, 0)
    sweep(groups // sets - 1, True)

    @pl.when(step_id == last_step)
    def _():
        for j in range(ahead * group):
            wait(j)


def peer_experts(xn, ids, gates, uv):
    n, d = xn.shape
    tb = PEER_TOKENS_PER_STEP
    slab, half, lanes = PEER_SLAB_ROWS, PEER_SLAB_ROWS // 2, PEER_SLAB_LANES
    seg = 2 * lanes
    cols = PEER_PICKS * half
    assert n % tb == 0 and tb % PEER_ROW_BUFFERS == 0 and half * lanes == d and cols % seg == 0
    assert 0 < PEER_FETCH_AHEAD < PEER_ROW_BUFFERS // PEER_GROUP
    same_pick = np.arange(seg)[:, None] // half == np.arange(seg)[None, :] // half
    out = pl.pallas_call(
        _peer_expert_body,
        grid=(n // tb,),
        in_specs=[
            pl.BlockSpec((tb, PEER_PICKS), lambda i: (i, 0), memory_space=pltpu.SMEM),
            pl.BlockSpec((tb, PEER_PICKS), lambda i: (jnp.minimum(i + 1, n // tb - 1), 0), memory_space=pltpu.SMEM),
            pl.BlockSpec((tb, half, lanes), lambda i: (i, 0, 0)),
            pl.BlockSpec((tb, cols // seg, seg), lambda i: (i, 0, 0)),
            pl.BlockSpec((seg, seg), lambda i: (0, 0)),
            pl.BlockSpec(memory_space=pl.ANY),
        ],
        out_specs=pl.BlockSpec((tb, half, lanes), lambda i: (i, 0, 0)),
        out_shape=jax.ShapeDtypeStruct((n, half, lanes), f32),
        scratch_shapes=[
            pltpu.VMEM((PEER_ROW_BUFFERS, PEER_PICKS, slab, lanes), f32),
            pltpu.SemaphoreType.DMA((PEER_ROW_BUFFERS,)),
        ],
        compiler_params=pltpu.CompilerParams(dimension_semantics=("arbitrary",)),
        name="peer_experts",
    )(ids, ids, xn.reshape(n, half, lanes), jnp.repeat(gates, half, axis=1).reshape(n, cols // seg, seg),
      jnp.asarray(same_pick, jnp.bfloat16), uv)
    return out.reshape(n, d)


PEER_ROUTE_TOKENS = 256
PEER_HALF_DIM = PEER_QUERY_DIM // 2


def _top_rows(s, row_id, k, payload=None):
    vals, picks = [], []
    sentinel = s.shape[0]
    for _ in range(k):
        m = jnp.max(s, axis=0, keepdims=True)
        first = jnp.min(jnp.where(s == m, row_id, sentinel), axis=0, keepdims=True)
        hit = row_id == first
        vals.append(m)
        if payload is None:
            picks.append(first)
        else:
            picks.append(jnp.max(jnp.where(hit, payload, -1), axis=0, keepdims=True))
        s = jnp.where(hit, -jnp.inf, s)
    return jnp.concatenate(vals, 0), jnp.concatenate(picks, 0)


def _peer_route_body(x_ref, wq_ref, sk_ref, ids_ref, gate_ref, q_scr):
    tn = x_ref.shape[0]
    q = jnp.dot(x_ref[...].astype(jnp.bfloat16), wq_ref[...], preferred_element_type=f32)
    for j in range(2 * PEER_HEADS):
        q_scr[j] = q[:, j * PEER_HALF_DIM:(j + 1) * PEER_HALF_DIM].astype(jnp.bfloat16)
    key_id = lax.broadcasted_iota(jnp.int32, (PEER_N_KEYS, tn), 0)
    n_cand = -(-sum(PEER_TOPK // (a + 1) for a in range(PEER_TOPK)) // 8) * 8
    cand_id = lax.broadcasted_iota(jnp.int32, (n_cand, tn), 0)

    def head(h, carry):
        tops = []
        for p in range(2):
            s = lax.dot_general(sk_ref[2 * h + p], q_scr[2 * h + p], (((1,), (1,)), ((), ())),
                                preferred_element_type=f32)
            tops.append(_top_rows(s, key_id, PEER_TOPK))
        (v0, i0), (v1, i1) = tops
        cs, ce = [], []
        for a in range(PEER_TOPK):
            nb = PEER_TOPK // (a + 1)
            cs.append(v0[a:a + 1] + v1[:nb])
            ce.append(i0[a:a + 1] * PEER_N_KEYS + i1[:nb])
        pad = n_cand - sum(c.shape[0] for c in cs)
        cand_s = jnp.concatenate(cs + [jnp.full((pad, tn), -jnp.inf, f32)], axis=0)
        cand_e = jnp.concatenate(ce + [jnp.zeros((pad, tn), jnp.int32)], axis=0)
        top_s, top_e = _top_rows(cand_s, cand_id, PEER_TOPK, payload=cand_e)
        e = jnp.exp(top_s - top_s[0:1])
        ids_ref[h] = top_e
        gate_ref[h] = e / jnp.sum(e, axis=0, keepdims=True)
        return carry

    lax.fori_loop(0, PEER_HEADS, head, 0)


def peer_route(xn, wq, sub_keys):
    n, d = xn.shape
    tn = PEER_ROUTE_TOKENS if n % PEER_ROUTE_TOKENS == 0 else PEER_ROUTE_TOKENS // 2
    assert n % tn == 0
    n_q = 2 * PEER_HEADS * PEER_HALF_DIM
    sk = sub_keys.reshape(2 * PEER_HEADS, PEER_N_KEYS, PEER_HALF_DIM).astype(jnp.bfloat16)
    ids_t, gates_t = pl.pallas_call(
        _peer_route_body,
        grid=(n // tn,),
        in_specs=[
            pl.BlockSpec((tn, d), lambda i: (i, 0)),
            pl.BlockSpec((d, n_q), lambda i: (0, 0)),
            pl.BlockSpec((2 * PEER_HEADS, PEER_N_KEYS, PEER_HALF_DIM), lambda i: (0, 0, 0)),
        ],
        out_specs=[
            pl.BlockSpec((PEER_HEADS, PEER_TOPK, tn), lambda i: (0, 0, i)),
            pl.BlockSpec((PEER_HEADS, PEER_TOPK, tn), lambda i: (0, 0, i)),
        ],
        out_shape=[
            jax.ShapeDtypeStruct((PEER_HEADS, PEER_TOPK, n), jnp.int32),
            jax.ShapeDtypeStruct((PEER_HEADS, PEER_TOPK, n), f32),
        ],
        scratch_shapes=[pltpu.VMEM((2 * PEER_HEADS, tn, PEER_HALF_DIM), jnp.bfloat16)],
        compiler_params=pltpu.CompilerParams(dimension_semantics=("arbitrary",)),
        name="peer_route",
    )(xn, wq.astype(jnp.bfloat16), sk)
    to_rows = lambda a: a.reshape(PEER_PICKS, n).T
    return to_rows(ids_t), to_rows(gates_t)


def kernel(x_prompt, x_sample, cache_cmp_kv, cache_slc_kv, cache_win_kv, state_mlstm_c, state_mlstm_n,
           state_mlstm_m, cache_mem_kv, page_table, mem_prompt, g_mix, w_in, b_in, b_forget, ml_head_gain,
           cmp_pe, cmp_w1, cmp_b1, cmp_w2, w_out, g_xattn, g_mem, w_xq, w_xk, w_xv, w_xo, g_ffn, peer_wq,
           peer_sub_keys, peer_u, peer_v, g_final):
    B, T = x_prompt.shape[:2]
    DB, S = x_sample.shape[:2]
    past_len = page_table.shape[1] * PAGE_SIZE
    l = 0

    o_i, o_nq, o_nkv, o_ng = IN_OFFSETS[3], IN_OFFSETS[5], IN_OFFSETS[6], IN_OFFSETS[7]
    n_small = 2 * ML_HEADS + 3 * NSA_HEADS
    small_pad = LANES - n_small
    regroup = lambda a: jnp.concatenate(
        [a[..., :o_i], a[..., o_nq:o_nkv], a[..., o_nkv:o_ng], a[..., o_i:o_nq], a[..., o_ng:],
         jnp.zeros(a.shape[:-1] + (small_pad,), a.dtype)], axis=-1)
    w_in_g, b_in_g = regroup(w_in[l]), regroup(b_in[l])
    in_splits = (4 * ML_WIDTH, NSA_WIDTH, 6 * NSA_KV_LANES, n_small + small_pad)

    def project(x):
        b, t, _ = x.shape
        ml, nq, nkv, small = fused_linear(x.reshape(b * t, D_MODEL), w_in_g, pre_gain=g_mix[l], bias=b_in_g,
                                          splits=in_splits)
        ng = jax.nn.sigmoid(small[:, 2 * ML_HEADS:n_small]).reshape(b, t, 3 * NSA_HEADS)
        return (ml.reshape(b, t, 4 * ML_WIDTH), small.reshape(b, t, -1), nq.reshape(b, t, NSA_WIDTH),
                nkv.reshape(b, t, 6 * NSA_KV_LANES), ng)

    def after_mixers(x, h_ml, h_nsa, mem_kv):
        b, t, _ = x.shape
        h = jnp.concatenate([h_ml, h_nsa], -1).reshape(b * t, D_MODEL)
        x1, xn = fused_linear(h, w_out[l], residual=x.reshape(b * t, D_MODEL), post_gain=g_xattn[l])
        q = fused_linear(xn, w_xq[l]).reshape(b, t, D_MODEL)
        t_pad = -t % 8
        o = cross_attention(jnp.pad(q, ((0, 0), (0, t_pad), (0, 0))), mem_kv)[:, :t]
        return fused_linear(o.reshape(b * t, D_MODEL), w_xo[l], residual=x1, post_gain=g_ffn[l])

    ml, small, nq, nkv, ng = project(x_prompt)
    h_ml, p_c, p_n, p_m = mlstm_pallas(
        ml, small, b_forget[l], ml_head_gain[l],
        jnp.zeros((B, ML_HEADS, ML_HEAD_DIM, ML_HEAD_DIM), f32),
        jnp.zeros((B, ML_HEADS, ML_HEAD_DIM), f32),
        jnp.full((B, ML_HEADS), -jnp.inf, f32))
    rows6 = nkv.reshape(B, T, 6, NSA_KV_HEADS, NSA_HEAD_DIM)
    p_cmp, p_slc, win_rows = rows6[:, :, 0:2], rows6[:, :, 2:4], rows6[:, :, 4:6]
    seg_w = CMP_STRIDE * 2 * NSA_KV_LANES
    kc, vc = compress_rows(nkv[..., :2 * NSA_KV_LANES].reshape(B, T // CMP_STRIDE, seg_w),
                           cmp_pe[l], cmp_w1[l], cmp_b1[l], cmp_w2[l])
    h_nsa = nsa_prompt(nq, ng, nkv, kc, vc)
    p_mem = fused_linear(mem_prompt.reshape(-1, D_MODEL), jnp.concatenate([w_xk[l], w_xv[l]], axis=1),
                         pre_gain=g_mem[l]).reshape(B, MEM_LEN, 2, MEM_HEADS, MEM_HEAD_DIM)
    xp, xp_ffn_in = after_mixers(x_prompt, h_ml, h_nsa, p_mem)
    p_win = win_rows[:, T - min(WINDOW, T):]

    ml, small, nq, nkv, ng = project(x_sample)
    t_pad = ML_CHUNK - S
    ml_p = jnp.pad(ml, ((0, 0), (0, t_pad), (0, 0)))
    small_p = jnp.pad(small, ((0, 0), (0, t_pad), (0, 0)))
    small_p = small_p.at[:, S:, :ML_HEADS].set(-jnp.inf).at[:, S:, ML_HEADS:2 * ML_HEADS].set(jnp.inf)
    h_ml, s_c, s_n, s_m = mlstm_pallas(ml_p, small_p, b_forget[l], ml_head_gain[l],
                                       state_mlstm_c[l], state_mlstm_n[l], state_mlstm_m[l])
    h_ml = h_ml[:, :S]
    rows6 = nkv.reshape(DB, S, 6, NSA_KV_HEADS, NSA_HEAD_DIM)
    s_cmp, s_slc, win_rows = rows6[:, :, 0:2], rows6[:, :, 2:4], rows6[:, :, 4:6]
    win_buf = cache_win_kv[l]
    win_ext = jnp.concatenate([win_buf, win_rows], axis=1)
    h_nsa = nsa_decode(nq, ng, past_len, cache_cmp_kv[l], cache_slc_kv[l], page_table, s_cmp, s_slc,
                       win_ext, past_len - win_buf.shape[1], cmp_pe[l], cmp_w1[l], cmp_b1[l], cmp_w2[l])
    xs, xs_ffn_in = after_mixers(x_sample, h_ml, h_nsa, cache_mem_kv[l])
    w_keep = min(WINDOW, past_len + S)
    s_win = win_ext[:, win_ext.shape[1] - w_keep:]

    n_p = B * T
    uv = jnp.concatenate([peer_u[l], peer_v[l]], axis=1).reshape(-1, PEER_SLAB_ROWS, PEER_SLAB_LANES)
    routes = [peer_route(x, peer_wq[l], peer_sub_keys[l]) for x in (xp_ffn_in, xs_ffn_in)]
    ids, gates = (jnp.concatenate(a, 0) for a in zip(*routes))
    ffn = peer_experts(jnp.concatenate([xp_ffn_in, xs_ffn_in], 0), ids, gates, uv)
    y_prompt = add_rmsnorm(xp, ffn[:n_p], g_final).reshape(x_prompt.shape)
    y_sample = add_rmsnorm(xs, ffn[n_p:], g_final).reshape(x_sample.shape)
    st = lambda a: a[None]
    return (y_prompt, y_sample,
            st(p_cmp), st(p_slc), st(p_win), st(p_c), st(p_n), st(p_m), st(p_mem),
            st(s_cmp), st(s_slc), st(s_win), st(s_c), st(s_n), st(s_m))
```

```python
import functools

import jax
import jax.numpy as jnp
from jax import lax
import numpy as np
from jax.experimental import pallas as pl
from jax.experimental.pallas import tpu as pltpu

D_MODEL = 1024
PAGE_SIZE = 128

ML_WIDTH = D_MODEL // 2
ML_HEADS = 4
ML_HEAD_DIM = ML_WIDTH // ML_HEADS
ML_CHUNK = 128
NSA_WIDTH = D_MODEL - ML_WIDTH
NSA_HEADS = 8
NSA_HEAD_DIM = NSA_WIDTH // NSA_HEADS
NSA_KV_HEADS = 2
NSA_GROUP = NSA_HEADS // NSA_KV_HEADS
CMP_BLOCK = 32
CMP_STRIDE = 16
SEL_BLOCK = 64
N_SEL = 16
WINDOW = 512
Q_BLOCK = 128
FORCED_SCORE = 1.0e4
INVALID_SCORE = -1.0
MEM_LEN = 256
MEM_HEADS = 4
MEM_HEAD_DIM = D_MODEL // MEM_HEADS
PEER_HEADS = 8
PEER_N_KEYS = 128
PEER_TOPK = 16
PEER_QUERY_DIM = 256
EPS = 1e-6
TINY = 1e-30
IN_SPLITS = (ML_WIDTH, ML_WIDTH, ML_WIDTH, ML_WIDTH, ML_HEADS, ML_HEADS,
             NSA_WIDTH, 6 * NSA_KV_HEADS * NSA_HEAD_DIM, 3 * NSA_HEADS)
IN_OFFSETS = tuple(int(o) for o in np.cumsum(IN_SPLITS)[:-1])

f32 = jnp.float32

LANES = 128
ROW_TILE = 512
WIDE_OUTPUT = 2048
DENSE_VMEM_BYTES = 56 * 1024 * 1024


def _rms(x, g):
    return x * lax.rsqrt(jnp.mean(x * x, -1, keepdims=True) + EPS) * g


def _add_norm_body(x_ref, r_ref, g_ref, o_ref):
    o_ref[...] = _rms(x_ref[...] + r_ref[...], g_ref[...])


def add_rmsnorm(x, r, g):
    shape = x.shape
    d = shape[-1]
    x2, r2 = x.reshape(-1, d), r.reshape(-1, d)
    n = x2.shape[0]
    tm = min(n, ROW_TILE)
    rows = pl.BlockSpec((tm, d), lambda i: (i, 0))
    out = pl.pallas_call(
        _add_norm_body,
        grid=(n // tm,),
        in_specs=[rows, rows, pl.BlockSpec((1, d), lambda i: (0, 0))],
        out_specs=rows,
        out_shape=jax.ShapeDtypeStruct((n, d), f32),
        name="add_rmsnorm",
    )(x2, r2, g.reshape(1, d))
    return out.reshape(shape)


def _linear_body(*refs, pre_norm, has_bias, has_res, post_norm, splits):
    it = iter(refs)
    x_ref, w_ref = next(it), next(it)
    x = x_ref[...]
    if pre_norm:
        x = _rms(x, next(it)[...])
    y = jnp.dot(x.astype(jnp.bfloat16), w_ref[...], preferred_element_type=f32)
    if has_bias:
        y = y + next(it)[...]
    if has_res:
        y = y + next(it)[...]
    post_gain = next(it)[...] if post_norm else None
    off = 0
    for m in splits:
        next(it)[...] = y[:, off:off + m]
        off += m
    if post_norm:
        next(it)[...] = _rms(y, post_gain)


def fused_linear(x, w, *, pre_gain=None, bias=None, residual=None, post_gain=None, splits=None):
    n, k = x.shape
    m = w.shape[1]
    splits = (m,) if splits is None else tuple(splits)
    assert sum(splits) == m and (post_gain is None or len(splits) == 1)
    tm = min(n, ROW_TILE if m <= WIDE_OUTPUT else ROW_TILE // 2)
    assert n % tm == 0
    row = lambda c: pl.BlockSpec((tm, c), lambda i: (i, 0))
    const = lambda r, c: pl.BlockSpec((r, c), lambda i: (0, 0))
    args, specs = [x, w.astype(jnp.bfloat16)], [row(k), const(k, m)]
    if pre_gain is not None:
        args.append(pre_gain.reshape(1, k)); specs.append(const(1, k))
    if bias is not None:
        args.append(bias.reshape(1, m)); specs.append(const(1, m))
    if residual is not None:
        args.append(residual); specs.append(row(m))
    if post_gain is not None:
        args.append(post_gain.reshape(1, m)); specs.append(const(1, m))
    out_cols = splits + ((m,) if post_gain is not None else ())
    outs = pl.pallas_call(
        functools.partial(_linear_body, pre_norm=pre_gain is not None, has_bias=bias is not None,
                          has_res=residual is not None, post_norm=post_gain is not None, splits=splits),
        grid=(n // tm,),
        in_specs=specs,
        out_specs=[row(c) for c in out_cols],
        out_shape=[jax.ShapeDtypeStruct((n, c), f32) for c in out_cols],
        compiler_params=pltpu.CompilerParams(dimension_semantics=("arbitrary",),
                                             vmem_limit_bytes=DENSE_VMEM_BYTES),
        name="fused_linear",
    )(*args)
    return outs[0] if len(outs) == 1 else tuple(outs)


def _xattn_body(q_ref, kv_ref, o_ref):
    bf16 = jnp.bfloat16
    d = MEM_HEAD_DIM
    q = q_ref[0]
    kv = kv_ref[0].astype(bf16)
    for h in range(MEM_HEADS):
        k_h = kv[:, h * d:(h + 1) * d]
        v_h = kv[:, (MEM_HEADS + h) * d:(MEM_HEADS + h + 1) * d]
        s = lax.dot_general(q[:, h * d:(h + 1) * d].astype(bf16), k_h, (((1,), (1,)), ((), ())),
                            preferred_element_type=f32) * (d ** -0.5)
        e = jnp.exp(s - jnp.max(s, axis=-1, keepdims=True))
        p = e / jnp.sum(e, axis=-1, keepdims=True)
        o_ref[0, :, h * d:(h + 1) * d] = jnp.dot(p.astype(bf16), v_h, preferred_element_type=f32)


def cross_attention(q, mem_kv):
    B, T, w = q.shape
    M = mem_kv.shape[1]
    kv = mem_kv.reshape(B, M, 2 * w)
    tm = min(T, ROW_TILE)
    assert T % tm == 0 and tm % 8 == 0
    return pl.pallas_call(
        _xattn_body,
        grid=(B, T // tm),
        in_specs=[pl.BlockSpec((1, tm, w), lambda b, i: (b, i, 0)),
                  pl.BlockSpec((1, M, 2 * w), lambda b, i: (b, 0, 0))],
        out_specs=pl.BlockSpec((1, tm, w), lambda b, i: (b, i, 0)),
        out_shape=jax.ShapeDtypeStruct((B, T, w), f32),
        compiler_params=pltpu.CompilerParams(dimension_semantics=("arbitrary", "arbitrary")),
        name="cross_attention",
    )(q, kv)


def masked_softmax(s, mask):
    s = jnp.where(mask, s.astype(f32), -jnp.inf)
    mx = jnp.max(s, -1, keepdims=True)
    mx = jnp.where(jnp.isfinite(mx), mx, 0.0)
    e = jnp.exp(s - mx)
    return e / jnp.maximum(e.sum(-1, keepdims=True), TINY)


def _log_sigmoid(x):
    return jnp.minimum(x, 0.0) - jnp.log1p(jnp.exp(-jnp.abs(x)))


def _cumsum_lanes(x):
    lane = lax.broadcasted_iota(jnp.int32, x.shape, 1)
    shift = 1
    while shift < x.shape[1]:
        x = x + jnp.where(lane >= shift, pltpu.roll(x, shift, axis=1), 0.0)
        shift *= 2
    return x


def _mlstm_body(ml_ref, small_ref, bias_ref, gain_ref, c0_ref, n0_ref, m0_ref, h_ref, c_ref, n_ref, m_ref):
    H, d, L = ML_HEADS, ML_HEAD_DIM, ml_ref.shape[1]
    bf16 = jnp.bfloat16
    nt = (((1,), (1,)), ((), ()))
    tn = (((0,), (0,)), ((), ()))

    @pl.when(pl.program_id(1) == 0)
    def _():
        c_ref[...] = c0_ref[...]
        n_ref[...] = n0_ref[...]
        m_ref[...] = m0_ref[...]

    blk = ml_ref[0]
    small = small_ref[0] + bias_ref[...]
    small_t = small.T
    causal = (lax.broadcasted_iota(jnp.int32, (L, L), 0) >= lax.broadcasted_iota(jnp.int32, (L, L), 1))
    for h in range(H):
        q = blk[:, h * d:(h + 1) * d]
        k = blk[:, (H + h) * d:(H + h + 1) * d] * (d ** -0.5)
        v = blk[:, (2 * H + h) * d:(2 * H + h + 1) * d]
        o_pre = blk[:, (3 * H + h) * d:(3 * H + h + 1) * d]
        ig_row, ig_col = small_t[h:h + 1, :], small[:, h:h + 1]
        b_row = _cumsum_lanes(_log_sigmoid(small_t[H + h:H + h + 1, :]))
        b_col = jnp.broadcast_to(b_row, (L, L)).T
        b_t, b_last = b_col[:, 0:1], b_row[:, L - 1:L]
        c_prev, n_prev, m_prev = c_ref[0, h], n_ref[0, h], m_ref[0, h]
        log_d = jnp.where(causal, b_col - b_row + ig_row, -jnp.inf)
        inter = b_t + m_prev
        m_t = jnp.maximum(inter, jnp.max(log_d, axis=1, keepdims=True))
        a = jnp.exp(inter - m_t)
        qb, vb = q.astype(bf16), v.astype(bf16)
        qk = lax.dot_general(qb, k.astype(bf16), nt, preferred_element_type=f32) * jnp.exp(log_d - m_t)
        num = (a * jnp.dot(qb, c_prev.astype(bf16), preferred_element_type=f32)
               + jnp.dot(qk.astype(bf16), vb, preferred_element_type=f32))
        den = a * jnp.sum(q * n_prev, axis=1, keepdims=True) + jnp.sum(qk, axis=1, keepdims=True)
        hid = num / jnp.maximum(jnp.abs(den), jnp.exp(-m_t))
        m_new = m_t[L - 1:L, :]
        kw = k * jnp.exp(b_last - b_t + ig_col - m_new)
        decay = jnp.exp(b_last + m_prev - m_new)
        c_ref[0, h] = decay * c_prev + lax.dot_general(kw.astype(bf16), vb, tn, preferred_element_type=f32)
        n_ref[0, h] = decay * n_prev + jnp.sum(kw, axis=0, keepdims=True)
        m_ref[0, h] = m_new
        gated = jax.nn.sigmoid(o_pre) * hid
        h_ref[0, :, h * d:(h + 1) * d] = _rms(gated, gain_ref[:, h * d:(h + 1) * d])


def mlstm_pallas(ml, small, b_forget, head_gain, c0, n0, m0):
    B, T, _ = ml.shape
    H, d, L = ML_HEADS, ML_HEAD_DIM, ML_CHUNK
    assert T % L == 0
    bias = jnp.zeros((1, small.shape[-1]), f32).at[0, H:2 * H].set(b_forget)
    state = lambda *s: pl.BlockSpec((1,) + s, lambda b, i: (b,) + (0,) * len(s))
    h, c, n, m = pl.pallas_call(
        _mlstm_body,
        grid=(B, T // L),
        in_specs=[
            pl.BlockSpec((1, L, ml.shape[-1]), lambda b, i: (b, i, 0)),
            pl.BlockSpec((1, L, small.shape[-1]), lambda b, i: (b, i, 0)),
            pl.BlockSpec((1, small.shape[-1]), lambda b, i: (0, 0)),
            pl.BlockSpec((1, H * d), lambda b, i: (0, 0)),
            state(H, d, d), state(H, 1, d), state(H, 1, 1),
        ],
        out_specs=[pl.BlockSpec((1, L, H * d), lambda b, i: (b, i, 0)),
                   state(H, d, d), state(H, 1, d), state(H, 1, 1)],
        out_shape=[jax.ShapeDtypeStruct((B, T, H * d), f32), jax.ShapeDtypeStruct((B, H, d, d), f32),
                   jax.ShapeDtypeStruct((B, H, 1, d), f32), jax.ShapeDtypeStruct((B, H, 1, 1), f32)],
        compiler_params=pltpu.CompilerParams(dimension_semantics=("arbitrary", "arbitrary")),
        name="mlstm",
    )(ml, small, bias, head_gain.reshape(1, H * d), c0, n0.reshape(B, H, 1, d), m0.reshape(B, H, 1, 1))
    return h, c, n.reshape(B, H, d), m.reshape(B, H)


def compress_rows(seg, cmp_pe, cmp_w1, cmp_b1, cmp_w2):
    B, n_seg, width = seg.shape
    G, d, t = NSA_KV_HEADS, NSA_HEAD_DIM, CMP_STRIDE
    e = cmp_w1.shape[-1]
    w1 = cmp_w1.reshape(2, 2, t, d, e)
    eye_kv, eye_g = jnp.eye(2, dtype=f32), jnp.eye(G, dtype=f32)
    w_big = jnp.einsum('kptde,kK,gG->tkgdpKGe', w1, eye_kv, eye_g).reshape(width, 2 * 2 * G * e)
    ab = fused_linear(seg.reshape(B * n_seg, width), w_big).reshape(B, n_seg, 2, 2, G, e)
    const = jnp.einsum('kr,kre->ke', cmp_pe.reshape(2, -1), cmp_w1) + cmp_b1
    pre = ab[:, :-1, 0] + ab[:, 1:, 1] + const[:, None, :]
    out = jnp.einsum('bnkge,kef->kbgnf', jax.nn.gelu(pre), cmp_w2)
    return out[0], out[1]


def cmp_to_sel(imp, nsb):
    r = SEL_BLOCK // CMP_STRIDE
    nc = imp.shape[-1]
    lead = imp.shape[:-1]
    tot = r * nsb
    padw = [(0, 0)] * len(lead)
    first = jnp.pad(imp, padw + [(0, tot - nc)]).reshape(lead + (nsb, r)).sum(-1)
    second = jnp.pad(imp, padw + [(1, tot - nc - 1)]).reshape(lead + (nsb, r)).sum(-1)
    return 0.5 * (first + second)


NSA_SEL_CHUNK = 1024
NSA_WIN_CHUNKS = WINDOW // Q_BLOCK + 1
NSA_KV_LANES = NSA_KV_HEADS * NSA_HEAD_DIM
NSA_VMEM_BYTES = 48 * 1024 * 1024
SEL_SHIFT = SEL_BLOCK.bit_length() - 1


def _softmax_masked(s, mask):
    sm = jnp.where(mask, s, -jnp.inf)
    mx = jnp.max(sm, axis=-1, keepdims=True)
    mx = jnp.where(mx == -jnp.inf, 0.0, mx)
    e = jnp.exp(sm - mx)
    return e * (1.0 / jnp.maximum(jnp.sum(e, axis=-1, keepdims=True), TINY))


def _top_rows_mask(s, row_id, k):
    sel = jnp.zeros(s.shape, f32)
    sentinel = s.shape[0]
    for _ in range(k):
        m = jnp.max(s, axis=0, keepdims=True)
        first = jnp.min(jnp.where(s == m, row_id, sentinel), axis=0, keepdims=True)
        hit = row_id == first
        sel = jnp.where(hit, 1.0, sel)
        s = jnp.where(hit, -jnp.inf, s)
    return sel


def _nsa_prompt_body(q_ref, gate_ref, kct_ref, vc_ref, mt_ref, kts_ref, vs_ref, ktw_ref, vw_ref, o_ref):
    j = pl.program_id(1)
    Q, R, d = Q_BLOCK, NSA_GROUP, NSA_HEAD_DIM
    n_cmp = kct_ref.shape[-1]
    n_blk = mt_ref.shape[0]
    bf16 = jnp.bfloat16
    q_all = q_ref[0] * (d ** -0.5)
    gates = gate_ref[0]
    q_pos = j * Q + lax.broadcasted_iota(jnp.int32, (Q, 1), 0)
    q_pos_l = j * Q + lax.broadcasted_iota(jnp.int32, (1, Q), 1)
    blk_r = lax.broadcasted_iota(jnp.int32, (n_blk, 1), 0)
    blk_id = lax.broadcasted_iota(jnp.int32, (n_blk, Q), 0)
    nt = (((1,), (1,)), ((), ()))

    for g in range(NSA_KV_HEADS):
        lanes = slice(g * d, (g + 1) * d)
        qg = jnp.concatenate([q_all[:, (g * R + r) * d:(g * R + r + 1) * d] for r in range(R)], axis=0).astype(bf16)

        s = jnp.dot(qg, kct_ref[0, g], preferred_element_type=f32).reshape(R, Q, n_cmp)
        n_id = lax.broadcasted_iota(jnp.int32, (1, n_cmp), 1)
        cmask = (n_id * CMP_STRIDE + (CMP_BLOCK - 1) <= q_pos) & (n_id < n_cmp - 1)
        p = _softmax_masked(s, cmask[None])
        o_c = jnp.dot(p.reshape(R * Q, n_cmp).astype(bf16), vc_ref[0], preferred_element_type=f32)[:, lanes]

        p_sum = p[0] + p[1] + p[2] + p[3]
        hi = p_sum.astype(bf16)
        lo = (p_sum - hi.astype(f32)).astype(bf16)
        mt = mt_ref[...]
        imp_t = (lax.dot_general(mt, hi, nt, preferred_element_type=f32)
                 + lax.dot_general(mt, lo, nt, preferred_element_type=f32))
        valid = blk_r * SEL_BLOCK <= q_pos_l
        forced = (blk_r == 0) | (blk_r == jnp.right_shift(q_pos_l, SEL_SHIFT))
        score = jnp.where(forced, FORCED_SCORE, jnp.where(valid, imp_t, INVALID_SCORE))
        sel = _top_rows_mask(score, blk_id, min(N_SEL, n_blk)).T.astype(bf16)

        kc = kts_ref.shape[-1]

        def chunk(c, carry):
            m, l, acc = carry
            kt = kts_ref[0, c, g * d:(g + 1) * d, :]
            sc = jnp.dot(qg, kt, preferred_element_type=f32).reshape(R, Q, kc)
            key = c * kc + lax.broadcasted_iota(jnp.int32, (1, kc), 1)
            expand = jnp.where(blk_r == jnp.right_shift(key, SEL_SHIFT), 1.0, 0.0).astype(bf16)
            picked = jnp.dot(sel, expand, preferred_element_type=f32) > 0.5
            mask = picked & (key <= q_pos)
            sm = jnp.where(mask[None], sc, -jnp.inf)
            m_new = jnp.maximum(m, jnp.max(sm, axis=-1, keepdims=True))
            m_safe = jnp.where(m_new == -jnp.inf, 0.0, m_new)
            alpha = jnp.exp(m - m_safe)
            pe = jnp.exp(sm - m_safe)
            l = l * alpha + jnp.sum(pe, axis=-1, keepdims=True)
            pv = jnp.dot(pe.reshape(R * Q, kc).astype(bf16), vs_ref[0, c], preferred_element_type=f32)
            acc = acc * alpha.reshape(R * Q, 1) + pv
            return m_new, l, acc

        n_chunks = (j * Q + Q + kc - 1) // kc
        init = (jnp.full((R, Q, 1), -jnp.inf, f32), jnp.zeros((R, Q, 1), f32),
                jnp.zeros((R * Q, NSA_KV_LANES), f32))
        _, l_s, acc_s = lax.fori_loop(0, n_chunks, chunk, init)
        o_s = acc_s[:, lanes] * (1.0 / jnp.maximum(l_s.reshape(R * Q, 1), TINY))

        ss, vv = [], []
        for i in range(NSA_WIN_CHUNKS):
            cc = jnp.maximum(j - (NSA_WIN_CHUNKS - 1) + i, 0)
            ss.append(jnp.dot(qg, ktw_ref[0, cc, g * d:(g + 1) * d, :], preferred_element_type=f32))
            vv.append(vw_ref[0, cc])
        span = NSA_WIN_CHUNKS * Q
        sw = jnp.concatenate(ss, axis=1).reshape(R, Q, span)
        k_pos = (j - (NSA_WIN_CHUNKS - 1)) * Q + lax.broadcasted_iota(jnp.int32, (1, span), 1)
        wmask = (k_pos >= 0) & (k_pos <= q_pos) & (k_pos >= q_pos - WINDOW)
        pw = _softmax_masked(sw, wmask[None])
        o_w = jnp.dot(pw.reshape(R * Q, span).astype(bf16), jnp.concatenate(vv, axis=0),
                      preferred_element_type=f32)[:, lanes]

        for r in range(R):
            h = g * R + r
            rows = slice(r * Q, (r + 1) * Q)
            o_ref[0, :, h * d:(h + 1) * d] = (gates[:, 3 * h:3 * h + 1] * o_c[rows]
                                              + gates[:, 3 * h + 1:3 * h + 2] * o_s[rows]
                                              + gates[:, 3 * h + 2:3 * h + 3] * o_w[rows])


def _cmp_to_sel_matrix(n_cmp, n_blk):
    r = SEL_BLOCK // CMP_STRIDE
    n = np.arange(n_cmp)
    b = np.arange(n_blk)[:, None]
    m = 0.5 * ((n // r == b).astype(np.float32) + ((n + 1) // r == b).astype(np.float32))
    m[:, n_cmp - 1] = 0.0
    return jnp.asarray(m, jnp.bfloat16)


def nsa_prompt(nq, gates, nkv, kc, vc):
    B, T, _ = nq.shape
    bf16 = jnp.bfloat16
    G, d, w = NSA_KV_HEADS, NSA_HEAD_DIM, NSA_KV_LANES
    n_cmp, n_blk = T // CMP_STRIDE, T // SEL_BLOCK
    kc_s, kc_w = NSA_SEL_CHUNK, Q_BLOCK
    pad = ((0, 0), (0, 0), (0, 1), (0, 0))
    kct = jnp.pad(kc, pad).transpose(0, 1, 3, 2).astype(bf16)
    vc2 = jnp.pad(vc, pad).transpose(0, 2, 1, 3).reshape(B, n_cmp, w).astype(bf16)
    chunks = lambda a, c: a.reshape(B, T // c, c, w).astype(bf16)
    kts = chunks(nkv[..., 2 * w:3 * w], kc_s).transpose(0, 1, 3, 2)
    vs = chunks(nkv[..., 3 * w:4 * w], kc_s)
    ktw = chunks(nkv[..., 4 * w:5 * w], kc_w).transpose(0, 1, 3, 2)
    vw = chunks(nkv[..., 5 * w:6 * w], kc_w)
    whole = lambda a: pl.BlockSpec((1,) + a.shape[1:], lambda b, j: (b,) + (0,) * (a.ndim - 1))
    mt = _cmp_to_sel_matrix(n_cmp, n_blk)
    return pl.pallas_call(
        _nsa_prompt_body,
        grid=(B, T // Q_BLOCK),
        in_specs=[
            pl.BlockSpec((1, Q_BLOCK, nq.shape[-1]), lambda b, j: (b, j, 0)),
            pl.BlockSpec((1, Q_BLOCK, gates.shape[-1]), lambda b, j: (b, j, 0)),
            whole(kct), whole(vc2),
            pl.BlockSpec(mt.shape, lambda b, j: (0, 0)),
            whole(kts), whole(vs), whole(ktw), whole(vw),
        ],
        out_specs=pl.BlockSpec((1, Q_BLOCK, nq.shape[-1]), lambda b, j: (b, j, 0)),
        out_shape=jax.ShapeDtypeStruct(nq.shape, f32),
        compiler_params=pltpu.CompilerParams(dimension_semantics=("arbitrary", "arbitrary"),
                                             vmem_limit_bytes=NSA_VMEM_BYTES),
        name="nsa_prompt",
    )(nq, gates, kct, vc2, mt, kts, vs, ktw, vw)


def nsa_decode(q, gates, past_len, pool_cmp, pool_slc, page_table, new_cmp, new_slc, win_rows, win_pos0,
               cmp_pe, cmp_w1, cmp_b1, cmp_w2):
    B, S, _ = q.shape
    G, R, d = NSA_KV_HEADS, NSA_GROUP, NSA_HEAD_DIM
    T = past_len + S
    scale = d ** -0.5
    q_pos = past_len + jnp.arange(S)
    qg = q.reshape(B, S, G, R, d).transpose(0, 2, 3, 1, 4)
    gg = gates.reshape(B, S, G, R, 3).transpose(0, 2, 3, 1, 4)

    n_seg = T // CMP_STRIDE
    rows = pool_cmp[page_table].reshape(B, past_len, 2, G, d)
    if n_seg * CMP_STRIDE > past_len:
        rows = jnp.concatenate([rows, new_cmp], axis=1)
    seg = rows[:, :n_seg * CMP_STRIDE].reshape(B, n_seg, CMP_STRIDE, 2, G, d)
    half = CMP_STRIDE * d

    def compress(kv):
        w1 = cmp_w1[kv]
        w_lo, w_hi = w1[:half].reshape(CMP_STRIDE, d, -1), w1[half:].reshape(CMP_STRIDE, d, -1)
        x = seg[:, :, :, kv]
        pre = (jnp.einsum('bntgd,tde->bgne', x[:, :-1], w_lo) + jnp.einsum('bntgd,tde->bgne', x[:, 1:], w_hi)
               + (cmp_pe[kv].reshape(-1) @ w1 + cmp_b1[kv]))
        return jax.nn.gelu(pre) @ cmp_w2[kv]

    kc, vc = compress(0), compress(1)
    c_end = jnp.arange(n_seg - 1) * CMP_STRIDE + (CMP_BLOCK - 1)
    p_c = masked_softmax(jnp.einsum('bgrqd,bgnd->bgrqn', qg, kc) * scale, c_end[None, :] <= q_pos[:, None])
    o_c = jnp.einsum('bgrqn,bgnd->bgrqd', p_c, vc)

    nsb = -(-T // SEL_BLOCK)
    n_past = past_len // SEL_BLOCK
    assert past_len % SEL_BLOCK == 0 and PAGE_SIZE % SEL_BLOCK == 0 and nsb - n_past <= 1
    imp = cmp_to_sel(p_c.sum(2), nsb)
    blk = jnp.arange(nsb)
    valid = blk[None, :] * SEL_BLOCK <= q_pos[:, None]
    forced = (blk[None, :] == 0) | (blk[None, :] == q_pos[:, None] // SEL_BLOCK)
    score = jnp.where(forced, FORCED_SCORE, jnp.where(valid, imp, INVALID_SCORE))
    _, idx = lax.top_k(score, min(N_SEL, nsb))
    per_page = PAGE_SIZE // SEL_BLOCK
    past = jnp.minimum(idx, n_past - 1)
    bi = jnp.arange(B)[:, None, None, None]
    gi = jnp.arange(G)[None, :, None, None]
    pool_blk = page_table[bi, past // per_page] * per_page + past % per_page
    blocks = pool_slc.reshape((-1, SEL_BLOCK) + pool_slc.shape[2:])[pool_blk]
    blocks = jnp.take_along_axis(blocks, gi[..., None, None, None, None], axis=6)[..., 0, :]
    tail = jnp.pad(new_slc, ((0, 0), (0, SEL_BLOCK - S), (0, 0), (0, 0), (0, 0)))
    tail = tail.transpose(0, 3, 1, 2, 4)[:, :, None, None]
    blocks = jnp.where((idx >= n_past)[..., None, None, None], tail, blocks)
    kg = blocks[..., 0, :].reshape(B, G, S, -1, d)
    vg = blocks[..., 1, :].reshape(B, G, S, -1, d)
    k_pos = (idx[..., None] * SEL_BLOCK + jnp.arange(SEL_BLOCK)).reshape(B, G, S, -1)
    p_s = masked_softmax(jnp.einsum('bgrqd,bgqsd->bgrqs', qg, kg) * scale,
                         (k_pos <= q_pos[None, None, :, None])[:, :, None])
    o_s = jnp.einsum('bgrqs,bgqsd->bgrqd', p_s, vg)

    kw = jnp.swapaxes(win_rows[:, :, 0], 1, 2)
    vw = jnp.swapaxes(win_rows[:, :, 1], 1, 2)
    w_pos = win_pos0 + jnp.arange(kw.shape[2])
    wmask = ((w_pos[None, :] <= q_pos[:, None]) & (w_pos[None, :] >= q_pos[:, None] - WINDOW)
             & (w_pos[None, :] >= 0))
    p_w = masked_softmax(jnp.einsum('bgrqd,bgkd->bgrqk', qg, kw) * scale, wmask)
    o_w = jnp.einsum('bgrqk,bgkd->bgrqd', p_w, vw)
    o = gg[..., 0:1] * o_c + gg[..., 1:2] * o_s + gg[..., 2:3] * o_w
    return o.transpose(0, 3, 1, 2, 4).reshape(B, S, NSA_HEADS * d)


PEER_PICKS = PEER_HEADS * PEER_TOPK
PEER_TOKENS_PER_STEP = 64
PEER_GROUP = 4
PEER_FETCH_AHEAD = 2
PEER_ROW_BUFFERS = 4 * PEER_GROUP
PEER_SLAB_LANES = LANES
PEER_SLAB_ROWS = 2 * D_MODEL // PEER_SLAB_LANES


def _gelu_tanh(x):
    return 0.5 * x * (1.0 + jnp.tanh(0.7978845608028654 * (x + 0.044715 * x * x * x)))


def _peer_expert_body(ids_ref, ids_next_ref, x_ref, g_ref, seg_ref, uv_ref, o_ref, rows, sems):
    tokens = x_ref.shape[0]
    depth = PEER_ROW_BUFFERS
    half, lanes = PEER_SLAB_ROWS // 2, PEER_SLAB_LANES
    cols = PEER_PICKS * half
    seg = seg_ref.shape[0]
    nt = (((1,), (1,)), ((), ()))
    bf16 = jnp.bfloat16

    col_row = lax.broadcasted_iota(jnp.int32, (half, cols), 1) & (half - 1)
    sub = lax.broadcasted_iota(jnp.int32, (half, cols), 0)
    diag = jnp.where(col_row == sub, 1.0, 0.0)

    group = PEER_GROUP
    groups = tokens // group
    n_seg = cols // seg
    step_id = pl.program_id(0)
    last_step = pl.num_programs(0) - 1

    def wait(slot):
        pltpu.make_async_copy(uv_ref.at[pl.ds(0, PEER_PICKS)], rows.at[slot], sems.at[slot]).wait()

    def fetcher(ids, t0, slot0):
        per = PEER_PICKS // 2

        def fetch(c):
            j, h = divmod(c, 2)
            for k in range(h * per, (h + 1) * per):
                pltpu.make_async_copy(uv_ref.at[ids[t0 + j, k]], rows.at[slot0 + j, k],
                                      sems.at[slot0 + j]).start(priority=k % 2)
        return fetch

    def mix_group(t0, slot0, fetch):
        parts = []
        for j in range(group):
            u_rows = rows[slot0 + j, :, :half, :].reshape(cols, lanes).astype(bf16)
            prod = lax.dot_general(x_ref[t0 + j].astype(bf16), u_rows, nt, preferred_element_type=f32)
            part = jnp.sum(prod * diag, axis=0, keepdims=True)
            parts += [part[:, i * seg:(i + 1) * seg] for i in range(n_seg)]
            fetch(j)
        part = jnp.concatenate(parts, axis=0)
        hi = part.astype(bf16)
        lo = (part - hi.astype(f32)).astype(bf16)
        ones = seg_ref[...]
        act = jnp.dot(hi, ones, preferred_element_type=f32) + jnp.dot(lo, ones, preferred_element_type=f32)
        for j in range(group):
            w = g_ref[t0 + j] * _gelu_tanh(act[j * n_seg:(j + 1) * n_seg])
            w = jnp.concatenate([jnp.broadcast_to(w[i:i + 1, :], (half, seg)) for i in range(n_seg)], axis=1)
            v_rows = rows[slot0 + j, :, half:, :].reshape(cols, lanes).astype(bf16)
            o_ref[t0 + j] = jnp.dot((w * diag).astype(bf16), v_rows, preferred_element_type=f32)
            fetch(group + j)

    sets, ahead = depth // group, PEER_FETCH_AHEAD

    @pl.when(step_id == 0)
    def _():
        for a in range(ahead):
            first = fetcher(ids_ref, a * group, a * group)
            for c in range(2 * group):
                first(c)

    def sweep(it, last):
        for q in range(sets):
            g = sets * it + q
            for j in range(group):
                wait(q * group + j)
            into = ((q + ahead) % sets) * group
            if last and q + ahead >= sets:
                fetch = fetcher(ids_next_ref, (q + ahead - sets) * group, into)
            else:
                fetch = fetcher(ids_ref, (g + ahead) * group, into)
            mix_group(g * group, q * group, fetch)

    def body(it, carry):
        sweep(it, False)
        return carry

    lax.fori_loop(0, groups // sets - 1, body, 0)
    sweep(groups // sets - 1, True)

    @pl.when(step_id == last_step)
    def _():
        for j in range(ahead * group):
            wait(j)


def peer_experts(xn, ids, gates, uv):
    n, d = xn.shape
    tb = PEER_TOKENS_PER_STEP
    slab, half, lanes = PEER_SLAB_ROWS, PEER_SLAB_ROWS // 2, PEER_SLAB_LANES
    seg = 2 * lanes
    cols = PEER_PICKS * half
    assert n % tb == 0 and tb % PEER_ROW_BUFFERS == 0 and half * lanes == d and cols % seg == 0
    assert 0 < PEER_FETCH_AHEAD < PEER_ROW_BUFFERS // PEER_GROUP
    same_pick = np.arange(seg)[:, None] // half == np.arange(seg)[None, :] // half
    out = pl.pallas_call(
        _peer_expert_body,
        grid=(n // tb,),
        in_specs=[
            pl.BlockSpec((tb, PEER_PICKS), lambda i: (i, 0), memory_space=pltpu.SMEM),
            pl.BlockSpec((tb, PEER_PICKS), lambda i: (jnp.minimum(i + 1, n // tb - 1), 0), memory_space=pltpu.SMEM),
            pl.BlockSpec((tb, half, lanes), lambda i: (i, 0, 0)),
            pl.BlockSpec((tb, cols // seg, seg), lambda i: (i, 0, 0)),
            pl.BlockSpec((seg, seg), lambda i: (0, 0)),
            pl.BlockSpec(memory_space=pl.ANY),
        ],
        out_specs=pl.BlockSpec((tb, half, lanes), lambda i: (i, 0, 0)),
        out_shape=jax.ShapeDtypeStruct((n, half, lanes), f32),
        scratch_shapes=[
            pltpu.VMEM((PEER_ROW_BUFFERS, PEER_PICKS, slab, lanes), f32),
            pltpu.SemaphoreType.DMA((PEER_ROW_BUFFERS,)),
        ],
        compiler_params=pltpu.CompilerParams(dimension_semantics=("arbitrary",)),
        name="peer_experts",
    )(ids, ids, xn.reshape(n, half, lanes), jnp.repeat(gates, half, axis=1).reshape(n, cols // seg, seg),
      jnp.asarray(same_pick, jnp.bfloat16), uv)
    return out.reshape(n, d)


PEER_ROUTE_TOKENS = 512
PEER_HALF_DIM = PEER_QUERY_DIM // 2


def _top_rows(s, row_id, k, payload=None):
    vals, picks = [], []
    sentinel = s.shape[0]
    for _ in range(k):
        m = jnp.max(s, axis=0, keepdims=True)
        first = jnp.min(jnp.where(s == m, row_id, sentinel), axis=0, keepdims=True)
        hit = row_id == first
        vals.append(m)
        if payload is None:
            picks.append(first)
        else:
            picks.append(jnp.max(jnp.where(hit, payload, -1), axis=0, keepdims=True))
        s = jnp.where(hit, -jnp.inf, s)
    return jnp.concatenate(vals, 0), jnp.concatenate(picks, 0)


def _peer_route_body(x_ref, wq_ref, sk_ref, ids_ref, gate_ref, q_scr):
    tn = x_ref.shape[0]
    q = jnp.dot(x_ref[...].astype(jnp.bfloat16), wq_ref[...], preferred_element_type=f32)
    for j in range(2 * PEER_HEADS):
        q_scr[j] = q[:, j * PEER_HALF_DIM:(j + 1) * PEER_HALF_DIM].astype(jnp.bfloat16)
    key_id = lax.broadcasted_iota(jnp.int32, (PEER_N_KEYS, tn), 0)
    n_cand = -(-sum(PEER_TOPK // (a + 1) for a in range(PEER_TOPK)) // 8) * 8
    cand_id = lax.broadcasted_iota(jnp.int32, (n_cand, tn), 0)

    def head(h, carry):
        tops = []
        for p in range(2):
            s = lax.dot_general(sk_ref[2 * h + p], q_scr[2 * h + p], (((1,), (1,)), ((), ())),
                                preferred_element_type=f32)
            tops.append(_top_rows(s, key_id, PEER_TOPK))
        (v0, i0), (v1, i1) = tops
        cs, ce = [], []
        for a in range(PEER_TOPK):
            nb = PEER_TOPK // (a + 1)
            cs.append(v0[a:a + 1] + v1[:nb])
            ce.append(i0[a:a + 1] * PEER_N_KEYS + i1[:nb])
        pad = n_cand - sum(c.shape[0] for c in cs)
        cand_s = jnp.concatenate(cs + [jnp.full((pad, tn), -jnp.inf, f32)], axis=0)
        cand_e = jnp.concatenate(ce + [jnp.zeros((pad, tn), jnp.int32)], axis=0)
        top_s, top_e = _top_rows(cand_s, cand_id, PEER_TOPK, payload=cand_e)
        e = jnp.exp(top_s - top_s[0:1])
        ids_ref[h] = top_e
        gate_ref[h] = e / jnp.sum(e, axis=0, keepdims=True)
        return carry

    lax.fori_loop(0, PEER_HEADS, head, 0)


def peer_route(xn, wq, sub_keys):
    n, d = xn.shape
    tn = PEER_ROUTE_TOKENS
    while n % tn:
        tn //= 2
    assert tn % LANES == 0
    n_q = 2 * PEER_HEADS * PEER_HALF_DIM
    sk = sub_keys.reshape(2 * PEER_HEADS, PEER_N_KEYS, PEER_HALF_DIM).astype(jnp.bfloat16)
    ids_t, gates_t = pl.pallas_call(
        _peer_route_body,
        grid=(n // tn,),
        in_specs=[
            pl.BlockSpec((tn, d), lambda i: (i, 0)),
            pl.BlockSpec((d, n_q), lambda i: (0, 0)),
            pl.BlockSpec((2 * PEER_HEADS, PEER_N_KEYS, PEER_HALF_DIM), lambda i: (0, 0, 0)),
        ],
        out_specs=[
            pl.BlockSpec((PEER_HEADS, PEER_TOPK, tn), lambda i: (0, 0, i)),
            pl.BlockSpec((PEER_HEADS, PEER_TOPK, tn), lambda i: (0, 0, i)),
        ],
        out_shape=[
            jax.ShapeDtypeStruct((PEER_HEADS, PEER_TOPK, n), jnp.int32),
            jax.ShapeDtypeStruct((PEER_HEADS, PEER_TOPK, n), f32),
        ],
        scratch_shapes=[pltpu.VMEM((2 * PEER_HEADS, tn, PEER_HALF_DIM), jnp.bfloat16)],
        compiler_params=pltpu.CompilerParams(dimension_semantics=("arbitrary",)),
        name="peer_route",
    )(xn, wq.astype(jnp.bfloat16), sk)
    to_rows = lambda a: a.reshape(PEER_PICKS, n).T
    return to_rows(ids_t), to_rows(gates_t)


def kernel(x_prompt, x_sample, cache_cmp_kv, cache_slc_kv, cache_win_kv, state_mlstm_c, state_mlstm_n,
           state_mlstm_m, cache_mem_kv, page_table, mem_prompt, g_mix, w_in, b_in, b_forget, ml_head_gain,
           cmp_pe, cmp_w1, cmp_b1, cmp_w2, w_out, g_xattn, g_mem, w_xq, w_xk, w_xv, w_xo, g_ffn, peer_wq,
           peer_sub_keys, peer_u, peer_v, g_final):
    B, T = x_prompt.shape[:2]
    DB, S = x_sample.shape[:2]
    past_len = page_table.shape[1] * PAGE_SIZE
    l = 0

    o_i, o_nq, o_nkv, o_ng = IN_OFFSETS[3], IN_OFFSETS[5], IN_OFFSETS[6], IN_OFFSETS[7]
    n_small = 2 * ML_HEADS + 3 * NSA_HEADS
    small_pad = LANES - n_small
    regroup = lambda a: jnp.concatenate(
        [a[..., :o_i], a[..., o_nq:o_nkv], a[..., o_nkv:o_ng], a[..., o_i:o_nq], a[..., o_ng:],
         jnp.zeros(a.shape[:-1] + (small_pad,), a.dtype)], axis=-1)
    w_in_g, b_in_g = regroup(w_in[l]), regroup(b_in[l])
    in_splits = (4 * ML_WIDTH, NSA_WIDTH, 6 * NSA_KV_LANES, n_small + small_pad)

    def project(x):
        b, t, _ = x.shape
        ml, nq, nkv, small = fused_linear(x.reshape(b * t, D_MODEL), w_in_g, pre_gain=g_mix[l], bias=b_in_g,
                                          splits=in_splits)
        ng = jax.nn.sigmoid(small[:, 2 * ML_HEADS:n_small]).reshape(b, t, 3 * NSA_HEADS)
        return (ml.reshape(b, t, 4 * ML_WIDTH), small.reshape(b, t, -1), nq.reshape(b, t, NSA_WIDTH),
                nkv.reshape(b, t, 6 * NSA_KV_LANES), ng)

    def after_mixers(x, h_ml, h_nsa, mem_kv):
        b, t, _ = x.shape
        h = jnp.concatenate([h_ml, h_nsa], -1).reshape(b * t, D_MODEL)
        x1, xn = fused_linear(h, w_out[l], residual=x.reshape(b * t, D_MODEL), post_gain=g_xattn[l])
        q = fused_linear(xn, w_xq[l]).reshape(b, t, D_MODEL)
        t_pad = -t % 8
        o = cross_attention(jnp.pad(q, ((0, 0), (0, t_pad), (0, 0))), mem_kv)[:, :t]
        return fused_linear(o.reshape(b * t, D_MODEL), w_xo[l], residual=x1, post_gain=g_ffn[l])

    ml, small, nq, nkv, ng = project(x_prompt)
    h_ml, p_c, p_n, p_m = mlstm_pallas(
        ml, small, b_forget[l], ml_head_gain[l],
        jnp.zeros((B, ML_HEADS, ML_HEAD_DIM, ML_HEAD_DIM), f32),
        jnp.zeros((B, ML_HEADS, ML_HEAD_DIM), f32),
        jnp.full((B, ML_HEADS), -jnp.inf, f32))
    rows6 = nkv.reshape(B, T, 6, NSA_KV_HEADS, NSA_HEAD_DIM)
    p_cmp, p_slc, win_rows = rows6[:, :, 0:2], rows6[:, :, 2:4], rows6[:, :, 4:6]
    seg_w = CMP_STRIDE * 2 * NSA_KV_LANES
    kc, vc = compress_rows(nkv[..., :2 * NSA_KV_LANES].reshape(B, T // CMP_STRIDE, seg_w),
                           cmp_pe[l], cmp_w1[l], cmp_b1[l], cmp_w2[l])
    h_nsa = nsa_prompt(nq, ng, nkv, kc, vc)
    p_mem = fused_linear(mem_prompt.reshape(-1, D_MODEL), jnp.concatenate([w_xk[l], w_xv[l]], axis=1),
                         pre_gain=g_mem[l]).reshape(B, MEM_LEN, 2, MEM_HEADS, MEM_HEAD_DIM)
    xp, xp_ffn_in = after_mixers(x_prompt, h_ml, h_nsa, p_mem)
    p_win = win_rows[:, T - min(WINDOW, T):]

    ml, small, nq, nkv, ng = project(x_sample)
    t_pad = ML_CHUNK - S
    ml_p = jnp.pad(ml, ((0, 0), (0, t_pad), (0, 0)))
    small_p = jnp.pad(small, ((0, 0), (0, t_pad), (0, 0)))
    small_p = small_p.at[:, S:, :ML_HEADS].set(-jnp.inf).at[:, S:, ML_HEADS:2 * ML_HEADS].set(jnp.inf)
    h_ml, s_c, s_n, s_m = mlstm_pallas(ml_p, small_p, b_forget[l], ml_head_gain[l],
                                       state_mlstm_c[l], state_mlstm_n[l], state_mlstm_m[l])
    h_ml = h_ml[:, :S]
    rows6 = nkv.reshape(DB, S, 6, NSA_KV_HEADS, NSA_HEAD_DIM)
    s_cmp, s_slc, win_rows = rows6[:, :, 0:2], rows6[:, :, 2:4], rows6[:, :, 4:6]
    win_buf = cache_win_kv[l]
    win_ext = jnp.concatenate([win_buf, win_rows], axis=1)
    h_nsa = nsa_decode(nq, ng, past_len, cache_cmp_kv[l], cache_slc_kv[l], page_table, s_cmp, s_slc,
                       win_ext, past_len - win_buf.shape[1], cmp_pe[l], cmp_w1[l], cmp_b1[l], cmp_w2[l])
    xs, xs_ffn_in = after_mixers(x_sample, h_ml, h_nsa, cache_mem_kv[l])
    w_keep = min(WINDOW, past_len + S)
    s_win = win_ext[:, win_ext.shape[1] - w_keep:]

    n_p = B * T
    uv = jnp.concatenate([peer_u[l], peer_v[l]], axis=1).reshape(-1, PEER_SLAB_ROWS, PEER_SLAB_LANES)
    routes = [peer_route(x, peer_wq[l], peer_sub_keys[l]) for x in (xp_ffn_in, xs_ffn_in)]
    ids, gates = (jnp.concatenate(a, 0) for a in zip(*routes))
    ffn = peer_experts(jnp.concatenate([xp_ffn_in, xs_ffn_in], 0), ids, gates, uv)
    y_prompt = add_rmsnorm(xp, ffn[:n_p], g_final).reshape(x_prompt.shape)
    y_sample = add_rmsnorm(xs, ffn[n_p:], g_final).reshape(x_sample.shape)
    st = lambda a: a[None]
    return (y_prompt, y_sample,
            st(p_cmp), st(p_slc), st(p_win), st(p_c), st(p_n), st(p_m), st(p_mem),
            st(s_cmp), st(s_slc), st(s_win), st(s_c), st(s_n), st(s_m))
```

```python
import functools

import jax
import jax.numpy as jnp
from jax import lax
import numpy as np
from jax.experimental import pallas as pl
from jax.experimental.pallas import tpu as pltpu

D_MODEL = 1024
PAGE_SIZE = 128

ML_WIDTH = D_MODEL // 2
ML_HEADS = 4
ML_HEAD_DIM = ML_WIDTH // ML_HEADS
ML_CHUNK = 128
NSA_WIDTH = D_MODEL - ML_WIDTH
NSA_HEADS = 8
NSA_HEAD_DIM = NSA_WIDTH // NSA_HEADS
NSA_KV_HEADS = 2
NSA_GROUP = NSA_HEADS // NSA_KV_HEADS
CMP_BLOCK = 32
CMP_STRIDE = 16
SEL_BLOCK = 64
N_SEL = 16
WINDOW = 512
Q_BLOCK = 256
FORCED_SCORE = 1.0e4
INVALID_SCORE = -1.0
MEM_LEN = 256
MEM_HEADS = 4
MEM_HEAD_DIM = D_MODEL // MEM_HEADS
PEER_HEADS = 8
PEER_N_KEYS = 128
PEER_TOPK = 16
PEER_QUERY_DIM = 256
EPS = 1e-6
TINY = 1e-30
IN_SPLITS = (ML_WIDTH, ML_WIDTH, ML_WIDTH, ML_WIDTH, ML_HEADS, ML_HEADS,
             NSA_WIDTH, 6 * NSA_KV_HEADS * NSA_HEAD_DIM, 3 * NSA_HEADS)
IN_OFFSETS = tuple(int(o) for o in np.cumsum(IN_SPLITS)[:-1])

f32 = jnp.float32

LANES = 128
ROW_TILE = 512
WIDE_OUTPUT = 2048
DENSE_VMEM_BYTES = 56 * 1024 * 1024


def _rms(x, g):
    return x * lax.rsqrt(jnp.mean(x * x, -1, keepdims=True) + EPS) * g


def _add_norm_body(x_ref, r_ref, g_ref, o_ref):
    o_ref[...] = _rms(x_ref[...] + r_ref[...], g_ref[...])


def add_rmsnorm(x, r, g):
    shape = x.shape
    d = shape[-1]
    x2, r2 = x.reshape(-1, d), r.reshape(-1, d)
    n = x2.shape[0]
    tm = min(n, ROW_TILE)
    rows = pl.BlockSpec((tm, d), lambda i: (i, 0))
    out = pl.pallas_call(
        _add_norm_body,
        grid=(n // tm,),
        in_specs=[rows, rows, pl.BlockSpec((1, d), lambda i: (0, 0))],
        out_specs=rows,
        out_shape=jax.ShapeDtypeStruct((n, d), f32),
        name="add_rmsnorm",
    )(x2, r2, g.reshape(1, d))
    return out.reshape(shape)


def _linear_body(*refs, pre_norm, has_bias, has_res, post_norm, splits):
    it = iter(refs)
    x_ref, w_ref = next(it), next(it)
    x = x_ref[...]
    if pre_norm:
        x = _rms(x, next(it)[...])
    y = jnp.dot(x.astype(jnp.bfloat16), w_ref[...], preferred_element_type=f32)
    if has_bias:
        y = y + next(it)[...]
    if has_res:
        y = y + next(it)[...]
    post_gain = next(it)[...] if post_norm else None
    off = 0
    for m in splits:
        next(it)[...] = y[:, off:off + m]
        off += m
    if post_norm:
        next(it)[...] = _rms(y, post_gain)


def fused_linear(x, w, *, pre_gain=None, bias=None, residual=None, post_gain=None, splits=None):
    n, k = x.shape
    m = w.shape[1]
    splits = (m,) if splits is None else tuple(splits)
    assert sum(splits) == m and (post_gain is None or len(splits) == 1)
    tm = min(n, ROW_TILE if m <= WIDE_OUTPUT else ROW_TILE // 2)
    assert n % tm == 0
    row = lambda c: pl.BlockSpec((tm, c), lambda i: (i, 0))
    const = lambda r, c: pl.BlockSpec((r, c), lambda i: (0, 0))
    args, specs = [x, w.astype(jnp.bfloat16)], [row(k), const(k, m)]
    if pre_gain is not None:
        args.append(pre_gain.reshape(1, k)); specs.append(const(1, k))
    if bias is not None:
        args.append(bias.reshape(1, m)); specs.append(const(1, m))
    if residual is not None:
        args.append(residual); specs.append(row(m))
    if post_gain is not None:
        args.append(post_gain.reshape(1, m)); specs.append(const(1, m))
    out_cols = splits + ((m,) if post_gain is not None else ())
    outs = pl.pallas_call(
        functools.partial(_linear_body, pre_norm=pre_gain is not None, has_bias=bias is not None,
                          has_res=residual is not None, post_norm=post_gain is not None, splits=splits),
        grid=(n // tm,),
        in_specs=specs,
        out_specs=[row(c) for c in out_cols],
        out_shape=[jax.ShapeDtypeStruct((n, c), f32) for c in out_cols],
        compiler_params=pltpu.CompilerParams(dimension_semantics=("arbitrary",),
                                             vmem_limit_bytes=DENSE_VMEM_BYTES),
        name="fused_linear",
    )(*args)
    return outs[0] if len(outs) == 1 else tuple(outs)


def _xattn_body(q_ref, kv_ref, o_ref):
    bf16 = jnp.bfloat16
    d = MEM_HEAD_DIM
    q = q_ref[0]
    kv = kv_ref[0].astype(bf16)
    for h in range(MEM_HEADS):
        k_h = kv[:, h * d:(h + 1) * d]
        v_h = kv[:, (MEM_HEADS + h) * d:(MEM_HEADS + h + 1) * d]
        s = lax.dot_general(q[:, h * d:(h + 1) * d].astype(bf16), k_h, (((1,), (1,)), ((), ())),
                            preferred_element_type=f32) * (d ** -0.5)
        e = jnp.exp(s - jnp.max(s, axis=-1, keepdims=True))
        p = e / jnp.sum(e, axis=-1, keepdims=True)
        o_ref[0, :, h * d:(h + 1) * d] = jnp.dot(p.astype(bf16), v_h, preferred_element_type=f32)


def cross_attention(q, mem_kv):
    B, T, w = q.shape
    M = mem_kv.shape[1]
    kv = mem_kv.reshape(B, M, 2 * w)
    tm = min(T, ROW_TILE)
    assert T % tm == 0 and tm % 8 == 0
    return pl.pallas_call(
        _xattn_body,
        grid=(B, T // tm),
        in_specs=[pl.BlockSpec((1, tm, w), lambda b, i: (b, i, 0)),
                  pl.BlockSpec((1, M, 2 * w), lambda b, i: (b, 0, 0))],
        out_specs=pl.BlockSpec((1, tm, w), lambda b, i: (b, i, 0)),
        out_shape=jax.ShapeDtypeStruct((B, T, w), f32),
        compiler_params=pltpu.CompilerParams(dimension_semantics=("arbitrary", "arbitrary")),
        name="cross_attention",
    )(q, kv)


def masked_softmax(s, mask):
    s = jnp.where(mask, s.astype(f32), -jnp.inf)
    mx = jnp.max(s, -1, keepdims=True)
    mx = jnp.where(jnp.isfinite(mx), mx, 0.0)
    e = jnp.exp(s - mx)
    return e / jnp.maximum(e.sum(-1, keepdims=True), TINY)


def _log_sigmoid(x):
    return jnp.minimum(x, 0.0) - jnp.log1p(jnp.exp(-jnp.abs(x)))


def _cumsum_lanes(x):
    lane = lax.broadcasted_iota(jnp.int32, x.shape, 1)
    shift = 1
    while shift < x.shape[1]:
        x = x + jnp.where(lane >= shift, pltpu.roll(x, shift, axis=1), 0.0)
        shift *= 2
    return x


def _mlstm_body(ml_ref, small_ref, bias_ref, gain_ref, c0_ref, n0_ref, m0_ref, h_ref, c_ref, n_ref, m_ref):
    H, d, L = ML_HEADS, ML_HEAD_DIM, ml_ref.shape[1]
    bf16 = jnp.bfloat16
    nt = (((1,), (1,)), ((), ()))
    tn = (((0,), (0,)), ((), ()))

    @pl.when(pl.program_id(1) == 0)
    def _():
        c_ref[...] = c0_ref[...]
        n_ref[...] = n0_ref[...]
        m_ref[...] = m0_ref[...]

    blk = ml_ref[0]
    small = small_ref[0] + bias_ref[...]
    small_t = small.T
    causal = (lax.broadcasted_iota(jnp.int32, (L, L), 0) >= lax.broadcasted_iota(jnp.int32, (L, L), 1))
    for h in range(H):
        q = blk[:, h * d:(h + 1) * d]
        k = blk[:, (H + h) * d:(H + h + 1) * d] * (d ** -0.5)
        v = blk[:, (2 * H + h) * d:(2 * H + h + 1) * d]
        o_pre = blk[:, (3 * H + h) * d:(3 * H + h + 1) * d]
        ig_row, ig_col = small_t[h:h + 1, :], small[:, h:h + 1]
        b_row = _cumsum_lanes(_log_sigmoid(small_t[H + h:H + h + 1, :]))
        b_col = jnp.broadcast_to(b_row, (L, L)).T
        b_t, b_last = b_col[:, 0:1], b_row[:, L - 1:L]
        c_prev, n_prev, m_prev = c_ref[0, h], n_ref[0, h], m_ref[0, h]
        log_d = jnp.where(causal, b_col - b_row + ig_row, -jnp.inf)
        inter = b_t + m_prev
        m_t = jnp.maximum(inter, jnp.max(log_d, axis=1, keepdims=True))
        a = jnp.exp(inter - m_t)
        qb, vb = q.astype(bf16), v.astype(bf16)
        qk = lax.dot_general(qb, k.astype(bf16), nt, preferred_element_type=f32) * jnp.exp(log_d - m_t)
        num = (a * jnp.dot(qb, c_prev.astype(bf16), preferred_element_type=f32)
               + jnp.dot(qk.astype(bf16), vb, preferred_element_type=f32))
        den = a * jnp.sum(q * n_prev, axis=1, keepdims=True) + jnp.sum(qk, axis=1, keepdims=True)
        hid = num / jnp.maximum(jnp.abs(den), jnp.exp(-m_t))
        m_new = m_t[L - 1:L, :]
        kw = k * jnp.exp(b_last - b_t + ig_col - m_new)
        decay = jnp.exp(b_last + m_prev - m_new)
        c_ref[0, h] = decay * c_prev + lax.dot_general(kw.astype(bf16), vb, tn, preferred_element_type=f32)
        n_ref[0, h] = decay * n_prev + jnp.sum(kw, axis=0, keepdims=True)
        m_ref[0, h] = m_new
        gated = jax.nn.sigmoid(o_pre) * hid
        h_ref[0, :, h * d:(h + 1) * d] = _rms(gated, gain_ref[:, h * d:(h + 1) * d])


def mlstm_pallas(ml, small, b_forget, head_gain, c0, n0, m0):
    B, T, _ = ml.shape
    H, d, L = ML_HEADS, ML_HEAD_DIM, ML_CHUNK
    assert T % L == 0
    bias = jnp.zeros((1, small.shape[-1]), f32).at[0, H:2 * H].set(b_forget)
    state = lambda *s: pl.BlockSpec((1,) + s, lambda b, i: (b,) + (0,) * len(s))
    h, c, n, m = pl.pallas_call(
        _mlstm_body,
        grid=(B, T // L),
        in_specs=[
            pl.BlockSpec((1, L, ml.shape[-1]), lambda b, i: (b, i, 0)),
            pl.BlockSpec((1, L, small.shape[-1]), lambda b, i: (b, i, 0)),
            pl.BlockSpec((1, small.shape[-1]), lambda b, i: (0, 0)),
            pl.BlockSpec((1, H * d), lambda b, i: (0, 0)),
            state(H, d, d), state(H, 1, d), state(H, 1, 1),
        ],
        out_specs=[pl.BlockSpec((1, L, H * d), lambda b, i: (b, i, 0)),
                   state(H, d, d), state(H, 1, d), state(H, 1, 1)],
        out_shape=[jax.ShapeDtypeStruct((B, T, H * d), f32), jax.ShapeDtypeStruct((B, H, d, d), f32),
                   jax.ShapeDtypeStruct((B, H, 1, d), f32), jax.ShapeDtypeStruct((B, H, 1, 1), f32)],
        compiler_params=pltpu.CompilerParams(dimension_semantics=("arbitrary", "arbitrary")),
        name="mlstm",
    )(ml, small, bias, head_gain.reshape(1, H * d), c0, n0.reshape(B, H, 1, d), m0.reshape(B, H, 1, 1))
    return h, c, n.reshape(B, H, d), m.reshape(B, H)


def compress_rows(seg, cmp_pe, cmp_w1, cmp_b1, cmp_w2):
    B, n_seg, width = seg.shape
    G, d, t = NSA_KV_HEADS, NSA_HEAD_DIM, CMP_STRIDE
    e = cmp_w1.shape[-1]
    w1 = cmp_w1.reshape(2, 2, t, d, e)
    eye_kv, eye_g = jnp.eye(2, dtype=f32), jnp.eye(G, dtype=f32)
    w_big = jnp.einsum('kptde,kK,gG->tkgdpKGe', w1, eye_kv, eye_g).reshape(width, 2 * 2 * G * e)
    ab = fused_linear(seg.reshape(B * n_seg, width), w_big).reshape(B, n_seg, 2, 2, G, e)
    const = jnp.einsum('kr,kre->ke', cmp_pe.reshape(2, -1), cmp_w1) + cmp_b1
    pre = ab[:, :-1, 0] + ab[:, 1:, 1] + const[:, None, :]
    out = jnp.einsum('bnkge,kef->kbgnf', jax.nn.gelu(pre), cmp_w2)
    return out[0], out[1]


def cmp_to_sel(imp, nsb):
    r = SEL_BLOCK // CMP_STRIDE
    nc = imp.shape[-1]
    lead = imp.shape[:-1]
    tot = r * nsb
    padw = [(0, 0)] * len(lead)
    first = jnp.pad(imp, padw + [(0, tot - nc)]).reshape(lead + (nsb, r)).sum(-1)
    second = jnp.pad(imp, padw + [(1, tot - nc - 1)]).reshape(lead + (nsb, r)).sum(-1)
    return 0.5 * (first + second)


NSA_SEL_CHUNK = 1024
NSA_WIN_CHUNKS = WINDOW // Q_BLOCK + 1
NSA_KV_LANES = NSA_KV_HEADS * NSA_HEAD_DIM
NSA_VMEM_BYTES = 48 * 1024 * 1024
SEL_SHIFT = SEL_BLOCK.bit_length() - 1


def _softmax_masked(s, mask):
    sm = jnp.where(mask, s, -jnp.inf)
    mx = jnp.max(sm, axis=-1, keepdims=True)
    mx = jnp.where(mx == -jnp.inf, 0.0, mx)
    e = jnp.exp(sm - mx)
    return e * (1.0 / jnp.maximum(jnp.sum(e, axis=-1, keepdims=True), TINY))


def _top_rows_mask(s, row_id, k):
    sel = jnp.zeros(s.shape, f32)
    sentinel = s.shape[0]
    for _ in range(k):
        m = jnp.max(s, axis=0, keepdims=True)
        first = jnp.min(jnp.where(s == m, row_id, sentinel), axis=0, keepdims=True)
        hit = row_id == first
        sel = jnp.where(hit, 1.0, sel)
        s = jnp.where(hit, -jnp.inf, s)
    return sel


def _nsa_prompt_body(q_ref, gate_ref, kct_ref, vc_ref, mt_ref, kts_ref, vs_ref, ktw_ref, vw_ref, o_ref):
    j = pl.program_id(1)
    Q, R, d = Q_BLOCK, NSA_GROUP, NSA_HEAD_DIM
    n_cmp = kct_ref.shape[-1]
    n_blk = mt_ref.shape[0]
    bf16 = jnp.bfloat16
    q_all = q_ref[0] * (d ** -0.5)
    gates = gate_ref[0]
    q_pos = j * Q + lax.broadcasted_iota(jnp.int32, (Q, 1), 0)
    q_pos_l = j * Q + lax.broadcasted_iota(jnp.int32, (1, Q), 1)
    blk_r = lax.broadcasted_iota(jnp.int32, (n_blk, 1), 0)
    blk_id = lax.broadcasted_iota(jnp.int32, (n_blk, Q), 0)
    nt = (((1,), (1,)), ((), ()))

    for g in range(NSA_KV_HEADS):
        lanes = slice(g * d, (g + 1) * d)
        qg = jnp.concatenate([q_all[:, (g * R + r) * d:(g * R + r + 1) * d] for r in range(R)], axis=0).astype(bf16)

        s = jnp.dot(qg, kct_ref[0, g], preferred_element_type=f32).reshape(R, Q, n_cmp)
        n_id = lax.broadcasted_iota(jnp.int32, (1, n_cmp), 1)
        cmask = (n_id * CMP_STRIDE + (CMP_BLOCK - 1) <= q_pos) & (n_id < n_cmp - 1)
        p = _softmax_masked(s, cmask[None])
        o_c = jnp.dot(p.reshape(R * Q, n_cmp).astype(bf16), vc_ref[0], preferred_element_type=f32)[:, lanes]

        p_sum = p[0] + p[1] + p[2] + p[3]
        hi = p_sum.astype(bf16)
        lo = (p_sum - hi.astype(f32)).astype(bf16)
        mt = mt_ref[...]
        imp_t = (lax.dot_general(mt, hi, nt, preferred_element_type=f32)
                 + lax.dot_general(mt, lo, nt, preferred_element_type=f32))
        valid = blk_r * SEL_BLOCK <= q_pos_l
        forced = (blk_r == 0) | (blk_r == jnp.right_shift(q_pos_l, SEL_SHIFT))
        score = jnp.where(forced, FORCED_SCORE, jnp.where(valid, imp_t, INVALID_SCORE))
        sel = _top_rows_mask(score, blk_id, min(N_SEL, n_blk)).T.astype(bf16)

        kc = kts_ref.shape[-1]

        def chunk(c, carry):
            m, l, acc = carry
            kt = kts_ref[0, c, g * d:(g + 1) * d, :]
            sc = jnp.dot(qg, kt, preferred_element_type=f32).reshape(R, Q, kc)
            key = c * kc + lax.broadcasted_iota(jnp.int32, (1, kc), 1)
            expand = jnp.where(blk_r == jnp.right_shift(key, SEL_SHIFT), 1.0, 0.0).astype(bf16)
            picked = jnp.dot(sel, expand, preferred_element_type=f32) > 0.5
            mask = picked & (key <= q_pos)
            sm = jnp.where(mask[None], sc, -jnp.inf)
            m_new = jnp.maximum(m, jnp.max(sm, axis=-1, keepdims=True))
            m_safe = jnp.where(m_new == -jnp.inf, 0.0, m_new)
            alpha = jnp.exp(m - m_safe)
            pe = jnp.exp(sm - m_safe)
            l = l * alpha + jnp.sum(pe, axis=-1, keepdims=True)
            pv = jnp.dot(pe.reshape(R * Q, kc).astype(bf16), vs_ref[0, c], preferred_element_type=f32)
            acc = acc * alpha.reshape(R * Q, 1) + pv
            return m_new, l, acc

        n_chunks = (j * Q + Q + kc - 1) // kc
        init = (jnp.full((R, Q, 1), -jnp.inf, f32), jnp.zeros((R, Q, 1), f32),
                jnp.zeros((R * Q, NSA_KV_LANES), f32))
        _, l_s, acc_s = lax.fori_loop(0, n_chunks, chunk, init)
        o_s = acc_s[:, lanes] * (1.0 / jnp.maximum(l_s.reshape(R * Q, 1), TINY))

        ss, vv = [], []
        for i in range(NSA_WIN_CHUNKS):
            cc = jnp.maximum(j - (NSA_WIN_CHUNKS - 1) + i, 0)
            ss.append(jnp.dot(qg, ktw_ref[0, cc, g * d:(g + 1) * d, :], preferred_element_type=f32))
            vv.append(vw_ref[0, cc])
        span = NSA_WIN_CHUNKS * Q
        sw = jnp.concatenate(ss, axis=1).reshape(R, Q, span)
        k_pos = (j - (NSA_WIN_CHUNKS - 1)) * Q + lax.broadcasted_iota(jnp.int32, (1, span), 1)
        wmask = (k_pos >= 0) & (k_pos <= q_pos) & (k_pos >= q_pos - WINDOW)
        pw = _softmax_masked(sw, wmask[None])
        o_w = jnp.dot(pw.reshape(R * Q, span).astype(bf16), jnp.concatenate(vv, axis=0),
                      preferred_element_type=f32)[:, lanes]

        for r in range(R):
            h = g * R + r
            rows = slice(r * Q, (r + 1) * Q)
            o_ref[0, :, h * d:(h + 1) * d] = (gates[:, 3 * h:3 * h + 1] * o_c[rows]
                                              + gates[:, 3 * h + 1:3 * h + 2] * o_s[rows]
                                              + gates[:, 3 * h + 2:3 * h + 3] * o_w[rows])


def _cmp_to_sel_matrix(n_cmp, n_blk):
    r = SEL_BLOCK // CMP_STRIDE
    n = np.arange(n_cmp)
    b = np.arange(n_blk)[:, None]
    m = 0.5 * ((n // r == b).astype(np.float32) + ((n + 1) // r == b).astype(np.float32))
    m[:, n_cmp - 1] = 0.0
    return jnp.asarray(m, jnp.bfloat16)


def nsa_prompt(nq, gates, nkv, kc, vc):
    B, T, _ = nq.shape
    bf16 = jnp.bfloat16
    G, d, w = NSA_KV_HEADS, NSA_HEAD_DIM, NSA_KV_LANES
    n_cmp, n_blk = T // CMP_STRIDE, T // SEL_BLOCK
    kc_s, kc_w = NSA_SEL_CHUNK, Q_BLOCK
    pad = ((0, 0), (0, 0), (0, 1), (0, 0))
    kct = jnp.pad(kc, pad).transpose(0, 1, 3, 2).astype(bf16)
    vc2 = jnp.pad(vc, pad).transpose(0, 2, 1, 3).reshape(B, n_cmp, w).astype(bf16)
    chunks = lambda a, c: a.reshape(B, T // c, c, w).astype(bf16)
    kts = chunks(nkv[..., 2 * w:3 * w], kc_s).transpose(0, 1, 3, 2)
    vs = chunks(nkv[..., 3 * w:4 * w], kc_s)
    ktw = chunks(nkv[..., 4 * w:5 * w], kc_w).transpose(0, 1, 3, 2)
    vw = chunks(nkv[..., 5 * w:6 * w], kc_w)
    whole = lambda a: pl.BlockSpec((1,) + a.shape[1:], lambda b, j: (b,) + (0,) * (a.ndim - 1))
    mt = _cmp_to_sel_matrix(n_cmp, n_blk)
    return pl.pallas_call(
        _nsa_prompt_body,
        grid=(B, T // Q_BLOCK),
        in_specs=[
            pl.BlockSpec((1, Q_BLOCK, nq.shape[-1]), lambda b, j: (b, j, 0)),
            pl.BlockSpec((1, Q_BLOCK, gates.shape[-1]), lambda b, j: (b, j, 0)),
            whole(kct), whole(vc2),
            pl.BlockSpec(mt.shape, lambda b, j: (0, 0)),
            whole(kts), whole(vs), whole(ktw), whole(vw),
        ],
        out_specs=pl.BlockSpec((1, Q_BLOCK, nq.shape[-1]), lambda b, j: (b, j, 0)),
        out_shape=jax.ShapeDtypeStruct(nq.shape, f32),
        compiler_params=pltpu.CompilerParams(dimension_semantics=("arbitrary", "arbitrary"),
                                             vmem_limit_bytes=NSA_VMEM_BYTES),
        name="nsa_prompt",
    )(nq, gates, kct, vc2, mt, kts, vs, ktw, vw)


def nsa_decode(q, gates, past_len, pool_cmp, pool_slc, page_table, new_cmp, new_slc, win_rows, win_pos0,
               cmp_pe, cmp_w1, cmp_b1, cmp_w2):
    B, S, _ = q.shape
    G, R, d = NSA_KV_HEADS, NSA_GROUP, NSA_HEAD_DIM
    T = past_len + S
    scale = d ** -0.5
    q_pos = past_len + jnp.arange(S)
    qg = q.reshape(B, S, G, R, d).transpose(0, 2, 3, 1, 4)
    gg = gates.reshape(B, S, G, R, 3).transpose(0, 2, 3, 1, 4)

    n_seg = T // CMP_STRIDE
    rows = pool_cmp[page_table].reshape(B, past_len, 2, G, d)
    if n_seg * CMP_STRIDE > past_len:
        rows = jnp.concatenate([rows, new_cmp], axis=1)
    seg = rows[:, :n_seg * CMP_STRIDE].reshape(B, n_seg, CMP_STRIDE, 2, G, d)
    half = CMP_STRIDE * d

    def compress(kv):
        w1 = cmp_w1[kv]
        w_lo, w_hi = w1[:half].reshape(CMP_STRIDE, d, -1), w1[half:].reshape(CMP_STRIDE, d, -1)
        x = seg[:, :, :, kv]
        pre = (jnp.einsum('bntgd,tde->bgne', x[:, :-1], w_lo) + jnp.einsum('bntgd,tde->bgne', x[:, 1:], w_hi)
               + (cmp_pe[kv].reshape(-1) @ w1 + cmp_b1[kv]))
        return jax.nn.gelu(pre) @ cmp_w2[kv]

    kc, vc = compress(0), compress(1)
    c_end = jnp.arange(n_seg - 1) * CMP_STRIDE + (CMP_BLOCK - 1)
    p_c = masked_softmax(jnp.einsum('bgrqd,bgnd->bgrqn', qg, kc) * scale, c_end[None, :] <= q_pos[:, None])
    o_c = jnp.einsum('bgrqn,bgnd->bgrqd', p_c, vc)

    nsb = -(-T // SEL_BLOCK)
    n_past = past_len // SEL_BLOCK
    assert past_len % SEL_BLOCK == 0 and PAGE_SIZE % SEL_BLOCK == 0 and nsb - n_past <= 1
    imp = cmp_to_sel(p_c.sum(2), nsb)
    blk = jnp.arange(nsb)
    valid = blk[None, :] * SEL_BLOCK <= q_pos[:, None]
    forced = (blk[None, :] == 0) | (blk[None, :] == q_pos[:, None] // SEL_BLOCK)
    score = jnp.where(forced, FORCED_SCORE, jnp.where(valid, imp, INVALID_SCORE))
    _, idx = lax.top_k(score, min(N_SEL, nsb))
    per_page = PAGE_SIZE // SEL_BLOCK
    past = jnp.minimum(idx, n_past - 1)
    bi = jnp.arange(B)[:, None, None, None]
    gi = jnp.arange(G)[None, :, None, None]
    pool_blk = page_table[bi, past // per_page] * per_page + past % per_page
    blocks = pool_slc.reshape((-1, SEL_BLOCK) + pool_slc.shape[2:])[pool_blk]
    blocks = jnp.take_along_axis(blocks, gi[..., None, None, None, None], axis=6)[..., 0, :]
    tail = jnp.pad(new_slc, ((0, 0), (0, SEL_BLOCK - S), (0, 0), (0, 0), (0, 0)))
    tail = tail.transpose(0, 3, 1, 2, 4)[:, :, None, None]
    blocks = jnp.where((idx >= n_past)[..., None, None, None], tail, blocks)
    kg = blocks[..., 0, :].reshape(B, G, S, -1, d)
    vg = blocks[..., 1, :].reshape(B, G, S, -1, d)
    k_pos = (idx[..., None] * SEL_BLOCK + jnp.arange(SEL_BLOCK)).reshape(B, G, S, -1)
    p_s = masked_softmax(jnp.einsum('bgrqd,bgqsd->bgrqs', qg, kg) * scale,
                         (k_pos <= q_pos[None, None, :, None])[:, :, None])
    o_s = jnp.einsum('bgrqs,bgqsd->bgrqd', p_s, vg)

    kw = jnp.swapaxes(win_rows[:, :, 0], 1, 2)
    vw = jnp.swapaxes(win_rows[:, :, 1], 1, 2)
    w_pos = win_pos0 + jnp.arange(kw.shape[2])
    wmask = ((w_pos[None, :] <= q_pos[:, None]) & (w_pos[None, :] >= q_pos[:, None] - WINDOW)
             & (w_pos[None, :] >= 0))
    p_w = masked_softmax(jnp.einsum('bgrqd,bgkd->bgrqk', qg, kw) * scale, wmask)
    o_w = jnp.einsum('bgrqk,bgkd->bgrqd', p_w, vw)
    o = gg[..., 0:1] * o_c + gg[..., 1:2] * o_s + gg[..., 2:3] * o_w
    return o.transpose(0, 3, 1, 2, 4).reshape(B, S, NSA_HEADS * d)


PEER_PICKS = PEER_HEADS * PEER_TOPK
PEER_TOKENS_PER_STEP = 64
PEER_GROUP = 4
PEER_FETCH_AHEAD = 2
PEER_ROW_BUFFERS = 4 * PEER_GROUP
PEER_SLAB_LANES = LANES
PEER_SLAB_ROWS = 2 * D_MODEL // PEER_SLAB_LANES


def _gelu_tanh(x):
    return 0.5 * x * (1.0 + jnp.tanh(0.7978845608028654 * (x + 0.044715 * x * x * x)))


def _peer_expert_body(ids_ref, ids_next_ref, x_ref, g_ref, seg_ref, uv_ref, o_ref, rows, sems):
    tokens = x_ref.shape[0]
    depth = PEER_ROW_BUFFERS
    half, lanes = PEER_SLAB_ROWS // 2, PEER_SLAB_LANES
    cols = PEER_PICKS * half
    seg = seg_ref.shape[0]
    nt = (((1,), (1,)), ((), ()))
    bf16 = jnp.bfloat16

    col_row = lax.broadcasted_iota(jnp.int32, (half, cols), 1) & (half - 1)
    sub = lax.broadcasted_iota(jnp.int32, (half, cols), 0)
    diag = jnp.where(col_row == sub, 1.0, 0.0)

    group = PEER_GROUP
    groups = tokens // group
    n_seg = cols // seg
    step_id = pl.program_id(0)
    last_step = pl.num_programs(0) - 1

    def wait(slot):
        pltpu.make_async_copy(uv_ref.at[pl.ds(0, PEER_PICKS)], rows.at[slot], sems.at[slot]).wait()

    def fetcher(ids, t0, slot0):
        per = PEER_PICKS // 2

        def fetch(c):
            j, h = divmod(c, 2)
            for k in range(h * per, (h + 1) * per):
                pltpu.make_async_copy(uv_ref.at[ids[t0 + j, k]], rows.at[slot0 + j, k],
                                      sems.at[slot0 + j]).start(priority=k % 2)
        return fetch

    def mix_group(t0, slot0, fetch):
        parts = []
        for j in range(group):
            u_rows = rows[slot0 + j, :, :half, :].reshape(cols, lanes).astype(bf16)
            prod = lax.dot_general(x_ref[t0 + j].astype(bf16), u_rows, nt, preferred_element_type=f32)
            part = jnp.sum(prod * diag, axis=0, keepdims=True)
            parts += [part[:, i * seg:(i + 1) * seg] for i in range(n_seg)]
            fetch(j)
        part = jnp.concatenate(parts, axis=0)
        hi = part.astype(bf16)
        lo = (part - hi.astype(f32)).astype(bf16)
        ones = seg_ref[...]
        act = jnp.dot(hi, ones, preferred_element_type=f32) + jnp.dot(lo, ones, preferred_element_type=f32)
        for j in range(group):
            w = g_ref[t0 + j] * _gelu_tanh(act[j * n_seg:(j + 1) * n_seg])
            w = jnp.concatenate([jnp.broadcast_to(w[i:i + 1, :], (half, seg)) for i in range(n_seg)], axis=1)
            v_rows = rows[slot0 + j, :, half:, :].reshape(cols, lanes).astype(bf16)
            o_ref[t0 + j] = jnp.dot((w * diag).astype(bf16), v_rows, preferred_element_type=f32)
            fetch(group + j)

    sets, ahead = depth // group, PEER_FETCH_AHEAD

    @pl.when(step_id == 0)
    def _():
        for a in range(ahead):
            first = fetcher(ids_ref, a * group, a * group)
            for c in range(2 * group):
                first(c)

    def sweep(it, last):
        for q in range(sets):
            g = sets * it + q
            for j in range(group):
                wait(q * group + j)
            into = ((q + ahead) % sets) * group
            if last and q + ahead >= sets:
                fetch = fetcher(ids_next_ref, (q + ahead - sets) * group, into)
            else:
                fetch = fetcher(ids_ref, (g + ahead) * group, into)
            mix_group(g * group, q * group, fetch)

    def body(it, carry):
        sweep(it, False)
        return carry

    lax.fori_loop(0, groups // sets - 1, body, 0)
    sweep(groups // sets - 1, True)

    @pl.when(step_id == last_step)
    def _():
        for j in range(ahead * group):
            wait(j)


def peer_experts(xn, ids, gates, uv):
    n, d = xn.shape
    tb = PEER_TOKENS_PER_STEP
    slab, half, lanes = PEER_SLAB_ROWS, PEER_SLAB_ROWS // 2, PEER_SLAB_LANES
    seg = 2 * lanes
    cols = PEER_PICKS * half
    assert n % tb == 0 and tb % PEER_ROW_BUFFERS == 0 and half * lanes == d and cols % seg == 0
    assert 0 < PEER_FETCH_AHEAD < PEER_ROW_BUFFERS // PEER_GROUP
    same_pick = np.arange(seg)[:, None] // half == np.arange(seg)[None, :] // half
    out = pl.pallas_call(
        _peer_expert_body,
        grid=(n // tb,),
        in_specs=[
            pl.BlockSpec((tb, PEER_PICKS), lambda i: (i, 0), memory_space=pltpu.SMEM),
            pl.BlockSpec((tb, PEER_PICKS), lambda i: (jnp.minimum(i + 1, n // tb - 1), 0), memory_space=pltpu.SMEM),
            pl.BlockSpec((tb, half, lanes), lambda i: (i, 0, 0)),
            pl.BlockSpec((tb, cols // seg, seg), lambda i: (i, 0, 0)),
            pl.BlockSpec((seg, seg), lambda i: (0, 0)),
            pl.BlockSpec(memory_space=pl.ANY),
        ],
        out_specs=pl.BlockSpec((tb, half, lanes), lambda i: (i, 0, 0)),
        out_shape=jax.ShapeDtypeStruct((n, half, lanes), f32),
        scratch_shapes=[
            pltpu.VMEM((PEER_ROW_BUFFERS, PEER_PICKS, slab, lanes), f32),
            pltpu.SemaphoreType.DMA((PEER_ROW_BUFFERS,)),
        ],
        compiler_params=pltpu.CompilerParams(dimension_semantics=("arbitrary",)),
        name="peer_experts",
    )(ids, ids, xn.reshape(n, half, lanes), jnp.repeat(gates, half, axis=1).reshape(n, cols // seg, seg),
      jnp.asarray(same_pick, jnp.bfloat16), uv)
    return out.reshape(n, d)


PEER_ROUTE_TOKENS = 512
PEER_HALF_DIM = PEER_QUERY_DIM // 2


def _top_rows(s, row_id, k, payload=None):
    vals, picks = [], []
    sentinel = s.shape[0]
    for _ in range(k):
        m = jnp.max(s, axis=0, keepdims=True)
        first = jnp.min(jnp.where(s == m, row_id, sentinel), axis=0, keepdims=True)
        hit = row_id == first
        vals.append(m)
        if payload is None:
            picks.append(first)
        else:
            picks.append(jnp.max(jnp.where(hit, payload, -1), axis=0, keepdims=True))
        s = jnp.where(hit, -jnp.inf, s)
    return jnp.concatenate(vals, 0), jnp.concatenate(picks, 0)


def _peer_route_body(x_ref, wq_ref, sk_ref, ids_ref, gate_ref, q_scr):
    tn = x_ref.shape[0]
    q = jnp.dot(x_ref[...].astype(jnp.bfloat16), wq_ref[...], preferred_element_type=f32)
    for j in range(2 * PEER_HEADS):
        q_scr[j] = q[:, j * PEER_HALF_DIM:(j + 1) * PEER_HALF_DIM].astype(jnp.bfloat16)
    key_id = lax.broadcasted_iota(jnp.int32, (PEER_N_KEYS, tn), 0)
    n_cand = -(-sum(PEER_TOPK // (a + 1) for a in range(PEER_TOPK)) // 8) * 8
    cand_id = lax.broadcasted_iota(jnp.int32, (n_cand, tn), 0)

    def head(h, carry):
        tops = []
        for p in range(2):
            s = lax.dot_general(sk_ref[2 * h + p], q_scr[2 * h + p], (((1,), (1,)), ((), ())),
                                preferred_element_type=f32)
            tops.append(_top_rows(s, key_id, PEER_TOPK))
        (v0, i0), (v1, i1) = tops
        cs, ce = [], []
        for a in range(PEER_TOPK):
            nb = PEER_TOPK // (a + 1)
            cs.append(v0[a:a + 1] + v1[:nb])
            ce.append(i0[a:a + 1] * PEER_N_KEYS + i1[:nb])
        pad = n_cand - sum(c.shape[0] for c in cs)
        cand_s = jnp.concatenate(cs + [jnp.full((pad, tn), -jnp.inf, f32)], axis=0)
        cand_e = jnp.concatenate(ce + [jnp.zeros((pad, tn), jnp.int32)], axis=0)
        top_s, top_e = _top_rows(cand_s, cand_id, PEER_TOPK, payload=cand_e)
        e = jnp.exp(top_s - top_s[0:1])
        ids_ref[h] = top_e
        gate_ref[h] = e / jnp.sum(e, axis=0, keepdims=True)
        return carry

    lax.fori_loop(0, PEER_HEADS, head, 0)


def peer_route(xn, wq, sub_keys):
    n, d = xn.shape
    tn = PEER_ROUTE_TOKENS
    while n % tn:
        tn //= 2
    assert tn % LANES == 0
    n_q = 2 * PEER_HEADS * PEER_HALF_DIM
    sk = sub_keys.reshape(2 * PEER_HEADS, PEER_N_KEYS, PEER_HALF_DIM).astype(jnp.bfloat16)
    ids_t, gates_t = pl.pallas_call(
        _peer_route_body,
        grid=(n // tn,),
        in_specs=[
            pl.BlockSpec((tn, d), lambda i: (i, 0)),
            pl.BlockSpec((d, n_q), lambda i: (0, 0)),
            pl.BlockSpec((2 * PEER_HEADS, PEER_N_KEYS, PEER_HALF_DIM), lambda i: (0, 0, 0)),
        ],
        out_specs=[
            pl.BlockSpec((PEER_HEADS, PEER_TOPK, tn), lambda i: (0, 0, i)),
            pl.BlockSpec((PEER_HEADS, PEER_TOPK, tn), lambda i: (0, 0, i)),
        ],
        out_shape=[
            jax.ShapeDtypeStruct((PEER_HEADS, PEER_TOPK, n), jnp.int32),
            jax.ShapeDtypeStruct((PEER_HEADS, PEER_TOPK, n), f32),
        ],
        scratch_shapes=[pltpu.VMEM((2 * PEER_HEADS, tn, PEER_HALF_DIM), jnp.bfloat16)],
        compiler_params=pltpu.CompilerParams(dimension_semantics=("arbitrary",)),
        name="peer_route",
    )(xn, wq.astype(jnp.bfloat16), sk)
    to_rows = lambda a: a.reshape(PEER_PICKS, n).T
    return to_rows(ids_t), to_rows(gates_t)


def kernel(x_prompt, x_sample, cache_cmp_kv, cache_slc_kv, cache_win_kv, state_mlstm_c, state_mlstm_n,
           state_mlstm_m, cache_mem_kv, page_table, mem_prompt, g_mix, w_in, b_in, b_forget, ml_head_gain,
           cmp_pe, cmp_w1, cmp_b1, cmp_w2, w_out, g_xattn, g_mem, w_xq, w_xk, w_xv, w_xo, g_ffn, peer_wq,
           peer_sub_keys, peer_u, peer_v, g_final):
    B, T = x_prompt.shape[:2]
    DB, S = x_sample.shape[:2]
    past_len = page_table.shape[1] * PAGE_SIZE
    l = 0

    o_i, o_nq, o_nkv, o_ng = IN_OFFSETS[3], IN_OFFSETS[5], IN_OFFSETS[6], IN_OFFSETS[7]
    n_small = 2 * ML_HEADS + 3 * NSA_HEADS
    small_pad = LANES - n_small
    regroup = lambda a: jnp.concatenate(
        [a[..., :o_i], a[..., o_nq:o_nkv], a[..., o_nkv:o_ng], a[..., o_i:o_nq], a[..., o_ng:],
         jnp.zeros(a.shape[:-1] + (small_pad,), a.dtype)], axis=-1)
    w_in_g, b_in_g = regroup(w_in[l]), regroup(b_in[l])
    in_splits = (4 * ML_WIDTH, NSA_WIDTH, 6 * NSA_KV_LANES, n_small + small_pad)

    def project(x):
        b, t, _ = x.shape
        ml, nq, nkv, small = fused_linear(x.reshape(b * t, D_MODEL), w_in_g, pre_gain=g_mix[l], bias=b_in_g,
                                          splits=in_splits)
        ng = jax.nn.sigmoid(small[:, 2 * ML_HEADS:n_small]).reshape(b, t, 3 * NSA_HEADS)
        return (ml.reshape(b, t, 4 * ML_WIDTH), small.reshape(b, t, -1), nq.reshape(b, t, NSA_WIDTH),
                nkv.reshape(b, t, 6 * NSA_KV_LANES), ng)

    def after_mixers(x, h_ml, h_nsa, mem_kv):
        b, t, _ = x.shape
        h = jnp.concatenate([h_ml, h_nsa], -1).reshape(b * t, D_MODEL)
        x1, xn = fused_linear(h, w_out[l], residual=x.reshape(b * t, D_MODEL), post_gain=g_xattn[l])
        q = fused_linear(xn, w_xq[l]).reshape(b, t, D_MODEL)
        t_pad = -t % 8
        o = cross_attention(jnp.pad(q, ((0, 0), (0, t_pad), (0, 0))), mem_kv)[:, :t]
        return fused_linear(o.reshape(b * t, D_MODEL), w_xo[l], residual=x1, post_gain=g_ffn[l])

    ml, small, nq, nkv, ng = project(x_prompt)
    h_ml, p_c, p_n, p_m = mlstm_pallas(
        ml, small, b_forget[l], ml_head_gain[l],
        jnp.zeros((B, ML_HEADS, ML_HEAD_DIM, ML_HEAD_DIM), f32),
        jnp.zeros((B, ML_HEADS, ML_HEAD_DIM), f32),
        jnp.full((B, ML_HEADS), -jnp.inf, f32))
    rows6 = nkv.reshape(B, T, 6, NSA_KV_HEADS, NSA_HEAD_DIM)
    p_cmp, p_slc, win_rows = rows6[:, :, 0:2], rows6[:, :, 2:4], rows6[:, :, 4:6]
    seg_w = CMP_STRIDE * 2 * NSA_KV_LANES
    kc, vc = compress_rows(nkv[..., :2 * NSA_KV_LANES].reshape(B, T // CMP_STRIDE, seg_w),
                           cmp_pe[l], cmp_w1[l], cmp_b1[l], cmp_w2[l])
    h_nsa = nsa_prompt(nq, ng, nkv, kc, vc)
    p_mem = fused_linear(mem_prompt.reshape(-1, D_MODEL), jnp.concatenate([w_xk[l], w_xv[l]], axis=1),
                         pre_gain=g_mem[l]).reshape(B, MEM_LEN, 2, MEM_HEADS, MEM_HEAD_DIM)
    xp, xp_ffn_in = after_mixers(x_prompt, h_ml, h_nsa, p_mem)
    p_win = win_rows[:, T - min(WINDOW, T):]

    ml, small, nq, nkv, ng = project(x_sample)
    t_pad = ML_CHUNK - S
    ml_p = jnp.pad(ml, ((0, 0), (0, t_pad), (0, 0)))
    small_p = jnp.pad(small, ((0, 0), (0, t_pad), (0, 0)))
    small_p = small_p.at[:, S:, :ML_HEADS].set(-jnp.inf).at[:, S:, ML_HEADS:2 * ML_HEADS].set(jnp.inf)
    h_ml, s_c, s_n, s_m = mlstm_pallas(ml_p, small_p, b_forget[l], ml_head_gain[l],
                                       state_mlstm_c[l], state_mlstm_n[l], state_mlstm_m[l])
    h_ml = h_ml[:, :S]
    rows6 = nkv.reshape(DB, S, 6, NSA_KV_HEADS, NSA_HEAD_DIM)
    s_cmp, s_slc, win_rows = rows6[:, :, 0:2], rows6[:, :, 2:4], rows6[:, :, 4:6]
    win_buf = cache_win_kv[l]
    win_ext = jnp.concatenate([win_buf, win_rows], axis=1)
    h_nsa = nsa_decode(nq, ng, past_len, cache_cmp_kv[l], cache_slc_kv[l], page_table, s_cmp, s_slc,
                       win_ext, past_len - win_buf.shape[1], cmp_pe[l], cmp_w1[l], cmp_b1[l], cmp_w2[l])
    xs, xs_ffn_in = after_mixers(x_sample, h_ml, h_nsa, cache_mem_kv[l])
    w_keep = min(WINDOW, past_len + S)
    s_win = win_ext[:, win_ext.shape[1] - w_keep:]

    n_p = B * T
    uv = jnp.concatenate([peer_u[l], peer_v[l]], axis=1).reshape(-1, PEER_SLAB_ROWS, PEER_SLAB_LANES)
    routes = [peer_route(x, peer_wq[l], peer_sub_keys[l]) for x in (xp_ffn_in, xs_ffn_in)]
    ids, gates = (jnp.concatenate(a, 0) for a in zip(*routes))
    ffn = peer_experts(jnp.concatenate([xp_ffn_in, xs_ffn_in], 0), ids, gates, uv)
    y_prompt = add_rmsnorm(xp, ffn[:n_p], g_final).reshape(x_prompt.shape)
    y_sample = add_rmsnorm(xs, ffn[n_p:], g_final).reshape(x_sample.shape)
    st = lambda a: a[None]
    return (y_prompt, y_sample,
            st(p_cmp), st(p_slc), st(p_win), st(p_c), st(p_n), st(p_m), st(p_mem),
            st(s_cmp), st(s_slc), st(s_win), st(s_c), st(s_n), st(s_m))
```

```python
import functools

import jax
import jax.numpy as jnp
from jax import lax
import numpy as np
from jax.experimental import pallas as pl
from jax.experimental.pallas import tpu as pltpu

D_MODEL = 1024
PAGE_SIZE = 128

ML_WIDTH = D_MODEL // 2
ML_HEADS = 4
ML_HEAD_DIM = ML_WIDTH // ML_HEADS
ML_CHUNK = 128
NSA_WIDTH = D_MODEL - ML_WIDTH
NSA_HEADS = 8
NSA_HEAD_DIM = NSA_WIDTH // NSA_HEADS
NSA_KV_HEADS = 2
NSA_GROUP = NSA_HEADS // NSA_KV_HEADS
CMP_BLOCK = 32
CMP_STRIDE = 16
SEL_BLOCK = 64
N_SEL = 16
WINDOW = 512
Q_BLOCK = 256
FORCED_SCORE = 1.0e4
INVALID_SCORE = -1.0
MEM_LEN = 256
MEM_HEADS = 4
MEM_HEAD_DIM = D_MODEL // MEM_HEADS
PEER_HEADS = 8
PEER_N_KEYS = 128
PEER_TOPK = 16
PEER_QUERY_DIM = 256
EPS = 1e-6
TINY = 1e-30
IN_SPLITS = (ML_WIDTH, ML_WIDTH, ML_WIDTH, ML_WIDTH, ML_HEADS, ML_HEADS,
             NSA_WIDTH, 6 * NSA_KV_HEADS * NSA_HEAD_DIM, 3 * NSA_HEADS)
IN_OFFSETS = tuple(int(o) for o in np.cumsum(IN_SPLITS)[:-1])

f32 = jnp.float32

LANES = 128
ROW_TILE = 512
WIDE_OUTPUT = 2048
DENSE_VMEM_BYTES = 56 * 1024 * 1024


def _rms(x, g):
    return x * lax.rsqrt(jnp.mean(x * x, -1, keepdims=True) + EPS) * g


def _add_norm_body(x_ref, r_ref, g_ref, o_ref):
    o_ref[...] = _rms(x_ref[...] + r_ref[...], g_ref[...])


def add_rmsnorm(x, r, g):
    shape = x.shape
    d = shape[-1]
    x2, r2 = x.reshape(-1, d), r.reshape(-1, d)
    n = x2.shape[0]
    tm = min(n, ROW_TILE)
    rows = pl.BlockSpec((tm, d), lambda i: (i, 0))
    out = pl.pallas_call(
        _add_norm_body,
        grid=(n // tm,),
        in_specs=[rows, rows, pl.BlockSpec((1, d), lambda i: (0, 0))],
        out_specs=rows,
        out_shape=jax.ShapeDtypeStruct((n, d), f32),
        name="add_rmsnorm",
    )(x2, r2, g.reshape(1, d))
    return out.reshape(shape)


def _linear_body(*refs, pre_norm, has_bias, has_res, post_norm, splits):
    it = iter(refs)
    x_ref, w_ref = next(it), next(it)
    x = x_ref[...]
    if pre_norm:
        x = _rms(x, next(it)[...])
    y = jnp.dot(x.astype(jnp.bfloat16), w_ref[...], preferred_element_type=f32)
    if has_bias:
        y = y + next(it)[...]
    if has_res:
        y = y + next(it)[...]
    post_gain = next(it)[...] if post_norm else None
    off = 0
    for m in splits:
        next(it)[...] = y[:, off:off + m]
        off += m
    if post_norm:
        next(it)[...] = _rms(y, post_gain)


def fused_linear(x, w, *, pre_gain=None, bias=None, residual=None, post_gain=None, splits=None):
    n, k = x.shape
    m = w.shape[1]
    splits = (m,) if splits is None else tuple(splits)
    assert sum(splits) == m and (post_gain is None or len(splits) == 1)
    tm = min(n, ROW_TILE if m <= WIDE_OUTPUT else ROW_TILE // 2)
    assert n % tm == 0
    row = lambda c: pl.BlockSpec((tm, c), lambda i: (i, 0))
    const = lambda r, c: pl.BlockSpec((r, c), lambda i: (0, 0))
    args, specs = [x, w.astype(jnp.bfloat16)], [row(k), const(k, m)]
    if pre_gain is not None:
        args.append(pre_gain.reshape(1, k)); specs.append(const(1, k))
    if bias is not None:
        args.append(bias.reshape(1, m)); specs.append(const(1, m))
    if residual is not None:
        args.append(residual); specs.append(row(m))
    if post_gain is not None:
        args.append(post_gain.reshape(1, m)); specs.append(const(1, m))
    out_cols = splits + ((m,) if post_gain is not None else ())
    outs = pl.pallas_call(
        functools.partial(_linear_body, pre_norm=pre_gain is not None, has_bias=bias is not None,
                          has_res=residual is not None, post_norm=post_gain is not None, splits=splits),
        grid=(n // tm,),
        in_specs=specs,
        out_specs=[row(c) for c in out_cols],
        out_shape=[jax.ShapeDtypeStruct((n, c), f32) for c in out_cols],
        compiler_params=pltpu.CompilerParams(dimension_semantics=("arbitrary",),
                                             vmem_limit_bytes=DENSE_VMEM_BYTES),
        name="fused_linear",
    )(*args)
    return outs[0] if len(outs) == 1 else tuple(outs)


def _xattn_body(q_ref, kv_ref, o_ref):
    bf16 = jnp.bfloat16
    d = MEM_HEAD_DIM
    q = q_ref[0]
    kv = kv_ref[0].astype(bf16)
    for h in range(MEM_HEADS):
        k_h = kv[:, h * d:(h + 1) * d]
        v_h = kv[:, (MEM_HEADS + h) * d:(MEM_HEADS + h + 1) * d]
        s = lax.dot_general(q[:, h * d:(h + 1) * d].astype(bf16), k_h, (((1,), (1,)), ((), ())),
                            preferred_element_type=f32) * (d ** -0.5)
        e = jnp.exp(s - jnp.max(s, axis=-1, keepdims=True))
        p = e / jnp.sum(e, axis=-1, keepdims=True)
        o_ref[0, :, h * d:(h + 1) * d] = jnp.dot(p.astype(bf16), v_h, preferred_element_type=f32)


def cross_attention(q, mem_kv):
    B, T, w = q.shape
    M = mem_kv.shape[1]
    kv = mem_kv.reshape(B, M, 2 * w)
    tm = min(T, ROW_TILE)
    assert T % tm == 0 and tm % 8 == 0
    return pl.pallas_call(
        _xattn_body,
        grid=(B, T // tm),
        in_specs=[pl.BlockSpec((1, tm, w), lambda b, i: (b, i, 0)),
                  pl.BlockSpec((1, M, 2 * w), lambda b, i: (b, 0, 0))],
        out_specs=pl.BlockSpec((1, tm, w), lambda b, i: (b, i, 0)),
        out_shape=jax.ShapeDtypeStruct((B, T, w), f32),
        compiler_params=pltpu.CompilerParams(dimension_semantics=("arbitrary", "arbitrary")),
        name="cross_attention",
    )(q, kv)


def masked_softmax(s, mask):
    s = jnp.where(mask, s.astype(f32), -jnp.inf)
    mx = jnp.max(s, -1, keepdims=True)
    mx = jnp.where(jnp.isfinite(mx), mx, 0.0)
    e = jnp.exp(s - mx)
    return e / jnp.maximum(e.sum(-1, keepdims=True), TINY)


def _log_sigmoid(x):
    return jnp.minimum(x, 0.0) - jnp.log1p(jnp.exp(-jnp.abs(x)))


def _cumsum_lanes(x):
    lane = lax.broadcasted_iota(jnp.int32, x.shape, 1)
    shift = 1
    while shift < x.shape[1]:
        x = x + jnp.where(lane >= shift, pltpu.roll(x, shift, axis=1), 0.0)
        shift *= 2
    return x


def _mlstm_body(ml_ref, small_ref, bias_ref, gain_ref, c0_ref, n0_ref, m0_ref, h_ref, c_ref, n_ref, m_ref):
    H, d, L = ML_HEADS, ML_HEAD_DIM, ml_ref.shape[1]
    bf16 = jnp.bfloat16
    nt = (((1,), (1,)), ((), ()))
    tn = (((0,), (0,)), ((), ()))

    @pl.when(pl.program_id(1) == 0)
    def _():
        c_ref[...] = c0_ref[...]
        n_ref[...] = n0_ref[...]
        m_ref[...] = m0_ref[...]

    blk = ml_ref[0]
    small = small_ref[0] + bias_ref[...]
    small_t = small.T
    causal = (lax.broadcasted_iota(jnp.int32, (L, L), 0) >= lax.broadcasted_iota(jnp.int32, (L, L), 1))
    heads = range(H)
    q = [blk[:, h * d:(h + 1) * d] for h in heads]
    k = [blk[:, (H + h) * d:(H + h + 1) * d] * (d ** -0.5) for h in heads]
    vb = [blk[:, (2 * H + h) * d:(2 * H + h + 1) * d].astype(bf16) for h in heads]
    qb = [x.astype(bf16) for x in q]
    prev = [(c_ref[0, h], n_ref[0, h], m_ref[0, h]) for h in heads]
    b_row = [_cumsum_lanes(_log_sigmoid(small_t[H + h:H + h + 1, :])) for h in heads]
    b_col = [jnp.broadcast_to(x, (L, L)).T for x in b_row]
    log_d = [jnp.where(causal, b_col[h] - b_row[h] + small_t[h:h + 1, :], -jnp.inf) for h in heads]
    inter = [b_col[h][:, 0:1] + prev[h][2] for h in heads]
    m_t = [jnp.maximum(inter[h], jnp.max(log_d[h], axis=1, keepdims=True)) for h in heads]
    a = [jnp.exp(inter[h] - m_t[h]) for h in heads]
    qk = [lax.dot_general(qb[h], k[h].astype(bf16), nt, preferred_element_type=f32) * jnp.exp(log_d[h] - m_t[h])
          for h in heads]
    num = [a[h] * jnp.dot(qb[h], prev[h][0].astype(bf16), preferred_element_type=f32)
           + jnp.dot(qk[h].astype(bf16), vb[h], preferred_element_type=f32) for h in heads]
    den = [a[h] * jnp.sum(q[h] * prev[h][1], axis=1, keepdims=True) + jnp.sum(qk[h], axis=1, keepdims=True)
           for h in heads]
    hid = [num[h] / jnp.maximum(jnp.abs(den[h]), jnp.exp(-m_t[h])) for h in heads]
    for h in heads:
        c_prev, n_prev, m_prev = prev[h]
        b_last, m_new = b_row[h][:, L - 1:L], m_t[h][L - 1:L, :]
        kw = k[h] * jnp.exp(b_last - b_col[h][:, 0:1] + small[:, h:h + 1] - m_new)
        decay = jnp.exp(b_last + m_prev - m_new)
        c_ref[0, h] = decay * c_prev + lax.dot_general(kw.astype(bf16), vb[h], tn, preferred_element_type=f32)
        n_ref[0, h] = decay * n_prev + jnp.sum(kw, axis=0, keepdims=True)
        m_ref[0, h] = m_new
        o_pre = blk[:, (3 * H + h) * d:(3 * H + h + 1) * d]
        h_ref[0, :, h * d:(h + 1) * d] = _rms(jax.nn.sigmoid(o_pre) * hid[h], gain_ref[:, h * d:(h + 1) * d])


def mlstm_pallas(ml, small, b_forget, head_gain, c0, n0, m0):
    B, T, _ = ml.shape
    H, d, L = ML_HEADS, ML_HEAD_DIM, ML_CHUNK
    assert T % L == 0
    bias = jnp.zeros((1, small.shape[-1]), f32).at[0, H:2 * H].set(b_forget)
    state = lambda *s: pl.BlockSpec((1,) + s, lambda b, i: (b,) + (0,) * len(s))
    h, c, n, m = pl.pallas_call(
        _mlstm_body,
        grid=(B, T // L),
        in_specs=[
            pl.BlockSpec((1, L, ml.shape[-1]), lambda b, i: (b, i, 0)),
            pl.BlockSpec((1, L, small.shape[-1]), lambda b, i: (b, i, 0)),
            pl.BlockSpec((1, small.shape[-1]), lambda b, i: (0, 0)),
            pl.BlockSpec((1, H * d), lambda b, i: (0, 0)),
            state(H, d, d), state(H, 1, d), state(H, 1, 1),
        ],
        out_specs=[pl.BlockSpec((1, L, H * d), lambda b, i: (b, i, 0)),
                   state(H, d, d), state(H, 1, d), state(H, 1, 1)],
        out_shape=[jax.ShapeDtypeStruct((B, T, H * d), f32), jax.ShapeDtypeStruct((B, H, d, d), f32),
                   jax.ShapeDtypeStruct((B, H, 1, d), f32), jax.ShapeDtypeStruct((B, H, 1, 1), f32)],
        compiler_params=pltpu.CompilerParams(dimension_semantics=("arbitrary", "arbitrary")),
        name="mlstm",
    )(ml, small, bias, head_gain.reshape(1, H * d), c0, n0.reshape(B, H, 1, d), m0.reshape(B, H, 1, 1))
    return h, c, n.reshape(B, H, d), m.reshape(B, H)


def compress_rows(seg, cmp_pe, cmp_w1, cmp_b1, cmp_w2):
    B, n_seg, width = seg.shape
    G, d, t = NSA_KV_HEADS, NSA_HEAD_DIM, CMP_STRIDE
    e = cmp_w1.shape[-1]
    w1 = cmp_w1.reshape(2, 2, t, d, e)
    eye_kv, eye_g = jnp.eye(2, dtype=f32), jnp.eye(G, dtype=f32)
    w_big = jnp.einsum('kptde,kK,gG->tkgdpKGe', w1, eye_kv, eye_g).reshape(width, 2 * 2 * G * e)
    ab = fused_linear(seg.reshape(B * n_seg, width), w_big).reshape(B, n_seg, 2, 2, G, e)
    const = jnp.einsum('kr,kre->ke', cmp_pe.reshape(2, -1), cmp_w1) + cmp_b1
    pre = ab[:, :-1, 0] + ab[:, 1:, 1] + const[:, None, :]
    out = jnp.einsum('bnkge,kef->kbgnf', jax.nn.gelu(pre), cmp_w2)
    return out[0], out[1]


def cmp_to_sel(imp, nsb):
    r = SEL_BLOCK // CMP_STRIDE
    nc = imp.shape[-1]
    lead = imp.shape[:-1]
    tot = r * nsb
    padw = [(0, 0)] * len(lead)
    first = jnp.pad(imp, padw + [(0, tot - nc)]).reshape(lead + (nsb, r)).sum(-1)
    second = jnp.pad(imp, padw + [(1, tot - nc - 1)]).reshape(lead + (nsb, r)).sum(-1)
    return 0.5 * (first + second)


NSA_SEL_CHUNK = 1024
NSA_WIN_CHUNKS = WINDOW // Q_BLOCK + 1
NSA_KV_LANES = NSA_KV_HEADS * NSA_HEAD_DIM
NSA_VMEM_BYTES = 48 * 1024 * 1024
SEL_SHIFT = SEL_BLOCK.bit_length() - 1


def _softmax_masked(s, mask):
    sm = jnp.where(mask, s, -jnp.inf)
    mx = jnp.max(sm, axis=-1, keepdims=True)
    mx = jnp.where(mx == -jnp.inf, 0.0, mx)
    e = jnp.exp(sm - mx)
    return e * (1.0 / jnp.maximum(jnp.sum(e, axis=-1, keepdims=True), TINY))


def _top_rows_mask(s, row_id, k):
    sel = jnp.zeros(s.shape, f32)
    sentinel = s.shape[0]
    for _ in range(k):
        m = jnp.max(s, axis=0, keepdims=True)
        first = jnp.min(jnp.where(s == m, row_id, sentinel), axis=0, keepdims=True)
        hit = row_id == first
        sel = jnp.where(hit, 1.0, sel)
        s = jnp.where(hit, -jnp.inf, s)
    return sel


def _nsa_prompt_body(q_ref, gate_ref, kct_ref, vc_ref, mt_ref, kts_ref, vs_ref, ktw_ref, vw_ref, o_ref):
    j = pl.program_id(1)
    Q, R, d = Q_BLOCK, NSA_GROUP, NSA_HEAD_DIM
    n_cmp = kct_ref.shape[-1]
    n_blk = mt_ref.shape[0]
    bf16 = jnp.bfloat16
    q_all = q_ref[0] * (d ** -0.5)
    gates = gate_ref[0]
    q_pos = j * Q + lax.broadcasted_iota(jnp.int32, (Q, 1), 0)
    q_pos_l = j * Q + lax.broadcasted_iota(jnp.int32, (1, Q), 1)
    blk_r = lax.broadcasted_iota(jnp.int32, (n_blk, 1), 0)
    blk_id = lax.broadcasted_iota(jnp.int32, (n_blk, Q), 0)
    nt = (((1,), (1,)), ((), ()))

    for g in range(NSA_KV_HEADS):
        lanes = slice(g * d, (g + 1) * d)
        qg = jnp.concatenate([q_all[:, (g * R + r) * d:(g * R + r + 1) * d] for r in range(R)], axis=0).astype(bf16)

        s = jnp.dot(qg, kct_ref[0, g], preferred_element_type=f32).reshape(R, Q, n_cmp)
        n_id = lax.broadcasted_iota(jnp.int32, (1, n_cmp), 1)
        cmask = (n_id * CMP_STRIDE + (CMP_BLOCK - 1) <= q_pos) & (n_id < n_cmp - 1)
        p = _softmax_masked(s, cmask[None])
        o_c = jnp.dot(p.reshape(R * Q, n_cmp).astype(bf16), vc_ref[0], preferred_element_type=f32)[:, lanes]

        p_sum = p[0] + p[1] + p[2] + p[3]
        hi = p_sum.astype(bf16)
        lo = (p_sum - hi.astype(f32)).astype(bf16)
        mt = mt_ref[...]
        imp_t = (lax.dot_general(mt, hi, nt, preferred_element_type=f32)
                 + lax.dot_general(mt, lo, nt, preferred_element_type=f32))
        valid = blk_r * SEL_BLOCK <= q_pos_l
        forced = (blk_r == 0) | (blk_r == jnp.right_shift(q_pos_l, SEL_SHIFT))
        score = jnp.where(forced, FORCED_SCORE, jnp.where(valid, imp_t, INVALID_SCORE))
        sel = _top_rows_mask(score, blk_id, min(N_SEL, n_blk)).T.astype(bf16)

        kc = kts_ref.shape[-1]

        def chunk(c, carry):
            m, l, acc = carry
            kt = kts_ref[0, c, g * d:(g + 1) * d, :]
            sc = jnp.dot(qg, kt, preferred_element_type=f32).reshape(R, Q, kc)
            key = c * kc + lax.broadcasted_iota(jnp.int32, (1, kc), 1)
            expand = jnp.where(blk_r == jnp.right_shift(key, SEL_SHIFT), 1.0, 0.0).astype(bf16)
            picked = jnp.dot(sel, expand, preferred_element_type=f32) > 0.5
            mask = picked & (key <= q_pos)
            sm = jnp.where(mask[None], sc, -jnp.inf)
            m_new = jnp.maximum(m, jnp.max(sm, axis=-1, keepdims=True))
            m_safe = jnp.where(m_new == -jnp.inf, 0.0, m_new)
            alpha = jnp.exp(m - m_safe)
            pe = jnp.exp(sm - m_safe)
            l = l * alpha + jnp.sum(pe, axis=-1, keepdims=True)
            pv = jnp.dot(pe.reshape(R * Q, kc).astype(bf16), vs_ref[0, c], preferred_element_type=f32)
            acc = acc * alpha.reshape(R * Q, 1) + pv
            return m_new, l, acc

        n_chunks = (j * Q + Q + kc - 1) // kc
        init = (jnp.full((R, Q, 1), -jnp.inf, f32), jnp.zeros((R, Q, 1), f32),
                jnp.zeros((R * Q, NSA_KV_LANES), f32))
        _, l_s, acc_s = lax.fori_loop(0, n_chunks, chunk, init)
        o_s = acc_s[:, lanes] * (1.0 / jnp.maximum(l_s.reshape(R * Q, 1), TINY))

        ss, vv = [], []
        for i in range(NSA_WIN_CHUNKS):
            cc = jnp.maximum(j - (NSA_WIN_CHUNKS - 1) + i, 0)
            ss.append(jnp.dot(qg, ktw_ref[0, cc, g * d:(g + 1) * d, :], preferred_element_type=f32))
            vv.append(vw_ref[0, cc])
        span = NSA_WIN_CHUNKS * Q
        sw = jnp.concatenate(ss, axis=1).reshape(R, Q, span)
        k_pos = (j - (NSA_WIN_CHUNKS - 1)) * Q + lax.broadcasted_iota(jnp.int32, (1, span), 1)
        wmask = (k_pos >= 0) & (k_pos <= q_pos) & (k_pos >= q_pos - WINDOW)
        pw = _softmax_masked(sw, wmask[None])
        o_w = jnp.dot(pw.reshape(R * Q, span).astype(bf16), jnp.concatenate(vv, axis=0),
                      preferred_element_type=f32)[:, lanes]

        for r in range(R):
            h = g * R + r
            rows = slice(r * Q, (r + 1) * Q)
            o_ref[0, :, h * d:(h + 1) * d] = (gates[:, 3 * h:3 * h + 1] * o_c[rows]
                                              + gates[:, 3 * h + 1:3 * h + 2] * o_s[rows]
                                              + gates[:, 3 * h + 2:3 * h + 3] * o_w[rows])


def _cmp_to_sel_matrix(n_cmp, n_blk):
    r = SEL_BLOCK // CMP_STRIDE
    n = np.arange(n_cmp)
    b = np.arange(n_blk)[:, None]
    m = 0.5 * ((n // r == b).astype(np.float32) + ((n + 1) // r == b).astype(np.float32))
    m[:, n_cmp - 1] = 0.0
    return jnp.asarray(m, jnp.bfloat16)


def nsa_prompt(nq, gates, nkv, kc, vc):
    B, T, _ = nq.shape
    bf16 = jnp.bfloat16
    G, d, w = NSA_KV_HEADS, NSA_HEAD_DIM, NSA_KV_LANES
    n_cmp, n_blk = T // CMP_STRIDE, T // SEL_BLOCK
    kc_s, kc_w = NSA_SEL_CHUNK, Q_BLOCK
    pad = ((0, 0), (0, 0), (0, 1), (0, 0))
    kct = jnp.pad(kc, pad).transpose(0, 1, 3, 2).astype(bf16)
    vc2 = jnp.pad(vc, pad).transpose(0, 2, 1, 3).reshape(B, n_cmp, w).astype(bf16)
    chunks = lambda a, c: a.reshape(B, T // c, c, w).astype(bf16)
    kts = chunks(nkv[..., 2 * w:3 * w], kc_s).transpose(0, 1, 3, 2)
    vs = chunks(nkv[..., 3 * w:4 * w], kc_s)
    ktw = chunks(nkv[..., 4 * w:5 * w], kc_w).transpose(0, 1, 3, 2)
    vw = chunks(nkv[..., 5 * w:6 * w], kc_w)
    whole = lambda a: pl.BlockSpec((1,) + a.shape[1:], lambda b, j: (b,) + (0,) * (a.ndim - 1))
    mt = _cmp_to_sel_matrix(n_cmp, n_blk)
    return pl.pallas_call(
        _nsa_prompt_body,
        grid=(B, T // Q_BLOCK),
        in_specs=[
            pl.BlockSpec((1, Q_BLOCK, nq.shape[-1]), lambda b, j: (b, j, 0)),
            pl.BlockSpec((1, Q_BLOCK, gates.shape[-1]), lambda b, j: (b, j, 0)),
            whole(kct), whole(vc2),
            pl.BlockSpec(mt.shape, lambda b, j: (0, 0)),
            whole(kts), whole(vs), whole(ktw), whole(vw),
        ],
        out_specs=pl.BlockSpec((1, Q_BLOCK, nq.shape[-1]), lambda b, j: (b, j, 0)),
        out_shape=jax.ShapeDtypeStruct(nq.shape, f32),
        compiler_params=pltpu.CompilerParams(dimension_semantics=("arbitrary", "arbitrary"),
                                             vmem_limit_bytes=NSA_VMEM_BYTES),
        name="nsa_prompt",
    )(nq, gates, kct, vc2, mt, kts, vs, ktw, vw)


def nsa_decode(q, gates, past_len, pool_cmp, pool_slc, page_table, new_cmp, new_slc, win_rows, win_pos0,
               cmp_pe, cmp_w1, cmp_b1, cmp_w2):
    B, S, _ = q.shape
    G, R, d = NSA_KV_HEADS, NSA_GROUP, NSA_HEAD_DIM
    T = past_len + S
    scale = d ** -0.5
    q_pos = past_len + jnp.arange(S)
    qg = q.reshape(B, S, G, R, d).transpose(0, 2, 3, 1, 4)
    gg = gates.reshape(B, S, G, R, 3).transpose(0, 2, 3, 1, 4)

    n_seg = T // CMP_STRIDE
    rows = pool_cmp[page_table].reshape(B, past_len, 2, G, d)
    if n_seg * CMP_STRIDE > past_len:
        rows = jnp.concatenate([rows, new_cmp], axis=1)
    seg = rows[:, :n_seg * CMP_STRIDE].reshape(B, n_seg, CMP_STRIDE, 2, G, d)
    half = CMP_STRIDE * d

    def compress(kv):
        w1 = cmp_w1[kv]
        w_lo, w_hi = w1[:half].reshape(CMP_STRIDE, d, -1), w1[half:].reshape(CMP_STRIDE, d, -1)
        x = seg[:, :, :, kv]
        pre = (jnp.einsum('bntgd,tde->bgne', x[:, :-1], w_lo) + jnp.einsum('bntgd,tde->bgne', x[:, 1:], w_hi)
               + (cmp_pe[kv].reshape(-1) @ w1 + cmp_b1[kv]))
        return jax.nn.gelu(pre) @ cmp_w2[kv]

    kc, vc = compress(0), compress(1)
    c_end = jnp.arange(n_seg - 1) * CMP_STRIDE + (CMP_BLOCK - 1)
    p_c = masked_softmax(jnp.einsum('bgrqd,bgnd->bgrqn', qg, kc) * scale, c_end[None, :] <= q_pos[:, None])
    o_c = jnp.einsum('bgrqn,bgnd->bgrqd', p_c, vc)

    nsb = -(-T // SEL_BLOCK)
    n_past = past_len // SEL_BLOCK
    assert past_len % SEL_BLOCK == 0 and PAGE_SIZE % SEL_BLOCK == 0 and nsb - n_past <= 1
    imp = cmp_to_sel(p_c.sum(2), nsb)
    blk = jnp.arange(nsb)
    valid = blk[None, :] * SEL_BLOCK <= q_pos[:, None]
    forced = (blk[None, :] == 0) | (blk[None, :] == q_pos[:, None] // SEL_BLOCK)
    score = jnp.where(forced, FORCED_SCORE, jnp.where(valid, imp, INVALID_SCORE))
    _, idx = lax.top_k(score, min(N_SEL, nsb))
    per_page = PAGE_SIZE // SEL_BLOCK
    past = jnp.minimum(idx, n_past - 1)
    bi = jnp.arange(B)[:, None, None, None]
    gi = jnp.arange(G)[None, :, None, None]
    pool_blk = page_table[bi, past // per_page] * per_page + past % per_page
    blocks = pool_slc.reshape((-1, SEL_BLOCK) + pool_slc.shape[2:])[pool_blk]
    blocks = jnp.take_along_axis(blocks, gi[..., None, None, None, None], axis=6)[..., 0, :]
    tail = jnp.pad(new_slc, ((0, 0), (0, SEL_BLOCK - S), (0, 0), (0, 0), (0, 0)))
    tail = tail.transpose(0, 3, 1, 2, 4)[:, :, None, None]
    blocks = jnp.where((idx >= n_past)[..., None, None, None], tail, blocks)
    kg = blocks[..., 0, :].reshape(B, G, S, -1, d)
    vg = blocks[..., 1, :].reshape(B, G, S, -1, d)
    k_pos = (idx[..., None] * SEL_BLOCK + jnp.arange(SEL_BLOCK)).reshape(B, G, S, -1)
    p_s = masked_softmax(jnp.einsum('bgrqd,bgqsd->bgrqs', qg, kg) * scale,
                         (k_pos <= q_pos[None, None, :, None])[:, :, None])
    o_s = jnp.einsum('bgrqs,bgqsd->bgrqd', p_s, vg)

    kw = jnp.swapaxes(win_rows[:, :, 0], 1, 2)
    vw = jnp.swapaxes(win_rows[:, :, 1], 1, 2)
    w_pos = win_pos0 + jnp.arange(kw.shape[2])
    wmask = ((w_pos[None, :] <= q_pos[:, None]) & (w_pos[None, :] >= q_pos[:, None] - WINDOW)
             & (w_pos[None, :] >= 0))
    p_w = masked_softmax(jnp.einsum('bgrqd,bgkd->bgrqk', qg, kw) * scale, wmask)
    o_w = jnp.einsum('bgrqk,bgkd->bgrqd', p_w, vw)
    o = gg[..., 0:1] * o_c + gg[..., 1:2] * o_s + gg[..., 2:3] * o_w
    return o.transpose(0, 3, 1, 2, 4).reshape(B, S, NSA_HEADS * d)


PEER_PICKS = PEER_HEADS * PEER_TOPK
PEER_TOKENS_PER_STEP = 64
PEER_GROUP = 4
PEER_FETCH_AHEAD = 2
PEER_ROW_BUFFERS = 4 * PEER_GROUP
PEER_SLAB_LANES = LANES
PEER_SLAB_ROWS = 2 * D_MODEL // PEER_SLAB_LANES


def _gelu_tanh(x):
    return 0.5 * x * (1.0 + jnp.tanh(0.7978845608028654 * (x + 0.044715 * x * x * x)))


def _peer_expert_body(ids_ref, ids_next_ref, x_ref, g_ref, seg_ref, uv_ref, o_ref, rows, sems):
    tokens = x_ref.shape[0]
    depth = PEER_ROW_BUFFERS
    half, lanes = PEER_SLAB_ROWS // 2, PEER_SLAB_LANES
    cols = PEER_PICKS * half
    seg = seg_ref.shape[0]
    nt = (((1,), (1,)), ((), ()))
    bf16 = jnp.bfloat16

    col_row = lax.broadcasted_iota(jnp.int32, (half, cols), 1) & (half - 1)
    sub = lax.broadcasted_iota(jnp.int32, (half, cols), 0)
    diag = jnp.where(col_row == sub, 1.0, 0.0)

    group = PEER_GROUP
    groups = tokens // group
    n_seg = cols // seg
    step_id = pl.program_id(0)
    last_step = pl.num_programs(0) - 1

    def wait(slot):
        pltpu.make_async_copy(uv_ref.at[pl.ds(0, PEER_PICKS)], rows.at[slot], sems.at[slot]).wait()

    def fetcher(ids, t0, slot0):
        per = PEER_PICKS // 2

        def fetch(c):
            j, h = divmod(c, 2)
            for k in range(h * per, (h + 1) * per):
                pltpu.make_async_copy(uv_ref.at[ids[t0 + j, k]], rows.at[slot0 + j, k],
                                      sems.at[slot0 + j]).start(priority=k % 2)
        return fetch

    def mix_group(t0, slot0, fetch):
        parts = []
        for j in range(group):
            u_rows = rows[slot0 + j, :, :half, :].reshape(cols, lanes).astype(bf16)
            prod = lax.dot_general(x_ref[t0 + j].astype(bf16), u_rows, nt, preferred_element_type=f32)
            part = jnp.sum(prod * diag, axis=0, keepdims=True)
            parts += [part[:, i * seg:(i + 1) * seg] for i in range(n_seg)]
            fetch(j)
        part = jnp.concatenate(parts, axis=0)
        hi = part.astype(bf16)
        lo = (part - hi.astype(f32)).astype(bf16)
        ones = seg_ref[...]
        act = jnp.dot(hi, ones, preferred_element_type=f32) + jnp.dot(lo, ones, preferred_element_type=f32)
        for j in range(group):
            w = g_ref[t0 + j] * _gelu_tanh(act[j * n_seg:(j + 1) * n_seg])
            w = jnp.concatenate([jnp.broadcast_to(w[i:i + 1, :], (half, seg)) for i in range(n_seg)], axis=1)
            v_rows = rows[slot0 + j, :, half:, :].reshape(cols, lanes).astype(bf16)
            o_ref[t0 + j] = jnp.dot((w * diag).astype(bf16), v_rows, preferred_element_type=f32)
            fetch(group + j)

    sets, ahead = depth // group, PEER_FETCH_AHEAD

    @pl.when(step_id == 0)
    def _():
        for a in range(ahead):
            first = fetcher(ids_ref, a * group, a * group)
            for c in range(2 * group):
                first(c)

    def sweep(it, last):
        for q in range(sets):
            g = sets * it + q
            for j in range(group):
                wait(q * group + j)
            into = ((q + ahead) % sets) * group
            if last and q + ahead >= sets:
                fetch = fetcher(ids_next_ref, (q + ahead - sets) * group, into)
            else:
                fetch = fetcher(ids_ref, (g + ahead) * group, into)
            mix_group(g * group, q * group, fetch)

    def body(it, carry):
        sweep(it, False)
        return carry

    lax.fori_loop(0, groups // sets - 1, body, 0)
    sweep(groups // sets - 1, True)

    @pl.when(step_id == last_step)
    def _():
        for j in range(ahead * group):
            wait(j)


def peer_experts(xn, ids, gates, uv):
    n, d = xn.shape
    tb = PEER_TOKENS_PER_STEP
    slab, half, lanes = PEER_SLAB_ROWS, PEER_SLAB_ROWS // 2, PEER_SLAB_LANES
    seg = 2 * lanes
    cols = PEER_PICKS * half
    assert n % tb == 0 and tb % PEER_ROW_BUFFERS == 0 and half * lanes == d and cols % seg == 0
    assert 0 < PEER_FETCH_AHEAD < PEER_ROW_BUFFERS // PEER_GROUP
    same_pick = np.arange(seg)[:, None] // half == np.arange(seg)[None, :] // half
    out = pl.pallas_call(
        _peer_expert_body,
        grid=(n // tb,),
        in_specs=[
            pl.BlockSpec((tb, PEER_PICKS), lambda i: (i, 0), memory_space=pltpu.SMEM),
            pl.BlockSpec((tb, PEER_PICKS), lambda i: (jnp.minimum(i + 1, n // tb - 1), 0), memory_space=pltpu.SMEM),
            pl.BlockSpec((tb, half, lanes), lambda i: (i, 0, 0)),
            pl.BlockSpec((tb, cols // seg, seg), lambda i: (i, 0, 0)),
            pl.BlockSpec((seg, seg), lambda i: (0, 0)),
            pl.BlockSpec(memory_space=pl.ANY),
        ],
        out_specs=pl.BlockSpec((tb, half, lanes), lambda i: (i, 0, 0)),
        out_shape=jax.ShapeDtypeStruct((n, half, lanes), f32),
        scratch_shapes=[
            pltpu.VMEM((PEER_ROW_BUFFERS, PEER_PICKS, slab, lanes), f32),
            pltpu.SemaphoreType.DMA((PEER_ROW_BUFFERS,)),
        ],
        compiler_params=pltpu.CompilerParams(dimension_semantics=("arbitrary",)),
        name="peer_experts",
    )(ids, ids, xn.reshape(n, half, lanes), jnp.repeat(gates, half, axis=1).reshape(n, cols // seg, seg),
      jnp.asarray(same_pick, jnp.bfloat16), uv)
    return out.reshape(n, d)


PEER_ROUTE_TOKENS = 512
PEER_HALF_DIM = PEER_QUERY_DIM // 2


def _top_rows(s, row_id, k, payload=None):
    vals, picks = [], []
    sentinel = s.shape[0]
    for _ in range(k):
        m = jnp.max(s, axis=0, keepdims=True)
        first = jnp.min(jnp.where(s == m, row_id, sentinel), axis=0, keepdims=True)
        hit = row_id == first
        vals.append(m)
        if payload is None:
            picks.append(first)
        else:
            picks.append(jnp.max(jnp.where(hit, payload, -1), axis=0, keepdims=True))
        s = jnp.where(hit, -jnp.inf, s)
    return jnp.concatenate(vals, 0), jnp.concatenate(picks, 0)


def _peer_route_body(x_ref, wq_ref, sk_ref, ids_ref, gate_ref, q_scr):
    tn = x_ref.shape[0]
    q = jnp.dot(x_ref[...].astype(jnp.bfloat16), wq_ref[...], preferred_element_type=f32)
    for j in range(2 * PEER_HEADS):
        q_scr[j] = q[:, j * PEER_HALF_DIM:(j + 1) * PEER_HALF_DIM].astype(jnp.bfloat16)
    key_id = lax.broadcasted_iota(jnp.int32, (PEER_N_KEYS, tn), 0)
    n_cand = -(-sum(PEER_TOPK // (a + 1) for a in range(PEER_TOPK)) // 8) * 8
    cand_id = lax.broadcasted_iota(jnp.int32, (n_cand, tn), 0)

    def head(h, carry):
        tops = []
        for p in range(2):
            s = lax.dot_general(sk_ref[2 * h + p], q_scr[2 * h + p], (((1,), (1,)), ((), ())),
                                preferred_element_type=f32)
            tops.append(_top_rows(s, key_id, PEER_TOPK))
        (v0, i0), (v1, i1) = tops
        cs, ce = [], []
        for a in range(PEER_TOPK):
            nb = PEER_TOPK // (a + 1)
            cs.append(v0[a:a + 1] + v1[:nb])
            ce.append(i0[a:a + 1] * PEER_N_KEYS + i1[:nb])
        pad = n_cand - sum(c.shape[0] for c in cs)
        cand_s = jnp.concatenate(cs + [jnp.full((pad, tn), -jnp.inf, f32)], axis=0)
        cand_e = jnp.concatenate(ce + [jnp.zeros((pad, tn), jnp.int32)], axis=0)
        top_s, top_e = _top_rows(cand_s, cand_id, PEER_TOPK, payload=cand_e)
        e = jnp.exp(top_s - top_s[0:1])
        ids_ref[h] = top_e
        gate_ref[h] = e / jnp.sum(e, axis=0, keepdims=True)
        return carry

    lax.fori_loop(0, PEER_HEADS, head, 0)


def peer_route(xn, wq, sub_keys):
    n, d = xn.shape
    tn = PEER_ROUTE_TOKENS
    while n % tn:
        tn //= 2
    assert tn % LANES == 0
    n_q = 2 * PEER_HEADS * PEER_HALF_DIM
    sk = sub_keys.reshape(2 * PEER_HEADS, PEER_N_KEYS, PEER_HALF_DIM).astype(jnp.bfloat16)
    ids_t, gates_t = pl.pallas_call(
        _peer_route_body,
        grid=(n // tn,),
        in_specs=[
            pl.BlockSpec((tn, d), lambda i: (i, 0)),
            pl.BlockSpec((d, n_q), lambda i: (0, 0)),
            pl.BlockSpec((2 * PEER_HEADS, PEER_N_KEYS, PEER_HALF_DIM), lambda i: (0, 0, 0)),
        ],
        out_specs=[
            pl.BlockSpec((PEER_HEADS, PEER_TOPK, tn), lambda i: (0, 0, i)),
            pl.BlockSpec((PEER_HEADS, PEER_TOPK, tn), lambda i: (0, 0, i)),
        ],
        out_shape=[
            jax.ShapeDtypeStruct((PEER_HEADS, PEER_TOPK, n), jnp.int32),
            jax.ShapeDtypeStruct((PEER_HEADS, PEER_TOPK, n), f32),
        ],
        scratch_shapes=[pltpu.VMEM((2 * PEER_HEADS, tn, PEER_HALF_DIM), jnp.bfloat16)],
        compiler_params=pltpu.CompilerParams(dimension_semantics=("arbitrary",)),
        name="peer_route",
    )(xn, wq.astype(jnp.bfloat16), sk)
    to_rows = lambda a: a.reshape(PEER_PICKS, n).T
    return to_rows(ids_t), to_rows(gates_t)


def kernel(x_prompt, x_sample, cache_cmp_kv, cache_slc_kv, cache_win_kv, state_mlstm_c, state_mlstm_n,
           state_mlstm_m, cache_mem_kv, page_table, mem_prompt, g_mix, w_in, b_in, b_forget, ml_head_gain,
           cmp_pe, cmp_w1, cmp_b1, cmp_w2, w_out, g_xattn, g_mem, w_xq, w_xk, w_xv, w_xo, g_ffn, peer_wq,
           peer_sub_keys, peer_u, peer_v, g_final):
    B, T = x_prompt.shape[:2]
    DB, S = x_sample.shape[:2]
    past_len = page_table.shape[1] * PAGE_SIZE
    l = 0

    o_i, o_nq, o_nkv, o_ng = IN_OFFSETS[3], IN_OFFSETS[5], IN_OFFSETS[6], IN_OFFSETS[7]
    n_small = 2 * ML_HEADS + 3 * NSA_HEADS
    small_pad = LANES - n_small
    regroup = lambda a: jnp.concatenate(
        [a[..., :o_i], a[..., o_nq:o_nkv], a[..., o_nkv:o_ng], a[..., o_i:o_nq], a[..., o_ng:],
         jnp.zeros(a.shape[:-1] + (small_pad,), a.dtype)], axis=-1)
    w_in_g, b_in_g = regroup(w_in[l]), regroup(b_in[l])
    in_splits = (4 * ML_WIDTH, NSA_WIDTH, 6 * NSA_KV_LANES, n_small + small_pad)

    def project(x):
        b, t, _ = x.shape
        ml, nq, nkv, small = fused_linear(x.reshape(b * t, D_MODEL), w_in_g, pre_gain=g_mix[l], bias=b_in_g,
                                          splits=in_splits)
        ng = jax.nn.sigmoid(small[:, 2 * ML_HEADS:n_small]).reshape(b, t, 3 * NSA_HEADS)
        return (ml.reshape(b, t, 4 * ML_WIDTH), small.reshape(b, t, -1), nq.reshape(b, t, NSA_WIDTH),
                nkv.reshape(b, t, 6 * NSA_KV_LANES), ng)

    def after_mixers(x, h_ml, h_nsa, mem_kv):
        b, t, _ = x.shape
        h = jnp.concatenate([h_ml, h_nsa], -1).reshape(b * t, D_MODEL)
        x1, xn = fused_linear(h, w_out[l], residual=x.reshape(b * t, D_MODEL), post_gain=g_xattn[l])
        q = fused_linear(xn, w_xq[l]).reshape(b, t, D_MODEL)
        t_pad = -t % 8
        o = cross_attention(jnp.pad(q, ((0, 0), (0, t_pad), (0, 0))), mem_kv)[:, :t]
        return fused_linear(o.reshape(b * t, D_MODEL), w_xo[l], residual=x1, post_gain=g_ffn[l])

    ml, small, nq, nkv, ng = project(x_prompt)
    h_ml, p_c, p_n, p_m = mlstm_pallas(
        ml, small, b_forget[l], ml_head_gain[l],
        jnp.zeros((B, ML_HEADS, ML_HEAD_DIM, ML_HEAD_DIM), f32),
        jnp.zeros((B, ML_HEADS, ML_HEAD_DIM), f32),
        jnp.full((B, ML_HEADS), -jnp.inf, f32))
    rows6 = nkv.reshape(B, T, 6, NSA_KV_HEADS, NSA_HEAD_DIM)
    p_cmp, p_slc, win_rows = rows6[:, :, 0:2], rows6[:, :, 2:4], rows6[:, :, 4:6]
    seg_w = CMP_STRIDE * 2 * NSA_KV_LANES
    kc, vc = compress_rows(nkv[..., :2 * NSA_KV_LANES].reshape(B, T // CMP_STRIDE, seg_w),
                           cmp_pe[l], cmp_w1[l], cmp_b1[l], cmp_w2[l])
    h_nsa = nsa_prompt(nq, ng, nkv, kc, vc)
    p_mem = fused_linear(mem_prompt.reshape(-1, D_MODEL), jnp.concatenate([w_xk[l], w_xv[l]], axis=1),
                         pre_gain=g_mem[l]).reshape(B, MEM_LEN, 2, MEM_HEADS, MEM_HEAD_DIM)
    xp, xp_ffn_in = after_mixers(x_prompt, h_ml, h_nsa, p_mem)
    p_win = win_rows[:, T - min(WINDOW, T):]

    ml, small, nq, nkv, ng = project(x_sample)
    t_pad = ML_CHUNK - S
    ml_p = jnp.pad(ml, ((0, 0), (0, t_pad), (0, 0)))
    small_p = jnp.pad(small, ((0, 0), (0, t_pad), (0, 0)))
    small_p = small_p.at[:, S:, :ML_HEADS].set(-jnp.inf).at[:, S:, ML_HEADS:2 * ML_HEADS].set(jnp.inf)
    h_ml, s_c, s_n, s_m = mlstm_pallas(ml_p, small_p, b_forget[l], ml_head_gain[l],
                                       state_mlstm_c[l], state_mlstm_n[l], state_mlstm_m[l])
    h_ml = h_ml[:, :S]
    rows6 = nkv.reshape(DB, S, 6, NSA_KV_HEADS, NSA_HEAD_DIM)
    s_cmp, s_slc, win_rows = rows6[:, :, 0:2], rows6[:, :, 2:4], rows6[:, :, 4:6]
    win_buf = cache_win_kv[l]
    win_ext = jnp.concatenate([win_buf, win_rows], axis=1)
    h_nsa = nsa_decode(nq, ng, past_len, cache_cmp_kv[l], cache_slc_kv[l], page_table, s_cmp, s_slc,
                       win_ext, past_len - win_buf.shape[1], cmp_pe[l], cmp_w1[l], cmp_b1[l], cmp_w2[l])
    xs, xs_ffn_in = after_mixers(x_sample, h_ml, h_nsa, cache_mem_kv[l])
    w_keep = min(WINDOW, past_len + S)
    s_win = win_ext[:, win_ext.shape[1] - w_keep:]

    n_p = B * T
    uv = jnp.concatenate([peer_u[l], peer_v[l]], axis=1).reshape(-1, PEER_SLAB_ROWS, PEER_SLAB_LANES)
    routes = [peer_route(x, peer_wq[l], peer_sub_keys[l]) for x in (xp_ffn_in, xs_ffn_in)]
    ids, gates = (jnp.concatenate(a, 0) for a in zip(*routes))
    ffn = peer_experts(jnp.concatenate([xp_ffn_in, xs_ffn_in], 0), ids, gates, uv)
    y_prompt = add_rmsnorm(xp, ffn[:n_p], g_final).reshape(x_prompt.shape)
    y_sample = add_rmsnorm(xs, ffn[n_p:], g_final).reshape(x_sample.shape)
    st = lambda a: a[None]
    return (y_prompt, y_sample,
            st(p_cmp), st(p_slc), st(p_win), st(p_c), st(p_n), st(p_m), st(p_mem),
            st(s_cmp), st(s_slc), st(s_win), st(s_c), st(s_n), st(s_m))
```
